```python
import math
import jax, jax.numpy as jnp
from jax import lax
import numpy as np

D_MODEL = 1024
BATCH = 16
SEQ = 256
DEPTH = 4
DEC_BATCH = 2
DEC_SEQ = 4096
PAST_LEN = 256

GRID_W = 64
D_MIX = D_MODEL
BR = D_MIX // 4
POOL_WINDOWS = (2, 4, 8, 16)
POOL_GROUPS = len(POOL_WINDOWS)
POOL_GD = BR // POOL_GROUPS
DN_HEADS = 4
DN_HEAD_DIM = BR // DN_HEADS
CONV_K = 5
CHUNK = 64
S5_P = 16
S5_G = BR // S5_P
S5_N = 64
FT_HEADS = 4
FT_HD = BR // FT_HEADS
SPLIT_SIZES = (BR, BR, 3 * BR, BR, 2 * DN_HEADS, 2 * DN_HEADS, BR, BR, BR, BR)
D_IN_PROJ = sum(SPLIT_SIZES)
EPS = 1e-6
F32 = jnp.float32

kernel_name = 'hybrid_pool_deltanet_s5_fourier_diffusion_step'


def rmsnorm(x, g):
    x32 = x.astype(F32)
    return x32 * lax.rsqrt(jnp.mean(x32 * x32, axis=-1, keepdims=True) + EPS) * g


def l2norm(x):
    return x * lax.rsqrt(jnp.sum(x * x, axis=-1, keepdims=True) + EPS)


def box_mean(x, w, axis):
    L = x.shape[axis]
    pos = np.arange(L)
    lo = np.clip(pos - w // 2, 0, L)
    hi = np.clip(pos - w // 2 + w, 0, L)
    pad = [(0, 0)] * x.ndim
    pad[axis] = (1, 0)
    cs = jnp.pad(jnp.cumsum(x, axis=axis), pad)
    cnt_shape = [1] * x.ndim
    cnt_shape[axis] = L
    cnt = (hi - lo).astype(np.float32).reshape(cnt_shape)
    return (jnp.take(cs, hi, axis=axis) - jnp.take(cs, lo, axis=axis)) / cnt


def pool_branch(u, pool_w, pool_scale, rows):
    B, T, _ = u.shape
    outs = []
    for gi, w in enumerate(POOL_WINDOWS):
        ug = u[..., gi * POOL_GD:(gi + 1) * POOL_GD]
        if rows is None:
            pooled = box_mean(ug, w, 1)
        else:
            ug2 = ug.reshape(B, rows, GRID_W, POOL_GD)
            pooled = box_mean(box_mean(ug2, w, 1), w, 2).reshape(B, T, POOL_GD)
        outs.append(pooled - ug)
    p = jnp.stack(outs, axis=2)
    p = jnp.einsum('btgc,gcd->btgd', p, pool_w).reshape(B, T, BR)
    return p * pool_scale


def short_conv(u, w):
    C = u.shape[-1]
    return lax.conv_general_dilated(
        u, w.astype(u.dtype)[:, None, :], window_strides=(1,),
        padding=[(CONV_K // 2, CONV_K // 2)],
        dimension_numbers=('NWC', 'WIO', 'NWC'), feature_group_count=C)


def gated_delta_chunked(q, k, v, g, beta, s0):
    B, T, H, DK = q.shape
    DV = v.shape[-1]
    n = T // CHUNK

    def chunks(a):
        a = a.reshape((B, n, CHUNK, H) + a.shape[3:])
        return jnp.moveaxis(a, (1, 3), (0, 2))

    qc = chunks(q) * (DK ** -0.5)
    kc = chunks(k)
    vc = chunks(v)
    bc = chunks(beta)
    gc = jnp.cumsum(chunks(g), axis=-1)
    incl = np.tril(np.ones((CHUNK, CHUNK), dtype=bool))
    strict = np.tril(np.ones((CHUNK, CHUNK), dtype=bool), -1)
    diff = gc[..., :, None] - gc[..., None, :]
    decay = jnp.where(incl, jnp.exp(jnp.where(incl, diff, 0.0)), 0.0)
    kk = jnp.einsum('nbhik,nbhjk->nbhij', kc, kc)
    a_mat = jnp.where(strict, kk * decay * bc[..., :, None], 0.0)
    rhs = jnp.concatenate([kc * (bc * jnp.exp(gc))[..., None], vc * bc[..., None]], axis=-1)
    sol = lax.linalg.triangular_solve(a_mat, rhs, left_side=True, lower=True, unit_diagonal=True)
    w_c, u_c = sol[..., :DK], sol[..., DK:]
    aqk = jnp.where(incl, jnp.einsum('nbhik,nbhjk->nbhij', qc, kc) * decay, 0.0)

    def step(S, xs):
        q_i, k_i, w_i, u_i, g_i, a_i = xs
        v_new = u_i - jnp.einsum('bhck,bhkv->bhcv', w_i, S)
        o = (jnp.einsum('bhck,bhkv->bhcv', q_i * jnp.exp(g_i)[..., None], S)
             + jnp.einsum('bhij,bhjv->bhiv', a_i, v_new))
        g_last = g_i[..., -1:]
        S = (S * jnp.exp(g_last)[..., None]
             + jnp.einsum('bhck,bhcv->bhkv', k_i * jnp.exp(g_last - g_i)[..., None], v_new))
        return S, o

    s_fin, o = lax.scan(step, s0, (qc, kc, w_c, u_c, gc, aqk))
    o = jnp.moveaxis(o, (0, 2), (1, 3)).reshape(B, T, H, DV)
    return o, s_fin


def delta_branch(qkv, b_raw, a_raw, lp, s0):
    B, T, _ = qkv.shape
    qkv = jax.nn.silu(short_conv(qkv, lp['dn_conv']))
    q, k, v = jnp.split(qkv, 3, axis=-1)
    shp = (B, T, DN_HEADS, DN_HEAD_DIM)
    q = l2norm(q.reshape(shp))
    k = l2norm(k.reshape(shp))
    v = v.reshape(shp)
    beta = jax.nn.sigmoid(b_raw).reshape(B, T, 2, DN_HEADS)
    g = -jnp.exp(lp['dn_a_log']) * jax.nn.softplus(a_raw.reshape(B, T, 2, DN_HEADS) + lp['dn_dt_bias'])
    o_f, s_f = gated_delta_chunked(q, k, v, g[:, :, 0], beta[:, :, 0], s0[:, 0])
    rev = lambda a: jnp.flip(a, axis=1)
    o_b, s_b = gated_delta_chunked(rev(q), rev(k), rev(v), rev(g[:, :, 1]), rev(beta[:, :, 1]), s0[:, 1])
    o = rmsnorm(o_f + rev(o_b), lp['dn_norm_g']).reshape(B, T, BR)
    return o, jnp.stack([s_f, s_b], axis=1)


def s5_scan(u, a_re, a_im, log_dt, b_re, b_im, c_re, c_im, s0_re, s0_im, reverse):
    dt = jnp.exp(log_dt)[:, None]
    mag = jnp.exp(a_re * dt)
    ab_re = mag * jnp.cos(a_im * dt)
    ab_im = mag * jnp.sin(a_im * dt)
    den = a_re * a_re + a_im * a_im
    nr = ab_re - 1.0
    coef_re = (nr * a_re + ab_im * a_im) / den
    coef_im = (ab_im * a_re - nr * a_im) / den
    bb_re = coef_re[..., None] * b_re - coef_im[..., None] * b_im
    bb_im = coef_re[..., None] * b_im + coef_im[..., None] * b_re
    bu_re = jnp.einsum('btgp,gnp->btgn', u, bb_re)
    bu_im = jnp.einsum('btgp,gnp->btgn', u, bb_im)
    if reverse:
        bu_re = jnp.flip(bu_re, axis=1)
        bu_im = jnp.flip(bu_im, axis=1)
    bu_re = bu_re.at[:, 0].add(ab_re * s0_re - ab_im * s0_im)
    bu_im = bu_im.at[:, 0].add(ab_re * s0_im + ab_im * s0_re)
    a_seq_re = jnp.broadcast_to(ab_re, bu_re.shape)
    a_seq_im = jnp.broadcast_to(ab_im, bu_im.shape)

    def combine(e1, e2):
        a1r, a1i, b1r, b1i = e1
        a2r, a2i, b2r, b2i = e2
        return (a2r * a1r - a2i * a1i, a2r * a1i + a2i * a1r,
                a2r * b1r - a2i * b1i + b2r, a2r * b1i + a2i * b1r + b2i)

    _, _, s_re, s_im = lax.associative_scan(combine, (a_seq_re, a_seq_im, bu_re, bu_im), axis=1)
    fin = (s_re[:, -1], s_im[:, -1])
    if reverse:
        s_re = jnp.flip(s_re, axis=1)
        s_im = jnp.flip(s_im, axis=1)
    y = jnp.einsum('btgn,gpn->btgp', s_re, c_re) - jnp.einsum('btgn,gpn->btgp', s_im, c_im)
    return y, fin


def s5_branch(u, lp, s0):
    B, T, _ = u.shape
    ug = u.reshape(B, T, S5_G, S5_P)
    ys, fins = [], []
    for d in range(2):
        y_d, fin_d = s5_scan(ug, lp['s5_a_re'][d], lp['s5_a_im'][d], lp['s5_log_dt'][d],
                             lp['s5_b_re'][d], lp['s5_b_im'][d], lp['s5_c_re'][d], lp['s5_c_im'][d],
                             s0[:, d, 0], s0[:, d, 1], reverse=(d == 1))
        ys.append(y_d)
        fins.append(jnp.stack(fin_d, axis=1))
    y = ys[0] + ys[1] + lp['s5_d'].reshape(S5_G, S5_P) * ug
    y = jax.nn.gelu(y.reshape(B, T, BR))
    y = y * jax.nn.sigmoid(y @ lp['s5_glu_w'] + lp['s5_glu_b'])
    return y, jnp.stack(fins, axis=1)


def fourier_branch(u):
    B, T, _ = u.shape
    f = jnp.fft.fftn(u.reshape(B, T, FT_HEADS, FT_HD), axes=(1, 3), norm='ortho').real
    return f.reshape(B, T, BR)


def trunk_layer(x, cond, lp, s_dn0, s_s50, rows):
    B = x.shape[0]
    if s_dn0 is None:
        s_dn0 = jnp.zeros((B, 2, DN_HEADS, DN_HEAD_DIM, DN_HEAD_DIM), F32)
        s_s50 = jnp.zeros((B, 2, 2, S5_G, S5_N), F32)
    ada = jax.nn.silu(cond) @ lp['w_ada'] + lp['b_ada']
    shift, scale, gate = jnp.split(ada[:, None, :], 3, axis=-1)
    h = rmsnorm(x, lp['norm_g']) * (1.0 + scale) + shift
    idx = [int(i) for i in np.cumsum(SPLIT_SIZES)[:-1]]
    pool_u, pool_z, dn_qkv, dn_z, dn_b, dn_a, s5_u, s5_z, ft_u, ft_z = jnp.split(h @ lp['w_in'], idx, axis=-1)
    y_pool = pool_branch(pool_u, lp['pool_w'], lp['pool_scale'], rows) * jax.nn.silu(pool_z)
    y_dn, st_dn = delta_branch(dn_qkv, dn_b, dn_a, lp, s_dn0)
    y_dn = y_dn * jax.nn.silu(dn_z)
    y_s5, st_s5 = s5_branch(s5_u, lp, s_s50)
    y_s5 = y_s5 * jax.nn.silu(s5_z)
    y_ft = (fourier_branch(ft_u) @ lp['ft_w']) * jax.nn.silu(ft_z)
    y = jnp.concatenate([y_pool, y_dn, y_s5, y_ft], axis=-1) @ lp['w_out']
    return x + gate * y, st_dn, st_s5


def setup_inputs(seed: int = 0) -> dict:
    key = jax.random.key(seed)
    ks = iter(jax.random.split(key, 40))
    nrm = lambda shape, s: jax.random.normal(next(ks), shape, jnp.float32) * s
    uni = lambda shape, lo, hi: jax.random.uniform(next(ks), shape, jnp.float32, minval=lo, maxval=hi)
    L = DEPTH
    x_prompt = nrm((BATCH, SEQ, D_MODEL), 1.0)
    x_sample = nrm((DEC_BATCH, DEC_SEQ, D_MODEL), 1.0)
    c = nrm((DEC_BATCH, D_MODEL), 1.0)
    state_delta = nrm((DEC_BATCH, DEPTH, 2, DN_HEADS, DN_HEAD_DIM, DN_HEAD_DIM), 0.1)
    state_s5 = nrm((DEC_BATCH, DEPTH, 2, 2, S5_G, S5_N), 0.3)
    c_ctx = nrm((D_MODEL,), 1.0)
    w_ada = nrm((L, D_MODEL, 3 * D_MODEL), 0.2 * D_MODEL ** -0.5)
    b_ada = nrm((L, 3 * D_MODEL), 0.02)
    norm_g = 1.0 + nrm((L, D_MODEL), 0.02)
    w_in = nrm((L, D_MODEL, D_IN_PROJ), D_MODEL ** -0.5)
    pool_w = nrm((L, POOL_GROUPS, POOL_GD, POOL_GD), POOL_GD ** -0.5)
    pool_scale = 1.0 + nrm((L, BR), 0.02)
    dn_conv = nrm((L, CONV_K, 3 * BR), CONV_K ** -0.5)
    dn_a_log = jnp.log(uni((L, 2, DN_HEADS), 1.0, 16.0))
    dn_dt = jnp.exp(uni((L, 2, DN_HEADS), math.log(1e-3), math.log(1e-1)))
    dn_dt_bias = dn_dt + jnp.log(-jnp.expm1(-dn_dt))
    dn_norm_g = 1.0 + nrm((L, DN_HEAD_DIM), 0.02)
    s5_a_re = -0.5 + nrm((L, 2, S5_G, S5_N), 0.01)
    s5_a_im = math.pi * jnp.arange(S5_N, dtype=jnp.float32) + nrm((L, 2, S5_G, S5_N), 0.01)
    s5_log_dt = uni((L, 2, S5_G), math.log(1e-3), math.log(1e-1))
    s5_b_re = nrm((L, 2, S5_G, S5_N, S5_P), (2 * S5_P) ** -0.5)
    s5_b_im = nrm((L, 2, S5_G, S5_N, S5_P), (2 * S5_P) ** -0.5)
    s5_c_re = nrm((L, 2, S5_G, S5_P, S5_N), (2 * S5_N) ** -0.5)
    s5_c_im = nrm((L, 2, S5_G, S5_P, S5_N), (2 * S5_N) ** -0.5)
    s5_d = nrm((L, BR), 1.0)
    s5_glu_w = nrm((L, BR, BR), BR ** -0.5)
    s5_glu_b = nrm((L, BR), 0.02)
    ft_w = nrm((L, BR, BR), BR ** -0.5)
    w_out = nrm((L, D_MIX, D_MODEL), D_MIX ** -0.5)
    final_g = 1.0 + nrm((D_MODEL,), 0.02)
    return {'x_prompt': x_prompt, 'x_sample': x_sample, 'c': c,
            'state_delta': state_delta, 'state_s5': state_s5, 'c_ctx': c_ctx,
            'w_ada': w_ada, 'b_ada': b_ada, 'norm_g': norm_g, 'w_in': w_in,
            'pool_w': pool_w, 'pool_scale': pool_scale,
            'dn_conv': dn_conv, 'dn_a_log': dn_a_log, 'dn_dt_bias': dn_dt_bias, 'dn_norm_g': dn_norm_g,
            's5_a_re': s5_a_re, 's5_a_im': s5_a_im, 's5_log_dt': s5_log_dt,
            's5_b_re': s5_b_re, 's5_b_im': s5_b_im, 's5_c_re': s5_c_re, 's5_c_im': s5_c_im,
            's5_d': s5_d, 's5_glu_w': s5_glu_w, 's5_glu_b': s5_glu_b,
            'ft_w': ft_w, 'w_out': w_out, 'final_g': final_g}


def reference(x_prompt, x_sample, c, state_delta, state_s5, c_ctx, w_ada, b_ada, norm_g, w_in,
              pool_w, pool_scale, dn_conv, dn_a_log, dn_dt_bias, dn_norm_g,
              s5_a_re, s5_a_im, s5_log_dt, s5_b_re, s5_b_im, s5_c_re, s5_c_im,
              s5_d, s5_glu_w, s5_glu_b, ft_w, w_out, final_g):
    params = {'w_ada': w_ada, 'b_ada': b_ada, 'norm_g': norm_g, 'w_in': w_in,
              'pool_w': pool_w, 'pool_scale': pool_scale,
              'dn_conv': dn_conv, 'dn_a_log': dn_a_log, 'dn_dt_bias': dn_dt_bias, 'dn_norm_g': dn_norm_g,
              's5_a_re': s5_a_re, 's5_a_im': s5_a_im, 's5_log_dt': s5_log_dt,
              's5_b_re': s5_b_re, 's5_b_im': s5_b_im, 's5_c_re': s5_c_re, 's5_c_im': s5_c_im,
              's5_d': s5_d, 's5_glu_w': s5_glu_w, 's5_glu_b': s5_glu_b,
              'ft_w': ft_w, 'w_out': w_out}
    xp = x_prompt.astype(F32)
    xs = x_sample.astype(F32)
    rows = xs.shape[1] // GRID_W
    ctx_cond = c_ctx.astype(F32)[None]
    lat_cond = c.astype(F32)
    new_dn, new_s5 = [], []
    for l in range(DEPTH):
        lp = {name: arr[l] for name, arr in params.items()}
        xp, st_dn, st_s5 = trunk_layer(xp, ctx_cond, lp, None, None, None)
        new_dn.append(st_dn)
        new_s5.append(st_s5)
        xs, _, _ = trunk_layer(xs, lat_cond, lp, state_delta[:, l].astype(F32),
                               state_s5[:, l].astype(F32), rows)
    y_prompt = rmsnorm(xp, final_g).astype(x_prompt.dtype)
    y_sample = rmsnorm(xs, final_g).astype(x_sample.dtype)
    new_state_delta = jnp.stack(new_dn, axis=1).astype(state_delta.dtype)
    new_state_s5 = jnp.stack(new_s5, axis=1).astype(state_s5.dtype)
    return (y_prompt, y_sample, new_state_delta, new_state_s5)
```

```python
import functools
import math

import numpy as np
import jax
import jax.numpy as jnp
from jax import lax
from jax.experimental import pallas as pl
from jax.experimental.pallas import tpu as pltpu

F32 = jnp.float32
BF16 = jnp.bfloat16

D_MODEL = 1024
BATCH = 16
SEQ = 256
DEPTH = 4
DEC_BATCH = 2
DEC_SEQ = 4096
GRID_W = 64
GRID_H = DEC_SEQ // GRID_W
BR = 256
POOL_WINDOWS = (2, 4, 8, 16)
POOL_GD = 64
DN_HEADS = 4
DN_HD = 64
CONV_K = 5
CHUNK = 64
S5_P = 16
S5_G = 16
S5_N = 64
S5_L = 16
FT_HD = 64
EPS = 1e-6

N_CTX = BATCH * SEQ
N_LAT = DEC_BATCH * DEC_SEQ
N_TOK = N_CTX + N_LAT
N_COND = 1 + DEC_BATCH
TM = 512
W_IN_COLS = 2688
VMEM_LIMIT = 56 * 1024 * 1024

SDS = jax.ShapeDtypeStruct
BS = pl.BlockSpec


def _cparams(*sem):
    return pltpu.CompilerParams(dimension_semantics=sem, vmem_limit_bytes=VMEM_LIMIT)


def bdot(a, b):
    return jnp.dot(a.astype(BF16), b.astype(BF16), preferred_element_type=F32)


def bdot_nt(a, b):
    return lax.dot_general(a.astype(BF16), b.astype(BF16), (((1,), (1,)), ((), ())), preferred_element_type=F32)


def bdot_tn(a, b):
    return lax.dot_general(a.astype(BF16), b.astype(BF16), (((0,), (0,)), ((), ())), preferred_element_type=F32)


def hdot(a, b):
    return jnp.dot(a, b, preferred_element_type=F32, precision=lax.Precision.HIGHEST)


def silu(x):
    return x * jax.nn.sigmoid(x)


def _cond_index(i):
    tiles_ctx = N_CTX // TM
    tiles_seq = DEC_SEQ // TM
    return jnp.where(i < tiles_ctx, 0, 1 + (i - tiles_ctx) // tiles_seq)


def _ada_body(c_ref, w_ref, b_ref, o_ref):
    o_ref[0] = hdot(silu(c_ref[...]), w_ref[0]) + b_ref[0]


def ada_all(cond8, w_ada, b_ada):
    tn = 512
    return pl.pallas_call(
        _ada_body,
        out_shape=SDS((DEPTH, 8, 3 * D_MODEL), F32),
        grid=(DEPTH, 3 * D_MODEL // tn),
        in_specs=[BS((8, D_MODEL), lambda l, j: (0, 0)),
                  BS((1, D_MODEL, tn), lambda l, j: (l, 0, j)),
                  BS((1, 1, tn), lambda l, j: (l, 0, j))],
        out_specs=BS((1, 8, tn), lambda l, j: (l, 0, j)),
        compiler_params=_cparams("parallel", "parallel"),
        name="ada",
    )(cond8, w_ada, b_ada.reshape(DEPTH, 1, 3 * D_MODEL))


def _inproj_body(x_ref, mod_ref, g_ref, w_ref, pool_ref, qkv_ref, dnz_ref, s5_ref, ft_ref, ba_ref):
    x = x_ref[...]
    m = mod_ref[0]
    shift = m[:, 0:D_MODEL]
    scale = m[:, D_MODEL:2 * D_MODEL]
    xn = x * lax.rsqrt(jnp.mean(x * x, axis=-1, keepdims=True) + EPS) * g_ref[...]
    h = (xn * (1.0 + scale) + shift).astype(BF16)

    def proj(lo, hi):
        return jnp.dot(h, w_ref[:, lo:hi], preferred_element_type=F32)

    pool_ref[...] = proj(0, 512)
    qkv_ref[...] = proj(512, 1280)
    dnz_ref[...] = proj(1280, 1536)
    s5_ref[...] = proj(1536, 2048)
    ft_ref[...] = proj(2048, 2560)
    ba_ref[...] = proj(2560, 2688)


def inproj(x, mod, norm_g, w_in_p):
    widths = (512, 768, 256, 512, 512, 128)
    return pl.pallas_call(
        _inproj_body,
        out_shape=[SDS((N_TOK, w), F32) for w in widths],
        grid=(N_TOK // TM,),
        in_specs=[BS((TM, D_MODEL), lambda i: (i, 0)),
                  BS((1, 1, 3 * D_MODEL), lambda i: (_cond_index(i), 0, 0)),
                  BS((1, D_MODEL), lambda i: (0, 0)),
                  BS((D_MODEL, W_IN_COLS), lambda i: (0, 0))],
        out_specs=[BS((TM, w), lambda i: (i, 0)) for w in widths],
        compiler_params=_cparams("parallel"),
        name="inproj",
    )(x, mod, norm_g, w_in_p)


def _pool_body(u_ref, z_ref, pm_ref, inv_ref, w_ref, sc_ref, o_ref, *scratch, two_d):
    nblk = u_ref.shape[0] // 256
    if two_d:
        pad_ref, v_ref = scratch
        halo = 8 * GRID_W
        pad_ref[0:halo, :] = jnp.zeros((halo, BR), F32)
        pad_ref[halo + DEC_SEQ:2 * halo + DEC_SEQ, :] = jnp.zeros((halo, BR), F32)
        pad_ref[halo:halo + DEC_SEQ, :] = u_ref[...]
        lane = lax.broadcasted_iota(jnp.int32, (GRID_W, 128), 1)

        def row_body(r, c):
            base = pl.multiple_of(r * GRID_W, GRID_W)

            def slab(d, lo):
                return pad_ref[pl.ds(base + (8 + d) * GRID_W, GRID_W), lo:lo + 128]

            s2 = slab(-1, 0) + slab(0, 0)
            s4 = s2 + slab(-2, 0) + slab(1, 0)
            v_ref[pl.ds(base, GRID_W), 0:128] = jnp.where(lane < 64, s2, s4)
            s8 = slab(-4, 128)
            for d in (-3, -2, -1, 0, 1, 2, 3):
                s8 = s8 + slab(d, 128)
            s16 = s8
            for d in (-8, -7, -6, -5, 4, 5, 6, 7):
                s16 = s16 + slab(d, 128)
            v_ref[pl.ds(base, GRID_W), 128:256] = jnp.where(lane < 64, s8, s16)
            return c

        lax.fori_loop(0, GRID_H, row_body, 0)
        src = v_ref
    else:
        src = u_ref
    grp = lax.broadcasted_iota(jnp.int32, (256, BR), 1) // POOL_GD

    def blk_body(b, c):
        r0 = pl.multiple_of(b * 256, 256)
        vb = src[pl.ds(r0, 256), :]
        hi = vb.astype(BF16)
        lo = (vb - hi.astype(F32)).astype(BF16)
        res = jnp.zeros((256, BR), F32)
        for g in range(len(POOL_WINDOWS)):
            pg = (jnp.dot(pm_ref[g], hi, preferred_element_type=F32)
                  + jnp.dot(pm_ref[g], lo, preferred_element_type=F32))
            res = jnp.where(grp == g, pg, res)
        pooled = res * inv_ref[pl.ds(r0, 256), :]
        d = pooled - u_ref[pl.ds(r0, 256), :]
        y = bdot(d, w_ref[...]) * sc_ref[...]
        o_ref[pl.ds(r0, 256), :] = y * silu(z_ref[pl.ds(r0, 256), :])
        return c

    lax.fori_loop(0, nblk, blk_body, 0)


def _band_matrices(seg):
    t = np.arange(256)
    out = []
    for w in POOL_WINDOWS:
        lo = t - w // 2
        hi = t - w // 2 + w
        s = t[None, :]
        m = (s >= lo[:, None]) & (s < hi[:, None]) & ((s // seg) == (t[:, None] // seg))
        out.append(m.astype(np.float32))
    return np.stack(out)


def _counts(length, w):
    pos = np.arange(length)
    return (np.clip(pos - w // 2 + w, 0, length) - np.clip(pos - w // 2, 0, length)).astype(np.float64)


def _pool_constants():
    inv_ctx = np.concatenate([np.repeat((1.0 / _counts(SEQ, w))[:, None], POOL_GD, 1) for w in POOL_WINDOWS], 1)
    inv_lat = []
    for w in POOL_WINDOWS:
        c2 = np.outer(_counts(GRID_H, w), _counts(GRID_W, w)).reshape(DEC_SEQ)
        inv_lat.append(np.repeat((1.0 / c2)[:, None], POOL_GD, 1))
    inv_lat = np.concatenate(inv_lat, 1)
    return (jnp.asarray(_band_matrices(SEQ), BF16), jnp.asarray(inv_ctx, F32),
            jnp.asarray(_band_matrices(GRID_W), BF16), jnp.asarray(inv_lat, F32))


def pool_branch(pool, pm, inv, w_bd, scale, two_d):
    if two_d:
        rows, nseq, blk0 = DEC_SEQ, DEC_BATCH, N_CTX // DEC_SEQ
        scratch = [pltpu.VMEM((DEC_SEQ + 16 * GRID_W, BR), F32), pltpu.VMEM((DEC_SEQ, BR), F32)]
    else:
        rows, nseq, blk0 = SEQ, BATCH, 0
        scratch = []
    return pl.pallas_call(
        functools.partial(_pool_body, two_d=two_d),
        out_shape=SDS((nseq * rows, BR), F32),
        grid=(nseq,),
        in_specs=[BS((rows, BR), lambda i: (blk0 + i, 0)),
                  BS((rows, BR), lambda i: (blk0 + i, 1)),
                  BS((4, 256, 256), lambda i: (0, 0, 0)),
                  BS((rows, BR), lambda i: (0, 0)),
                  BS((BR, BR), lambda i: (0, 0)),
                  BS((1, BR), lambda i: (0, 0))],
        out_specs=BS((rows, BR), lambda i: (i, 0)),
        scratch_shapes=scratch,
        compiler_params=_cparams("parallel"),
        name="pool2d" if two_d else "pool1d",
    )(pool, pool, pm, inv, w_bd, scale)


def _ft_ctx_body(u_ref, z_ref, fpos_ref, fch_ref, w_ref, o_ref):
    uc = bdot(u_ref[...], fch_ref[...])
    st = jnp.concatenate([uc[:, 0:BR], uc[:, BR:2 * BR]], axis=0)
    f = bdot(fpos_ref[...], st)
    o_ref[...] = bdot(f, w_ref[...]) * silu(z_ref[...])


def ft_ctx(ft, fpos, fch, ft_w):
    return pl.pallas_call(
        _ft_ctx_body,
        out_shape=SDS((N_CTX, BR), F32),
        grid=(BATCH,),
        in_specs=[BS((SEQ, BR), lambda i: (i, 0)),
                  BS((SEQ, BR), lambda i: (i, 1)),
                  BS((SEQ, 2 * SEQ), lambda i: (0, 0)),
                  BS((BR, 2 * BR), lambda i: (0, 0)),
                  BS((BR, BR), lambda i: (0, 0))],
        out_specs=BS((SEQ, BR), lambda i: (i, 0)),
        compiler_params=_cparams("parallel"),
        name="ft_ctx",
    )(ft, ft, fpos, fch, ft_w)


def _ft_lat_body(u_ref, z_ref, g_ref, h_ref, fch_ref, w_ref, o_ref, x_ref, yr_ref, yi_ref):
    for hf in range(2):
        x_ref[hf] = u_ref[:, hf * 128:(hf + 1) * 128]

    def stage1(t2, c):
        xs = jnp.concatenate([x_ref[hf, pl.ds(t2, GRID_H, stride=GRID_W), :] for hf in range(2)], axis=1)
        y = jnp.dot(g_ref[t2], xs.astype(BF16), preferred_element_type=F32)
        r0 = pl.multiple_of(t2 * GRID_W, GRID_W)
        for hf in range(2):
            yr_ref[hf, pl.ds(r0, GRID_W), :] = y[0:64, hf * 128:(hf + 1) * 128]
            yi_ref[hf, pl.ds(r0, GRID_W), :] = y[64:128, hf * 128:(hf + 1) * 128]
        return c

    lax.fori_loop(0, GRID_W, stage1, 0)

    def stage2(kb, c):
        yr = jnp.concatenate([yr_ref[hf, pl.ds(kb, GRID_W, stride=GRID_W), :] for hf in range(2)], axis=1)
        yi = jnp.concatenate([yi_ref[hf, pl.ds(kb, GRID_W, stride=GRID_W), :] for hf in range(2)], axis=1)
        st = jnp.concatenate([yr, yi], axis=0).astype(BF16)
        a = jnp.dot(h_ref[...], st, preferred_element_type=F32)
        for hf in range(2):
            yr_ref[hf, pl.ds(kb, GRID_W, stride=GRID_W), :] = a[0:64, hf * 128:(hf + 1) * 128]
            yi_ref[hf, pl.ds(kb, GRID_W, stride=GRID_W), :] = a[64:128, hf * 128:(hf + 1) * 128]
        return c

    lax.fori_loop(0, GRID_W, stage2, 0)

    def stage3(b, c):
        r0 = pl.multiple_of(b * TM, TM)
        ar = jnp.concatenate([yr_ref[hf, pl.ds(r0, TM), :] for hf in range(2)], axis=1)
        ai = jnp.concatenate([yi_ref[hf, pl.ds(r0, TM), :] for hf in range(2)], axis=1)
        f = bdot(ar, fch_ref[0:BR, :]) + bdot(ai, fch_ref[BR:2 * BR, :])
        o_ref[pl.ds(r0, TM), :] = bdot(f, w_ref[...]) * silu(z_ref[pl.ds(r0, TM), :])
        return c

    lax.fori_loop(0, DEC_SEQ // TM, stage3, 0)


def ft_lat(ft, g1, h2, fch2, ft_w):
    blk0 = N_CTX // DEC_SEQ
    return pl.pallas_call(
        _ft_lat_body,
        out_shape=SDS((N_LAT, BR), F32),
        grid=(DEC_BATCH,),
        in_specs=[BS((DEC_SEQ, BR), lambda i: (blk0 + i, 0)),
                  BS((DEC_SEQ, BR), lambda i: (blk0 + i, 1)),
                  BS((GRID_W, 128, GRID_H), lambda i: (0, 0, 0)),
                  BS((128, 128), lambda i: (0, 0)),
                  BS((2 * BR, BR), lambda i: (0, 0)),
                  BS((BR, BR), lambda i: (0, 0))],
        out_specs=BS((DEC_SEQ, BR), lambda i: (i, 0)),
        scratch_shapes=[pltpu.VMEM((2, DEC_SEQ, 128), F32)] * 3,
        compiler_params=_cparams("parallel"),
        name="ft_lat",
    )(ft, ft, g1, h2, fch2, ft_w)


def _ft_constants():
    c = np.arange(FT_HD)
    ang = 2.0 * np.pi * np.outer(c, c) / FT_HD
    eye4 = np.eye(BR // FT_HD)
    cc = np.kron(eye4, np.cos(ang)) / 8.0
    sc = np.kron(eye4, np.sin(ang)) / 8.0
    t = np.arange(SEQ)
    angt = 2.0 * np.pi * (np.outer(t, t) % SEQ) / SEQ
    fpos = np.concatenate([np.cos(angt), -np.sin(angt)], axis=1) / 16.0
    fch = np.concatenate([cc, sc], axis=1)
    kb = np.arange(GRID_W)[None, :, None]
    t1 = np.arange(GRID_H)[None, None, :]
    t2 = np.arange(GRID_W)[:, None, None]
    a1 = 2.0 * np.pi * ((kb * (GRID_W * t1 + t2)) % DEC_SEQ) / DEC_SEQ
    g1 = np.concatenate([np.cos(a1), -np.sin(a1)], axis=1) / 8.0
    a2 = 2.0 * np.pi * (np.outer(np.arange(GRID_W), np.arange(GRID_W)) % GRID_W) / GRID_W
    c2, s2 = np.cos(a2) / 8.0, np.sin(a2) / 8.0
    h2 = np.block([[c2, s2], [-s2, c2]])
    fch2 = np.concatenate([cc, sc], axis=0)
    as_bf = lambda a: jnp.asarray(a, F32).astype(BF16)
    return as_bf(fpos), as_bf(fch), as_bf(g1), as_bf(h2), as_bf(fch2)


def _s5_intra_body(u_ref, w_ref, y_ref, e_ref):
    v = jnp.dot(u_ref[0], w_ref[0], preferred_element_type=F32)
    y_ref[0] = v[:, 0:256]
    e_ref[0] = v[:, 256:512]


def s5_intra(ug, wcat):
    rows = ug.shape[1]
    return pl.pallas_call(
        _s5_intra_body,
        out_shape=[SDS((S5_G, rows, 256), F32), SDS((S5_G, rows, 256), F32)],
        grid=(S5_G,),
        in_specs=[BS((1, rows, 256), lambda g: (g, 0, 0)), BS((1, 256, 512), lambda g: (g, 0, 0))],
        out_specs=[BS((1, rows, 256), lambda g: (g, 0, 0)), BS((1, rows, 256), lambda g: (g, 0, 0))],
        compiler_params=_cparams("parallel"),
        name="s5_intra",
    )(ug, wcat)


def _s5_scan_body(e_ref, al_ref, s0_ref, sp_ref, fin_ref):
    nc = e_ref.shape[0]
    a_re = al_ref[:, 0:128]
    a_im = al_ref[:, 128:256]

    def step(c, carry):
        s_re, s_im = carry
        sp_ref[c, :, 0:128] = s_re
        sp_ref[c, :, 128:256] = s_im
        n_re = a_re * s_re - a_im * s_im + e_ref[c, :, 0:128]
        n_im = a_re * s_im + a_im * s_re + e_ref[c, :, 128:256]
        return n_re, n_im

    s_re, s_im = lax.fori_loop(0, nc, step, (s0_ref[:, 0:128], s0_ref[:, 128:256]))
    fin_ref[:, 0:128] = s_re
    fin_ref[:, 128:256] = s_im


def s5_scan(e_scan, al, s0):
    nc, rows, _ = e_scan.shape
    return pl.pallas_call(
        _s5_scan_body,
        out_shape=[SDS((nc, rows, 256), F32), SDS((rows, 256), F32)],
        compiler_params=pltpu.CompilerParams(vmem_limit_bytes=VMEM_LIMIT),
        name="s5_scan",
    )(e_scan, al, s0)


def _s5_out_body(y_ref, s_ref, r_ref, o_ref):
    o_ref[0] = y_ref[0] + jnp.dot(s_ref[0], r_ref[0], preferred_element_type=F32)


def s5_out(yi, sp, r):
    rows = yi.shape[1]
    spec = BS((1, rows, 256), lambda g: (g, 0, 0))
    return pl.pallas_call(
        _s5_out_body,
        out_shape=SDS((S5_G, rows, 256), F32),
        grid=(S5_G,),
        in_specs=[spec, spec, BS((1, 256, 256), lambda g: (g, 0, 0))],
        out_specs=spec,
        compiler_params=_cparams("parallel"),
        name="s5_out",
    )(yi, sp, r)


def _s5_weights(a_re, a_im, log_dt, b_re, b_im, c_re, c_im):
    hi = lax.Precision.HIGHEST
    L = S5_L
    dt = jnp.exp(log_dt)[..., None]
    xr, xi = a_re * dt, a_im * dt
    mag = jnp.exp(xr)
    ab_re, ab_im = mag * jnp.cos(xi), mag * jnp.sin(xi)
    den = a_re * a_re + a_im * a_im
    nr = ab_re - 1.0
    coef_re = (nr * a_re + ab_im * a_im) / den
    coef_im = (ab_im * a_re - nr * a_im) / den
    bb_re = coef_re[..., None] * b_re - coef_im[..., None] * b_im
    bb_im = coef_re[..., None] * b_im + coef_im[..., None] * b_re
    m = jnp.arange(L + 1, dtype=F32)[:, None, None, None]
    pw_re = jnp.exp(m * xr) * jnp.cos(m * xi)
    pw_im = jnp.exp(m * xr) * jnp.sin(m * xi)
    cp_re = c_re[None] * pw_re[:, :, :, None, :] - c_im[None] * pw_im[:, :, :, None, :]
    cp_im = c_re[None] * pw_im[:, :, :, None, :] + c_im[None] * pw_re[:, :, :, None, :]
    k = (jnp.einsum('mdgqn,dgnp->mdgpq', cp_re, bb_re, precision=hi)
         - jnp.einsum('mdgqn,dgnp->mdgpq', cp_im, bb_im, precision=hi))
    ii = np.arange(L)[:, None]
    jj = np.arange(L)[None, :]
    mf = jnp.where((jj >= ii)[:, :, None, None, None], k[:, 0][np.clip(jj - ii, 0, L)], 0.0)
    mb = jnp.where((ii >= jj)[:, :, None, None, None], k[:, 1][np.clip(ii - jj, 0, L)], 0.0)
    mt = (mf + mb).transpose(2, 0, 3, 1, 4).reshape(S5_G, L * S5_P, L * S5_P)

    def inject(pr, pi, d):
        q_re = pr[..., None] * bb_re[d][None] - pi[..., None] * bb_im[d][None]
        q_im = pr[..., None] * bb_im[d][None] + pi[..., None] * bb_re[d][None]
        f = lambda q: q.transpose(1, 0, 3, 2).reshape(S5_G, L * S5_P, S5_N)
        return f(q_re), f(q_im)

    qf_re, qf_im = inject(pw_re[:L, 0][::-1], pw_im[:L, 0][::-1], 0)
    qb_re, qb_im = inject(pw_re[:L, 1], pw_im[:L, 1], 1)
    wcat = jnp.concatenate([mt, qf_re, qb_re, qf_im, qb_im], axis=-1)

    def readout(pr, pi, d):
        d_re = c_re[d][None] * pr[:, :, None, :] - c_im[d][None] * pi[:, :, None, :]
        d_im = c_re[d][None] * pi[:, :, None, :] + c_im[d][None] * pr[:, :, None, :]
        f = lambda x: x.transpose(1, 3, 0, 2).reshape(S5_G, S5_N, L * S5_P)
        return f(d_re), f(-d_im)

    rf_re, rf_im = readout(pw_re[1:L + 1, 0], pw_im[1:L + 1, 0], 0)
    rb_re, rb_im = readout(pw_re[1:L + 1, 1][::-1], pw_im[1:L + 1, 1][::-1], 1)
    r = jnp.concatenate([rf_re, rb_re, rf_im, rb_im], axis=1)
    al = jnp.concatenate([pw_re[L, 0], pw_re[L, 1], pw_im[L, 0], pw_im[L, 1]], axis=-1)
    return wcat.astype(BF16), r.astype(BF16), al


_BWD_LANES = np.concatenate([np.zeros(64, bool), np.ones(64, bool), np.zeros(64, bool), np.ones(64, bool)])


def _flip_bwd(x):
    return jnp.where(_BWD_LANES, jnp.flip(x, axis=0), x)


def s5_branch(s5_u, wcat, r, al, s0_lat):
    L = S5_L
    rows = N_TOK // L
    ug = s5_u.reshape(rows, L, S5_G, S5_P).transpose(2, 0, 1, 3).reshape(S5_G, rows, L * S5_P).astype(BF16)
    yi, e = s5_intra(ug, wcat)
    rc, ncc = N_CTX // L, SEQ // L
    ncl = DEC_SEQ // L
    e_ctx = e[:, :rc].reshape(S5_G, BATCH, ncc, 256).transpose(2, 1, 0, 3).reshape(ncc, BATCH * S5_G, 256)
    e_lat = e[:, rc:].reshape(S5_G, DEC_BATCH, ncl, 256).transpose(2, 1, 0, 3).reshape(ncl, DEC_BATCH * S5_G, 256)
    sp_ctx, fin_ctx = s5_scan(_flip_bwd(e_ctx), jnp.tile(al, (BATCH, 1)), jnp.zeros((BATCH * S5_G, 256), F32))
    sp_lat, _ = s5_scan(_flip_bwd(e_lat), jnp.tile(al, (DEC_BATCH, 1)), s0_lat)
    sp_ctx = _flip_bwd(sp_ctx).reshape(ncc, BATCH, S5_G, 256).transpose(2, 1, 0, 3).reshape(S5_G, rc, 256)
    sp_lat = _flip_bwd(sp_lat).reshape(ncl, DEC_BATCH, S5_G, 256).transpose(2, 1, 0, 3).reshape(S5_G, rows - rc, 256)
    sp = jnp.concatenate([sp_ctx, sp_lat], axis=1).astype(BF16)
    y = s5_out(yi, sp, r)
    y = y.reshape(S5_G, rows, L, S5_P).transpose(1, 2, 0, 3).reshape(N_TOK, BR)
    return y, fin_ctx


DN_TILE = 256
DN_HALO = 8


def _dn_conv_body(x_ref, prev_ref, next_ref, ba_ref, w_ref, gp_ref, o_ref, gate_ref, pad_ref):
    i = pl.program_id(0)
    tiles_ctx = N_CTX // DN_TILE
    tiles_seq = DEC_SEQ // DN_TILE
    j = (i - tiles_ctx) % tiles_seq
    first = jnp.logical_or(i < tiles_ctx, j == 0)
    last = jnp.logical_or(i < tiles_ctx, j == tiles_seq - 1)
    pad_ref[0:DN_HALO, :] = jnp.where(first, 0.0, prev_ref[...])
    pad_ref[DN_HALO:DN_HALO + DN_TILE, :] = x_ref[...]
    pad_ref[DN_HALO + DN_TILE:2 * DN_HALO + DN_TILE, :] = jnp.where(last, 0.0, next_ref[...])
    acc = jnp.zeros((DN_TILE, 3 * BR), F32)
    for t in range(CONV_K):
        acc = acc + pad_ref[pl.ds(DN_HALO - CONV_K // 2 + t, DN_TILE), :] * w_ref[t:t + 1, :]
    o_ref[...] = silu(acc)
    raw = ba_ref[...]
    lane = lax.broadcasted_iota(jnp.int32, raw.shape, 1)
    xa = raw + gp_ref[1:2, :]
    sp = jnp.maximum(xa, 0.0) + jnp.log1p(jnp.exp(-jnp.abs(xa)))
    gate_ref[...] = jnp.where(lane < 2 * DN_HEADS, jax.nn.sigmoid(raw), -jnp.exp(gp_ref[0:1, :]) * sp)


def dn_conv(qkv, ba, conv_w, gate_p):
    nt = N_TOK // DN_TILE
    per = DN_TILE // DN_HALO
    nhb = N_TOK // DN_HALO
    return pl.pallas_call(
        _dn_conv_body,
        out_shape=[SDS((N_TOK, 3 * BR), F32), SDS((N_TOK, 128), F32)],
        grid=(nt,),
        in_specs=[BS((DN_TILE, 3 * BR), lambda i: (i, 0)),
                  BS((DN_HALO, 3 * BR), lambda i: (jnp.maximum(i * per - 1, 0), 0)),
                  BS((DN_HALO, 3 * BR), lambda i: (jnp.minimum((i + 1) * per, nhb - 1), 0)),
                  BS((DN_TILE, 128), lambda i: (i, 0)),
                  BS((8, 3 * BR), lambda i: (0, 0)),
                  BS((8, 128), lambda i: (0, 0))],
        out_specs=[BS((DN_TILE, 3 * BR), lambda i: (i, 0)), BS((DN_TILE, 128), lambda i: (i, 0))],
        scratch_shapes=[pltpu.VMEM((DN_TILE + 2 * DN_HALO, 3 * BR), F32)],
        compiler_params=_cparams("parallel"),
        name="dn_conv",
    )(qkv, qkv, qkv, ba, conv_w, gate_p)


def _dn_scan_body(q_ref, k_ref, v_ref, b_ref, g_ref, s0_ref, o_ref, sfin_ref, s_ref, *, reverse):
    j = pl.program_id(1)

    @pl.when(j == 0)
    def _():
        s_ref[...] = s0_ref[0]

    row = lax.broadcasted_iota(jnp.int32, (CHUNK, CHUNK), 0)
    col = lax.broadcasted_iota(jnp.int32, (CHUNK, CHUNK), 1)
    if reverse:
        incl, strict, last = col >= row, col > row, 0
    else:
        incl, strict, last = col <= row, col < row, CHUNK - 1
    tinc = incl.astype(F32)
    eye = (row == col).astype(F32)
    for h in range(DN_HEADS):
        q = q_ref[h]
        k = k_ref[h]
        v = v_ref[h]
        beta = b_ref[h]
        q = q * lax.rsqrt(jnp.sum(q * q, axis=-1, keepdims=True) + EPS) * (DN_HD ** -0.5)
        k = k * lax.rsqrt(jnp.sum(k * k, axis=-1, keepdims=True) + EPS)
        gc = hdot(tinc, g_ref[h])
        diff = gc - gc.T
        decay = jnp.where(incl, jnp.exp(jnp.where(incl, diff, 0.0)), 0.0)
        a = jnp.where(strict, bdot_nt(k, k) * decay * beta, 0.0)
        tinv = eye - a
        pw = a
        for _ in range(5):
            pw = bdot(pw, pw)
            tinv = tinv + bdot(tinv, pw)
        egc = jnp.exp(gc)
        w = hdot(tinv, k * (beta * egc))
        u = hdot(tinv, v * beta)
        aqk = jnp.where(incl, bdot_nt(q, k) * decay, 0.0)
        g_last = gc[last:last + 1, :]
        s = s_ref[h]
        v_new = u - bdot(w, s)
        o_ref[h] = bdot(q * egc, s) + bdot(aqk, v_new)
        s_ref[h] = s * jnp.exp(g_last) + bdot_tn(k * jnp.exp(g_last - gc), v_new)

    @pl.when(j == pl.num_programs(1) - 1)
    def _():
        sfin_ref[0] = s_ref[...]


def dn_scan(q, k, v, beta_b, g_b, s0, chunk0, nseq, nc, reverse):
    def tok_map(s, j):
        jj = nc - 1 - j if reverse else j
        return (0, chunk0 + s * nc + jj, 0)

    def out_map(s, j):
        jj = nc - 1 - j if reverse else j
        return (0, s * nc + jj, 0)

    tok = BS((DN_HEADS, CHUNK, DN_HD), tok_map)
    st = BS((1, DN_HEADS, DN_HD, DN_HD), lambda s, j: (s, 0, 0, 0))
    return pl.pallas_call(
        functools.partial(_dn_scan_body, reverse=reverse),
        out_shape=[SDS((DN_HEADS, nseq * nc * CHUNK, DN_HD), F32), SDS((nseq, DN_HEADS, DN_HD, DN_HD), F32)],
        grid=(nseq, nc),
        in_specs=[tok, tok, tok, tok, tok, st],
        out_specs=[BS((DN_HEADS, CHUNK, DN_HD), out_map), st],
        scratch_shapes=[pltpu.VMEM((DN_HEADS, DN_HD, DN_HD), F32)],
        compiler_params=_cparams("parallel", "arbitrary"),
        name="dn_scan_b" if reverse else "dn_scan_f",
    )(q, k, v, beta_b, g_b, s0)


def _dn_norm_body(of_ref, ob_ref, g_ref, o_ref):
    o = of_ref[...] + ob_ref[...]
    o_ref[...] = o * lax.rsqrt(jnp.mean(o * o, axis=-1, keepdims=True) + EPS) * g_ref[...]


def dn_norm(o_f, o_b, norm_g):
    n = o_f.shape[1]
    t = 1024
    spec = BS((DN_HEADS, t, DN_HD), lambda i: (0, i, 0))
    return pl.pallas_call(
        _dn_norm_body,
        out_shape=SDS((DN_HEADS, n, DN_HD), F32),
        grid=(n // t,),
        in_specs=[spec, spec, BS((1, 1, DN_HD), lambda i: (0, 0, 0))],
        out_specs=spec,
        compiler_params=_cparams("parallel"),
        name="dn_norm",
    )(o_f, o_b, norm_g.reshape(1, 1, DN_HD))


def dn_branch(qkv, ba, conv_w, gate_p, norm_g, s0_lat):
    qkvc, gates = dn_conv(qkv, ba, conv_w, gate_p)
    heads = qkvc.reshape(N_TOK, 3 * DN_HEADS, DN_HD).transpose(1, 0, 2)
    q, k, v = heads[0:4], heads[4:8], heads[8:12]
    gt = jnp.broadcast_to(gates[:, 0:16].T[:, :, None], (16, N_TOK, DN_HD))
    s0_ctx = jnp.zeros((BATCH, DN_HEADS, DN_HD, DN_HD), F32)
    nc_ctx, nc_lat = SEQ // CHUNK, DEC_SEQ // CHUNK
    outs, fins = [], []
    for d in range(2):
        beta_b = gt[4 * d:4 * d + 4]
        g_b = gt[8 + 4 * d:12 + 4 * d]
        o_c, fin_c = dn_scan(q, k, v, beta_b, g_b, s0_ctx, 0, BATCH, nc_ctx, bool(d))
        o_l, _ = dn_scan(q, k, v, beta_b, g_b, s0_lat[:, d], N_CTX // CHUNK, DEC_BATCH, nc_lat, bool(d))
        outs.append(jnp.concatenate([o_c, o_l], axis=1))
        fins.append(fin_c)
    o = dn_norm(outs[0], outs[1], norm_g)
    return o.transpose(1, 0, 2).reshape(N_TOK, BR), jnp.stack(fins, axis=1)


def _outproj_body(x_ref, mod_ref, ypool_ref, dno_ref, dnz_ref, s5y_ref, s5_ref, yft_ref,
                  d_ref, gw_ref, gb_ref, w_ref, fg_ref, o_ref, *, final):
    gate = mod_ref[0][:, 2 * D_MODEL:3 * D_MODEL]
    y_dn = dno_ref[...] * silu(dnz_ref[...])
    y = s5y_ref[...] + d_ref[...] * s5_ref[:, 0:BR]
    y = jax.nn.gelu(y)
    y = y * jax.nn.sigmoid(bdot(y, gw_ref[...]) + gb_ref[...])
    y_s5 = y * silu(s5_ref[:, BR:2 * BR])
    acc = bdot(ypool_ref[...], w_ref[0:BR, :])
    acc = acc + bdot(y_dn, w_ref[BR:2 * BR, :])
    acc = acc + bdot(y_s5, w_ref[2 * BR:3 * BR, :])
    acc = acc + bdot(yft_ref[...], w_ref[3 * BR:4 * BR, :])
    xn = x_ref[...] + gate * acc
    if final:
        xn = xn * lax.rsqrt(jnp.mean(xn * xn, axis=-1, keepdims=True) + EPS) * fg_ref[...]
    o_ref[...] = xn


def outproj(x, mod, y_pool, dn_o, dn_z, s5_y, s5, y_ft, s5_d, glu_w, glu_b, w_out, final_g, final):
    row = lambda w: BS((TM, w), lambda i: (i, 0))
    full = lambda a, b: BS((a, b), lambda i: (0, 0))
    return pl.pallas_call(
        functools.partial(_outproj_body, final=final),
        out_shape=SDS((N_TOK, D_MODEL), F32),
        grid=(N_TOK // TM,),
        in_specs=[row(D_MODEL),
                  BS((1, 1, 3 * D_MODEL), lambda i: (_cond_index(i), 0, 0)),
                  row(BR), row(BR), row(BR), row(BR), row(2 * BR), row(BR),
                  full(1, BR), full(BR, BR), full(1, BR), full(D_MODEL, D_MODEL), full(1, D_MODEL)],
        out_specs=row(D_MODEL),
        compiler_params=_cparams("parallel"),
        name="outproj",
    )(x, mod, y_pool, dn_o, dn_z, s5_y, s5, y_ft, s5_d, glu_w, glu_b, w_out, final_g)


def _permute_w_in(w_in):
    main = jnp.concatenate([w_in[:, 0:1536], w_in[:, 1552:2576]], axis=1)
    ba = jnp.pad(w_in[:, 1536:1552], ((0, 0), (0, 112)))
    return jnp.concatenate([main, ba], axis=1).astype(BF16)


def kernel(x_prompt, x_sample, c, state_delta, state_s5, c_ctx, w_ada, b_ada, norm_g, w_in, pool_w, pool_scale,
           dn_conv, dn_a_log, dn_dt_bias, dn_norm_g, s5_a_re, s5_a_im, s5_log_dt, s5_b_re, s5_b_im, s5_c_re,
           s5_c_im, s5_d, s5_glu_w, s5_glu_b, ft_w, w_out, final_g):
    x = jnp.concatenate([x_prompt.astype(F32).reshape(N_CTX, D_MODEL),
                         x_sample.astype(F32).reshape(N_LAT, D_MODEL)], axis=0)
    cond8 = jnp.concatenate([c_ctx.astype(F32)[None], c.astype(F32),
                             jnp.zeros((8 - N_COND, D_MODEL), F32)], axis=0)
    ada = ada_all(cond8, w_ada, b_ada)
    pm_ctx, inv_ctx, pm_lat, inv_lat = _pool_constants()
    fpos, fch, g1, h2, fch2 = _ft_constants()
    new_dn, new_s5 = [], []
    for l in range(DEPTH):
        mod = ada[l, 0:N_COND].reshape(N_COND, 1, 3 * D_MODEL)
        pool, qkv, dn_z, s5, ft, ba = inproj(x, mod, norm_g[l].reshape(1, D_MODEL), _permute_w_in(w_in[l]))

        w_bd = jax.scipy.linalg.block_diag(*[pool_w[l, g] for g in range(4)]).astype(BF16)
        sc = pool_scale[l].reshape(1, BR)
        y_pool = jnp.concatenate([pool_branch(pool, pm_ctx, inv_ctx, w_bd, sc, False),
                                  pool_branch(pool, pm_lat, inv_lat, w_bd, sc, True)], axis=0)

        ftw = ft_w[l].astype(BF16)
        y_ft = jnp.concatenate([ft_ctx(ft, fpos, fch, ftw), ft_lat(ft, g1, h2, fch2, ftw)], axis=0)

        wcat, r, al = _s5_weights(s5_a_re[l], s5_a_im[l], s5_log_dt[l], s5_b_re[l], s5_b_im[l],
                                  s5_c_re[l], s5_c_im[l])
        s0_s5 = state_s5[:, l].astype(F32).transpose(0, 3, 2, 1, 4).reshape(DEC_BATCH * S5_G, 256)
        s5_y, fin_s5 = s5_branch(s5[:, 0:BR], wcat, r, al, s0_s5)
        new_s5.append(fin_s5.reshape(BATCH, S5_G, 2, 2, S5_N).transpose(0, 3, 2, 1, 4))

        conv_w = jnp.pad(dn_conv[l], ((0, 8 - CONV_K), (0, 0)))
        gate_p = jnp.zeros((8, 128), F32)
        gate_p = gate_p.at[0, 8:16].set(dn_a_log[l].reshape(8)).at[1, 8:16].set(dn_dt_bias[l].reshape(8))
        dn_o, fin_dn = dn_branch(qkv, ba, conv_w, gate_p, dn_norm_g[l], state_delta[:, l].astype(F32))
        new_dn.append(fin_dn)

        x = outproj(x, mod, y_pool, dn_o, dn_z, s5_y, s5, y_ft, s5_d[l].reshape(1, BR),
                    s5_glu_w[l].astype(BF16), s5_glu_b[l].reshape(1, BR), w_out[l].astype(BF16),
                    final_g.reshape(1, D_MODEL), l == DEPTH - 1)

    y_prompt = x[:N_CTX].reshape(BATCH, SEQ, D_MODEL).astype(x_prompt.dtype)
    y_sample = x[N_CTX:].reshape(DEC_BATCH, DEC_SEQ, D_MODEL).astype(x_sample.dtype)
    new_state_delta = jnp.stack(new_dn, axis=1).astype(state_delta.dtype)
    new_state_s5 = jnp.stack(new_s5, axis=1).astype(state_s5.dtype)
    return (y_prompt, y_sample, new_state_delta, new_state_s5)
```

```python
import functools
import math

import numpy as np
import jax
import jax.numpy as jnp
from jax import lax
from jax.experimental import pallas as pl
from jax.experimental.pallas import tpu as pltpu

F32 = jnp.float32
BF16 = jnp.bfloat16

D_MODEL = 1024
BATCH = 16
SEQ = 256
DEPTH = 4
DEC_BATCH = 2
DEC_SEQ = 4096
GRID_W = 64
GRID_H = DEC_SEQ // GRID_W
BR = 256
POOL_WINDOWS = (2, 4, 8, 16)
POOL_GD = 64
DN_HEADS = 4
DN_HD = 64
CONV_K = 5
CHUNK = 64
S5_P = 16
S5_G = 16
S5_N = 64
S5_L = 16
FT_HD = 64
EPS = 1e-6

N_CTX = BATCH * SEQ
N_LAT = DEC_BATCH * DEC_SEQ
N_TOK = N_CTX + N_LAT
N_COND = 1 + DEC_BATCH
TM = 512
W_IN_COLS = 2688
VMEM_LIMIT = 56 * 1024 * 1024

SDS = jax.ShapeDtypeStruct
BS = pl.BlockSpec


def _cparams(*sem):
    return pltpu.CompilerParams(dimension_semantics=sem, vmem_limit_bytes=VMEM_LIMIT)


def bdot(a, b):
    return jnp.dot(a.astype(BF16), b.astype(BF16), preferred_element_type=F32)


def hdot(a, b):
    return jnp.dot(a, b, preferred_element_type=F32, precision=lax.Precision.HIGHEST)


def silu(x):
    return x * jax.nn.sigmoid(x)


def _cond_index(i):
    tiles_ctx = N_CTX // TM
    tiles_seq = DEC_SEQ // TM
    return jnp.where(i < tiles_ctx, 0, 1 + (i - tiles_ctx) // tiles_seq)


def _ada_body(c_ref, w_ref, b_ref, o_ref):
    o_ref[0] = hdot(silu(c_ref[...]), w_ref[0]) + b_ref[0]


def ada_all(cond8, w_ada, b_ada):
    tn = 512
    return pl.pallas_call(
        _ada_body,
        out_shape=SDS((DEPTH, 8, 3 * D_MODEL), F32),
        grid=(DEPTH, 3 * D_MODEL // tn),
        in_specs=[BS((8, D_MODEL), lambda l, j: (0, 0)),
                  BS((1, D_MODEL, tn), lambda l, j: (l, 0, j)),
                  BS((1, 1, tn), lambda l, j: (l, 0, j))],
        out_specs=BS((1, 8, tn), lambda l, j: (l, 0, j)),
        compiler_params=_cparams("parallel", "parallel"),
        name="ada",
    )(cond8, w_ada, b_ada.reshape(DEPTH, 1, 3 * D_MODEL))


def _inproj_body(x_ref, mod_ref, g_ref, w_ref, pool_ref, qkv_ref, dnz_ref, s5_ref, ft_ref, ba_ref):
    x = x_ref[...]
    m = mod_ref[0]
    shift = m[:, 0:D_MODEL]
    scale = m[:, D_MODEL:2 * D_MODEL]
    xn = x * lax.rsqrt(jnp.mean(x * x, axis=-1, keepdims=True) + EPS) * g_ref[...]
    h = (xn * (1.0 + scale) + shift).astype(BF16)

    def proj(lo, hi):
        return jnp.dot(h, w_ref[:, lo:hi], preferred_element_type=F32)

    pool_ref[...] = proj(0, 512)
    qkv_ref[...] = proj(512, 1280)
    dnz_ref[...] = proj(1280, 1536)
    s5_ref[...] = proj(1536, 2048)
    ft_ref[...] = proj(2048, 2560)
    ba_ref[...] = proj(2560, 2688)


def inproj(x, mod, norm_g, w_in_p):
    widths = (512, 768, 256, 512, 512, 128)
    return pl.pallas_call(
        _inproj_body,
        out_shape=[SDS((N_TOK, w), F32) for w in widths],
        grid=(N_TOK // TM,),
        in_specs=[BS((TM, D_MODEL), lambda i: (i, 0)),
                  BS((1, 1, 3 * D_MODEL), lambda i: (_cond_index(i), 0, 0)),
                  BS((1, D_MODEL), lambda i: (0, 0)),
                  BS((D_MODEL, W_IN_COLS), lambda i: (0, 0))],
        out_specs=[BS((TM, w), lambda i: (i, 0)) for w in widths],
        compiler_params=_cparams("parallel"),
        name="inproj",
    )(x, mod, norm_g, w_in_p)


def _pool_body(u_ref, z_ref, pm_ref, inv_ref, w_ref, sc_ref, o_ref, *scratch, two_d):
    nblk = u_ref.shape[0] // 256
    if two_d:
        pad_ref, v_ref = scratch
        halo = 8 * GRID_W
        pad_ref[0:halo, :] = jnp.zeros((halo, BR), F32)
        pad_ref[halo + DEC_SEQ:2 * halo + DEC_SEQ, :] = jnp.zeros((halo, BR), F32)
        pad_ref[halo:halo + DEC_SEQ, :] = u_ref[...]
        lane = lax.broadcasted_iota(jnp.int32, (GRID_W, 128), 1)

        def row_body(r, c):
            base = pl.multiple_of(r * GRID_W, GRID_W)

            def slab(d, lo):
                return pad_ref[pl.ds(base + (8 + d) * GRID_W, GRID_W), lo:lo + 128]

            s2 = slab(-1, 0) + slab(0, 0)
            s4 = s2 + slab(-2, 0) + slab(1, 0)
            v_ref[pl.ds(base, GRID_W), 0:128] = jnp.where(lane < 64, s2, s4)
            s8 = slab(-4, 128)
            for d in (-3, -2, -1, 0, 1, 2, 3):
                s8 = s8 + slab(d, 128)
            s16 = s8
            for d in (-8, -7, -6, -5, 4, 5, 6, 7):
                s16 = s16 + slab(d, 128)
            v_ref[pl.ds(base, GRID_W), 128:256] = jnp.where(lane < 64, s8, s16)
            return c

        lax.fori_loop(0, GRID_H, row_body, 0)
        src = v_ref
    else:
        src = u_ref
    grp = lax.broadcasted_iota(jnp.int32, (256, BR), 1) // POOL_GD

    def blk_body(b, c):
        r0 = pl.multiple_of(b * 256, 256)
        vb = src[pl.ds(r0, 256), :]
        hi = vb.astype(BF16)
        lo = (vb - hi.astype(F32)).astype(BF16)
        res = jnp.zeros((256, BR), F32)
        for g in range(len(POOL_WINDOWS)):
            pg = (jnp.dot(pm_ref[g], hi, preferred_element_type=F32)
                  + jnp.dot(pm_ref[g], lo, preferred_element_type=F32))
            res = jnp.where(grp == g, pg, res)
        pooled = res * inv_ref[pl.ds(r0, 256), :]
        d = pooled - u_ref[pl.ds(r0, 256), :]
        y = bdot(d, w_ref[...]) * sc_ref[...]
        o_ref[pl.ds(r0, 256), :] = y * silu(z_ref[pl.ds(r0, 256), :])
        return c

    lax.fori_loop(0, nblk, blk_body, 0)


def _band_matrices(seg):
    t = np.arange(256)
    out = []
    for w in POOL_WINDOWS:
        lo = t - w // 2
        hi = t - w // 2 + w
        s = t[None, :]
        m = (s >= lo[:, None]) & (s < hi[:, None]) & ((s // seg) == (t[:, None] // seg))
        out.append(m.astype(np.float32))
    return np.stack(out)


def _counts(length, w):
    pos = np.arange(length)
    return (np.clip(pos - w // 2 + w, 0, length) - np.clip(pos - w // 2, 0, length)).astype(np.float64)


def _pool_constants():
    inv_ctx = np.concatenate([np.repeat((1.0 / _counts(SEQ, w))[:, None], POOL_GD, 1) for w in POOL_WINDOWS], 1)
    inv_lat = []
    for w in POOL_WINDOWS:
        c2 = np.outer(_counts(GRID_H, w), _counts(GRID_W, w)).reshape(DEC_SEQ)
        inv_lat.append(np.repeat((1.0 / c2)[:, None], POOL_GD, 1))
    inv_lat = np.concatenate(inv_lat, 1)
    return (jnp.asarray(_band_matrices(SEQ), BF16), jnp.asarray(inv_ctx, F32),
            jnp.asarray(_band_matrices(GRID_W), BF16), jnp.asarray(inv_lat, F32))


def pool_branch(pool, pm, inv, w_bd, scale, two_d):
    if two_d:
        rows, nseq, blk0 = DEC_SEQ, DEC_BATCH, N_CTX // DEC_SEQ
        scratch = [pltpu.VMEM((DEC_SEQ + 16 * GRID_W, BR), F32), pltpu.VMEM((DEC_SEQ, BR), F32)]
    else:
        rows, nseq, blk0 = SEQ, BATCH, 0
        scratch = []
    return pl.pallas_call(
        functools.partial(_pool_body, two_d=two_d),
        out_shape=SDS((nseq * rows, BR), F32),
        grid=(nseq,),
        in_specs=[BS((rows, BR), lambda i: (blk0 + i, 0)),
                  BS((rows, BR), lambda i: (blk0 + i, 1)),
                  BS((4, 256, 256), lambda i: (0, 0, 0)),
                  BS((rows, BR), lambda i: (0, 0)),
                  BS((BR, BR), lambda i: (0, 0)),
                  BS((1, BR), lambda i: (0, 0))],
        out_specs=BS((rows, BR), lambda i: (i, 0)),
        scratch_shapes=scratch,
        compiler_params=_cparams("parallel"),
        name="pool2d" if two_d else "pool1d",
    )(pool, pool, pm, inv, w_bd, scale)


def _ft_ctx_body(u_ref, z_ref, fpos_ref, fch_ref, w_ref, o_ref):
    uc = bdot(u_ref[...], fch_ref[...])
    st = jnp.concatenate([uc[:, 0:BR], uc[:, BR:2 * BR]], axis=0)
    f = bdot(fpos_ref[...], st)
    o_ref[...] = bdot(f, w_ref[...]) * silu(z_ref[...])


def ft_ctx(ft, fpos, fch, ft_w):
    return pl.pallas_call(
        _ft_ctx_body,
        out_shape=SDS((N_CTX, BR), F32),
        grid=(BATCH,),
        in_specs=[BS((SEQ, BR), lambda i: (i, 0)),
                  BS((SEQ, BR), lambda i: (i, 1)),
                  BS((SEQ, 2 * SEQ), lambda i: (0, 0)),
                  BS((BR, 2 * BR), lambda i: (0, 0)),
                  BS((BR, BR), lambda i: (0, 0))],
        out_specs=BS((SEQ, BR), lambda i: (i, 0)),
        compiler_params=_cparams("parallel"),
        name="ft_ctx",
    )(ft, ft, fpos, fch, ft_w)


def _ft_lat_body(u_ref, z_ref, g_ref, h_ref, fch_ref, w_ref, o_ref, x_ref, yr_ref, yi_ref):
    for hf in range(2):
        x_ref[hf] = u_ref[:, hf * 128:(hf + 1) * 128]

    def stage1(t2, c):
        xs = jnp.concatenate([x_ref[hf, pl.ds(t2, GRID_H, stride=GRID_W), :] for hf in range(2)], axis=1)
        y = jnp.dot(g_ref[t2], xs.astype(BF16), preferred_element_type=F32)
        r0 = pl.multiple_of(t2 * GRID_W, GRID_W)
        for hf in range(2):
            yr_ref[hf, pl.ds(r0, GRID_W), :] = y[0:64, hf * 128:(hf + 1) * 128]
            yi_ref[hf, pl.ds(r0, GRID_W), :] = y[64:128, hf * 128:(hf + 1) * 128]
        return c

    lax.fori_loop(0, GRID_W, stage1, 0)

    def stage2(kb, c):
        yr = jnp.concatenate([yr_ref[hf, pl.ds(kb, GRID_W, stride=GRID_W), :] for hf in range(2)], axis=1)
        yi = jnp.concatenate([yi_ref[hf, pl.ds(kb, GRID_W, stride=GRID_W), :] for hf in range(2)], axis=1)
        st = jnp.concatenate([yr, yi], axis=0).astype(BF16)
        a = jnp.dot(h_ref[...], st, preferred_element_type=F32)
        for hf in range(2):
            yr_ref[hf, pl.ds(kb, GRID_W, stride=GRID_W), :] = a[0:64, hf * 128:(hf + 1) * 128]
            yi_ref[hf, pl.ds(kb, GRID_W, stride=GRID_W), :] = a[64:128, hf * 128:(hf + 1) * 128]
        return c

    lax.fori_loop(0, GRID_W, stage2, 0)

    def stage3(b, c):
        r0 = pl.multiple_of(b * TM, TM)
        ar = jnp.concatenate([yr_ref[hf, pl.ds(r0, TM), :] for hf in range(2)], axis=1)
        ai = jnp.concatenate([yi_ref[hf, pl.ds(r0, TM), :] for hf in range(2)], axis=1)
        f = bdot(ar, fch_ref[0:BR, :]) + bdot(ai, fch_ref[BR:2 * BR, :])
        o_ref[pl.ds(r0, TM), :] = bdot(f, w_ref[...]) * silu(z_ref[pl.ds(r0, TM), :])
        return c

    lax.fori_loop(0, DEC_SEQ // TM, stage3, 0)


def ft_lat(ft, g1, h2, fch2, ft_w):
    blk0 = N_CTX // DEC_SEQ
    return pl.pallas_call(
        _ft_lat_body,
        out_shape=SDS((N_LAT, BR), F32),
        grid=(DEC_BATCH,),
        in_specs=[BS((DEC_SEQ, BR), lambda i: (blk0 + i, 0)),
                  BS((DEC_SEQ, BR), lambda i: (blk0 + i, 1)),
                  BS((GRID_W, 128, GRID_H), lambda i: (0, 0, 0)),
                  BS((128, 128), lambda i: (0, 0)),
                  BS((2 * BR, BR), lambda i: (0, 0)),
                  BS((BR, BR), lambda i: (0, 0))],
        out_specs=BS((DEC_SEQ, BR), lambda i: (i, 0)),
        scratch_shapes=[pltpu.VMEM((2, DEC_SEQ, 128), F32)] * 3,
        compiler_params=_cparams("parallel"),
        name="ft_lat",
    )(ft, ft, g1, h2, fch2, ft_w)


def _ft_constants():
    c = np.arange(FT_HD)
    ang = 2.0 * np.pi * np.outer(c, c) / FT_HD
    eye4 = np.eye(BR // FT_HD)
    cc = np.kron(eye4, np.cos(ang)) / 8.0
    sc = np.kron(eye4, np.sin(ang)) / 8.0
    t = np.arange(SEQ)
    angt = 2.0 * np.pi * (np.outer(t, t) % SEQ) / SEQ
    fpos = np.concatenate([np.cos(angt), -np.sin(angt)], axis=1) / 16.0
    fch = np.concatenate([cc, sc], axis=1)
    kb = np.arange(GRID_W)[None, :, None]
    t1 = np.arange(GRID_H)[None, None, :]
    t2 = np.arange(GRID_W)[:, None, None]
    a1 = 2.0 * np.pi * ((kb * (GRID_W * t1 + t2)) % DEC_SEQ) / DEC_SEQ
    g1 = np.concatenate([np.cos(a1), -np.sin(a1)], axis=1) / 8.0
    a2 = 2.0 * np.pi * (np.outer(np.arange(GRID_W), np.arange(GRID_W)) % GRID_W) / GRID_W
    c2, s2 = np.cos(a2) / 8.0, np.sin(a2) / 8.0
    h2 = np.block([[c2, s2], [-s2, c2]])
    fch2 = np.concatenate([cc, sc], axis=0)
    as_bf = lambda a: jnp.asarray(a, F32).astype(BF16)
    return as_bf(fpos), as_bf(fch), as_bf(g1), as_bf(h2), as_bf(fch2)


def _s5_intra_body(u_ref, w_ref, y_ref, e_ref):
    v = jnp.dot(u_ref[0], w_ref[0], preferred_element_type=F32)
    y_ref[0] = v[:, 0:256]
    e_ref[0] = v[:, 256:512]


def s5_intra(ug, wcat):
    rows = ug.shape[1]
    return pl.pallas_call(
        _s5_intra_body,
        out_shape=[SDS((S5_G, rows, 256), F32), SDS((S5_G, rows, 256), F32)],
        grid=(S5_G,),
        in_specs=[BS((1, rows, 256), lambda g: (g, 0, 0)), BS((1, 256, 512), lambda g: (g, 0, 0))],
        out_specs=[BS((1, rows, 256), lambda g: (g, 0, 0)), BS((1, rows, 256), lambda g: (g, 0, 0))],
        compiler_params=_cparams("parallel"),
        name="s5_intra",
    )(ug, wcat)


def _s5_scan_body(e_ref, al_ref, s0_ref, sp_ref, fin_ref):
    nc = e_ref.shape[0]
    a_re = al_ref[:, 0:128]
    a_im = al_ref[:, 128:256]

    def step(c, carry):
        s_re, s_im = carry
        sp_ref[c, :, 0:128] = s_re
        sp_ref[c, :, 128:256] = s_im
        n_re = a_re * s_re - a_im * s_im + e_ref[c, :, 0:128]
        n_im = a_re * s_im + a_im * s_re + e_ref[c, :, 128:256]
        return n_re, n_im

    s_re, s_im = lax.fori_loop(0, nc, step, (s0_ref[:, 0:128], s0_ref[:, 128:256]))
    fin_ref[:, 0:128] = s_re
    fin_ref[:, 128:256] = s_im


def s5_scan(e_scan, al, s0):
    nc, rows, _ = e_scan.shape
    return pl.pallas_call(
        _s5_scan_body,
        out_shape=[SDS((nc, rows, 256), F32), SDS((rows, 256), F32)],
        compiler_params=pltpu.CompilerParams(vmem_limit_bytes=VMEM_LIMIT),
        name="s5_scan",
    )(e_scan, al, s0)


def _s5_out_body(y_ref, s_ref, r_ref, o_ref):
    o_ref[0] = y_ref[0] + jnp.dot(s_ref[0], r_ref[0], preferred_element_type=F32)


def s5_out(yi, sp, r):
    rows = yi.shape[1]
    spec = BS((1, rows, 256), lambda g: (g, 0, 0))
    return pl.pallas_call(
        _s5_out_body,
        out_shape=SDS((S5_G, rows, 256), F32),
        grid=(S5_G,),
        in_specs=[spec, spec, BS((1, 256, 256), lambda g: (g, 0, 0))],
        out_specs=spec,
        compiler_params=_cparams("parallel"),
        name="s5_out",
    )(yi, sp, r)


def _s5_weights(a_re, a_im, log_dt, b_re, b_im, c_re, c_im):
    hi = lax.Precision.HIGHEST
    L = S5_L
    dt = jnp.exp(log_dt)[..., None]
    xr, xi = a_re * dt, a_im * dt
    mag = jnp.exp(xr)
    ab_re, ab_im = mag * jnp.cos(xi), mag * jnp.sin(xi)
    den = a_re * a_re + a_im * a_im
    nr = ab_re - 1.0
    coef_re = (nr * a_re + ab_im * a_im) / den
    coef_im = (ab_im * a_re - nr * a_im) / den
    bb_re = coef_re[..., None] * b_re - coef_im[..., None] * b_im
    bb_im = coef_re[..., None] * b_im + coef_im[..., None] * b_re
    m = jnp.arange(L + 1, dtype=F32)[:, None, None, None]
    pw_re = jnp.exp(m * xr) * jnp.cos(m * xi)
    pw_im = jnp.exp(m * xr) * jnp.sin(m * xi)
    cp_re = c_re[None] * pw_re[:, :, :, None, :] - c_im[None] * pw_im[:, :, :, None, :]
    cp_im = c_re[None] * pw_im[:, :, :, None, :] + c_im[None] * pw_re[:, :, :, None, :]
    k = (jnp.einsum('mdgqn,dgnp->mdgpq', cp_re, bb_re, precision=hi)
         - jnp.einsum('mdgqn,dgnp->mdgpq', cp_im, bb_im, precision=hi))
    ii = np.arange(L)[:, None]
    jj = np.arange(L)[None, :]
    mf = jnp.where((jj >= ii)[:, :, None, None, None], k[:, 0][np.clip(jj - ii, 0, L)], 0.0)
    mb = jnp.where((ii >= jj)[:, :, None, None, None], k[:, 1][np.clip(ii - jj, 0, L)], 0.0)
    mt = (mf + mb).transpose(2, 0, 3, 1, 4).reshape(S5_G, L * S5_P, L * S5_P)

    def inject(pr, pi, d):
        q_re = pr[..., None] * bb_re[d][None] - pi[..., None] * bb_im[d][None]
        q_im = pr[..., None] * bb_im[d][None] + pi[..., None] * bb_re[d][None]
        f = lambda q: q.transpose(1, 0, 3, 2).reshape(S5_G, L * S5_P, S5_N)
        return f(q_re), f(q_im)

    qf_re, qf_im = inject(pw_re[:L, 0][::-1], pw_im[:L, 0][::-1], 0)
    qb_re, qb_im = inject(pw_re[:L, 1], pw_im[:L, 1], 1)
    wcat = jnp.concatenate([mt, qf_re, qb_re, qf_im, qb_im], axis=-1)

    def readout(pr, pi, d):
        d_re = c_re[d][None] * pr[:, :, None, :] - c_im[d][None] * pi[:, :, None, :]
        d_im = c_re[d][None] * pi[:, :, None, :] + c_im[d][None] * pr[:, :, None, :]
        f = lambda x: x.transpose(1, 3, 0, 2).reshape(S5_G, S5_N, L * S5_P)
        return f(d_re), f(-d_im)

    rf_re, rf_im = readout(pw_re[1:L + 1, 0], pw_im[1:L + 1, 0], 0)
    rb_re, rb_im = readout(pw_re[1:L + 1, 1][::-1], pw_im[1:L + 1, 1][::-1], 1)
    r = jnp.concatenate([rf_re, rb_re, rf_im, rb_im], axis=1)
    al = jnp.concatenate([pw_re[L, 0], pw_re[L, 1], pw_im[L, 0], pw_im[L, 1]], axis=-1)
    return wcat.astype(BF16), r.astype(BF16), al


_BWD_LANES = np.concatenate([np.zeros(64, bool), np.ones(64, bool), np.zeros(64, bool), np.ones(64, bool)])


def _flip_bwd(x):
    return jnp.where(_BWD_LANES, jnp.flip(x, axis=0), x)


def s5_branch(s5_u, wcat, r, al, s0_lat):
    L = S5_L
    rows = N_TOK // L
    ug = s5_u.reshape(rows, L, S5_G, S5_P).transpose(2, 0, 1, 3).reshape(S5_G, rows, L * S5_P).astype(BF16)
    yi, e = s5_intra(ug, wcat)
    rc, ncc = N_CTX // L, SEQ // L
    ncl = DEC_SEQ // L
    e_ctx = e[:, :rc].reshape(S5_G, BATCH, ncc, 256).transpose(2, 1, 0, 3).reshape(ncc, BATCH * S5_G, 256)
    e_lat = e[:, rc:].reshape(S5_G, DEC_BATCH, ncl, 256).transpose(2, 1, 0, 3).reshape(ncl, DEC_BATCH * S5_G, 256)
    sp_ctx, fin_ctx = s5_scan(_flip_bwd(e_ctx), jnp.tile(al, (BATCH, 1)), jnp.zeros((BATCH * S5_G, 256), F32))
    sp_lat, _ = s5_scan(_flip_bwd(e_lat), jnp.tile(al, (DEC_BATCH, 1)), s0_lat)
    sp_ctx = _flip_bwd(sp_ctx).reshape(ncc, BATCH, S5_G, 256).transpose(2, 1, 0, 3).reshape(S5_G, rc, 256)
    sp_lat = _flip_bwd(sp_lat).reshape(ncl, DEC_BATCH, S5_G, 256).transpose(2, 1, 0, 3).reshape(S5_G, rows - rc, 256)
    sp = jnp.concatenate([sp_ctx, sp_lat], axis=1).astype(BF16)
    y = s5_out(yi, sp, r)
    y = y.reshape(S5_G, rows, L, S5_P).transpose(1, 2, 0, 3).reshape(N_TOK, BR)
    return y, fin_ctx


DN_TILE = 256
DN_HALO = 8


N_SEG = N_TOK // DN_TILE
N_SEQ = BATCH + DEC_BATCH


def _dn_conv_body(x_ref, prev_ref, next_ref, ba_ref, w_ref, gp_ref, heads_ref, gb_ref, pad_ref):
    i = pl.program_id(0)
    tiles_ctx = N_CTX // DN_TILE
    tiles_seq = DEC_SEQ // DN_TILE
    j = (i - tiles_ctx) % tiles_seq
    first = jnp.logical_or(i < tiles_ctx, j == 0)
    last = jnp.logical_or(i < tiles_ctx, j == tiles_seq - 1)
    pad_ref[0:DN_HALO, :] = jnp.where(first, 0.0, prev_ref[...])
    pad_ref[DN_HALO:DN_HALO + DN_TILE, :] = x_ref[...]
    pad_ref[DN_HALO + DN_TILE:2 * DN_HALO + DN_TILE, :] = jnp.where(last, 0.0, next_ref[...])
    acc = jnp.zeros((DN_TILE, 3 * BR), F32)
    for t in range(CONV_K):
        acc = acc + pad_ref[pl.ds(DN_HALO - CONV_K // 2 + t, DN_TILE), :] * w_ref[t:t + 1, :]
    y = silu(acc)
    for h in range(3 * DN_HEADS):
        heads_ref[h] = y[:, h * DN_HD:(h + 1) * DN_HD]
    raw = ba_ref[...]
    lane = lax.broadcasted_iota(jnp.int32, raw.shape, 1)
    xa = raw + gp_ref[1:2, :]
    sp = jnp.maximum(xa, 0.0) + jnp.log1p(jnp.exp(-jnp.abs(xa)))
    gates = jnp.where(lane < 2 * DN_HEADS, jax.nn.sigmoid(raw), -jnp.exp(gp_ref[0:1, :]) * sp)
    r = lax.broadcasted_iota(jnp.int32, (DN_TILE, DN_TILE), 0)
    c = lax.broadcasted_iota(jnp.int32, (DN_TILE, DN_TILE), 1)
    same = (r // CHUNK) == (c // CHUNK)
    gc_f = hdot(jnp.logical_and(same, c <= r).astype(F32), gates)
    gc_b = hdot(jnp.logical_and(same, c >= r).astype(F32), gates)
    for d in range(2):
        for h in range(DN_HEADS):
            n = d * DN_HEADS + h
            gb_ref[n] = jnp.broadcast_to(gates[:, n:n + 1], (DN_TILE, DN_HD))
            gc = gc_f if d == 0 else gc_b
            gb_ref[2 * DN_HEADS + n] = jnp.broadcast_to(gc[:, 2 * DN_HEADS + n:2 * DN_HEADS + n + 1], (DN_TILE, DN_HD))


def dn_conv(qkv, ba, conv_w, gate_p):
    nt = N_TOK // DN_TILE
    per = DN_TILE // DN_HALO
    nhb = N_TOK // DN_HALO
    return pl.pallas_call(
        _dn_conv_body,
        out_shape=[SDS((3 * DN_HEADS, N_TOK, DN_HD), F32), SDS((4 * DN_HEADS, N_TOK, DN_HD), F32)],
        grid=(nt,),
        in_specs=[BS((DN_TILE, 3 * BR), lambda i: (i, 0)),
                  BS((DN_HALO, 3 * BR), lambda i: (jnp.maximum(i * per - 1, 0), 0)),
                  BS((DN_HALO, 3 * BR), lambda i: (jnp.minimum((i + 1) * per, nhb - 1), 0)),
                  BS((DN_TILE, 128), lambda i: (i, 0)),
                  BS((8, 3 * BR), lambda i: (0, 0)),
                  BS((8, 128), lambda i: (0, 0))],
        out_specs=[BS((3 * DN_HEADS, DN_TILE, DN_HD), lambda i: (0, i, 0)),
                   BS((4 * DN_HEADS, DN_TILE, DN_HD), lambda i: (0, i, 0))],
        scratch_shapes=[pltpu.VMEM((DN_TILE + 2 * DN_HALO, 3 * BR), F32)],
        compiler_params=_cparams("parallel"),
        name="dn_conv",
    )(qkv, qkv, qkv, ba, conv_w, gate_p)


def bmm(a, b):
    return jnp.einsum('bij,bjk->bik', a.astype(BF16), b.astype(BF16), preferred_element_type=F32)


def bmm_nt(a, b):
    return jnp.einsum('bik,bjk->bij', a.astype(BF16), b.astype(BF16), preferred_element_type=F32)


def _split_bf16(x):
    hi = x.astype(BF16)
    return hi, (x - hi.astype(F32)).astype(BF16)


def bmm3(a, b):
    ah, al = _split_bf16(a)
    bh, bl = _split_bf16(b)
    return bmm(ah, bh) + bmm(ah, bl) + bmm(al, bh)


def _dn_local_body(q_ref, k_ref, v_ref, b_ref, gc_ref, w_ref, u_ref, qt_ref, kt_ref, aqk_ref, eg_ref, *, reverse):
    ncb = DN_TILE // CHUNK
    sh = (DN_HEADS * ncb, CHUNK, DN_HD)
    ld = lambda ref: ref[...].reshape(sh)
    q, k, v, beta, gc = ld(q_ref), ld(k_ref), ld(v_ref), ld(b_ref), ld(gc_ref)
    row = lax.broadcasted_iota(jnp.int32, (CHUNK, CHUNK), 0)
    col = lax.broadcasted_iota(jnp.int32, (CHUNK, CHUNK), 1)
    if reverse:
        incl, strict, last = col >= row, col > row, 0
    else:
        incl, strict, last = col <= row, col < row, CHUNK - 1
    eye = (row == col).astype(F32)
    q = q * lax.rsqrt(jnp.sum(q * q, axis=-1, keepdims=True) + EPS) * (DN_HD ** -0.5)
    k = k * lax.rsqrt(jnp.sum(k * k, axis=-1, keepdims=True) + EPS)
    diff = gc - jnp.swapaxes(gc, 1, 2)
    decay = jnp.where(incl, jnp.exp(jnp.where(incl, diff, 0.0)), 0.0)
    a = jnp.where(strict, bmm_nt(k, k) * decay * beta, 0.0)
    tinv = eye - a
    pw = a
    for _ in range(5):
        pw = bmm(pw, pw)
        tinv = tinv + bmm(tinv, pw)
    egc = jnp.exp(gc)
    w = bmm3(tinv, k * (beta * egc))
    u = bmm3(tinv, v * beta)
    aqk = jnp.where(incl, bmm_nt(q, k) * decay, 0.0)
    g_last = gc[:, last:last + 1, :]
    kt = k * jnp.exp(g_last - gc)
    outs = (w, u, q * egc, jnp.swapaxes(kt, 1, 2), aqk, jnp.broadcast_to(jnp.exp(g_last), sh))
    for ref, x in zip((w_ref, u_ref, qt_ref, kt_ref, aqk_ref, eg_ref), outs):
        ref[...] = x.reshape(DN_HEADS, DN_TILE, DN_HD).astype(ref.dtype)


def dn_local(heads, gb, d):
    blk = lambda r: BS((DN_HEADS, DN_TILE, DN_HD), lambda i: (r, i, 0))
    dts = (BF16, F32, BF16, BF16, BF16, F32)
    return pl.pallas_call(
        functools.partial(_dn_local_body, reverse=bool(d)),
        out_shape=[SDS((DN_HEADS, N_TOK, DN_HD), dt) for dt in dts],
        grid=(N_SEG,),
        in_specs=[blk(0), blk(1), blk(2), blk(d), blk(2 + d)],
        out_specs=[blk(0)] * len(dts),
        compiler_params=_cparams("parallel"),
        name="dn_local_b" if d else "dn_local_f",
    )(heads, heads, heads, gb, gb)


def _seg_seq(g):
    nctx = N_CTX // DN_TILE
    per = DEC_SEQ // DN_TILE
    return jnp.where(g < nctx, g, nctx + (g - nctx) // per)


def _seg_is_first(g):
    nctx = N_CTX // DN_TILE
    per = DEC_SEQ // DN_TILE
    return jnp.logical_or(g < nctx, (g - nctx) % per == 0)


def _seg_is_last(g):
    nctx = N_CTX // DN_TILE
    per = DEC_SEQ // DN_TILE
    return jnp.logical_or(g < nctx, (g - nctx) % per == per - 1)


def _dn_seq_body(*refs):
    f_in, b_in = refs[0:6], refs[6:12]
    s0f_ref, s0b_ref, of_ref, ob_ref, finf_ref, finb_ref, s_ref = refs[12:19]
    g = pl.program_id(0)
    gb = N_SEG - 1 - g

    @pl.when(_seg_is_first(g))
    def _():
        s_ref[0] = s0f_ref[0]

    @pl.when(_seg_is_last(gb))
    def _():
        s_ref[1] = s0b_ref[0]

    ncb = DN_TILE // CHUNK
    ld = lambda ref: ref[...].reshape(DN_HEADS, ncb, CHUNK, DN_HD)
    fw = [ld(r) for r in f_in]
    bw = [ld(r) for r in b_in]

    def step(s, arrs, c):
        w, u, qt, kt, aqk, eg = [a[:, c] for a in arrs]
        v_new = u - bmm(w, s)
        o = bmm(qt, s) + bmm(aqk, v_new)
        return s * eg + bmm(kt, v_new), o

    s_f = s_ref[0]
    s_b = s_ref[1]
    for t in range(ncb):
        cb = ncb - 1 - t
        s_f, o_f = step(s_f, fw, t)
        s_b, o_b = step(s_b, bw, cb)
        of_ref[:, t * CHUNK:(t + 1) * CHUNK, :] = o_f
        ob_ref[:, cb * CHUNK:(cb + 1) * CHUNK, :] = o_b
    s_ref[0] = s_f
    s_ref[1] = s_b
    finf_ref[0] = s_f
    finb_ref[0] = s_b


def dn_seq(loc_f, loc_b, s0_f, s0_b):
    fwd = BS((DN_HEADS, DN_TILE, DN_HD), lambda g: (0, g, 0))
    bwd = BS((DN_HEADS, DN_TILE, DN_HD), lambda g: (0, N_SEG - 1 - g, 0))
    st_f = BS((1, DN_HEADS, DN_HD, DN_HD), lambda g: (_seg_seq(g), 0, 0, 0))
    st_b = BS((1, DN_HEADS, DN_HD, DN_HD), lambda g: (_seg_seq(N_SEG - 1 - g), 0, 0, 0))
    st_shape = SDS((N_SEQ, DN_HEADS, DN_HD, DN_HD), F32)
    return pl.pallas_call(
        _dn_seq_body,
        out_shape=[SDS((DN_HEADS, N_TOK, DN_HD), F32), SDS((DN_HEADS, N_TOK, DN_HD), F32), st_shape, st_shape],
        grid=(N_SEG,),
        in_specs=[fwd] * 6 + [bwd] * 6 + [st_f, st_b],
        out_specs=[fwd, bwd, st_f, st_b],
        scratch_shapes=[pltpu.VMEM((2, DN_HEADS, DN_HD, DN_HD), F32)],
        compiler_params=_cparams("arbitrary"),
        name="dn_seq",
    )(*loc_f, *loc_b, s0_f, s0_b)


def _dn_norm_body(of_ref, ob_ref, g_ref, o_ref):
    outs = []
    for h in range(DN_HEADS):
        o = of_ref[h] + ob_ref[h]
        outs.append(o * lax.rsqrt(jnp.mean(o * o, axis=-1, keepdims=True) + EPS) * g_ref[...])
    o_ref[...] = jnp.concatenate(outs, axis=-1)


def dn_norm(o_f, o_b, norm_g):
    t = 1024
    spec = BS((DN_HEADS, t, DN_HD), lambda i: (0, i, 0))
    return pl.pallas_call(
        _dn_norm_body,
        out_shape=SDS((N_TOK, BR), F32),
        grid=(N_TOK // t,),
        in_specs=[spec, spec, BS((1, DN_HD), lambda i: (0, 0))],
        out_specs=BS((t, BR), lambda i: (i, 0)),
        compiler_params=_cparams("parallel"),
        name="dn_norm",
    )(o_f, o_b, norm_g.reshape(1, DN_HD))


def dn_branch(qkv, ba, conv_w, gate_p, norm_g, s0_lat):
    heads, gb = dn_conv(qkv, ba, conv_w, gate_p)
    s0_ctx = jnp.zeros((BATCH, DN_HEADS, DN_HD, DN_HD), F32)
    s0 = [jnp.concatenate([s0_ctx, s0_lat[:, d]], axis=0) for d in range(2)]
    o_f, o_b, fin_f, fin_b = dn_seq(dn_local(heads, gb, 0), dn_local(heads, gb, 1), s0[0], s0[1])
    fin = jnp.stack([fin_f[:BATCH], fin_b[:BATCH]], axis=1)
    return dn_norm(o_f, o_b, norm_g), fin


def _outproj_body(x_ref, mod_ref, ypc_ref, ypl_ref, dno_ref, dnz_ref, s5y_ref, s5_ref, yfc_ref, yfl_ref,
                  d_ref, gw_ref, gb_ref, w_ref, fg_ref, o_ref, *, final):
    is_ctx = pl.program_id(0) < N_CTX // TM
    y_pool = jnp.where(is_ctx, ypc_ref[...], ypl_ref[...])
    y_ft = jnp.where(is_ctx, yfc_ref[...], yfl_ref[...])
    gate = mod_ref[0][:, 2 * D_MODEL:3 * D_MODEL]
    y_dn = dno_ref[...] * silu(dnz_ref[...])
    y = s5y_ref[...] + d_ref[...] * s5_ref[:, 0:BR]
    y = jax.nn.gelu(y)
    y = y * jax.nn.sigmoid(bdot(y, gw_ref[...]) + gb_ref[...])
    y_s5 = y * silu(s5_ref[:, BR:2 * BR])
    acc = bdot(y_pool, w_ref[0:BR, :])
    acc = acc + bdot(y_dn, w_ref[BR:2 * BR, :])
    acc = acc + bdot(y_s5, w_ref[2 * BR:3 * BR, :])
    acc = acc + bdot(y_ft, w_ref[3 * BR:4 * BR, :])
    xn = x_ref[...] + gate * acc
    if final:
        xn = xn * lax.rsqrt(jnp.mean(xn * xn, axis=-1, keepdims=True) + EPS) * fg_ref[...]
    o_ref[...] = xn


def outproj(x, mod, yp_ctx, yp_lat, dn_o, dn_z, s5_y, s5, yf_ctx, yf_lat, s5_d, glu_w, glu_b, w_out, final_g, final):
    row = lambda w: BS((TM, w), lambda i: (i, 0))
    full = lambda a, b: BS((a, b), lambda i: (0, 0))
    tiles_ctx = N_CTX // TM
    ctx_row = BS((TM, BR), lambda i: (jnp.minimum(i, tiles_ctx - 1), 0))
    lat_row = BS((TM, BR), lambda i: (jnp.maximum(i - tiles_ctx, 0), 0))
    return pl.pallas_call(
        functools.partial(_outproj_body, final=final),
        out_shape=SDS((N_TOK, D_MODEL), F32),
        grid=(N_TOK // TM,),
        in_specs=[row(D_MODEL),
                  BS((1, 1, 3 * D_MODEL), lambda i: (_cond_index(i), 0, 0)),
                  ctx_row, lat_row, row(BR), row(BR), row(BR), row(2 * BR), ctx_row, lat_row,
                  full(1, BR), full(BR, BR), full(1, BR), full(D_MODEL, D_MODEL), full(1, D_MODEL)],
        out_specs=row(D_MODEL),
        compiler_params=_cparams("parallel"),
        name="outproj",
    )(x, mod, yp_ctx, yp_lat, dn_o, dn_z, s5_y, s5, yf_ctx, yf_lat, s5_d, glu_w, glu_b, w_out, final_g)


def _permute_w_in(w_in):
    main = jnp.concatenate([w_in[:, 0:1536], w_in[:, 1552:2576]], axis=1)
    ba = jnp.pad(w_in[:, 1536:1552], ((0, 0), (0, 112)))
    return jnp.concatenate([main, ba], axis=1).astype(BF16)


def kernel(x_prompt, x_sample, c, state_delta, state_s5, c_ctx, w_ada, b_ada, norm_g, w_in, pool_w, pool_scale,
           dn_conv, dn_a_log, dn_dt_bias, dn_norm_g, s5_a_re, s5_a_im, s5_log_dt, s5_b_re, s5_b_im, s5_c_re,
           s5_c_im, s5_d, s5_glu_w, s5_glu_b, ft_w, w_out, final_g):
    x = jnp.concatenate([x_prompt.astype(F32).reshape(N_CTX, D_MODEL),
                         x_sample.astype(F32).reshape(N_LAT, D_MODEL)], axis=0)
    cond8 = jnp.concatenate([c_ctx.astype(F32)[None], c.astype(F32),
                             jnp.zeros((8 - N_COND, D_MODEL), F32)], axis=0)
    ada = ada_all(cond8, w_ada, b_ada)
    pm_ctx, inv_ctx, pm_lat, inv_lat = _pool_constants()
    fpos, fch, g1, h2, fch2 = _ft_constants()
    new_dn, new_s5 = [], []
    for l in range(DEPTH):
        mod = ada[l, 0:N_COND].reshape(N_COND, 1, 3 * D_MODEL)
        pool, qkv, dn_z, s5, ft, ba = inproj(x, mod, norm_g[l].reshape(1, D_MODEL), _permute_w_in(w_in[l]))

        w_bd = jax.scipy.linalg.block_diag(*[pool_w[l, g] for g in range(4)]).astype(BF16)
        sc = pool_scale[l].reshape(1, BR)
        yp_ctx = pool_branch(pool, pm_ctx, inv_ctx, w_bd, sc, False)
        yp_lat = pool_branch(pool, pm_lat, inv_lat, w_bd, sc, True)

        ftw = ft_w[l].astype(BF16)
        yf_ctx = ft_ctx(ft, fpos, fch, ftw)
        yf_lat = ft_lat(ft, g1, h2, fch2, ftw)

        wcat, r, al = _s5_weights(s5_a_re[l], s5_a_im[l], s5_log_dt[l], s5_b_re[l], s5_b_im[l],
                                  s5_c_re[l], s5_c_im[l])
        s0_s5 = state_s5[:, l].astype(F32).transpose(0, 3, 2, 1, 4).reshape(DEC_BATCH * S5_G, 256)
        s5_y, fin_s5 = s5_branch(s5[:, 0:BR], wcat, r, al, s0_s5)
        new_s5.append(fin_s5.reshape(BATCH, S5_G, 2, 2, S5_N).transpose(0, 3, 2, 1, 4))

        conv_w = jnp.pad(dn_conv[l], ((0, 8 - CONV_K), (0, 0)))
        gate_p = jnp.zeros((8, 128), F32)
        gate_p = gate_p.at[0, 8:16].set(dn_a_log[l].reshape(8)).at[1, 8:16].set(dn_dt_bias[l].reshape(8))
        dn_o, fin_dn = dn_branch(qkv, ba, conv_w, gate_p, dn_norm_g[l], state_delta[:, l].astype(F32))
        new_dn.append(fin_dn)

        x = outproj(x, mod, yp_ctx, yp_lat, dn_o, dn_z, s5_y, s5, yf_ctx, yf_lat, s5_d[l].reshape(1, BR),
                    s5_glu_w[l].astype(BF16), s5_glu_b[l].reshape(1, BR), w_out[l].astype(BF16),
                    final_g.reshape(1, D_MODEL), l == DEPTH - 1)

    y_prompt = x[:N_CTX].reshape(BATCH, SEQ, D_MODEL).astype(x_prompt.dtype)
    y_sample = x[N_CTX:].reshape(DEC_BATCH, DEC_SEQ, D_MODEL).astype(x_sample.dtype)
    new_state_delta = jnp.stack(new_dn, axis=1).astype(state_delta.dtype)
    new_state_s5 = jnp.stack(new_s5, axis=1).astype(state_s5.dtype)
    return (y_prompt, y_sample, new_state_delta, new_state_s5)
```

```python
import functools
import math

import numpy as np
import jax
import jax.numpy as jnp
from jax import lax
from jax.experimental import pallas as pl
from jax.experimental.pallas import tpu as pltpu

F32 = jnp.float32
BF16 = jnp.bfloat16

D_MODEL = 1024
BATCH = 16
SEQ = 256
DEPTH = 4
DEC_BATCH = 2
DEC_SEQ = 4096
GRID_W = 64
GRID_H = DEC_SEQ // GRID_W
BR = 256
POOL_WINDOWS = (2, 4, 8, 16)
POOL_GD = 64
DN_HEADS = 4
DN_HD = 64
CONV_K = 5
CHUNK = 64
S5_P = 16
S5_G = 16
S5_N = 64
S5_L = 16
FT_HD = 64
EPS = 1e-6

N_CTX = BATCH * SEQ
N_LAT = DEC_BATCH * DEC_SEQ
N_TOK = N_CTX + N_LAT
N_COND = 1 + DEC_BATCH
TM = 512
W_IN_COLS = 2688
VMEM_LIMIT = 56 * 1024 * 1024

SDS = jax.ShapeDtypeStruct
BS = pl.BlockSpec


def _cparams(*sem):
    return pltpu.CompilerParams(dimension_semantics=sem, vmem_limit_bytes=VMEM_LIMIT)


def bdot(a, b):
    return jnp.dot(a.astype(BF16), b.astype(BF16), preferred_element_type=F32)


def hdot(a, b):
    return jnp.dot(a, b, preferred_element_type=F32, precision=lax.Precision.HIGHEST)


def silu(x):
    return x * jax.nn.sigmoid(x)


def _cond_index(i):
    tiles_ctx = N_CTX // TM
    tiles_seq = DEC_SEQ // TM
    return jnp.where(i < tiles_ctx, 0, 1 + (i - tiles_ctx) // tiles_seq)


def _ada_body(c_ref, w_ref, b_ref, o_ref):
    o_ref[0] = hdot(silu(c_ref[...]), w_ref[0]) + b_ref[0]


def ada_all(cond8, w_ada, b_ada):
    tn = 512
    return pl.pallas_call(
        _ada_body,
        out_shape=SDS((DEPTH, 8, 3 * D_MODEL), F32),
        grid=(DEPTH, 3 * D_MODEL // tn),
        in_specs=[BS((8, D_MODEL), lambda l, j: (0, 0)),
                  BS((1, D_MODEL, tn), lambda l, j: (l, 0, j)),
                  BS((1, 1, tn), lambda l, j: (l, 0, j))],
        out_specs=BS((1, 8, tn), lambda l, j: (l, 0, j)),
        compiler_params=_cparams("parallel", "parallel"),
        name="ada",
    )(cond8, w_ada, b_ada.reshape(DEPTH, 1, 3 * D_MODEL))


def _inproj_body(x_ref, mod_ref, g_ref, w_ref, pool_ref, qkv_ref, dnz_ref, s5_ref, ft_ref, ba_ref):
    x = x_ref[...]
    m = mod_ref[0]
    shift = m[:, 0:D_MODEL]
    scale = m[:, D_MODEL:2 * D_MODEL]
    xn = x * lax.rsqrt(jnp.mean(x * x, axis=-1, keepdims=True) + EPS) * g_ref[...]
    h = (xn * (1.0 + scale) + shift).astype(BF16)

    def proj(lo, hi):
        return jnp.dot(h, w_ref[:, lo:hi], preferred_element_type=F32)

    pool_ref[...] = proj(0, 512)
    qkv_ref[...] = proj(512, 1280)
    dnz_ref[...] = proj(1280, 1536)
    s5_ref[...] = proj(1536, 2048)
    ft_ref[...] = proj(2048, 2560)
    ba_ref[...] = proj(2560, 2688)


def inproj(x, mod, norm_g, w_in_p):
    widths = (512, 768, 256, 512, 512, 128)
    return pl.pallas_call(
        _inproj_body,
        out_shape=[SDS((N_TOK, w), F32) for w in widths],
        grid=(N_TOK // TM,),
        in_specs=[BS((TM, D_MODEL), lambda i: (i, 0)),
                  BS((1, 1, 3 * D_MODEL), lambda i: (_cond_index(i), 0, 0)),
                  BS((1, D_MODEL), lambda i: (0, 0)),
                  BS((D_MODEL, W_IN_COLS), lambda i: (0, 0))],
        out_specs=[BS((TM, w), lambda i: (i, 0)) for w in widths],
        compiler_params=_cparams("parallel"),
        name="inproj",
    )(x, mod, norm_g, w_in_p)


def _pool_body(u_ref, z_ref, pm_ref, inv_ref, w_ref, sc_ref, o_ref, *scratch, two_d):
    nblk = u_ref.shape[0] // 256
    if two_d:
        pad_ref, v_ref = scratch
        halo = 8 * GRID_W
        pad_ref[0:halo, :] = jnp.zeros((halo, BR), F32)
        pad_ref[halo + DEC_SEQ:2 * halo + DEC_SEQ, :] = jnp.zeros((halo, BR), F32)
        pad_ref[halo:halo + DEC_SEQ, :] = u_ref[...]
        lane = lax.broadcasted_iota(jnp.int32, (GRID_W, 128), 1)

        def row_body(r, c):
            base = pl.multiple_of(r * GRID_W, GRID_W)

            def slab(d, lo):
                return pad_ref[pl.ds(base + (8 + d) * GRID_W, GRID_W), lo:lo + 128]

            s2 = slab(-1, 0) + slab(0, 0)
            s4 = s2 + slab(-2, 0) + slab(1, 0)
            v_ref[pl.ds(base, GRID_W), 0:128] = jnp.where(lane < 64, s2, s4)
            s8 = slab(-4, 128)
            for d in (-3, -2, -1, 0, 1, 2, 3):
                s8 = s8 + slab(d, 128)
            s16 = s8
            for d in (-8, -7, -6, -5, 4, 5, 6, 7):
                s16 = s16 + slab(d, 128)
            v_ref[pl.ds(base, GRID_W), 128:256] = jnp.where(lane < 64, s8, s16)
            return c

        lax.fori_loop(0, GRID_H, row_body, 0)
        src = v_ref
    else:
        src = u_ref
    grp = lax.broadcasted_iota(jnp.int32, (256, BR), 1) // POOL_GD

    def blk_body(b, c):
        r0 = pl.multiple_of(b * 256, 256)
        vb = src[pl.ds(r0, 256), :]
        hi = vb.astype(BF16)
        lo = (vb - hi.astype(F32)).astype(BF16)
        res = jnp.zeros((256, BR), F32)
        for g in range(len(POOL_WINDOWS)):
            pg = (jnp.dot(pm_ref[g], hi, preferred_element_type=F32)
                  + jnp.dot(pm_ref[g], lo, preferred_element_type=F32))
            res = jnp.where(grp == g, pg, res)
        pooled = res * inv_ref[pl.ds(r0, 256), :]
        d = pooled - u_ref[pl.ds(r0, 256), :]
        y = bdot(d, w_ref[...]) * sc_ref[...]
        o_ref[pl.ds(r0, 256), :] = y * silu(z_ref[pl.ds(r0, 256), :])
        return c

    lax.fori_loop(0, nblk, blk_body, 0)


def _band_matrices(seg):
    t = np.arange(256)
    out = []
    for w in POOL_WINDOWS:
        lo = t - w // 2
        hi = t - w // 2 + w
        s = t[None, :]
        m = (s >= lo[:, None]) & (s < hi[:, None]) & ((s // seg) == (t[:, None] // seg))
        out.append(m.astype(np.float32))
    return np.stack(out)


def _counts(length, w):
    pos = np.arange(length)
    return (np.clip(pos - w // 2 + w, 0, length) - np.clip(pos - w // 2, 0, length)).astype(np.float64)


def _pool_constants():
    inv_ctx = np.concatenate([np.repeat((1.0 / _counts(SEQ, w))[:, None], POOL_GD, 1) for w in POOL_WINDOWS], 1)
    inv_lat = []
    for w in POOL_WINDOWS:
        c2 = np.outer(_counts(GRID_H, w), _counts(GRID_W, w)).reshape(DEC_SEQ)
        inv_lat.append(np.repeat((1.0 / c2)[:, None], POOL_GD, 1))
    inv_lat = np.concatenate(inv_lat, 1)
    return (jnp.asarray(_band_matrices(SEQ), BF16), jnp.asarray(inv_ctx, F32),
            jnp.asarray(_band_matrices(GRID_W), BF16), jnp.asarray(inv_lat, F32))


def pool_branch(pool, pm, inv, w_bd, scale, two_d):
    if two_d:
        rows, nseq, blk0 = DEC_SEQ, DEC_BATCH, N_CTX // DEC_SEQ
        scratch = [pltpu.VMEM((DEC_SEQ + 16 * GRID_W, BR), F32), pltpu.VMEM((DEC_SEQ, BR), F32)]
    else:
        rows, nseq, blk0 = SEQ, BATCH, 0
        scratch = []
    return pl.pallas_call(
        functools.partial(_pool_body, two_d=two_d),
        out_shape=SDS((nseq * rows, BR), F32),
        grid=(nseq,),
        in_specs=[BS((rows, BR), lambda i: (blk0 + i, 0)),
                  BS((rows, BR), lambda i: (blk0 + i, 1)),
                  BS((4, 256, 256), lambda i: (0, 0, 0)),
                  BS((rows, BR), lambda i: (0, 0)),
                  BS((BR, BR), lambda i: (0, 0)),
                  BS((1, BR), lambda i: (0, 0))],
        out_specs=BS((rows, BR), lambda i: (i, 0)),
        scratch_shapes=scratch,
        compiler_params=_cparams("parallel"),
        name="pool2d" if two_d else "pool1d",
    )(pool, pool, pm, inv, w_bd, scale)


def _ft_ctx_body(u_ref, z_ref, fpos_ref, fch_ref, w_ref, o_ref):
    uc = bdot(u_ref[...], fch_ref[...])
    st = jnp.concatenate([uc[:, 0:BR], uc[:, BR:2 * BR]], axis=0)
    f = bdot(fpos_ref[...], st)
    o_ref[...] = bdot(f, w_ref[...]) * silu(z_ref[...])


def ft_ctx(ft, fpos, fch, ft_w):
    return pl.pallas_call(
        _ft_ctx_body,
        out_shape=SDS((N_CTX, BR), F32),
        grid=(BATCH,),
        in_specs=[BS((SEQ, BR), lambda i: (i, 0)),
                  BS((SEQ, BR), lambda i: (i, 1)),
                  BS((SEQ, 2 * SEQ), lambda i: (0, 0)),
                  BS((BR, 2 * BR), lambda i: (0, 0)),
                  BS((BR, BR), lambda i: (0, 0))],
        out_specs=BS((SEQ, BR), lambda i: (i, 0)),
        compiler_params=_cparams("parallel"),
        name="ft_ctx",
    )(ft, ft, fpos, fch, ft_w)


def _ft_lat_body(u_ref, z_ref, g_ref, h_ref, fch_ref, w_ref, o_ref, x_ref, yr_ref, yi_ref):
    for hf in range(2):
        x_ref[hf] = u_ref[:, hf * 128:(hf + 1) * 128]

    def stage1(t2, c):
        xs = jnp.concatenate([x_ref[hf, pl.ds(t2, GRID_H, stride=GRID_W), :] for hf in range(2)], axis=1)
        y = jnp.dot(g_ref[t2], xs.astype(BF16), preferred_element_type=F32)
        r0 = pl.multiple_of(t2 * GRID_W, GRID_W)
        for hf in range(2):
            yr_ref[hf, pl.ds(r0, GRID_W), :] = y[0:64, hf * 128:(hf + 1) * 128]
            yi_ref[hf, pl.ds(r0, GRID_W), :] = y[64:128, hf * 128:(hf + 1) * 128]
        return c

    lax.fori_loop(0, GRID_W, stage1, 0)

    def stage2(kb, c):
        yr = jnp.concatenate([yr_ref[hf, pl.ds(kb, GRID_W, stride=GRID_W), :] for hf in range(2)], axis=1)
        yi = jnp.concatenate([yi_ref[hf, pl.ds(kb, GRID_W, stride=GRID_W), :] for hf in range(2)], axis=1)
        st = jnp.concatenate([yr, yi], axis=0).astype(BF16)
        a = jnp.dot(h_ref[...], st, preferred_element_type=F32)
        for hf in range(2):
            yr_ref[hf, pl.ds(kb, GRID_W, stride=GRID_W), :] = a[0:64, hf * 128:(hf + 1) * 128]
            yi_ref[hf, pl.ds(kb, GRID_W, stride=GRID_W), :] = a[64:128, hf * 128:(hf + 1) * 128]
        return c

    lax.fori_loop(0, GRID_W, stage2, 0)

    def stage3(b, c):
        r0 = pl.multiple_of(b * TM, TM)
        ar = jnp.concatenate([yr_ref[hf, pl.ds(r0, TM), :] for hf in range(2)], axis=1)
        ai = jnp.concatenate([yi_ref[hf, pl.ds(r0, TM), :] for hf in range(2)], axis=1)
        f = bdot(ar, fch_ref[0:BR, :]) + bdot(ai, fch_ref[BR:2 * BR, :])
        o_ref[pl.ds(r0, TM), :] = bdot(f, w_ref[...]) * silu(z_ref[pl.ds(r0, TM), :])
        return c

    lax.fori_loop(0, DEC_SEQ // TM, stage3, 0)


def ft_lat(ft, g1, h2, fch2, ft_w):
    blk0 = N_CTX // DEC_SEQ
    return pl.pallas_call(
        _ft_lat_body,
        out_shape=SDS((N_LAT, BR), F32),
        grid=(DEC_BATCH,),
        in_specs=[BS((DEC_SEQ, BR), lambda i: (blk0 + i, 0)),
                  BS((DEC_SEQ, BR), lambda i: (blk0 + i, 1)),
                  BS((GRID_W, 128, GRID_H), lambda i: (0, 0, 0)),
                  BS((128, 128), lambda i: (0, 0)),
                  BS((2 * BR, BR), lambda i: (0, 0)),
                  BS((BR, BR), lambda i: (0, 0))],
        out_specs=BS((DEC_SEQ, BR), lambda i: (i, 0)),
        scratch_shapes=[pltpu.VMEM((2, DEC_SEQ, 128), F32)] * 3,
        compiler_params=_cparams("parallel"),
        name="ft_lat",
    )(ft, ft, g1, h2, fch2, ft_w)


def _ft_constants():
    c = np.arange(FT_HD)
    ang = 2.0 * np.pi * np.outer(c, c) / FT_HD
    eye4 = np.eye(BR // FT_HD)
    cc = np.kron(eye4, np.cos(ang)) / 8.0
    sc = np.kron(eye4, np.sin(ang)) / 8.0
    t = np.arange(SEQ)
    angt = 2.0 * np.pi * (np.outer(t, t) % SEQ) / SEQ
    fpos = np.concatenate([np.cos(angt), -np.sin(angt)], axis=1) / 16.0
    fch = np.concatenate([cc, sc], axis=1)
    kb = np.arange(GRID_W)[None, :, None]
    t1 = np.arange(GRID_H)[None, None, :]
    t2 = np.arange(GRID_W)[:, None, None]
    a1 = 2.0 * np.pi * ((kb * (GRID_W * t1 + t2)) % DEC_SEQ) / DEC_SEQ
    g1 = np.concatenate([np.cos(a1), -np.sin(a1)], axis=1) / 8.0
    a2 = 2.0 * np.pi * (np.outer(np.arange(GRID_W), np.arange(GRID_W)) % GRID_W) / GRID_W
    c2, s2 = np.cos(a2) / 8.0, np.sin(a2) / 8.0
    h2 = np.block([[c2, s2], [-s2, c2]])
    fch2 = np.concatenate([cc, sc], axis=0)
    as_bf = lambda a: jnp.asarray(a, F32).astype(BF16)
    return as_bf(fpos), as_bf(fch), as_bf(g1), as_bf(h2), as_bf(fch2)


def _s5_intra_body(u_ref, w_ref, y_ref, e_ref):
    v = jnp.dot(u_ref[0], w_ref[0], preferred_element_type=F32)
    y_ref[0] = v[:, 0:256]
    e_ref[0] = v[:, 256:512]


def s5_intra(ug, wcat):
    rows = ug.shape[1]
    return pl.pallas_call(
        _s5_intra_body,
        out_shape=[SDS((S5_G, rows, 256), F32), SDS((S5_G, rows, 256), F32)],
        grid=(S5_G,),
        in_specs=[BS((1, rows, 256), lambda g: (g, 0, 0)), BS((1, 256, 512), lambda g: (g, 0, 0))],
        out_specs=[BS((1, rows, 256), lambda g: (g, 0, 0)), BS((1, rows, 256), lambda g: (g, 0, 0))],
        compiler_params=_cparams("parallel"),
        name="s5_intra",
    )(ug, wcat)


def _s5_scan_body(e_ref, al_ref, s0_ref, sp_ref, fin_ref):
    nc = e_ref.shape[0]
    a_re = al_ref[:, 0:128]
    a_im = al_ref[:, 128:256]

    def step(c, carry):
        s_re, s_im = carry
        sp_ref[c, :, 0:128] = s_re
        sp_ref[c, :, 128:256] = s_im
        n_re = a_re * s_re - a_im * s_im + e_ref[c, :, 0:128]
        n_im = a_re * s_im + a_im * s_re + e_ref[c, :, 128:256]
        return n_re, n_im

    s_re, s_im = lax.fori_loop(0, nc, step, (s0_ref[:, 0:128], s0_ref[:, 128:256]))
    fin_ref[:, 0:128] = s_re
    fin_ref[:, 128:256] = s_im


def s5_scan(e_scan, al, s0):
    nc, rows, _ = e_scan.shape
    return pl.pallas_call(
        _s5_scan_body,
        out_shape=[SDS((nc, rows, 256), F32), SDS((rows, 256), F32)],
        compiler_params=pltpu.CompilerParams(vmem_limit_bytes=VMEM_LIMIT),
        name="s5_scan",
    )(e_scan, al, s0)


def _s5_out_body(y_ref, s_ref, r_ref, o_ref):
    o_ref[0] = y_ref[0] + jnp.dot(s_ref[0], r_ref[0], preferred_element_type=F32)


def s5_out(yi, sp, r):
    rows = yi.shape[1]
    spec = BS((1, rows, 256), lambda g: (g, 0, 0))
    return pl.pallas_call(
        _s5_out_body,
        out_shape=SDS((S5_G, rows, 256), F32),
        grid=(S5_G,),
        in_specs=[spec, spec, BS((1, 256, 256), lambda g: (g, 0, 0))],
        out_specs=spec,
        compiler_params=_cparams("parallel"),
        name="s5_out",
    )(yi, sp, r)


def _s5_weights(a_re, a_im, log_dt, b_re, b_im, c_re, c_im):
    hi = lax.Precision.HIGHEST
    L = S5_L
    dt = jnp.exp(log_dt)[..., None]
    xr, xi = a_re * dt, a_im * dt
    mag = jnp.exp(xr)
    ab_re, ab_im = mag * jnp.cos(xi), mag * jnp.sin(xi)
    den = a_re * a_re + a_im * a_im
    nr = ab_re - 1.0
    coef_re = (nr * a_re + ab_im * a_im) / den
    coef_im = (ab_im * a_re - nr * a_im) / den
    bb_re = coef_re[..., None] * b_re - coef_im[..., None] * b_im
    bb_im = coef_re[..., None] * b_im + coef_im[..., None] * b_re
    m = jnp.arange(L + 1, dtype=F32)[:, None, None, None]
    pw_re = jnp.exp(m * xr) * jnp.cos(m * xi)
    pw_im = jnp.exp(m * xr) * jnp.sin(m * xi)
    cp_re = c_re[None] * pw_re[:, :, :, None, :] - c_im[None] * pw_im[:, :, :, None, :]
    cp_im = c_re[None] * pw_im[:, :, :, None, :] + c_im[None] * pw_re[:, :, :, None, :]
    k = (jnp.einsum('mdgqn,dgnp->mdgpq', cp_re, bb_re, precision=hi)
         - jnp.einsum('mdgqn,dgnp->mdgpq', cp_im, bb_im, precision=hi))
    ii = np.arange(L)[:, None]
    jj = np.arange(L)[None, :]
    mf = jnp.where((jj >= ii)[:, :, None, None, None], k[:, 0][np.clip(jj - ii, 0, L)], 0.0)
    mb = jnp.where((ii >= jj)[:, :, None, None, None], k[:, 1][np.clip(ii - jj, 0, L)], 0.0)
    mt = (mf + mb).transpose(2, 0, 3, 1, 4).reshape(S5_G, L * S5_P, L * S5_P)

    def inject(pr, pi, d):
        q_re = pr[..., None] * bb_re[d][None] - pi[..., None] * bb_im[d][None]
        q_im = pr[..., None] * bb_im[d][None] + pi[..., None] * bb_re[d][None]
        f = lambda q: q.transpose(1, 0, 3, 2).reshape(S5_G, L * S5_P, S5_N)
        return f(q_re), f(q_im)

    qf_re, qf_im = inject(pw_re[:L, 0][::-1], pw_im[:L, 0][::-1], 0)
    qb_re, qb_im = inject(pw_re[:L, 1], pw_im[:L, 1], 1)
    wcat = jnp.concatenate([mt, qf_re, qb_re, qf_im, qb_im], axis=-1)

    def readout(pr, pi, d):
        d_re = c_re[d][None] * pr[:, :, None, :] - c_im[d][None] * pi[:, :, None, :]
        d_im = c_re[d][None] * pi[:, :, None, :] + c_im[d][None] * pr[:, :, None, :]
        f = lambda x: x.transpose(1, 3, 0, 2).reshape(S5_G, S5_N, L * S5_P)
        return f(d_re), f(-d_im)

    rf_re, rf_im = readout(pw_re[1:L + 1, 0], pw_im[1:L + 1, 0], 0)
    rb_re, rb_im = readout(pw_re[1:L + 1, 1][::-1], pw_im[1:L + 1, 1][::-1], 1)
    r = jnp.concatenate([rf_re, rb_re, rf_im, rb_im], axis=1)
    al = jnp.concatenate([pw_re[L, 0], pw_re[L, 1], pw_im[L, 0], pw_im[L, 1]], axis=-1)
    return wcat.astype(BF16), r.astype(BF16), al


_BWD_LANES = np.concatenate([np.zeros(64, bool), np.ones(64, bool), np.zeros(64, bool), np.ones(64, bool)])


def _flip_bwd(x):
    return jnp.where(_BWD_LANES, jnp.flip(x, axis=0), x)


def s5_branch(s5_u, wcat, r, al, s0_lat):
    L = S5_L
    rows = N_TOK // L
    ug = s5_u.reshape(rows, L, S5_G, S5_P).transpose(2, 0, 1, 3).reshape(S5_G, rows, L * S5_P).astype(BF16)
    yi, e = s5_intra(ug, wcat)
    rc, ncc = N_CTX // L, SEQ // L
    ncl = DEC_SEQ // L
    e_ctx = e[:, :rc].reshape(S5_G, BATCH, ncc, 256).transpose(2, 1, 0, 3).reshape(ncc, BATCH * S5_G, 256)
    e_lat = e[:, rc:].reshape(S5_G, DEC_BATCH, ncl, 256).transpose(2, 1, 0, 3).reshape(ncl, DEC_BATCH * S5_G, 256)
    sp_ctx, fin_ctx = s5_scan(_flip_bwd(e_ctx), jnp.tile(al, (BATCH, 1)), jnp.zeros((BATCH * S5_G, 256), F32))
    sp_lat, _ = s5_scan(_flip_bwd(e_lat), jnp.tile(al, (DEC_BATCH, 1)), s0_lat)
    sp_ctx = _flip_bwd(sp_ctx).reshape(ncc, BATCH, S5_G, 256).transpose(2, 1, 0, 3).reshape(S5_G, rc, 256)
    sp_lat = _flip_bwd(sp_lat).reshape(ncl, DEC_BATCH, S5_G, 256).transpose(2, 1, 0, 3).reshape(S5_G, rows - rc, 256)
    sp = jnp.concatenate([sp_ctx, sp_lat], axis=1).astype(BF16)
    y = s5_out(yi, sp, r)
    y = y.reshape(S5_G, rows, L, S5_P).transpose(1, 2, 0, 3).reshape(N_TOK, BR)
    return y, fin_ctx


DN_TILE = 256
DN_HALO = 8


N_SEG = N_TOK // DN_TILE
assert N_CTX == DEC_SEQ


def _split3(x):
    x1 = x.astype(BF16)
    r1 = x - x1.astype(F32)
    x2 = r1.astype(BF16)
    return x1, x2, (r1 - x2.astype(F32)).astype(BF16)


def dot3(a, b, exact):
    if exact == 'b':
        return sum(jnp.dot(p, b.astype(BF16), preferred_element_type=F32) for p in _split3(a))
    return sum(jnp.dot(a.astype(BF16), p, preferred_element_type=F32) for p in _split3(b))


def _dn_conv_body(x_ref, prev_ref, next_ref, ba_ref, w_ref, gp_ref, ex_ref, o_ref, gb_ref, pad_ref):
    i = pl.program_id(0)
    tiles_ctx = N_CTX // DN_TILE
    tiles_seq = DEC_SEQ // DN_TILE
    j = (i - tiles_ctx) % tiles_seq
    first = jnp.logical_or(i < tiles_ctx, j == 0)
    last = jnp.logical_or(i < tiles_ctx, j == tiles_seq - 1)
    pad_ref[0:DN_HALO, :] = jnp.where(first, 0.0, prev_ref[...])
    pad_ref[DN_HALO:DN_HALO + DN_TILE, :] = x_ref[...]
    pad_ref[DN_HALO + DN_TILE:2 * DN_HALO + DN_TILE, :] = jnp.where(last, 0.0, next_ref[...])
    acc = jnp.zeros((DN_TILE, 3 * BR), F32)
    for t in range(CONV_K):
        acc = acc + pad_ref[pl.ds(DN_HALO - CONV_K // 2 + t, DN_TILE), :] * w_ref[t:t + 1, :]
    o_ref[...] = silu(acc)
    raw = ba_ref[...]
    lane = lax.broadcasted_iota(jnp.int32, raw.shape, 1)
    xa = raw + gp_ref[1:2, :]
    sp = jnp.maximum(xa, 0.0) + jnp.log1p(jnp.exp(-jnp.abs(xa)))
    gates = jnp.where(lane < 2 * DN_HEADS, jax.nn.sigmoid(raw), -jnp.exp(gp_ref[0:1, :]) * sp)
    r = lax.broadcasted_iota(jnp.int32, (DN_TILE, DN_TILE), 0)
    c = lax.broadcasted_iota(jnp.int32, (DN_TILE, DN_TILE), 1)
    same = (r // CHUNK) == (c // CHUNK)
    ex = dot3(gates, ex_ref[...], 'b')
    gb_ref[:, 0:2 * BR] = ex[:, 0:2 * BR]
    gb_ref[:, 2 * BR:3 * BR] = dot3(jnp.logical_and(same, c <= r), ex[:, 2 * BR:3 * BR], 'a')
    gb_ref[:, 3 * BR:4 * BR] = dot3(jnp.logical_and(same, c >= r), ex[:, 3 * BR:4 * BR], 'a')


def _gate_expand():
    e = np.zeros((128, 4 * BR), np.float32)
    for blk in range(4):
        for h in range(DN_HEADS):
            e[blk * DN_HEADS + h, blk * BR + h * DN_HD:blk * BR + (h + 1) * DN_HD] = 1.0
    return jnp.asarray(e, BF16)


def dn_conv(qkv, ba, conv_w, gate_p):
    nt = N_TOK // DN_TILE
    per = DN_TILE // DN_HALO
    nhb = N_TOK // DN_HALO
    return pl.pallas_call(
        _dn_conv_body,
        out_shape=[SDS((N_TOK, 3 * BR), F32), SDS((N_TOK, 4 * BR), F32)],
        grid=(nt,),
        in_specs=[BS((DN_TILE, 3 * BR), lambda i: (i, 0)),
                  BS((DN_HALO, 3 * BR), lambda i: (jnp.maximum(i * per - 1, 0), 0)),
                  BS((DN_HALO, 3 * BR), lambda i: (jnp.minimum((i + 1) * per, nhb - 1), 0)),
                  BS((DN_TILE, 128), lambda i: (i, 0)),
                  BS((8, 3 * BR), lambda i: (0, 0)),
                  BS((8, 128), lambda i: (0, 0)),
                  BS((128, 4 * BR), lambda i: (0, 0))],
        out_specs=[BS((DN_TILE, 3 * BR), lambda i: (i, 0)), BS((DN_TILE, 4 * BR), lambda i: (i, 0))],
        scratch_shapes=[pltpu.VMEM((DN_TILE + 2 * DN_HALO, 3 * BR), F32)],
        compiler_params=_cparams("parallel"),
        name="dn_conv",
    )(qkv, qkv, qkv, ba, conv_w, gate_p, _gate_expand())


def _head_block_mask():
    r = lax.broadcasted_iota(jnp.int32, (BR, BR), 0) // DN_HD
    c = lax.broadcasted_iota(jnp.int32, (BR, BR), 1) // DN_HD
    return r == c


def _split2(x):
    hi = x.astype(BF16)
    return hi, (x - hi.astype(F32)).astype(BF16)


def _dn_local_body(qkv_ref, gb_ref, *out_refs):
    f_refs, b_refs = out_refs[0:6], out_refs[6:12]
    ncb = DN_TILE // CHUNK
    nb = 2 * ncb
    bmask = _head_block_mask()
    ri = lax.broadcasted_iota(jnp.int32, (nb, CHUNK, BR), 1)
    cj = lax.broadcasted_iota(jnp.int32, (nb, CHUNK, BR), 2) % DN_HD
    bwd = lax.broadcasted_iota(jnp.int32, (nb, CHUNK, BR), 0) >= ncb
    eye = cj == ri
    incl = jnp.logical_or(jnp.logical_and(bwd, cj >= ri), jnp.logical_and(jnp.logical_not(bwd), cj <= ri))
    strict = jnp.logical_and(incl, jnp.logical_not(eye))
    ones_blk = bmask.astype(BF16)

    def chunks(x):
        return x.reshape(ncb, CHUNK, BR)

    def both(x):
        return jnp.concatenate([x, x], axis=0)

    def head_sum(x):
        return dot3(x.reshape(-1, BR), ones_blk, 'b').reshape(x.shape)

    def bd(x):
        return jnp.where(bmask, jnp.concatenate([x] * DN_HEADS, axis=1), jnp.zeros((), x.dtype))

    def bmm(a, b):
        return jnp.einsum('bij,bjk->bik', a, b, preferred_element_type=F32)

    q = chunks(qkv_ref[:, 0:BR])
    k = chunks(qkv_ref[:, BR:2 * BR])
    v = both(chunks(qkv_ref[:, 2 * BR:3 * BR]))
    q = q * lax.rsqrt(head_sum(q * q) + EPS) * (DN_HD ** -0.5)
    k = k * lax.rsqrt(head_sum(k * k) + EPS)
    kq = jnp.einsum('bik,bjk->bij', jnp.concatenate([k, q], axis=1).astype(BF16), bd(k.astype(BF16)),
                    preferred_element_type=F32)
    kk, qk = both(kq[:, 0:CHUNK]), both(kq[:, CHUNK:2 * CHUNK])
    q, k = both(q), both(k)
    beta = jnp.concatenate([chunks(gb_ref[:, 0:BR]), chunks(gb_ref[:, BR:2 * BR])], axis=0)
    gc = jnp.concatenate([chunks(gb_ref[:, 2 * BR:3 * BR]), chunks(gb_ref[:, 3 * BR:4 * BR])], axis=0)
    crow = jnp.sum(jnp.where(eye, gc, 0.0), axis=1, keepdims=True)
    decay = jnp.where(incl, jnp.exp(jnp.where(incl, gc - crow, 0.0)), 0.0)
    a = jnp.where(strict, kk * decay * beta, 0.0)
    tinv = jnp.where(eye, 1.0, 0.0) - a
    pw = a
    pw_bd = bd(pw.astype(BF16))
    for _ in range(5):
        pw = bmm(pw.astype(BF16), pw_bd)
        pw_bd = bd(pw.astype(BF16))
        tinv = tinv + bmm(tinv.astype(BF16), pw_bd)
    egc = jnp.exp(gc)
    t_hi, t_lo = _split2(tinv)

    def solve(rhs):
        r_hi, r_lo = _split2(rhs)
        r_hi, r_lo = bd(r_hi), bd(r_lo)
        return bmm(t_hi, r_hi) + bmm(t_hi, r_lo) + bmm(t_lo, r_hi)

    bwd_row = lax.broadcasted_iota(jnp.int32, (nb, 1, BR), 0) >= ncb
    g_last = jnp.where(bwd_row, gc[:, 0:1], gc[:, CHUNK - 1:CHUNK])
    eg = jnp.exp(g_last)
    outs = (solve(k * (beta * egc)), solve(v * beta), q * egc, k * jnp.exp(g_last - gc),
            jnp.where(incl, qk * decay, 0.0))
    for d, refs in enumerate((f_refs, b_refs)):
        for ref, x in zip(refs[0:5], outs):
            ref[...] = x[d * ncb:(d + 1) * ncb].reshape(DN_TILE, BR).astype(ref.dtype)
        refs[5][0] = jnp.concatenate([eg[d * ncb:(d + 1) * ncb, 0], jnp.zeros((8 - ncb, BR), F32)], axis=0)


_DN_LOCAL_DTYPES = (BF16, F32, BF16, BF16, BF16)


def dn_local(qkvc, gb):
    tok = BS((DN_TILE, BR), lambda i: (i, 0))
    shapes = [SDS((N_TOK, BR), dt) for dt in _DN_LOCAL_DTYPES] + [SDS((N_SEG, 8, BR), F32)]
    specs = [tok] * len(_DN_LOCAL_DTYPES) + [BS((1, 8, BR), lambda i: (i, 0, 0))]
    outs = pl.pallas_call(
        _dn_local_body,
        out_shape=shapes * 2,
        grid=(N_SEG,),
        in_specs=[BS((DN_TILE, 3 * BR), lambda i: (i, 0)), BS((DN_TILE, 4 * BR), lambda i: (i, 0))],
        out_specs=specs * 2,
        compiler_params=_cparams("parallel"),
        name="dn_local",
    )(qkvc, gb)
    return outs[0:6], outs[6:12]


def _dn_advance(chains, s, bmask):
    ncb = DN_TILE // CHUNK
    bmm = lambda a, b: jnp.einsum('bij,bjk->bik', a, b, preferred_element_type=F32)
    steps = []
    for t in range(ncb):
        cs = [ncb - 1 - t if rev else t for _, rev in chains]
        w, u, qt, kt, aqk, eg = [jnp.stack([load(k, c) for (load, _), c in zip(chains, cs)]) for k in range(6)]
        sb = s.astype(BF16)
        v_new = u - bmm(w, sb)
        vb = v_new.astype(BF16)
        v_bd = jnp.where(bmask, jnp.concatenate([vb] * DN_HEADS, axis=1), jnp.zeros((), BF16))
        o = bmm(qt, sb) + bmm(aqk, v_bd)
        upd = jnp.einsum('btk,btv->bkv', kt, vb, preferred_element_type=F32)
        s = s * eg + jnp.where(bmask, upd, 0.0)
        steps.append((cs, o))
    return s, steps


def _dn_seq_ctx_body(*refs):
    nseq = 4
    f_in, b_in = refs[0:6], refs[6:12]
    of_ref, ob_ref, finf_ref, finb_ref = refs[12:16]
    bmask = _head_block_mask()

    def loader(in_refs, q):
        def load(k, c):
            if k == 5:
                return in_refs[5][q, c:c + 1, :]
            return in_refs[k][q * DN_TILE + c * CHUNK:q * DN_TILE + (c + 1) * CHUNK, :]
        return load

    chains = [(loader(f_in, q), False) for q in range(nseq)] + [(loader(b_in, q), True) for q in range(nseq)]
    s, steps = _dn_advance(chains, jnp.zeros((2 * nseq, BR, BR), F32), bmask)
    for cs, o in steps:
        for i, c in enumerate(cs):
            o_ref, q = (of_ref, i) if i < nseq else (ob_ref, i - nseq)
            o_ref[q * DN_TILE + c * CHUNK:q * DN_TILE + (c + 1) * CHUNK, :] = o[i]
    finf_ref[...] = s[0:nseq]
    finb_ref[...] = s[nseq:2 * nseq]


def dn_seq_ctx(loc_f, loc_b):
    nseq = 4
    tok = BS((nseq * DN_TILE, BR), lambda i: (i, 0))
    specs = [tok] * len(_DN_LOCAL_DTYPES) + [BS((nseq, 8, BR), lambda i: (i, 0, 0))]
    st = BS((nseq, BR, BR), lambda i: (i, 0, 0))
    return pl.pallas_call(
        _dn_seq_ctx_body,
        out_shape=[SDS((N_CTX, BR), F32), SDS((N_CTX, BR), F32), SDS((BATCH, BR, BR), F32), SDS((BATCH, BR, BR), F32)],
        grid=(BATCH // nseq,),
        in_specs=specs * 2,
        out_specs=[tok, tok, st, st],
        compiler_params=_cparams("parallel"),
        name="dn_seq_ctx",
    )(*loc_f, *loc_b)


def _dn_seq_lat_body(*refs):
    f_in, b_in = refs[0:6], refs[6:12]
    s0_ref, of_ref, ob_ref, s_ref = refs[12:16]

    @pl.when(pl.program_id(0) == 0)
    def _():
        s_ref[...] = s0_ref[...]

    def loader(in_refs, q):
        def load(k, c):
            if k == 5:
                return in_refs[5][1 + q, 0, c:c + 1, :]
            return in_refs[k][1 + q, c * CHUNK:(c + 1) * CHUNK, :]
        return load

    chains = ([(loader(f_in, q), False) for q in range(DEC_BATCH)]
              + [(loader(b_in, q), True) for q in range(DEC_BATCH)])
    s, steps = _dn_advance(chains, s_ref[...], _head_block_mask())
    s_ref[...] = s
    for cs, o in steps:
        for i, c in enumerate(cs):
            o_ref, q = (of_ref, i) if i < DEC_BATCH else (ob_ref, i - DEC_BATCH)
            o_ref[q, c * CHUNK:(c + 1) * CHUNK, :] = o[i]


def dn_seq_lat(loc_f, loc_b, s0):
    nseg = DEC_SEQ // DN_TILE
    grp = lambda a: a.reshape((N_COND, nseg) + a.shape[1:])

    def specs(m):
        return ([BS((N_COND, DN_TILE, BR), lambda g: (0, m(g), 0))] * len(_DN_LOCAL_DTYPES)
                + [BS((N_COND, 1, 8, BR), lambda g: (0, m(g), 0, 0))])

    fwd = lambda g: g
    bwd = lambda g: nseg - 1 - g
    views = lambda loc: [a.reshape(N_COND, DEC_SEQ, BR) for a in loc[0:5]] + [grp(loc[5])]
    out = SDS((DEC_BATCH, DEC_SEQ, BR), F32)
    return pl.pallas_call(
        _dn_seq_lat_body,
        out_shape=[out, out],
        grid=(nseg,),
        in_specs=specs(fwd) + specs(bwd) + [BS((2 * DEC_BATCH, BR, BR), lambda g: (0, 0, 0))],
        out_specs=[BS((DEC_BATCH, DN_TILE, BR), lambda g: (0, fwd(g), 0)),
                   BS((DEC_BATCH, DN_TILE, BR), lambda g: (0, bwd(g), 0))],
        scratch_shapes=[pltpu.VMEM((2 * DEC_BATCH, BR, BR), F32)],
        compiler_params=_cparams("arbitrary"),
        name="dn_seq_lat",
    )(*views(loc_f), *views(loc_b), s0)


def dn_branch(qkv, ba, conv_w, gate_p, s0_lat):
    qkvc, gb = dn_conv(qkv, ba, conv_w, gate_p)
    loc_f, loc_b = dn_local(qkvc, gb)
    eye_h = jnp.eye(DN_HEADS, dtype=F32)
    s0 = s0_lat.transpose(1, 0, 2, 3, 4).reshape(2 * DEC_BATCH, DN_HEADS, DN_HD, DN_HD)
    s0 = jnp.einsum('shkv,hg->shkgv', s0, eye_h).reshape(2 * DEC_BATCH, BR, BR)
    of_c, ob_c, fin_f, fin_b = dn_seq_ctx(loc_f, loc_b)
    of_l, ob_l = dn_seq_lat(loc_f, loc_b, s0)
    unblock = lambda s: jnp.einsum('shkgv,hg->shkv', s.reshape(BATCH, DN_HEADS, DN_HD, DN_HEADS, DN_HD), eye_h)
    fin = jnp.stack([unblock(fin_f), unblock(fin_b)], axis=1)
    return (of_c, ob_c), (of_l.reshape(N_LAT, BR), ob_l.reshape(N_LAT, BR)), fin


def _outproj_body(x_ref, mod_ref, ypc_ref, ypl_ref, dofc_ref, dofl_ref, dobc_ref, dobl_ref, dnz_ref, s5y_ref, s5_ref,
                  yfc_ref, yfl_ref, d_ref, gw_ref, gb_ref, dng_ref, w_ref, fg_ref, o_ref, *, final):
    is_ctx = pl.program_id(0) < N_CTX // TM
    y_pool = jnp.where(is_ctx, ypc_ref[...], ypl_ref[...])
    y_ft = jnp.where(is_ctx, yfc_ref[...], yfl_ref[...])
    gate = mod_ref[0][:, 2 * D_MODEL:3 * D_MODEL]
    o = jnp.where(is_ctx, dofc_ref[...] + dobc_ref[...], dofl_ref[...] + dobl_ref[...])
    head_mean = jnp.where(_head_block_mask(), 1.0 / DN_HD, 0.0)
    y_dn = o * lax.rsqrt(dot3(o * o, head_mean, 'b') + EPS) * dng_ref[...] * silu(dnz_ref[...])
    y = s5y_ref[...] + d_ref[...] * s5_ref[:, 0:BR]
    y = jax.nn.gelu(y)
    y = y * jax.nn.sigmoid(bdot(y, gw_ref[...]) + gb_ref[...])
    y_s5 = y * silu(s5_ref[:, BR:2 * BR])
    acc = bdot(y_pool, w_ref[0:BR, :])
    acc = acc + bdot(y_dn, w_ref[BR:2 * BR, :])
    acc = acc + bdot(y_s5, w_ref[2 * BR:3 * BR, :])
    acc = acc + bdot(y_ft, w_ref[3 * BR:4 * BR, :])
    xn = x_ref[...] + gate * acc
    if final:
        xn = xn * lax.rsqrt(jnp.mean(xn * xn, axis=-1, keepdims=True) + EPS) * fg_ref[...]
    o_ref[...] = xn


def outproj(x, mod, yp_ctx, yp_lat, dn_ctx, dn_lat, dn_z, s5_y, s5, yf_ctx, yf_lat, s5_d, glu_w, glu_b, dn_g, w_out,
            final_g, final):
    row = lambda w: BS((TM, w), lambda i: (i, 0))
    full = lambda a, b: BS((a, b), lambda i: (0, 0))
    tiles_ctx = N_CTX // TM
    ctx_row = BS((TM, BR), lambda i: (jnp.minimum(i, tiles_ctx - 1), 0))
    lat_row = BS((TM, BR), lambda i: (jnp.maximum(i - tiles_ctx, 0), 0))
    return pl.pallas_call(
        functools.partial(_outproj_body, final=final),
        out_shape=SDS((N_TOK, D_MODEL), F32),
        grid=(N_TOK // TM,),
        in_specs=[row(D_MODEL),
                  BS((1, 1, 3 * D_MODEL), lambda i: (_cond_index(i), 0, 0)),
                  ctx_row, lat_row, ctx_row, lat_row, ctx_row, lat_row, row(BR), row(BR), row(2 * BR), ctx_row, lat_row,
                  full(1, BR), full(BR, BR), full(1, BR), full(1, BR), full(D_MODEL, D_MODEL), full(1, D_MODEL)],
        out_specs=row(D_MODEL),
        compiler_params=_cparams("parallel"),
        name="outproj",
    )(x, mod, yp_ctx, yp_lat, dn_ctx[0], dn_lat[0], dn_ctx[1], dn_lat[1], dn_z, s5_y, s5, yf_ctx, yf_lat,
      s5_d, glu_w, glu_b, dn_g, w_out, final_g)


def _permute_w_in(w_in):
    main = jnp.concatenate([w_in[:, 0:1536], w_in[:, 1552:2576]], axis=1)
    ba = jnp.pad(w_in[:, 1536:1552], ((0, 0), (0, 112)))
    return jnp.concatenate([main, ba], axis=1).astype(BF16)


def kernel(x_prompt, x_sample, c, state_delta, state_s5, c_ctx, w_ada, b_ada, norm_g, w_in, pool_w, pool_scale,
           dn_conv, dn_a_log, dn_dt_bias, dn_norm_g, s5_a_re, s5_a_im, s5_log_dt, s5_b_re, s5_b_im, s5_c_re,
           s5_c_im, s5_d, s5_glu_w, s5_glu_b, ft_w, w_out, final_g):
    x = jnp.concatenate([x_prompt.astype(F32).reshape(N_CTX, D_MODEL),
                         x_sample.astype(F32).reshape(N_LAT, D_MODEL)], axis=0)
    cond8 = jnp.concatenate([c_ctx.astype(F32)[None], c.astype(F32),
                             jnp.zeros((8 - N_COND, D_MODEL), F32)], axis=0)
    ada = ada_all(cond8, w_ada, b_ada)
    pm_ctx, inv_ctx, pm_lat, inv_lat = _pool_constants()
    fpos, fch, g1, h2, fch2 = _ft_constants()
    new_dn, new_s5 = [], []
    for l in range(DEPTH):
        mod = ada[l, 0:N_COND].reshape(N_COND, 1, 3 * D_MODEL)
        pool, qkv, dn_z, s5, ft, ba = inproj(x, mod, norm_g[l].reshape(1, D_MODEL), _permute_w_in(w_in[l]))

        w_bd = jax.scipy.linalg.block_diag(*[pool_w[l, g] for g in range(4)]).astype(BF16)
        sc = pool_scale[l].reshape(1, BR)
        yp_ctx = pool_branch(pool, pm_ctx, inv_ctx, w_bd, sc, False)
        yp_lat = pool_branch(pool, pm_lat, inv_lat, w_bd, sc, True)

        ftw = ft_w[l].astype(BF16)
        yf_ctx = ft_ctx(ft, fpos, fch, ftw)
        yf_lat = ft_lat(ft, g1, h2, fch2, ftw)

        wcat, r, al = _s5_weights(s5_a_re[l], s5_a_im[l], s5_log_dt[l], s5_b_re[l], s5_b_im[l],
                                  s5_c_re[l], s5_c_im[l])
        s0_s5 = state_s5[:, l].astype(F32).transpose(0, 3, 2, 1, 4).reshape(DEC_BATCH * S5_G, 256)
        s5_y, fin_s5 = s5_branch(s5[:, 0:BR], wcat, r, al, s0_s5)
        new_s5.append(fin_s5.reshape(BATCH, S5_G, 2, 2, S5_N).transpose(0, 3, 2, 1, 4))

        conv_w = jnp.pad(dn_conv[l], ((0, 8 - CONV_K), (0, 0)))
        gate_p = jnp.zeros((8, 128), F32)
        gate_p = gate_p.at[0, 8:16].set(dn_a_log[l].reshape(8)).at[1, 8:16].set(dn_dt_bias[l].reshape(8))
        dn_ctx, dn_lat, fin_dn = dn_branch(qkv, ba, conv_w, gate_p, state_delta[:, l].astype(F32))
        new_dn.append(fin_dn)

        x = outproj(x, mod, yp_ctx, yp_lat, dn_ctx, dn_lat, dn_z, s5_y, s5, yf_ctx, yf_lat, s5_d[l].reshape(1, BR),
                    s5_glu_w[l].astype(BF16), s5_glu_b[l].reshape(1, BR),
                    jnp.tile(dn_norm_g[l], DN_HEADS).reshape(1, BR), w_out[l].astype(BF16),
                    final_g.reshape(1, D_MODEL), l == DEPTH - 1)

    y_prompt = x[:N_CTX].reshape(BATCH, SEQ, D_MODEL).astype(x_prompt.dtype)
    y_sample = x[N_CTX:].reshape(DEC_BATCH, DEC_SEQ, D_MODEL).astype(x_sample.dtype)
    new_state_delta = jnp.stack(new_dn, axis=1).astype(state_delta.dtype)
    new_state_s5 = jnp.stack(new_s5, axis=1).astype(state_s5.dtype)
    return (y_prompt, y_sample, new_state_delta, new_state_s5)
```

```python
import functools
import math

import numpy as np
import jax
import jax.numpy as jnp
from jax import lax
from jax.experimental import pallas as pl
from jax.experimental.pallas import tpu as pltpu

F32 = jnp.float32
BF16 = jnp.bfloat16

D_MODEL = 1024
BATCH = 16
SEQ = 256
DEPTH = 4
DEC_BATCH = 2
DEC_SEQ = 4096
GRID_W = 64
GRID_H = DEC_SEQ // GRID_W
BR = 256
POOL_WINDOWS = (2, 4, 8, 16)
POOL_GD = 64
DN_HEADS = 4
DN_HD = 64
CONV_K = 5
CHUNK = 64
S5_P = 16
S5_G = 16
S5_N = 64
S5_L = 16
FT_HD = 64
EPS = 1e-6

N_CTX = BATCH * SEQ
N_LAT = DEC_BATCH * DEC_SEQ
N_TOK = N_CTX + N_LAT
N_COND = 1 + DEC_BATCH
TM = 512
W_IN_COLS = 2688
VMEM_LIMIT = 56 * 1024 * 1024

SDS = jax.ShapeDtypeStruct
BS = pl.BlockSpec


def _cparams(*sem):
    return pltpu.CompilerParams(dimension_semantics=sem, vmem_limit_bytes=VMEM_LIMIT)


def bdot(a, b):
    return jnp.dot(a.astype(BF16), b.astype(BF16), preferred_element_type=F32)


def hdot(a, b):
    return jnp.dot(a, b, preferred_element_type=F32, precision=lax.Precision.HIGHEST)


def silu(x):
    return x * jax.nn.sigmoid(x)


def _cond_index(i):
    tiles_ctx = N_CTX // TM
    tiles_seq = DEC_SEQ // TM
    return jnp.where(i < tiles_ctx, 0, 1 + (i - tiles_ctx) // tiles_seq)


def _ada_body(c_ref, w_ref, b_ref, o_ref):
    o_ref[0] = hdot(silu(c_ref[...]), w_ref[0]) + b_ref[0]


def ada_all(cond8, w_ada, b_ada):
    tn = 512
    return pl.pallas_call(
        _ada_body,
        out_shape=SDS((DEPTH, 8, 3 * D_MODEL), F32),
        grid=(DEPTH, 3 * D_MODEL // tn),
        in_specs=[BS((8, D_MODEL), lambda l, j: (0, 0)),
                  BS((1, D_MODEL, tn), lambda l, j: (l, 0, j)),
                  BS((1, 1, tn), lambda l, j: (l, 0, j))],
        out_specs=BS((1, 8, tn), lambda l, j: (l, 0, j)),
        compiler_params=_cparams("parallel", "parallel"),
        name="ada",
    )(cond8, w_ada, b_ada.reshape(DEPTH, 1, 3 * D_MODEL))


def _inproj_body(x_ref, mod_ref, g_ref, w_ref, pool_ref, qkv_ref, dnz_ref, s5_ref, ft_ref, ba_ref):
    x = x_ref[...]
    m = mod_ref[0]
    shift = m[:, 0:D_MODEL]
    scale = m[:, D_MODEL:2 * D_MODEL]
    xn = x * lax.rsqrt(jnp.mean(x * x, axis=-1, keepdims=True) + EPS) * g_ref[...]
    h = (xn * (1.0 + scale) + shift).astype(BF16)

    def proj(lo, hi):
        return jnp.dot(h, w_ref[:, lo:hi], preferred_element_type=F32)

    pool_ref[...] = proj(0, 512)
    qkv_ref[...] = proj(512, 1280)
    dnz_ref[...] = proj(1280, 1536)
    s5_ref[...] = proj(1536, 2048)
    ft_ref[...] = proj(2048, 2560)
    ba_ref[...] = proj(2560, 2688)


def inproj(x, mod, norm_g, w_in_p):
    widths = (512, 768, 256, 512, 512, 128)
    return pl.pallas_call(
        _inproj_body,
        out_shape=[SDS((N_TOK, w), F32) for w in widths],
        grid=(N_TOK // TM,),
        in_specs=[BS((TM, D_MODEL), lambda i: (i, 0)),
                  BS((1, 1, 3 * D_MODEL), lambda i: (_cond_index(i), 0, 0)),
                  BS((1, D_MODEL), lambda i: (0, 0)),
                  BS((D_MODEL, W_IN_COLS), lambda i: (0, 0))],
        out_specs=[BS((TM, w), lambda i: (i, 0)) for w in widths],
        compiler_params=_cparams("parallel"),
        name="inproj",
    )(x, mod, norm_g, w_in_p)


def _pool_body(u_ref, z_ref, pm_ref, inv_ref, w_ref, sc_ref, o_ref, *scratch, two_d):
    nblk = u_ref.shape[0] // 256
    if two_d:
        pad_ref, v_ref = scratch
        halo = 8 * GRID_W
        pad_ref[0:halo, :] = jnp.zeros((halo, BR), F32)
        pad_ref[halo + DEC_SEQ:2 * halo + DEC_SEQ, :] = jnp.zeros((halo, BR), F32)
        pad_ref[halo:halo + DEC_SEQ, :] = u_ref[...]
        lane = lax.broadcasted_iota(jnp.int32, (GRID_W, 128), 1)

        def row_body(r, c):
            base = pl.multiple_of(r * GRID_W, GRID_W)

            def slab(d, lo):
                return pad_ref[pl.ds(base + (8 + d) * GRID_W, GRID_W), lo:lo + 128]

            s2 = slab(-1, 0) + slab(0, 0)
            s4 = s2 + slab(-2, 0) + slab(1, 0)
            v_ref[pl.ds(base, GRID_W), 0:128] = jnp.where(lane < 64, s2, s4)
            s8 = slab(-4, 128)
            for d in (-3, -2, -1, 0, 1, 2, 3):
                s8 = s8 + slab(d, 128)
            s16 = s8
            for d in (-8, -7, -6, -5, 4, 5, 6, 7):
                s16 = s16 + slab(d, 128)
            v_ref[pl.ds(base, GRID_W), 128:256] = jnp.where(lane < 64, s8, s16)
            return c

        lax.fori_loop(0, GRID_H, row_body, 0)
        src = v_ref
    else:
        src = u_ref
    grp = lax.broadcasted_iota(jnp.int32, (256, BR), 1) // POOL_GD

    def blk_body(b, c):
        r0 = pl.multiple_of(b * 256, 256)
        vb = src[pl.ds(r0, 256), :]
        hi = vb.astype(BF16)
        lo = (vb - hi.astype(F32)).astype(BF16)
        res = jnp.zeros((256, BR), F32)
        for g in range(len(POOL_WINDOWS)):
            pg = (jnp.dot(pm_ref[g], hi, preferred_element_type=F32)
                  + jnp.dot(pm_ref[g], lo, preferred_element_type=F32))
            res = jnp.where(grp == g, pg, res)
        pooled = res * inv_ref[pl.ds(r0, 256), :]
        d = pooled - u_ref[pl.ds(r0, 256), :]
        y = bdot(d, w_ref[...]) * sc_ref[...]
        o_ref[pl.ds(r0, 256), :] = y * silu(z_ref[pl.ds(r0, 256), :])
        return c

    lax.fori_loop(0, nblk, blk_body, 0)


def _band_matrices(seg):
    t = np.arange(256)
    out = []
    for w in POOL_WINDOWS:
        lo = t - w // 2
        hi = t - w // 2 + w
        s = t[None, :]
        m = (s >= lo[:, None]) & (s < hi[:, None]) & ((s // seg) == (t[:, None] // seg))
        out.append(m.astype(np.float32))
    return np.stack(out)


def _counts(length, w):
    pos = np.arange(length)
    return (np.clip(pos - w // 2 + w, 0, length) - np.clip(pos - w // 2, 0, length)).astype(np.float64)


def _pool_constants():
    inv_ctx = np.concatenate([np.repeat((1.0 / _counts(SEQ, w))[:, None], POOL_GD, 1) for w in POOL_WINDOWS], 1)
    inv_lat = []
    for w in POOL_WINDOWS:
        c2 = np.outer(_counts(GRID_H, w), _counts(GRID_W, w)).reshape(DEC_SEQ)
        inv_lat.append(np.repeat((1.0 / c2)[:, None], POOL_GD, 1))
    inv_lat = np.concatenate(inv_lat, 1)
    return (jnp.asarray(_band_matrices(SEQ), BF16), jnp.asarray(inv_ctx, F32),
            jnp.asarray(_band_matrices(GRID_W), BF16), jnp.asarray(inv_lat, F32))


def pool_branch(pool, pm, inv, w_bd, scale, two_d):
    if two_d:
        rows, nseq, blk0 = DEC_SEQ, DEC_BATCH, N_CTX // DEC_SEQ
        scratch = [pltpu.VMEM((DEC_SEQ + 16 * GRID_W, BR), F32), pltpu.VMEM((DEC_SEQ, BR), F32)]
    else:
        rows, nseq, blk0 = SEQ, BATCH, 0
        scratch = []
    return pl.pallas_call(
        functools.partial(_pool_body, two_d=two_d),
        out_shape=SDS((nseq * rows, BR), F32),
        grid=(nseq,),
        in_specs=[BS((rows, BR), lambda i: (blk0 + i, 0)),
                  BS((rows, BR), lambda i: (blk0 + i, 1)),
                  BS((4, 256, 256), lambda i: (0, 0, 0)),
                  BS((rows, BR), lambda i: (0, 0)),
                  BS((BR, BR), lambda i: (0, 0)),
                  BS((1, BR), lambda i: (0, 0))],
        out_specs=BS((rows, BR), lambda i: (i, 0)),
        scratch_shapes=scratch,
        compiler_params=_cparams("parallel"),
        name="pool2d" if two_d else "pool1d",
    )(pool, pool, pm, inv, w_bd, scale)


def _ft_ctx_body(u_ref, z_ref, fpos_ref, fch_ref, w_ref, o_ref):
    uc = bdot(u_ref[...], fch_ref[...])
    st = jnp.concatenate([uc[:, 0:BR], uc[:, BR:2 * BR]], axis=0)
    f = bdot(fpos_ref[...], st)
    o_ref[...] = bdot(f, w_ref[...]) * silu(z_ref[...])


def ft_ctx(ft, fpos, fch, ft_w):
    return pl.pallas_call(
        _ft_ctx_body,
        out_shape=SDS((N_CTX, BR), F32),
        grid=(BATCH,),
        in_specs=[BS((SEQ, BR), lambda i: (i, 0)),
                  BS((SEQ, BR), lambda i: (i, 1)),
                  BS((SEQ, 2 * SEQ), lambda i: (0, 0)),
                  BS((BR, 2 * BR), lambda i: (0, 0)),
                  BS((BR, BR), lambda i: (0, 0))],
        out_specs=BS((SEQ, BR), lambda i: (i, 0)),
        compiler_params=_cparams("parallel"),
        name="ft_ctx",
    )(ft, ft, fpos, fch, ft_w)


def _ft_lat_body(u_ref, z_ref, g_ref, h_ref, fch_ref, w_ref, o_ref, x_ref, yr_ref, yi_ref):
    for hf in range(2):
        x_ref[hf] = u_ref[:, hf * 128:(hf + 1) * 128]

    def stage1(t2, c):
        xs = jnp.concatenate([x_ref[hf, pl.ds(t2, GRID_H, stride=GRID_W), :] for hf in range(2)], axis=1)
        y = jnp.dot(g_ref[t2], xs.astype(BF16), preferred_element_type=F32)
        r0 = pl.multiple_of(t2 * GRID_W, GRID_W)
        for hf in range(2):
            yr_ref[hf, pl.ds(r0, GRID_W), :] = y[0:64, hf * 128:(hf + 1) * 128]
            yi_ref[hf, pl.ds(r0, GRID_W), :] = y[64:128, hf * 128:(hf + 1) * 128]
        return c

    lax.fori_loop(0, GRID_W, stage1, 0)

    def stage2(kb, c):
        yr = jnp.concatenate([yr_ref[hf, pl.ds(kb, GRID_W, stride=GRID_W), :] for hf in range(2)], axis=1)
        yi = jnp.concatenate([yi_ref[hf, pl.ds(kb, GRID_W, stride=GRID_W), :] for hf in range(2)], axis=1)
        st = jnp.concatenate([yr, yi], axis=0).astype(BF16)
        a = jnp.dot(h_ref[...], st, preferred_element_type=F32)
        for hf in range(2):
            yr_ref[hf, pl.ds(kb, GRID_W, stride=GRID_W), :] = a[0:64, hf * 128:(hf + 1) * 128]
            yi_ref[hf, pl.ds(kb, GRID_W, stride=GRID_W), :] = a[64:128, hf * 128:(hf + 1) * 128]
        return c

    lax.fori_loop(0, GRID_W, stage2, 0)

    def stage3(b, c):
        r0 = pl.multiple_of(b * TM, TM)
        ar = jnp.concatenate([yr_ref[hf, pl.ds(r0, TM), :] for hf in range(2)], axis=1)
        ai = jnp.concatenate([yi_ref[hf, pl.ds(r0, TM), :] for hf in range(2)], axis=1)
        f = bdot(ar, fch_ref[0:BR, :]) + bdot(ai, fch_ref[BR:2 * BR, :])
        o_ref[pl.ds(r0, TM), :] = bdot(f, w_ref[...]) * silu(z_ref[pl.ds(r0, TM), :])
        return c

    lax.fori_loop(0, DEC_SEQ // TM, stage3, 0)


def ft_lat(ft, g1, h2, fch2, ft_w):
    blk0 = N_CTX // DEC_SEQ
    return pl.pallas_call(
        _ft_lat_body,
        out_shape=SDS((N_LAT, BR), F32),
        grid=(DEC_BATCH,),
        in_specs=[BS((DEC_SEQ, BR), lambda i: (blk0 + i, 0)),
                  BS((DEC_SEQ, BR), lambda i: (blk0 + i, 1)),
                  BS((GRID_W, 128, GRID_H), lambda i: (0, 0, 0)),
                  BS((128, 128), lambda i: (0, 0)),
                  BS((2 * BR, BR), lambda i: (0, 0)),
                  BS((BR, BR), lambda i: (0, 0))],
        out_specs=BS((DEC_SEQ, BR), lambda i: (i, 0)),
        scratch_shapes=[pltpu.VMEM((2, DEC_SEQ, 128), F32)] * 3,
        compiler_params=_cparams("parallel"),
        name="ft_lat",
    )(ft, ft, g1, h2, fch2, ft_w)


def _ft_constants():
    c = np.arange(FT_HD)
    ang = 2.0 * np.pi * np.outer(c, c) / FT_HD
    eye4 = np.eye(BR // FT_HD)
    cc = np.kron(eye4, np.cos(ang)) / 8.0
    sc = np.kron(eye4, np.sin(ang)) / 8.0
    t = np.arange(SEQ)
    angt = 2.0 * np.pi * (np.outer(t, t) % SEQ) / SEQ
    fpos = np.concatenate([np.cos(angt), -np.sin(angt)], axis=1) / 16.0
    fch = np.concatenate([cc, sc], axis=1)
    kb = np.arange(GRID_W)[None, :, None]
    t1 = np.arange(GRID_H)[None, None, :]
    t2 = np.arange(GRID_W)[:, None, None]
    a1 = 2.0 * np.pi * ((kb * (GRID_W * t1 + t2)) % DEC_SEQ) / DEC_SEQ
    g1 = np.concatenate([np.cos(a1), -np.sin(a1)], axis=1) / 8.0
    a2 = 2.0 * np.pi * (np.outer(np.arange(GRID_W), np.arange(GRID_W)) % GRID_W) / GRID_W
    c2, s2 = np.cos(a2) / 8.0, np.sin(a2) / 8.0
    h2 = np.block([[c2, s2], [-s2, c2]])
    fch2 = np.concatenate([cc, sc], axis=0)
    as_bf = lambda a: jnp.asarray(a, F32).astype(BF16)
    return as_bf(fpos), as_bf(fch), as_bf(g1), as_bf(h2), as_bf(fch2)


S5_ROWS = N_TOK // S5_L
S5_TR = TM // S5_L
S5_CTX_C = SEQ // S5_L
S5_LAT_C = DEC_SEQ // S5_L
S5_LT = 2 * S5_G


def _s5_prep_body(ar_ref, ai_ref, ldt_ref, br_ref, bi_ref, btr_ref, bti_ref, cr_ref, ci_ref,
                  mt_ref, q4_ref, rt_ref, al_ref):
    L = S5_L
    m = lax.broadcasted_iota(jnp.int32, (2 * L, S5_N), 0).astype(F32)
    rts, qs, ds, als = [], [], [], []
    for d in range(2):
        a_re, a_im = ar_ref[0, d, 0], ai_ref[0, d, 0]
        dt = jnp.exp(ldt_ref[0, d, 0])
        xr, xi = a_re * dt, a_im * dt
        mag = jnp.exp(xr)
        ab_re, ab_im = mag * jnp.cos(xi), mag * jnp.sin(xi)
        den = a_re * a_re + a_im * a_im
        nr = ab_re - 1.0
        coef_re = (nr * a_re + ab_im * a_im) / den
        coef_im = (ab_im * a_re - nr * a_im) / den
        pw_re = jnp.exp(m * xr) * jnp.cos(m * xi)
        pw_im = jnp.exp(m * xr) * jnp.sin(m * xi)
        cq_re = pw_re * coef_re - pw_im * coef_im
        cq_im = pw_re * coef_im + pw_im * coef_re
        c_re, c_im = cr_ref[0, d, 0], ci_ref[0, d, 0]
        bt_re, bt_im = btr_ref[0, d, 0], bti_ref[0, d, 0]
        row = lambda x, e: x[e:e + 1, :]
        order = range(L) if d == 0 else range(L - 1, -1, -1)
        cp_re = jnp.concatenate([c_re * row(cq_re, e) - c_im * row(cq_im, e) for e in order], axis=0)
        cp_im = jnp.concatenate([c_re * row(cq_im, e) + c_im * row(cq_re, e) for e in order], axis=0)
        rts.append(hdot(cp_re, br_ref[0, d, 0]) - hdot(cp_im, bi_ref[0, d, 0]))
        inj = [L - 1 - i for i in range(L)] if d == 0 else list(range(L))
        q_re = jnp.concatenate([bt_re * row(cq_re, e) - bt_im * row(cq_im, e) for e in inj], axis=0)
        q_im = jnp.concatenate([bt_im * row(cq_re, e) + bt_re * row(cq_im, e) for e in inj], axis=0)
        qs.append((q_re, q_im))
        out = [j + 1 for j in range(L)] if d == 0 else [L - j for j in range(L)]
        d_re = jnp.concatenate([c_re * row(pw_re, e) - c_im * row(pw_im, e) for e in out], axis=0)
        d_im = jnp.concatenate([c_re * row(pw_im, e) + c_im * row(pw_re, e) for e in out], axis=0)
        ds.append((d_re, d_im))
        als.append((row(pw_re, L), row(pw_im, L)))
    pad = jnp.zeros(((L - 1) * S5_P, S5_P), F32)
    z = jnp.concatenate([pad, rts[0]], axis=0) + jnp.concatenate([rts[1], pad], axis=0)
    mt = jnp.concatenate([z[(L - 1 - i) * S5_P:(L - 1 - i) * S5_P + L * S5_P, :] for i in range(L)], axis=1)
    mt_ref[0, 0] = mt.astype(BF16)
    q4_ref[0, 0] = jnp.concatenate([qs[0][0], qs[1][0], qs[0][1], qs[1][1]], axis=1).astype(BF16)
    rt_ref[0, 0] = jnp.concatenate([ds[0][0], ds[1][0], -ds[0][1], -ds[1][1]], axis=1).astype(BF16)
    al_ref[0, 0] = jnp.concatenate([jnp.concatenate([als[0][0], als[1][0]], axis=1),
                                    jnp.concatenate([als[0][1], als[1][1]], axis=1)], axis=0)


def s5_prep(a_re, a_im, log_dt, b_re, b_im, c_re, c_im):
    vec = lambda x: x.reshape(DEPTH, 2, S5_G, 1, S5_N)
    ldt = jnp.broadcast_to(log_dt[..., None, None], (DEPTH, 2, S5_G, 1, S5_N))
    bt = lambda x: x.transpose(0, 1, 2, 4, 3)
    vspec = BS((1, 2, 1, 1, S5_N), lambda i: (i // S5_G, 0, i % S5_G, 0, 0))
    bspec = BS((1, 2, 1, S5_N, S5_P), lambda i: (i // S5_G, 0, i % S5_G, 0, 0))
    cspec = BS((1, 2, 1, S5_P, S5_N), lambda i: (i // S5_G, 0, i % S5_G, 0, 0))
    mat = SDS((DEPTH, S5_G, 256, 256), BF16)
    mspec = BS((1, 1, 256, 256), lambda i: (i // S5_G, i % S5_G, 0, 0))
    return pl.pallas_call(
        _s5_prep_body,
        out_shape=[mat, mat, mat, SDS((DEPTH, S5_G, 2, 128), F32)],
        grid=(DEPTH * S5_G,),
        in_specs=[vspec, vspec, vspec, bspec, bspec, cspec, cspec, cspec, cspec],
        out_specs=[mspec, mspec, mspec, BS((1, 1, 2, 128), lambda i: (i // S5_G, i % S5_G, 0, 0))],
        compiler_params=_cparams("parallel"),
        name="s5_prep",
    )(vec(a_re), vec(a_im), ldt, b_re, b_im, bt(b_re), bt(b_im), c_re, c_im)


def _block_transpose(arrs):
    blk = lax.broadcasted_iota(jnp.int32, arrs[0].shape, 1) // 16
    cur = list(arrs)
    for b in range(4):
        s = 16 << b
        hi = ((blk >> b) & 1) == 1
        nxt = list(cur)
        for x in range(16):
            if (x >> b) & 1:
                continue
            y = x | (1 << b)
            nxt[x] = jnp.where(hi, pltpu.roll(cur[y], s, 1), cur[x])
            nxt[y] = jnp.where(hi, cur[y], pltpu.roll(cur[x], 256 - s, 1))
        cur = nxt
    return cur


def _s5_in_body(s5_ref, mt_ref, q4_ref, y_ref, e_ref, x_ref):
    for hf in range(2):
        x_ref[hf] = s5_ref[:, hf * 128:(hf + 1) * 128]
    xs = [jnp.concatenate([x_ref[hf, pl.ds(i, S5_TR, stride=S5_L), :] for hf in range(2)], axis=1)
          for i in range(S5_L)]
    us = _block_transpose(xs)
    for g in range(S5_G):
        ub = us[g].astype(BF16)
        y_ref[:, g * 256:(g + 1) * 256] = lax.dot_general(ub, mt_ref[g], (((1,), (1,)), ((), ())),
                                                          preferred_element_type=F32)
        e = jnp.dot(ub, q4_ref[g], preferred_element_type=F32)
        e_ref[2 * g] = e[:, 0:128]
        e_ref[2 * g + 1] = e[:, 128:256]


def s5_in(s5, mt, q4):
    wspec = BS((S5_G, 256, 256), lambda t: (0, 0, 0))
    return pl.pallas_call(
        _s5_in_body,
        out_shape=[SDS((S5_ROWS, S5_G * 256), F32), SDS((S5_LT, S5_ROWS, 128), F32)],
        grid=(N_TOK // TM,),
        in_specs=[BS((TM, BR), lambda t: (t, 0)), wspec, wspec],
        out_specs=[BS((S5_TR, S5_G * 256), lambda t: (t, 0)), BS((S5_LT, S5_TR, 128), lambda t: (0, t, 0))],
        scratch_shapes=[pltpu.VMEM((2, TM, 128), F32)],
        compiler_params=_cparams("parallel"),
        name="s5_in",
    )(s5, mt, q4)


def _s5_chunk_scan_body(e_ref, al_ref, s0_ref, spf_ref, spb_ref, fin_ref):
    def update(g, s_re, s_im, e_re, e_im):
        a_re, a_im = al_ref[g, 0:1, :], al_ref[g, 1:2, :]
        return a_re * s_re - a_im * s_im + e_re, a_re * s_im + a_im * s_re + e_im

    fwd_c = lax.broadcasted_iota(jnp.int32, (BATCH, 128), 1) < S5_N

    def ctx_group(g, carry):
        s_re = jnp.zeros((BATCH, 128), F32)
        s_im = jnp.zeros((BATCH, 128), F32)
        for c in range(S5_CTX_C):
            rf = pl.ds(c, BATCH, stride=S5_CTX_C)
            rb = pl.ds(S5_CTX_C - 1 - c, BATCH, stride=S5_CTX_C)
            spf_ref[2 * g, rf, :] = s_re
            spf_ref[2 * g + 1, rf, :] = s_im
            spb_ref[2 * g, rb, :] = s_re
            spb_ref[2 * g + 1, rb, :] = s_im
            e_re = jnp.where(fwd_c, e_ref[2 * g, rf, :], e_ref[2 * g, rb, :])
            e_im = jnp.where(fwd_c, e_ref[2 * g + 1, rf, :], e_ref[2 * g + 1, rb, :])
            s_re, s_im = update(g, s_re, s_im, e_re, e_im)
        fin_ref[2 * g] = s_re
        fin_ref[2 * g + 1] = s_im
        return carry

    lax.fori_loop(0, S5_G, ctx_group, 0)

    row0 = BATCH * S5_CTX_C
    fwd_l = lax.broadcasted_iota(jnp.int32, (DEC_BATCH, 128), 1) < S5_N

    def lat_step(c, state):
        rf = pl.ds(row0 + c, DEC_BATCH, stride=S5_LAT_C)
        rb = pl.ds(row0 + S5_LAT_C - 1 - c, DEC_BATCH, stride=S5_LAT_C)
        new = []
        for g in range(S5_G):
            s_re, s_im = state[2 * g], state[2 * g + 1]
            spf_ref[2 * g, rf, :] = s_re
            spf_ref[2 * g + 1, rf, :] = s_im
            spb_ref[2 * g, rb, :] = s_re
            spb_ref[2 * g + 1, rb, :] = s_im
            e_re = jnp.where(fwd_l, e_ref[2 * g, rf, :], e_ref[2 * g, rb, :])
            e_im = jnp.where(fwd_l, e_ref[2 * g + 1, rf, :], e_ref[2 * g + 1, rb, :])
            new.extend(update(g, s_re, s_im, e_re, e_im))
        return tuple(new)

    lax.fori_loop(0, S5_LAT_C, lat_step, tuple(s0_ref[t] for t in range(S5_LT)))


def s5_chunk_scan(e3, al, s0_lat):
    sp = SDS((S5_LT, S5_ROWS, 128), F32)
    return pl.pallas_call(
        _s5_chunk_scan_body,
        out_shape=[sp, sp, SDS((S5_LT, BATCH, 128), F32)],
        compiler_params=pltpu.CompilerParams(vmem_limit_bytes=VMEM_LIMIT),
        name="s5_chunk_scan",
    )(e3, al, s0_lat)


def _s5_fin_body(y_ref, spf_ref, spb_ref, rt_ref, o_ref, t_ref):
    fwd = lax.broadcasted_iota(jnp.int32, (S5_TR, 128), 1) < S5_N
    ys = []
    for g in range(S5_G):
        s_re = jnp.where(fwd, spf_ref[2 * g], spb_ref[2 * g])
        s_im = jnp.where(fwd, spf_ref[2 * g + 1], spb_ref[2 * g + 1])
        sp = jnp.concatenate([s_re, s_im], axis=1).astype(BF16)
        ys.append(y_ref[:, g * 256:(g + 1) * 256]
                  + lax.dot_general(sp, rt_ref[g], (((1,), (1,)), ((), ())), preferred_element_type=F32))
    xs = _block_transpose(ys)
    for j in range(S5_L):
        for hf in range(2):
            t_ref[hf, pl.ds(j, S5_TR, stride=S5_L), :] = xs[j][:, hf * 128:(hf + 1) * 128]
    o_ref[...] = jnp.concatenate([t_ref[0], t_ref[1]], axis=1)


def s5_fin(yi, spf, spb, rt):
    lt = BS((S5_LT, S5_TR, 128), lambda t: (0, t, 0))
    return pl.pallas_call(
        _s5_fin_body,
        out_shape=SDS((N_TOK, BR), F32),
        grid=(N_TOK // TM,),
        in_specs=[BS((S5_TR, S5_G * 256), lambda t: (t, 0)), lt, lt, BS((S5_G, 256, 256), lambda t: (0, 0, 0))],
        out_specs=BS((TM, BR), lambda t: (t, 0)),
        scratch_shapes=[pltpu.VMEM((2, TM, 128), F32)],
        compiler_params=_cparams("parallel"),
        name="s5_fin",
    )(yi, spf, spb, rt)


def s5_branch(s5, mt, q4, rt, al, state_l):
    yi, e3 = s5_in(s5, mt, q4)
    s0 = state_l.transpose(3, 2, 0, 1, 4).reshape(S5_LT, DEC_BATCH, 2 * S5_N)
    spf, spb, fin = s5_chunk_scan(e3, al, s0)
    y = s5_fin(yi, spf, spb, rt)
    fin = fin.reshape(S5_G, 2, BATCH, 2, S5_N).transpose(2, 3, 1, 0, 4)
    return y, fin


DN_TILE = 256
DN_HALO = 8


N_SEG = N_TOK // DN_TILE
assert N_CTX == DEC_SEQ


def _split3(x):
    x1 = x.astype(BF16)
    r1 = x - x1.astype(F32)
    x2 = r1.astype(BF16)
    return x1, x2, (r1 - x2.astype(F32)).astype(BF16)


def dot3(a, b, exact):
    if exact == 'b':
        return sum(jnp.dot(p, b.astype(BF16), preferred_element_type=F32) for p in _split3(a))
    return sum(jnp.dot(a.astype(BF16), p, preferred_element_type=F32) for p in _split3(b))


def _dn_conv_body(x_ref, prev_ref, next_ref, ba_ref, w_ref, gp_ref, ex_ref, o_ref, gb_ref, pad_ref):
    i = pl.program_id(0)
    tiles_ctx = N_CTX // DN_TILE
    tiles_seq = DEC_SEQ // DN_TILE
    j = (i - tiles_ctx) % tiles_seq
    first = jnp.logical_or(i < tiles_ctx, j == 0)
    last = jnp.logical_or(i < tiles_ctx, j == tiles_seq - 1)
    pad_ref[0:DN_HALO, :] = jnp.where(first, 0.0, prev_ref[...])
    pad_ref[DN_HALO:DN_HALO + DN_TILE, :] = x_ref[...]
    pad_ref[DN_HALO + DN_TILE:2 * DN_HALO + DN_TILE, :] = jnp.where(last, 0.0, next_ref[...])
    acc = jnp.zeros((DN_TILE, 3 * BR), F32)
    for t in range(CONV_K):
        acc = acc + pad_ref[pl.ds(DN_HALO - CONV_K // 2 + t, DN_TILE), :] * w_ref[t:t + 1, :]
    o_ref[...] = silu(acc)
    raw = ba_ref[...]
    lane = lax.broadcasted_iota(jnp.int32, raw.shape, 1)
    xa = raw + gp_ref[1:2, :]
    sp = jnp.maximum(xa, 0.0) + jnp.log1p(jnp.exp(-jnp.abs(xa)))
    gates = jnp.where(lane < 2 * DN_HEADS, jax.nn.sigmoid(raw), -jnp.exp(gp_ref[0:1, :]) * sp)
    r = lax.broadcasted_iota(jnp.int32, (DN_TILE, DN_TILE), 0)
    c = lax.broadcasted_iota(jnp.int32, (DN_TILE, DN_TILE), 1)
    same = (r // CHUNK) == (c // CHUNK)
    ex = dot3(gates, ex_ref[...], 'b')
    gb_ref[:, 0:2 * BR] = ex[:, 0:2 * BR]
    gb_ref[:, 2 * BR:3 * BR] = dot3(jnp.logical_and(same, c <= r), ex[:, 2 * BR:3 * BR], 'a')
    gb_ref[:, 3 * BR:4 * BR] = dot3(jnp.logical_and(same, c >= r), ex[:, 3 * BR:4 * BR], 'a')


def _gate_expand():
    e = np.zeros((128, 4 * BR), np.float32)
    for blk in range(4):
        for h in range(DN_HEADS):
            e[blk * DN_HEADS + h, blk * BR + h * DN_HD:blk * BR + (h + 1) * DN_HD] = 1.0
    return jnp.asarray(e, BF16)


def dn_conv(qkv, ba, conv_w, gate_p):
    nt = N_TOK // DN_TILE
    per = DN_TILE // DN_HALO
    nhb = N_TOK // DN_HALO
    return pl.pallas_call(
        _dn_conv_body,
        out_shape=[SDS((N_TOK, 3 * BR), F32), SDS((N_TOK, 4 * BR), F32)],
        grid=(nt,),
        in_specs=[BS((DN_TILE, 3 * BR), lambda i: (i, 0)),
                  BS((DN_HALO, 3 * BR), lambda i: (jnp.maximum(i * per - 1, 0), 0)),
                  BS((DN_HALO, 3 * BR), lambda i: (jnp.minimum((i + 1) * per, nhb - 1), 0)),
                  BS((DN_TILE, 128), lambda i: (i, 0)),
                  BS((8, 3 * BR), lambda i: (0, 0)),
                  BS((8, 128), lambda i: (0, 0)),
                  BS((128, 4 * BR), lambda i: (0, 0))],
        out_specs=[BS((DN_TILE, 3 * BR), lambda i: (i, 0)), BS((DN_TILE, 4 * BR), lambda i: (i, 0))],
        scratch_shapes=[pltpu.VMEM((DN_TILE + 2 * DN_HALO, 3 * BR), F32)],
        compiler_params=_cparams("parallel"),
        name="dn_conv",
    )(qkv, qkv, qkv, ba, conv_w, gate_p, _gate_expand())


def _head_block_mask():
    r = lax.broadcasted_iota(jnp.int32, (BR, BR), 0) // DN_HD
    c = lax.broadcasted_iota(jnp.int32, (BR, BR), 1) // DN_HD
    return r == c


def _split2(x):
    hi = x.astype(BF16)
    return hi, (x - hi.astype(F32)).astype(BF16)


def _dn_local_body(qkv_ref, gb_ref, *out_refs):
    f_refs, b_refs = out_refs[0:6], out_refs[6:12]
    ncb = DN_TILE // CHUNK
    nb = 2 * ncb
    bmask = _head_block_mask()
    ri = lax.broadcasted_iota(jnp.int32, (nb, CHUNK, BR), 1)
    cj = lax.broadcasted_iota(jnp.int32, (nb, CHUNK, BR), 2) % DN_HD
    bwd = lax.broadcasted_iota(jnp.int32, (nb, CHUNK, BR), 0) >= ncb
    eye = cj == ri
    incl = jnp.logical_or(jnp.logical_and(bwd, cj >= ri), jnp.logical_and(jnp.logical_not(bwd), cj <= ri))
    strict = jnp.logical_and(incl, jnp.logical_not(eye))
    ones_blk = bmask.astype(BF16)

    def chunks(x):
        return x.reshape(ncb, CHUNK, BR)

    def both(x):
        return jnp.concatenate([x, x], axis=0)

    def head_sum(x):
        return dot3(x.reshape(-1, BR), ones_blk, 'b').reshape(x.shape)

    def bd(x):
        return jnp.where(bmask, jnp.concatenate([x] * DN_HEADS, axis=1), jnp.zeros((), x.dtype))

    def bmm(a, b):
        return jnp.einsum('bij,bjk->bik', a, b, preferred_element_type=F32)

    q = chunks(qkv_ref[:, 0:BR])
    k = chunks(qkv_ref[:, BR:2 * BR])
    v = both(chunks(qkv_ref[:, 2 * BR:3 * BR]))
    q = q * lax.rsqrt(head_sum(q * q) + EPS) * (DN_HD ** -0.5)
    k = k * lax.rsqrt(head_sum(k * k) + EPS)
    kq = jnp.einsum('bik,bjk->bij', jnp.concatenate([k, q], axis=1).astype(BF16), bd(k.astype(BF16)),
                    preferred_element_type=F32)
    kk, qk = both(kq[:, 0:CHUNK]), both(kq[:, CHUNK:2 * CHUNK])
    q, k = both(q), both(k)
    beta = jnp.concatenate([chunks(gb_ref[:, 0:BR]), chunks(gb_ref[:, BR:2 * BR])], axis=0)
    gc = jnp.concatenate([chunks(gb_ref[:, 2 * BR:3 * BR]), chunks(gb_ref[:, 3 * BR:4 * BR])], axis=0)
    crow = jnp.sum(jnp.where(eye, gc, 0.0), axis=1, keepdims=True)
    decay = jnp.where(incl, jnp.exp(jnp.where(incl, gc - crow, 0.0)), 0.0)
    a = jnp.where(strict, kk * decay * beta, 0.0)
    tinv = jnp.where(eye, 1.0, 0.0) - a
    pw = a
    pw_bd = bd(pw.astype(BF16))
    for _ in range(5):
        pw = bmm(pw.astype(BF16), pw_bd)
        pw_bd = bd(pw.astype(BF16))
        tinv = tinv + bmm(tinv.astype(BF16), pw_bd)
    egc = jnp.exp(gc)
    t_hi, t_lo = _split2(tinv)

    def solve(rhs):
        r_hi, r_lo = _split2(rhs)
        r_hi, r_lo = bd(r_hi), bd(r_lo)
        return bmm(t_hi, r_hi) + bmm(t_hi, r_lo) + bmm(t_lo, r_hi)

    bwd_row = lax.broadcasted_iota(jnp.int32, (nb, 1, BR), 0) >= ncb
    g_last = jnp.where(bwd_row, gc[:, 0:1], gc[:, CHUNK - 1:CHUNK])
    eg = jnp.exp(g_last)
    outs = (solve(k * (beta * egc)), solve(v * beta), q * egc, k * jnp.exp(g_last - gc),
            jnp.where(incl, qk * decay, 0.0))
    for d, refs in enumerate((f_refs, b_refs)):
        for ref, x in zip(refs[0:5], outs):
            ref[...] = x[d * ncb:(d + 1) * ncb].reshape(DN_TILE, BR).astype(ref.dtype)
        refs[5][0] = jnp.concatenate([eg[d * ncb:(d + 1) * ncb, 0], jnp.zeros((8 - ncb, BR), F32)], axis=0)


_DN_LOCAL_DTYPES = (BF16, F32, BF16, BF16, BF16)


def dn_local(qkvc, gb):
    tok = BS((DN_TILE, BR), lambda i: (i, 0))
    shapes = [SDS((N_TOK, BR), dt) for dt in _DN_LOCAL_DTYPES] + [SDS((N_SEG, 8, BR), F32)]
    specs = [tok] * len(_DN_LOCAL_DTYPES) + [BS((1, 8, BR), lambda i: (i, 0, 0))]
    outs = pl.pallas_call(
        _dn_local_body,
        out_shape=shapes * 2,
        grid=(N_SEG,),
        in_specs=[BS((DN_TILE, 3 * BR), lambda i: (i, 0)), BS((DN_TILE, 4 * BR), lambda i: (i, 0))],
        out_specs=specs * 2,
        compiler_params=_cparams("parallel"),
        name="dn_local",
    )(qkvc, gb)
    return outs[0:6], outs[6:12]


def _dn_advance(chains, s, bmask):
    ncb = DN_TILE // CHUNK
    bmm = lambda a, b: jnp.einsum('bij,bjk->bik', a, b, preferred_element_type=F32)
    steps = []
    for t in range(ncb):
        cs = [ncb - 1 - t if rev else t for _, rev in chains]
        w, u, qt, kt, aqk, eg = [jnp.stack([load(k, c) for (load, _), c in zip(chains, cs)]) for k in range(6)]
        sb = s.astype(BF16)
        v_new = u - bmm(w, sb)
        vb = v_new.astype(BF16)
        v_bd = jnp.where(bmask, jnp.concatenate([vb] * DN_HEADS, axis=1), jnp.zeros((), BF16))
        o = bmm(qt, sb) + bmm(aqk, v_bd)
        upd = jnp.einsum('btk,btv->bkv', kt, vb, preferred_element_type=F32)
        s = s * eg + jnp.where(bmask, upd, 0.0)
        steps.append((cs, o))
    return s, steps


def _dn_seq_ctx_body(*refs):
    nseq = 4
    f_in, b_in = refs[0:6], refs[6:12]
    of_ref, ob_ref, finf_ref, finb_ref = refs[12:16]
    bmask = _head_block_mask()

    def loader(in_refs, q):
        def load(k, c):
            if k == 5:
                return in_refs[5][q, c:c + 1, :]
            return in_refs[k][q * DN_TILE + c * CHUNK:q * DN_TILE + (c + 1) * CHUNK, :]
        return load

    chains = [(loader(f_in, q), False) for q in range(nseq)] + [(loader(b_in, q), True) for q in range(nseq)]
    s, steps = _dn_advance(chains, jnp.zeros((2 * nseq, BR, BR), F32), bmask)
    for cs, o in steps:
        for i, c in enumerate(cs):
            o_ref, q = (of_ref, i) if i < nseq else (ob_ref, i - nseq)
            o_ref[q * DN_TILE + c * CHUNK:q * DN_TILE + (c + 1) * CHUNK, :] = o[i]
    finf_ref[...] = s[0:nseq]
    finb_ref[...] = s[nseq:2 * nseq]


def dn_seq_ctx(loc_f, loc_b):
    nseq = 4
    tok = BS((nseq * DN_TILE, BR), lambda i: (i, 0))
    specs = [tok] * len(_DN_LOCAL_DTYPES) + [BS((nseq, 8, BR), lambda i: (i, 0, 0))]
    st = BS((nseq, BR, BR), lambda i: (i, 0, 0))
    return pl.pallas_call(
        _dn_seq_ctx_body,
        out_shape=[SDS((N_CTX, BR), F32), SDS((N_CTX, BR), F32), SDS((BATCH, BR, BR), F32), SDS((BATCH, BR, BR), F32)],
        grid=(BATCH // nseq,),
        in_specs=specs * 2,
        out_specs=[tok, tok, st, st],
        compiler_params=_cparams("parallel"),
        name="dn_seq_ctx",
    )(*loc_f, *loc_b)


def _dn_seq_lat_body(*refs):
    f_in, b_in = refs[0:6], refs[6:12]
    s0_ref, of_ref, ob_ref, s_ref = refs[12:16]

    @pl.when(pl.program_id(0) == 0)
    def _():
        s_ref[...] = s0_ref[...]

    def loader(in_refs, q):
        def load(k, c):
            if k == 5:
                return in_refs[5][1 + q, 0, c:c + 1, :]
            return in_refs[k][1 + q, c * CHUNK:(c + 1) * CHUNK, :]
        return load

    chains = ([(loader(f_in, q), False) for q in range(DEC_BATCH)]
              + [(loader(b_in, q), True) for q in range(DEC_BATCH)])
    s, steps = _dn_advance(chains, s_ref[...], _head_block_mask())
    s_ref[...] = s
    for cs, o in steps:
        for i, c in enumerate(cs):
            o_ref, q = (of_ref, i) if i < DEC_BATCH else (ob_ref, i - DEC_BATCH)
            o_ref[q, c * CHUNK:(c + 1) * CHUNK, :] = o[i]


def dn_seq_lat(loc_f, loc_b, s0):
    nseg = DEC_SEQ // DN_TILE
    grp = lambda a: a.reshape((N_COND, nseg) + a.shape[1:])

    def specs(m):
        return ([BS((N_COND, DN_TILE, BR), lambda g: (0, m(g), 0))] * len(_DN_LOCAL_DTYPES)
                + [BS((N_COND, 1, 8, BR), lambda g: (0, m(g), 0, 0))])

    fwd = lambda g: g
    bwd = lambda g: nseg - 1 - g
    views = lambda loc: [a.reshape(N_COND, DEC_SEQ, BR) for a in loc[0:5]] + [grp(loc[5])]
    out = SDS((DEC_BATCH, DEC_SEQ, BR), F32)
    return pl.pallas_call(
        _dn_seq_lat_body,
        out_shape=[out, out],
        grid=(nseg,),
        in_specs=specs(fwd) + specs(bwd) + [BS((2 * DEC_BATCH, BR, BR), lambda g: (0, 0, 0))],
        out_specs=[BS((DEC_BATCH, DN_TILE, BR), lambda g: (0, fwd(g), 0)),
                   BS((DEC_BATCH, DN_TILE, BR), lambda g: (0, bwd(g), 0))],
        scratch_shapes=[pltpu.VMEM((2 * DEC_BATCH, BR, BR), F32)],
        compiler_params=_cparams("arbitrary"),
        name="dn_seq_lat",
    )(*views(loc_f), *views(loc_b), s0)


def dn_branch(qkv, ba, conv_w, gate_p, s0_lat):
    qkvc, gb = dn_conv(qkv, ba, conv_w, gate_p)
    loc_f, loc_b = dn_local(qkvc, gb)
    eye_h = jnp.eye(DN_HEADS, dtype=F32)
    s0 = s0_lat.transpose(1, 0, 2, 3, 4).reshape(2 * DEC_BATCH, DN_HEADS, DN_HD, DN_HD)
    s0 = jnp.einsum('shkv,hg->shkgv', s0, eye_h).reshape(2 * DEC_BATCH, BR, BR)
    of_c, ob_c, fin_f, fin_b = dn_seq_ctx(loc_f, loc_b)
    of_l, ob_l = dn_seq_lat(loc_f, loc_b, s0)
    unblock = lambda s: jnp.einsum('shkgv,hg->shkv', s.reshape(BATCH, DN_HEADS, DN_HD, DN_HEADS, DN_HD), eye_h)
    fin = jnp.stack([unblock(fin_f), unblock(fin_b)], axis=1)
    return (of_c, ob_c), (of_l.reshape(N_LAT, BR), ob_l.reshape(N_LAT, BR)), fin


def _outproj_body(x_ref, mod_ref, ypc_ref, ypl_ref, dofc_ref, dofl_ref, dobc_ref, dobl_ref, dnz_ref, s5y_ref, s5_ref,
                  yfc_ref, yfl_ref, d_ref, gw_ref, gb_ref, dng_ref, w_ref, fg_ref, o_ref, *, final):
    is_ctx = pl.program_id(0) < N_CTX // TM
    y_pool = jnp.where(is_ctx, ypc_ref[...], ypl_ref[...])
    y_ft = jnp.where(is_ctx, yfc_ref[...], yfl_ref[...])
    gate = mod_ref[0][:, 2 * D_MODEL:3 * D_MODEL]
    o = jnp.where(is_ctx, dofc_ref[...] + dobc_ref[...], dofl_ref[...] + dobl_ref[...])
    head_mean = jnp.where(_head_block_mask(), 1.0 / DN_HD, 0.0)
    y_dn = o * lax.rsqrt(dot3(o * o, head_mean, 'b') + EPS) * dng_ref[...] * silu(dnz_ref[...])
    y = s5y_ref[...] + d_ref[...] * s5_ref[:, 0:BR]
    y = jax.nn.gelu(y)
    y = y * jax.nn.sigmoid(bdot(y, gw_ref[...]) + gb_ref[...])
    y_s5 = y * silu(s5_ref[:, BR:2 * BR])
    acc = bdot(y_pool, w_ref[0:BR, :])
    acc = acc + bdot(y_dn, w_ref[BR:2 * BR, :])
    acc = acc + bdot(y_s5, w_ref[2 * BR:3 * BR, :])
    acc = acc + bdot(y_ft, w_ref[3 * BR:4 * BR, :])
    xn = x_ref[...] + gate * acc
    if final:
        xn = xn * lax.rsqrt(jnp.mean(xn * xn, axis=-1, keepdims=True) + EPS) * fg_ref[...]
    o_ref[...] = xn


def outproj(x, mod, yp_ctx, yp_lat, dn_ctx, dn_lat, dn_z, s5_y, s5, yf_ctx, yf_lat, s5_d, glu_w, glu_b, dn_g, w_out,
            final_g, final):
    row = lambda w: BS((TM, w), lambda i: (i, 0))
    full = lambda a, b: BS((a, b), lambda i: (0, 0))
    tiles_ctx = N_CTX // TM
    ctx_row = BS((TM, BR), lambda i: (jnp.minimum(i, tiles_ctx - 1), 0))
    lat_row = BS((TM, BR), lambda i: (jnp.maximum(i - tiles_ctx, 0), 0))
    return pl.pallas_call(
        functools.partial(_outproj_body, final=final),
        out_shape=SDS((N_TOK, D_MODEL), F32),
        grid=(N_TOK // TM,),
        in_specs=[row(D_MODEL),
                  BS((1, 1, 3 * D_MODEL), lambda i: (_cond_index(i), 0, 0)),
                  ctx_row, lat_row, ctx_row, lat_row, ctx_row, lat_row, row(BR), row(BR), row(2 * BR), ctx_row, lat_row,
                  full(1, BR), full(BR, BR), full(1, BR), full(1, BR), full(D_MODEL, D_MODEL), full(1, D_MODEL)],
        out_specs=row(D_MODEL),
        compiler_params=_cparams("parallel"),
        name="outproj",
    )(x, mod, yp_ctx, yp_lat, dn_ctx[0], dn_lat[0], dn_ctx[1], dn_lat[1], dn_z, s5_y, s5, yf_ctx, yf_lat,
      s5_d, glu_w, glu_b, dn_g, w_out, final_g)


def _permute_w_in(w_in):
    main = jnp.concatenate([w_in[:, 0:1536], w_in[:, 1552:2576]], axis=1)
    ba = jnp.pad(w_in[:, 1536:1552], ((0, 0), (0, 112)))
    return jnp.concatenate([main, ba], axis=1).astype(BF16)


def kernel(x_prompt, x_sample, c, state_delta, state_s5, c_ctx, w_ada, b_ada, norm_g, w_in, pool_w, pool_scale,
           dn_conv, dn_a_log, dn_dt_bias, dn_norm_g, s5_a_re, s5_a_im, s5_log_dt, s5_b_re, s5_b_im, s5_c_re,
           s5_c_im, s5_d, s5_glu_w, s5_glu_b, ft_w, w_out, final_g):
    x = jnp.concatenate([x_prompt.astype(F32).reshape(N_CTX, D_MODEL),
                         x_sample.astype(F32).reshape(N_LAT, D_MODEL)], axis=0)
    cond8 = jnp.concatenate([c_ctx.astype(F32)[None], c.astype(F32),
                             jnp.zeros((8 - N_COND, D_MODEL), F32)], axis=0)
    ada = ada_all(cond8, w_ada, b_ada)
    pm_ctx, inv_ctx, pm_lat, inv_lat = _pool_constants()
    fpos, fch, g1, h2, fch2 = _ft_constants()
    s5_mt, s5_q4, s5_rt, s5_al = s5_prep(s5_a_re, s5_a_im, s5_log_dt, s5_b_re, s5_b_im, s5_c_re, s5_c_im)
    new_dn, new_s5 = [], []
    for l in range(DEPTH):
        mod = ada[l, 0:N_COND].reshape(N_COND, 1, 3 * D_MODEL)
        pool, qkv, dn_z, s5, ft, ba = inproj(x, mod, norm_g[l].reshape(1, D_MODEL), _permute_w_in(w_in[l]))

        w_bd = jax.scipy.linalg.block_diag(*[pool_w[l, g] for g in range(4)]).astype(BF16)
        sc = pool_scale[l].reshape(1, BR)
        yp_ctx = pool_branch(pool, pm_ctx, inv_ctx, w_bd, sc, False)
        yp_lat = pool_branch(pool, pm_lat, inv_lat, w_bd, sc, True)

        ftw = ft_w[l].astype(BF16)
        yf_ctx = ft_ctx(ft, fpos, fch, ftw)
        yf_lat = ft_lat(ft, g1, h2, fch2, ftw)

        s5_y, fin_s5 = s5_branch(s5, s5_mt[l], s5_q4[l], s5_rt[l], s5_al[l], state_s5[:, l].astype(F32))
        new_s5.append(fin_s5)

        conv_w = jnp.pad(dn_conv[l], ((0, 8 - CONV_K), (0, 0)))
        gate_p = jnp.zeros((8, 128), F32)
        gate_p = gate_p.at[0, 8:16].set(dn_a_log[l].reshape(8)).at[1, 8:16].set(dn_dt_bias[l].reshape(8))
        dn_ctx, dn_lat, fin_dn = dn_branch(qkv, ba, conv_w, gate_p, state_delta[:, l].astype(F32))
        new_dn.append(fin_dn)

        x = outproj(x, mod, yp_ctx, yp_lat, dn_ctx, dn_lat, dn_z, s5_y, s5, yf_ctx, yf_lat, s5_d[l].reshape(1, BR),
                    s5_glu_w[l].astype(BF16), s5_glu_b[l].reshape(1, BR),
                    jnp.tile(dn_norm_g[l], DN_HEADS).reshape(1, BR), w_out[l].astype(BF16),
                    final_g.reshape(1, D_MODEL), l == DEPTH - 1)

    y_prompt = x[:N_CTX].reshape(BATCH, SEQ, D_MODEL).astype(x_prompt.dtype)
    y_sample = x[N_CTX:].reshape(DEC_BATCH, DEC_SEQ, D_MODEL).astype(x_sample.dtype)
    new_state_delta = jnp.stack(new_dn, axis=1).astype(state_delta.dtype)
    new_state_s5 = jnp.stack(new_s5, axis=1).astype(state_s5.dtype)
    return (y_prompt, y_sample, new_state_delta, new_state_s5)
```

```python
import functools
import math

import numpy as np
import jax
import jax.numpy as jnp
from jax import lax
from jax.experimental import pallas as pl
from jax.experimental.pallas import tpu as pltpu

F32 = jnp.float32
BF16 = jnp.bfloat16

D_MODEL = 1024
BATCH = 16
SEQ = 256
DEPTH = 4
DEC_BATCH = 2
DEC_SEQ = 4096
GRID_W = 64
GRID_H = DEC_SEQ // GRID_W
BR = 256
POOL_WINDOWS = (2, 4, 8, 16)
POOL_GD = 64
DN_HEADS = 4
DN_HD = 64
CONV_K = 5
CHUNK = 64
S5_P = 16
S5_G = 16
S5_N = 64
S5_L = 16
FT_HD = 64
EPS = 1e-6

N_CTX = BATCH * SEQ
N_LAT = DEC_BATCH * DEC_SEQ
N_TOK = N_CTX + N_LAT
N_COND = 1 + DEC_BATCH
TM = 512
W_IN_COLS = 2688
VMEM_LIMIT = 56 * 1024 * 1024

SDS = jax.ShapeDtypeStruct
BS = pl.BlockSpec


def _cparams(*sem):
    return pltpu.CompilerParams(dimension_semantics=sem, vmem_limit_bytes=VMEM_LIMIT)


def bdot(a, b):
    return jnp.dot(a.astype(BF16), b.astype(BF16), preferred_element_type=F32)


def hdot(a, b):
    return jnp.dot(a, b, preferred_element_type=F32, precision=lax.Precision.HIGHEST)


def silu(x):
    return x * jax.nn.sigmoid(x)


def _cond_index(i):
    tiles_ctx = N_CTX // TM
    tiles_seq = DEC_SEQ // TM
    return jnp.where(i < tiles_ctx, 0, 1 + (i - tiles_ctx) // tiles_seq)


def _ada_body(c_ref, w_ref, b_ref, o_ref):
    o_ref[0] = hdot(silu(c_ref[...]), w_ref[0]) + b_ref[0]


def ada_all(cond8, w_ada, b_ada):
    tn = 512
    return pl.pallas_call(
        _ada_body,
        out_shape=SDS((DEPTH, 8, 3 * D_MODEL), F32),
        grid=(DEPTH, 3 * D_MODEL // tn),
        in_specs=[BS((8, D_MODEL), lambda l, j: (0, 0)),
                  BS((1, D_MODEL, tn), lambda l, j: (l, 0, j)),
                  BS((1, 1, tn), lambda l, j: (l, 0, j))],
        out_specs=BS((1, 8, tn), lambda l, j: (l, 0, j)),
        compiler_params=_cparams("parallel", "parallel"),
        name="ada",
    )(cond8, w_ada, b_ada.reshape(DEPTH, 1, 3 * D_MODEL))


def _inproj_body(x_ref, mod_ref, g_ref, w_ref, pool_ref, qkv_ref, dnz_ref, s5_ref, ft_ref, ba_ref):
    x = x_ref[...]
    m = mod_ref[0]
    shift = m[:, 0:D_MODEL]
    scale = m[:, D_MODEL:2 * D_MODEL]
    xn = x * lax.rsqrt(jnp.mean(x * x, axis=-1, keepdims=True) + EPS) * g_ref[...]
    h = (xn * (1.0 + scale) + shift).astype(BF16)

    def proj(lo, hi):
        return jnp.dot(h, w_ref[:, lo:hi], preferred_element_type=F32)

    pool_ref[...] = proj(0, 512)
    qkv_ref[...] = proj(512, 1280)
    dnz_ref[...] = proj(1280, 1536)
    s5_ref[...] = proj(1536, 2048)
    ft_ref[...] = proj(2048, 2560)
    ba_ref[...] = proj(2560, 2688)


def inproj(x, mod, norm_g, w_in_p):
    widths = (512, 768, 256, 512, 512, 128)
    return pl.pallas_call(
        _inproj_body,
        out_shape=[SDS((N_TOK, w), F32) for w in widths],
        grid=(N_TOK // TM,),
        in_specs=[BS((TM, D_MODEL), lambda i: (i, 0)),
                  BS((1, 1, 3 * D_MODEL), lambda i: (_cond_index(i), 0, 0)),
                  BS((1, D_MODEL), lambda i: (0, 0)),
                  BS((D_MODEL, W_IN_COLS), lambda i: (0, 0))],
        out_specs=[BS((TM, w), lambda i: (i, 0)) for w in widths],
        compiler_params=_cparams("parallel"),
        name="inproj",
    )(x, mod, norm_g, w_in_p)


def _pool_body(u_ref, z_ref, pm_ref, inv_ref, w_ref, sc_ref, o_ref, *scratch, two_d):
    nblk = u_ref.shape[0] // 256
    if two_d:
        pad_ref, v_ref = scratch
        halo = 8 * GRID_W
        pad_ref[0:halo, :] = jnp.zeros((halo, BR), F32)
        pad_ref[halo + DEC_SEQ:2 * halo + DEC_SEQ, :] = jnp.zeros((halo, BR), F32)
        pad_ref[halo:halo + DEC_SEQ, :] = u_ref[...]
        lane = lax.broadcasted_iota(jnp.int32, (GRID_W, 128), 1)

        def row_body(r, c):
            base = pl.multiple_of(r * GRID_W, GRID_W)

            def slab(d, lo):
                return pad_ref[pl.ds(base + (8 + d) * GRID_W, GRID_W), lo:lo + 128]

            s2 = slab(-1, 0) + slab(0, 0)
            s4 = s2 + slab(-2, 0) + slab(1, 0)
            v_ref[pl.ds(base, GRID_W), 0:128] = jnp.where(lane < 64, s2, s4)
            s8 = slab(-4, 128)
            for d in (-3, -2, -1, 0, 1, 2, 3):
                s8 = s8 + slab(d, 128)
            s16 = s8
            for d in (-8, -7, -6, -5, 4, 5, 6, 7):
                s16 = s16 + slab(d, 128)
            v_ref[pl.ds(base, GRID_W), 128:256] = jnp.where(lane < 64, s8, s16)
            return c

        lax.fori_loop(0, GRID_H, row_body, 0)
        src = v_ref
    else:
        src = u_ref
    grp = lax.broadcasted_iota(jnp.int32, (256, BR), 1) // POOL_GD

    def blk_body(b, c):
        r0 = pl.multiple_of(b * 256, 256)
        vb = src[pl.ds(r0, 256), :]
        hi = vb.astype(BF16)
        lo = (vb - hi.astype(F32)).astype(BF16)
        res = jnp.zeros((256, BR), F32)
        for g in range(len(POOL_WINDOWS)):
            pg = (jnp.dot(pm_ref[g], hi, preferred_element_type=F32)
                  + jnp.dot(pm_ref[g], lo, preferred_element_type=F32))
            res = jnp.where(grp == g, pg, res)
        pooled = res * inv_ref[pl.ds(r0, 256), :]
        d = pooled - u_ref[pl.ds(r0, 256), :]
        y = bdot(d, w_ref[...]) * sc_ref[...]
        o_ref[pl.ds(r0, 256), :] = y * silu(z_ref[pl.ds(r0, 256), :])
        return c

    lax.fori_loop(0, nblk, blk_body, 0)


def _band_matrices(seg):
    t = np.arange(256)
    out = []
    for w in POOL_WINDOWS:
        lo = t - w // 2
        hi = t - w // 2 + w
        s = t[None, :]
        m = (s >= lo[:, None]) & (s < hi[:, None]) & ((s // seg) == (t[:, None] // seg))
        out.append(m.astype(np.float32))
    return np.stack(out)


def _counts(length, w):
    pos = np.arange(length)
    return (np.clip(pos - w // 2 + w, 0, length) - np.clip(pos - w // 2, 0, length)).astype(np.float64)


def _pool_constants():
    inv_ctx = np.concatenate([np.repeat((1.0 / _counts(SEQ, w))[:, None], POOL_GD, 1) for w in POOL_WINDOWS], 1)
    inv_lat = []
    for w in POOL_WINDOWS:
        c2 = np.outer(_counts(GRID_H, w), _counts(GRID_W, w)).reshape(DEC_SEQ)
        inv_lat.append(np.repeat((1.0 / c2)[:, None], POOL_GD, 1))
    inv_lat = np.concatenate(inv_lat, 1)
    return (jnp.asarray(_band_matrices(SEQ), BF16), jnp.asarray(inv_ctx, F32),
            jnp.asarray(_band_matrices(GRID_W), BF16), jnp.asarray(inv_lat, F32))


def pool_branch(pool, pm, inv, w_bd, scale, two_d):
    if two_d:
        rows, nseq, blk0 = DEC_SEQ, DEC_BATCH, N_CTX // DEC_SEQ
        scratch = [pltpu.VMEM((DEC_SEQ + 16 * GRID_W, BR), F32), pltpu.VMEM((DEC_SEQ, BR), F32)]
    else:
        rows, nseq, blk0 = SEQ, BATCH, 0
        scratch = []
    return pl.pallas_call(
        functools.partial(_pool_body, two_d=two_d),
        out_shape=SDS((nseq * rows, BR), F32),
        grid=(nseq,),
        in_specs=[BS((rows, BR), lambda i: (blk0 + i, 0)),
                  BS((rows, BR), lambda i: (blk0 + i, 1)),
                  BS((4, 256, 256), lambda i: (0, 0, 0)),
                  BS((rows, BR), lambda i: (0, 0)),
                  BS((BR, BR), lambda i: (0, 0)),
                  BS((1, BR), lambda i: (0, 0))],
        out_specs=BS((rows, BR), lambda i: (i, 0)),
        scratch_shapes=scratch,
        compiler_params=_cparams("parallel"),
        name="pool2d" if two_d else "pool1d",
    )(pool, pool, pm, inv, w_bd, scale)


def _ft_ctx_body(u_ref, z_ref, fpos_ref, fch_ref, w_ref, o_ref):
    uc = bdot(u_ref[...], fch_ref[...])
    st = jnp.concatenate([uc[:, 0:BR], uc[:, BR:2 * BR]], axis=0)
    f = bdot(fpos_ref[...], st)
    o_ref[...] = bdot(f, w_ref[...]) * silu(z_ref[...])


def ft_ctx(ft, fpos, fch, ft_w):
    return pl.pallas_call(
        _ft_ctx_body,
        out_shape=SDS((N_CTX, BR), F32),
        grid=(BATCH,),
        in_specs=[BS((SEQ, BR), lambda i: (i, 0)),
                  BS((SEQ, BR), lambda i: (i, 1)),
                  BS((SEQ, 2 * SEQ), lambda i: (0, 0)),
                  BS((BR, 2 * BR), lambda i: (0, 0)),
                  BS((BR, BR), lambda i: (0, 0))],
        out_specs=BS((SEQ, BR), lambda i: (i, 0)),
        compiler_params=_cparams("parallel"),
        name="ft_ctx",
    )(ft, ft, fpos, fch, ft_w)


def _ft_lat_body(u_ref, z_ref, g_ref, h_ref, fch_ref, w_ref, o_ref, x_ref, yr_ref, yi_ref):
    for hf in range(2):
        x_ref[hf] = u_ref[:, hf * 128:(hf + 1) * 128]

    def stage1(t2, c):
        xs = jnp.concatenate([x_ref[hf, pl.ds(t2, GRID_H, stride=GRID_W), :] for hf in range(2)], axis=1)
        y = jnp.dot(g_ref[t2], xs.astype(BF16), preferred_element_type=F32)
        r0 = pl.multiple_of(t2 * GRID_W, GRID_W)
        for hf in range(2):
            yr_ref[hf, pl.ds(r0, GRID_W), :] = y[0:64, hf * 128:(hf + 1) * 128]
            yi_ref[hf, pl.ds(r0, GRID_W), :] = y[64:128, hf * 128:(hf + 1) * 128]
        return c

    lax.fori_loop(0, GRID_W, stage1, 0, unroll=8)

    def stage2(kb, c):
        yr = jnp.concatenate([yr_ref[hf, pl.ds(kb, GRID_W, stride=GRID_W), :] for hf in range(2)], axis=1)
        yi = jnp.concatenate([yi_ref[hf, pl.ds(kb, GRID_W, stride=GRID_W), :] for hf in range(2)], axis=1)
        st = jnp.concatenate([yr, yi], axis=0).astype(BF16)
        a = jnp.dot(h_ref[...], st, preferred_element_type=F32)
        for hf in range(2):
            yr_ref[hf, pl.ds(kb, GRID_W, stride=GRID_W), :] = a[0:64, hf * 128:(hf + 1) * 128]
            yi_ref[hf, pl.ds(kb, GRID_W, stride=GRID_W), :] = a[64:128, hf * 128:(hf + 1) * 128]
        return c

    lax.fori_loop(0, GRID_W, stage2, 0, unroll=8)

    def stage3(b, c):
        r0 = pl.multiple_of(b * TM, TM)
        ar = jnp.concatenate([yr_ref[hf, pl.ds(r0, TM), :] for hf in range(2)], axis=1)
        ai = jnp.concatenate([yi_ref[hf, pl.ds(r0, TM), :] for hf in range(2)], axis=1)
        f = bdot(ar, fch_ref[0:BR, :]) + bdot(ai, fch_ref[BR:2 * BR, :])
        o_ref[pl.ds(r0, TM), :] = bdot(f, w_ref[...]) * silu(z_ref[pl.ds(r0, TM), :])
        return c

    lax.fori_loop(0, DEC_SEQ // TM, stage3, 0)


def ft_lat(ft, g1, h2, fch2, ft_w):
    blk0 = N_CTX // DEC_SEQ
    return pl.pallas_call(
        _ft_lat_body,
        out_shape=SDS((N_LAT, BR), F32),
        grid=(DEC_BATCH,),
        in_specs=[BS((DEC_SEQ, BR), lambda i: (blk0 + i, 0)),
                  BS((DEC_SEQ, BR), lambda i: (blk0 + i, 1)),
                  BS((GRID_W, 128, GRID_H), lambda i: (0, 0, 0)),
                  BS((128, 128), lambda i: (0, 0)),
                  BS((2 * BR, BR), lambda i: (0, 0)),
                  BS((BR, BR), lambda i: (0, 0))],
        out_specs=BS((DEC_SEQ, BR), lambda i: (i, 0)),
        scratch_shapes=[pltpu.VMEM((2, DEC_SEQ, 128), F32)] * 3,
        compiler_params=_cparams("parallel"),
        name="ft_lat",
    )(ft, ft, g1, h2, fch2, ft_w)


def _ft_constants():
    c = np.arange(FT_HD)
    ang = 2.0 * np.pi * np.outer(c, c) / FT_HD
    eye4 = np.eye(BR // FT_HD)
    cc = np.kron(eye4, np.cos(ang)) / 8.0
    sc = np.kron(eye4, np.sin(ang)) / 8.0
    t = np.arange(SEQ)
    angt = 2.0 * np.pi * (np.outer(t, t) % SEQ) / SEQ
    fpos = np.concatenate([np.cos(angt), -np.sin(angt)], axis=1) / 16.0
    fch = np.concatenate([cc, sc], axis=1)
    kb = np.arange(GRID_W)[None, :, None]
    t1 = np.arange(GRID_H)[None, None, :]
    t2 = np.arange(GRID_W)[:, None, None]
    a1 = 2.0 * np.pi * ((kb * (GRID_W * t1 + t2)) % DEC_SEQ) / DEC_SEQ
    g1 = np.concatenate([np.cos(a1), -np.sin(a1)], axis=1) / 8.0
    a2 = 2.0 * np.pi * (np.outer(np.arange(GRID_W), np.arange(GRID_W)) % GRID_W) / GRID_W
    c2, s2 = np.cos(a2) / 8.0, np.sin(a2) / 8.0
    h2 = np.block([[c2, s2], [-s2, c2]])
    fch2 = np.concatenate([cc, sc], axis=0)
    as_bf = lambda a: jnp.asarray(a, F32).astype(BF16)
    return as_bf(fpos), as_bf(fch), as_bf(g1), as_bf(h2), as_bf(fch2)


S5_ROWS = N_TOK // S5_L
S5_TM = 1024
S5_TR = S5_TM // S5_L
S5_CTX_C = SEQ // S5_L
S5_LAT_C = DEC_SEQ // S5_L
S5_LT = 2 * S5_G


def _s5_prep_body(ar_ref, ai_ref, ldt_ref, br_ref, bi_ref, btr_ref, bti_ref, cr_ref, ci_ref,
                  mt_ref, q4_ref, rt_ref, al_ref):
    L = S5_L
    m = lax.broadcasted_iota(jnp.int32, (2 * L, S5_N), 0).astype(F32)
    rts, qs, ds, als = [], [], [], []
    for d in range(2):
        a_re, a_im = ar_ref[0, d, 0], ai_ref[0, d, 0]
        dt = jnp.exp(ldt_ref[0, d, 0])
        xr, xi = a_re * dt, a_im * dt
        mag = jnp.exp(xr)
        ab_re, ab_im = mag * jnp.cos(xi), mag * jnp.sin(xi)
        den = a_re * a_re + a_im * a_im
        nr = ab_re - 1.0
        coef_re = (nr * a_re + ab_im * a_im) / den
        coef_im = (ab_im * a_re - nr * a_im) / den
        pw_re = jnp.exp(m * xr) * jnp.cos(m * xi)
        pw_im = jnp.exp(m * xr) * jnp.sin(m * xi)
        cq_re = pw_re * coef_re - pw_im * coef_im
        cq_im = pw_re * coef_im + pw_im * coef_re
        c_re, c_im = cr_ref[0, d, 0], ci_ref[0, d, 0]
        bt_re, bt_im = btr_ref[0, d, 0], bti_ref[0, d, 0]
        row = lambda x, e: x[e:e + 1, :]
        order = range(L) if d == 0 else range(L - 1, -1, -1)
        cp_re = jnp.concatenate([c_re * row(cq_re, e) - c_im * row(cq_im, e) for e in order], axis=0)
        cp_im = jnp.concatenate([c_re * row(cq_im, e) + c_im * row(cq_re, e) for e in order], axis=0)
        rts.append(hdot(cp_re, br_ref[0, d, 0]) - hdot(cp_im, bi_ref[0, d, 0]))
        inj = [L - 1 - i for i in range(L)] if d == 0 else list(range(L))
        q_re = jnp.concatenate([bt_re * row(cq_re, e) - bt_im * row(cq_im, e) for e in inj], axis=0)
        q_im = jnp.concatenate([bt_im * row(cq_re, e) + bt_re * row(cq_im, e) for e in inj], axis=0)
        qs.append((q_re, q_im))
        out = [j + 1 for j in range(L)] if d == 0 else [L - j for j in range(L)]
        d_re = jnp.concatenate([c_re * row(pw_re, e) - c_im * row(pw_im, e) for e in out], axis=0)
        d_im = jnp.concatenate([c_re * row(pw_im, e) + c_im * row(pw_re, e) for e in out], axis=0)
        ds.append((d_re, d_im))
        als.append((row(pw_re, L), row(pw_im, L)))
    pad = jnp.zeros(((L - 1) * S5_P, S5_P), F32)
    z = jnp.concatenate([pad, rts[0]], axis=0) + jnp.concatenate([rts[1], pad], axis=0)
    mt = jnp.concatenate([z[(L - 1 - i) * S5_P:(L - 1 - i) * S5_P + L * S5_P, :] for i in range(L)], axis=1)
    mt_ref[0, 0] = mt.astype(BF16)
    q4_ref[0, 0] = jnp.concatenate([qs[0][0], qs[1][0], qs[0][1], qs[1][1]], axis=1).astype(BF16)
    rt_ref[0, 0] = jnp.concatenate([ds[0][0], ds[1][0], -ds[0][1], -ds[1][1]], axis=1).astype(BF16)
    al_ref[0, 0] = jnp.concatenate([jnp.concatenate([als[0][0], als[1][0]], axis=1),
                                    jnp.concatenate([als[0][1], als[1][1]], axis=1)], axis=0)


def s5_prep(a_re, a_im, log_dt, b_re, b_im, c_re, c_im):
    vec = lambda x: x.reshape(DEPTH, 2, S5_G, 1, S5_N)
    ldt = jnp.broadcast_to(log_dt[..., None, None], (DEPTH, 2, S5_G, 1, S5_N))
    bt = lambda x: x.transpose(0, 1, 2, 4, 3)
    vspec = BS((1, 2, 1, 1, S5_N), lambda i: (i // S5_G, 0, i % S5_G, 0, 0))
    bspec = BS((1, 2, 1, S5_N, S5_P), lambda i: (i // S5_G, 0, i % S5_G, 0, 0))
    cspec = BS((1, 2, 1, S5_P, S5_N), lambda i: (i // S5_G, 0, i % S5_G, 0, 0))
    mat = SDS((DEPTH, S5_G, 256, 256), BF16)
    mspec = BS((1, 1, 256, 256), lambda i: (i // S5_G, i % S5_G, 0, 0))
    return pl.pallas_call(
        _s5_prep_body,
        out_shape=[mat, mat, mat, SDS((DEPTH, S5_G, 2, 128), F32)],
        grid=(DEPTH * S5_G,),
        in_specs=[vspec, vspec, vspec, bspec, bspec, cspec, cspec, cspec, cspec],
        out_specs=[mspec, mspec, mspec, BS((1, 1, 2, 128), lambda i: (i // S5_G, i % S5_G, 0, 0))],
        compiler_params=_cparams("parallel"),
        name="s5_prep",
    )(vec(a_re), vec(a_im), ldt, b_re, b_im, bt(b_re), bt(b_im), c_re, c_im)


def _block_transpose(arrs):
    blk = lax.broadcasted_iota(jnp.int32, arrs[0].shape, 1) // 16
    cur = list(arrs)
    for b in range(4):
        s = 16 << b
        hi = ((blk >> b) & 1) == 1
        nxt = list(cur)
        for x in range(16):
            if (x >> b) & 1:
                continue
            y = x | (1 << b)
            nxt[x] = jnp.where(hi, pltpu.roll(cur[y], s, 1), cur[x])
            nxt[y] = jnp.where(hi, cur[y], pltpu.roll(cur[x], 256 - s, 1))
        cur = nxt
    return cur


def _s5_in_body(s5_ref, mt_ref, q4_ref, y_ref, e_ref, x_ref):
    for hf in range(2):
        x_ref[hf] = s5_ref[:, hf * 128:(hf + 1) * 128]
    xs = [jnp.concatenate([x_ref[hf, pl.ds(i, S5_TR, stride=S5_L), :] for hf in range(2)], axis=1)
          for i in range(S5_L)]
    us = _block_transpose(xs)
    for g in range(S5_G):
        ub = us[g].astype(BF16)
        y_ref[:, g * 256:(g + 1) * 256] = lax.dot_general(ub, mt_ref[g], (((1,), (1,)), ((), ())),
                                                          preferred_element_type=F32)
        e = jnp.dot(ub, q4_ref[g], preferred_element_type=F32)
        e_ref[2 * g] = e[:, 0:128]
        e_ref[2 * g + 1] = e[:, 128:256]


def s5_in(s5, mt, q4):
    wspec = BS((S5_G, 256, 256), lambda t: (0, 0, 0))
    return pl.pallas_call(
        _s5_in_body,
        out_shape=[SDS((S5_ROWS, S5_G * 256), F32), SDS((S5_LT, S5_ROWS, 128), F32)],
        grid=(N_TOK // S5_TM,),
        in_specs=[BS((S5_TM, BR), lambda t: (t, 0)), wspec, wspec],
        out_specs=[BS((S5_TR, S5_G * 256), lambda t: (t, 0)), BS((S5_LT, S5_TR, 128), lambda t: (0, t, 0))],
        scratch_shapes=[pltpu.VMEM((2, S5_TM, 128), F32)],
        compiler_params=_cparams("parallel"),
        name="s5_in",
    )(s5, mt, q4)


def _s5_chunk_scan_body(e_ref, al_ref, s0_ref, spf_ref, spb_ref, fin_ref):
    def update(g, s_re, s_im, e_re, e_im):
        a_re, a_im = al_ref[g, 0:1, :], al_ref[g, 1:2, :]
        return a_re * s_re - a_im * s_im + e_re, a_re * s_im + a_im * s_re + e_im

    fwd_c = lax.broadcasted_iota(jnp.int32, (BATCH, 128), 1) < S5_N

    def ctx_group(g, carry):
        s_re = jnp.zeros((BATCH, 128), F32)
        s_im = jnp.zeros((BATCH, 128), F32)
        for c in range(S5_CTX_C):
            rf = pl.ds(c, BATCH, stride=S5_CTX_C)
            rb = pl.ds(S5_CTX_C - 1 - c, BATCH, stride=S5_CTX_C)
            spf_ref[2 * g, rf, :] = s_re
            spf_ref[2 * g + 1, rf, :] = s_im
            spb_ref[2 * g, rb, :] = s_re
            spb_ref[2 * g + 1, rb, :] = s_im
            e_re = jnp.where(fwd_c, e_ref[2 * g, rf, :], e_ref[2 * g, rb, :])
            e_im = jnp.where(fwd_c, e_ref[2 * g + 1, rf, :], e_ref[2 * g + 1, rb, :])
            s_re, s_im = update(g, s_re, s_im, e_re, e_im)
        fin_ref[2 * g] = s_re
        fin_ref[2 * g + 1] = s_im
        return carry

    lax.fori_loop(0, S5_G, ctx_group, 0)

    row0 = BATCH * S5_CTX_C
    fwd_l = lax.broadcasted_iota(jnp.int32, (DEC_BATCH, 128), 1) < S5_N

    def lat_step(c, state):
        rf = pl.ds(row0 + c, DEC_BATCH, stride=S5_LAT_C)
        rb = pl.ds(row0 + S5_LAT_C - 1 - c, DEC_BATCH, stride=S5_LAT_C)
        new = []
        for g in range(S5_G):
            s_re, s_im = state[2 * g], state[2 * g + 1]
            spf_ref[2 * g, rf, :] = s_re
            spf_ref[2 * g + 1, rf, :] = s_im
            spb_ref[2 * g, rb, :] = s_re
            spb_ref[2 * g + 1, rb, :] = s_im
            e_re = jnp.where(fwd_l, e_ref[2 * g, rf, :], e_ref[2 * g, rb, :])
            e_im = jnp.where(fwd_l, e_ref[2 * g + 1, rf, :], e_ref[2 * g + 1, rb, :])
            new.extend(update(g, s_re, s_im, e_re, e_im))
        return tuple(new)

    lax.fori_loop(0, S5_LAT_C, lat_step, tuple(s0_ref[t] for t in range(S5_LT)))


def s5_chunk_scan(e3, al, s0_lat):
    sp = SDS((S5_LT, S5_ROWS, 128), F32)
    return pl.pallas_call(
        _s5_chunk_scan_body,
        out_shape=[sp, sp, SDS((S5_LT, BATCH, 128), F32)],
        compiler_params=pltpu.CompilerParams(vmem_limit_bytes=VMEM_LIMIT),
        name="s5_chunk_scan",
    )(e3, al, s0_lat)


def _s5_fin_body(y_ref, spf_ref, spb_ref, rt_ref, o_ref, t_ref):
    fwd = lax.broadcasted_iota(jnp.int32, (S5_TR, 128), 1) < S5_N
    ys = []
    for g in range(S5_G):
        s_re = jnp.where(fwd, spf_ref[2 * g], spb_ref[2 * g])
        s_im = jnp.where(fwd, spf_ref[2 * g + 1], spb_ref[2 * g + 1])
        sp = jnp.concatenate([s_re, s_im], axis=1).astype(BF16)
        ys.append(y_ref[:, g * 256:(g + 1) * 256]
                  + lax.dot_general(sp, rt_ref[g], (((1,), (1,)), ((), ())), preferred_element_type=F32))
    xs = _block_transpose(ys)
    for j in range(S5_L):
        for hf in range(2):
            t_ref[hf, pl.ds(j, S5_TR, stride=S5_L), :] = xs[j][:, hf * 128:(hf + 1) * 128]
    o_ref[...] = jnp.concatenate([t_ref[0], t_ref[1]], axis=1)


def s5_fin(yi, spf, spb, rt):
    lt = BS((S5_LT, S5_TR, 128), lambda t: (0, t, 0))
    return pl.pallas_call(
        _s5_fin_body,
        out_shape=SDS((N_TOK, BR), F32),
        grid=(N_TOK // S5_TM,),
        in_specs=[BS((S5_TR, S5_G * 256), lambda t: (t, 0)), lt, lt, BS((S5_G, 256, 256), lambda t: (0, 0, 0))],
        out_specs=BS((S5_TM, BR), lambda t: (t, 0)),
        scratch_shapes=[pltpu.VMEM((2, S5_TM, 128), F32)],
        compiler_params=_cparams("parallel"),
        name="s5_fin",
    )(yi, spf, spb, rt)


def s5_branch(s5, mt, q4, rt, al, state_l):
    yi, e3 = s5_in(s5, mt, q4)
    s0 = state_l.transpose(3, 2, 0, 1, 4).reshape(S5_LT, DEC_BATCH, 2 * S5_N)
    spf, spb, fin = s5_chunk_scan(e3, al, s0)
    y = s5_fin(yi, spf, spb, rt)
    fin = fin.reshape(S5_G, 2, BATCH, 2, S5_N).transpose(2, 3, 1, 0, 4)
    return y, fin


DN_TILE = 256
DN_HALO = 8


N_SEG = N_TOK // DN_TILE
assert N_CTX == DEC_SEQ


def _split3(x):
    x1 = x.astype(BF16)
    r1 = x - x1.astype(F32)
    x2 = r1.astype(BF16)
    return x1, x2, (r1 - x2.astype(F32)).astype(BF16)


def dot3(a, b, exact):
    if exact == 'b':
        return sum(jnp.dot(p, b.astype(BF16), preferred_element_type=F32) for p in _split3(a))
    return sum(jnp.dot(a.astype(BF16), p, preferred_element_type=F32) for p in _split3(b))


def _dn_conv_body(x_ref, prev_ref, next_ref, ba_ref, w_ref, gp_ref, ex_ref, o_ref, gb_ref, pad_ref):
    i = pl.program_id(0)
    tiles_ctx = N_CTX // DN_TILE
    tiles_seq = DEC_SEQ // DN_TILE
    j = (i - tiles_ctx) % tiles_seq
    first = jnp.logical_or(i < tiles_ctx, j == 0)
    last = jnp.logical_or(i < tiles_ctx, j == tiles_seq - 1)
    pad_ref[0:DN_HALO, :] = jnp.where(first, 0.0, prev_ref[...])
    pad_ref[DN_HALO:DN_HALO + DN_TILE, :] = x_ref[...]
    pad_ref[DN_HALO + DN_TILE:2 * DN_HALO + DN_TILE, :] = jnp.where(last, 0.0, next_ref[...])
    for r0 in range(0, DN_TILE, 128):
        for c0 in range(0, 3 * BR, 128):
            acc = jnp.zeros((128, 128), F32)
            for t in range(CONV_K):
                acc = acc + (pad_ref[pl.ds(r0 + DN_HALO - CONV_K // 2 + t, 128), c0:c0 + 128]
                             * w_ref[t:t + 1, c0:c0 + 128])
            o_ref[r0:r0 + 128, c0:c0 + 128] = silu(acc)
    raw = ba_ref[...]
    lane = lax.broadcasted_iota(jnp.int32, raw.shape, 1)
    xa = raw + gp_ref[1:2, :]
    sp = jnp.maximum(xa, 0.0) + jnp.log1p(jnp.exp(-jnp.abs(xa)))
    gates = jnp.where(lane < 2 * DN_HEADS, jax.nn.sigmoid(raw), -jnp.exp(gp_ref[0:1, :]) * sp)
    r = lax.broadcasted_iota(jnp.int32, (DN_TILE, DN_TILE), 0)
    c = lax.broadcasted_iota(jnp.int32, (DN_TILE, DN_TILE), 1)
    same = (r // CHUNK) == (c // CHUNK)
    ex = dot3(gates, ex_ref[...], 'b')
    gb_ref[:, 0:2 * BR] = ex[:, 0:2 * BR]
    gb_ref[:, 2 * BR:3 * BR] = dot3(jnp.logical_and(same, c <= r), ex[:, 2 * BR:3 * BR], 'a')
    gb_ref[:, 3 * BR:4 * BR] = dot3(jnp.logical_and(same, c >= r), ex[:, 3 * BR:4 * BR], 'a')


def _gate_expand():
    e = np.zeros((128, 4 * BR), np.float32)
    for blk in range(4):
        for h in range(DN_HEADS):
            e[blk * DN_HEADS + h, blk * BR + h * DN_HD:blk * BR + (h + 1) * DN_HD] = 1.0
    return jnp.asarray(e, BF16)


def dn_conv(qkv, ba, conv_w, gate_p):
    nt = N_TOK // DN_TILE
    per = DN_TILE // DN_HALO
    nhb = N_TOK // DN_HALO
    return pl.pallas_call(
        _dn_conv_body,
        out_shape=[SDS((N_TOK, 3 * BR), F32), SDS((N_TOK, 4 * BR), F32)],
        grid=(nt,),
        in_specs=[BS((DN_TILE, 3 * BR), lambda i: (i, 0)),
                  BS((DN_HALO, 3 * BR), lambda i: (jnp.maximum(i * per - 1, 0), 0)),
                  BS((DN_HALO, 3 * BR), lambda i: (jnp.minimum((i + 1) * per, nhb - 1), 0)),
                  BS((DN_TILE, 128), lambda i: (i, 0)),
                  BS((8, 3 * BR), lambda i: (0, 0)),
                  BS((8, 128), lambda i: (0, 0)),
                  BS((128, 4 * BR), lambda i: (0, 0))],
        out_specs=[BS((DN_TILE, 3 * BR), lambda i: (i, 0)), BS((DN_TILE, 4 * BR), lambda i: (i, 0))],
        scratch_shapes=[pltpu.VMEM((DN_TILE + 2 * DN_HALO, 3 * BR), F32)],
        compiler_params=_cparams("parallel"),
        name="dn_conv",
    )(qkv, qkv, qkv, ba, conv_w, gate_p, _gate_expand())


def _head_block_mask():
    r = lax.broadcasted_iota(jnp.int32, (BR, BR), 0) // DN_HD
    c = lax.broadcasted_iota(jnp.int32, (BR, BR), 1) // DN_HD
    return r == c


def _split2(x):
    hi = x.astype(BF16)
    return hi, (x - hi.astype(F32)).astype(BF16)


def _dn_local_body(qkv_ref, gb_ref, *out_refs):
    f_refs, b_refs = out_refs[0:6], out_refs[6:12]
    ncb = DN_TILE // CHUNK
    nb = 2 * ncb
    bmask = _head_block_mask()
    ri = lax.broadcasted_iota(jnp.int32, (nb, CHUNK, BR), 1)
    cj = lax.broadcasted_iota(jnp.int32, (nb, CHUNK, BR), 2) % DN_HD
    bwd = lax.broadcasted_iota(jnp.int32, (nb, CHUNK, BR), 0) >= ncb
    eye = cj == ri
    incl = jnp.logical_or(jnp.logical_and(bwd, cj >= ri), jnp.logical_and(jnp.logical_not(bwd), cj <= ri))
    strict = jnp.logical_and(incl, jnp.logical_not(eye))
    ones_blk = bmask.astype(BF16)

    def chunks(x):
        return x.reshape(ncb, CHUNK, BR)

    def both(x):
        return jnp.concatenate([x, x], axis=0)

    def head_sum(x):
        return dot3(x.reshape(-1, BR), ones_blk, 'b').reshape(x.shape)

    def bd(x):
        return jnp.where(bmask, jnp.concatenate([x] * DN_HEADS, axis=1), jnp.zeros((), x.dtype))

    def bmm(a, b):
        return jnp.einsum('bij,bjk->bik', a, b, preferred_element_type=F32)

    q = chunks(qkv_ref[:, 0:BR])
    k = chunks(qkv_ref[:, BR:2 * BR])
    v = both(chunks(qkv_ref[:, 2 * BR:3 * BR]))
    q = q * lax.rsqrt(head_sum(q * q) + EPS) * (DN_HD ** -0.5)
    k = k * lax.rsqrt(head_sum(k * k) + EPS)
    kq = jnp.einsum('bik,bjk->bij', jnp.concatenate([k, q], axis=1).astype(BF16), bd(k.astype(BF16)),
                    preferred_element_type=F32)
    kk, qk = both(kq[:, 0:CHUNK]), both(kq[:, CHUNK:2 * CHUNK])
    q, k = both(q), both(k)
    beta = jnp.concatenate([chunks(gb_ref[:, 0:BR]), chunks(gb_ref[:, BR:2 * BR])], axis=0)
    gc = jnp.concatenate([chunks(gb_ref[:, 2 * BR:3 * BR]), chunks(gb_ref[:, 3 * BR:4 * BR])], axis=0)
    crow = jnp.sum(jnp.where(eye, gc, 0.0), axis=1, keepdims=True)
    decay = jnp.where(incl, jnp.exp(jnp.where(incl, gc - crow, 0.0)), 0.0)
    a = jnp.where(strict, kk * decay * beta, 0.0)
    tinv = jnp.where(eye, 1.0, 0.0) - a
    pw = a
    pw_bd = bd(pw.astype(BF16))
    for _ in range(5):
        pw = bmm(pw.astype(BF16), pw_bd)
        pw_bd = bd(pw.astype(BF16))
        tinv = tinv + bmm(tinv.astype(BF16), pw_bd)
    egc = jnp.exp(gc)
    t_hi, t_lo = _split2(tinv)

    def solve(rhs):
        r_hi, r_lo = _split2(rhs)
        r_hi, r_lo = bd(r_hi), bd(r_lo)
        return bmm(t_hi, r_hi) + bmm(t_hi, r_lo) + bmm(t_lo, r_hi)

    w = bmm(t_hi, bd((k * (beta * egc)).astype(BF16)))

    bwd_row = lax.broadcasted_iota(jnp.int32, (nb, 1, BR), 0) >= ncb
    g_last = jnp.where(bwd_row, gc[:, 0:1], gc[:, CHUNK - 1:CHUNK])
    eg = jnp.exp(g_last)
    outs = (w, solve(v * beta), q * egc, k * jnp.exp(g_last - gc),
            jnp.where(incl, qk * decay, 0.0))
    for d, refs in enumerate((f_refs, b_refs)):
        for ref, x in zip(refs[0:5], outs):
            ref[...] = x[d * ncb:(d + 1) * ncb].reshape(DN_TILE, BR).astype(ref.dtype)
        refs[5][0] = jnp.concatenate([eg[d * ncb:(d + 1) * ncb, 0], jnp.zeros((8 - ncb, BR), F32)], axis=0)


_DN_LOCAL_DTYPES = (BF16, F32, BF16, BF16, BF16)


def dn_local(qkvc, gb):
    tok = BS((DN_TILE, BR), lambda i: (i, 0))
    shapes = [SDS((N_TOK, BR), dt) for dt in _DN_LOCAL_DTYPES] + [SDS((N_SEG, 8, BR), F32)]
    specs = [tok] * len(_DN_LOCAL_DTYPES) + [BS((1, 8, BR), lambda i: (i, 0, 0))]
    outs = pl.pallas_call(
        _dn_local_body,
        out_shape=shapes * 2,
        grid=(N_SEG,),
        in_specs=[BS((DN_TILE, 3 * BR), lambda i: (i, 0)), BS((DN_TILE, 4 * BR), lambda i: (i, 0))],
        out_specs=specs * 2,
        compiler_params=_cparams("parallel"),
        name="dn_local",
    )(qkvc, gb)
    return outs[0:6], outs[6:12]


def _dn_advance(chains, s, bmask):
    ncb = DN_TILE // CHUNK
    bmm = lambda a, b: jnp.einsum('bij,bjk->bik', a, b, preferred_element_type=F32)
    steps = []
    for t in range(ncb):
        cs = [ncb - 1 - t if rev else t for _, rev in chains]
        w, u, qt, kt, aqk, eg = [jnp.stack([load(k, c) for (load, _), c in zip(chains, cs)]) for k in range(6)]
        sb = s.astype(BF16)
        v_new = u - bmm(w, sb)
        vb = v_new.astype(BF16)
        v_bd = jnp.where(bmask, jnp.concatenate([vb] * DN_HEADS, axis=1), jnp.zeros((), BF16))
        o = bmm(qt, sb) + bmm(aqk, v_bd)
        upd = jnp.einsum('btk,btv->bkv', kt, vb, preferred_element_type=F32)
        s = s * eg + jnp.where(bmask, upd, 0.0)
        steps.append((cs, o))
    return s, steps


def _dn_seq_ctx_body(*refs):
    nseq = 4
    f_in, b_in = refs[0:6], refs[6:12]
    of_ref, ob_ref, finf_ref, finb_ref = refs[12:16]
    bmask = _head_block_mask()

    def loader(in_refs, q):
        def load(k, c):
            if k == 5:
                return in_refs[5][q, c:c + 1, :]
            return in_refs[k][q * DN_TILE + c * CHUNK:q * DN_TILE + (c + 1) * CHUNK, :]
        return load

    chains = [(loader(f_in, q), False) for q in range(nseq)] + [(loader(b_in, q), True) for q in range(nseq)]
    s, steps = _dn_advance(chains, jnp.zeros((2 * nseq, BR, BR), F32), bmask)
    for cs, o in steps:
        for i, c in enumerate(cs):
            o_ref, q = (of_ref, i) if i < nseq else (ob_ref, i - nseq)
            o_ref[q * DN_TILE + c * CHUNK:q * DN_TILE + (c + 1) * CHUNK, :] = o[i]
    finf_ref[...] = s[0:nseq]
    finb_ref[...] = s[nseq:2 * nseq]


def dn_seq_ctx(loc_f, loc_b):
    nseq = 4
    tok = BS((nseq * DN_TILE, BR), lambda i: (i, 0))
    specs = [tok] * len(_DN_LOCAL_DTYPES) + [BS((nseq, 8, BR), lambda i: (i, 0, 0))]
    st = BS((nseq, BR, BR), lambda i: (i, 0, 0))
    return pl.pallas_call(
        _dn_seq_ctx_body,
        out_shape=[SDS((N_CTX, BR), F32), SDS((N_CTX, BR), F32), SDS((BATCH, BR, BR), F32), SDS((BATCH, BR, BR), F32)],
        grid=(BATCH // nseq,),
        in_specs=specs * 2,
        out_specs=[tok, tok, st, st],
        compiler_params=_cparams("parallel"),
        name="dn_seq_ctx",
    )(*loc_f, *loc_b)


def _dn_seq_lat_body(*refs):
    f_in, b_in = refs[0:6], refs[6:12]
    s0_ref, of_ref, ob_ref, s_ref = refs[12:16]

    @pl.when(pl.program_id(0) == 0)
    def _():
        s_ref[...] = s0_ref[...]

    def loader(in_refs, q):
        def load(k, c):
            if k == 5:
                return in_refs[5][1 + q, 0, c:c + 1, :]
            return in_refs[k][1 + q, c * CHUNK:(c + 1) * CHUNK, :]
        return load

    chains = ([(loader(f_in, q), False) for q in range(DEC_BATCH)]
              + [(loader(b_in, q), True) for q in range(DEC_BATCH)])
    s, steps = _dn_advance(chains, s_ref[...], _head_block_mask())
    s_ref[...] = s
    for cs, o in steps:
        for i, c in enumerate(cs):
            o_ref, q = (of_ref, i) if i < DEC_BATCH else (ob_ref, i - DEC_BATCH)
            o_ref[q, c * CHUNK:(c + 1) * CHUNK, :] = o[i]


def dn_seq_lat(loc_f, loc_b, s0):
    nseg = DEC_SEQ // DN_TILE
    grp = lambda a: a.reshape((N_COND, nseg) + a.shape[1:])

    def specs(m):
        return ([BS((N_COND, DN_TILE, BR), lambda g: (0, m(g), 0))] * len(_DN_LOCAL_DTYPES)
                + [BS((N_COND, 1, 8, BR), lambda g: (0, m(g), 0, 0))])

    fwd = lambda g: g
    bwd = lambda g: nseg - 1 - g
    views = lambda loc: [a.reshape(N_COND, DEC_SEQ, BR) for a in loc[0:5]] + [grp(loc[5])]
    out = SDS((DEC_BATCH, DEC_SEQ, BR), F32)
    return pl.pallas_call(
        _dn_seq_lat_body,
        out_shape=[out, out],
        grid=(nseg,),
        in_specs=specs(fwd) + specs(bwd) + [BS((2 * DEC_BATCH, BR, BR), lambda g: (0, 0, 0))],
        out_specs=[BS((DEC_BATCH, DN_TILE, BR), lambda g: (0, fwd(g), 0)),
                   BS((DEC_BATCH, DN_TILE, BR), lambda g: (0, bwd(g), 0))],
        scratch_shapes=[pltpu.VMEM((2 * DEC_BATCH, BR, BR), F32)],
        compiler_params=_cparams("arbitrary"),
        name="dn_seq_lat",
    )(*views(loc_f), *views(loc_b), s0)


def dn_branch(qkv, ba, conv_w, gate_p, s0_lat):
    qkvc, gb = dn_conv(qkv, ba, conv_w, gate_p)
    loc_f, loc_b = dn_local(qkvc, gb)
    eye_h = jnp.eye(DN_HEADS, dtype=F32)
    s0 = s0_lat.transpose(1, 0, 2, 3, 4).reshape(2 * DEC_BATCH, DN_HEADS, DN_HD, DN_HD)
    s0 = jnp.einsum('shkv,hg->shkgv', s0, eye_h).reshape(2 * DEC_BATCH, BR, BR)
    of_c, ob_c, fin_f, fin_b = dn_seq_ctx(loc_f, loc_b)
    of_l, ob_l = dn_seq_lat(loc_f, loc_b, s0)
    unblock = lambda s: jnp.einsum('shkgv,hg->shkv', s.reshape(BATCH, DN_HEADS, DN_HD, DN_HEADS, DN_HD), eye_h)
    fin = jnp.stack([unblock(fin_f), unblock(fin_b)], axis=1)
    return (of_c, ob_c), (of_l.reshape(N_LAT, BR), ob_l.reshape(N_LAT, BR)), fin


def _outproj_body(x_ref, mod_ref, ypc_ref, ypl_ref, dofc_ref, dofl_ref, dobc_ref, dobl_ref, dnz_ref, s5y_ref, s5_ref,
                  yfc_ref, yfl_ref, d_ref, gw_ref, gb_ref, dng_ref, w_ref, fg_ref, o_ref, *, final):
    is_ctx = pl.program_id(0) < N_CTX // TM
    y_pool = jnp.where(is_ctx, ypc_ref[...], ypl_ref[...])
    y_ft = jnp.where(is_ctx, yfc_ref[...], yfl_ref[...])
    gate = mod_ref[0][:, 2 * D_MODEL:3 * D_MODEL]
    o = jnp.where(is_ctx, dofc_ref[...] + dobc_ref[...], dofl_ref[...] + dobl_ref[...])
    head_mean = jnp.where(_head_block_mask(), 1.0 / DN_HD, 0.0)
    y_dn = o * lax.rsqrt(dot3(o * o, head_mean, 'b') + EPS) * dng_ref[...] * silu(dnz_ref[...])
    y = s5y_ref[...] + d_ref[...] * s5_ref[:, 0:BR]
    y = jax.nn.gelu(y)
    y = y * jax.nn.sigmoid(bdot(y, gw_ref[...]) + gb_ref[...])
    y_s5 = y * silu(s5_ref[:, BR:2 * BR])
    acc = bdot(y_pool, w_ref[0:BR, :])
    acc = acc + bdot(y_dn, w_ref[BR:2 * BR, :])
    acc = acc + bdot(y_s5, w_ref[2 * BR:3 * BR, :])
    acc = acc + bdot(y_ft, w_ref[3 * BR:4 * BR, :])
    xn = x_ref[...] + gate * acc
    if final:
        xn = xn * lax.rsqrt(jnp.mean(xn * xn, axis=-1, keepdims=True) + EPS) * fg_ref[...]
    o_ref[...] = xn


def outproj(x, mod, yp_ctx, yp_lat, dn_ctx, dn_lat, dn_z, s5_y, s5, yf_ctx, yf_lat, s5_d, glu_w, glu_b, dn_g, w_out,
            final_g, final):
    row = lambda w: BS((TM, w), lambda i: (i, 0))
    full = lambda a, b: BS((a, b), lambda i: (0, 0))
    tiles_ctx = N_CTX // TM
    ctx_row = BS((TM, BR), lambda i: (jnp.minimum(i, tiles_ctx - 1), 0))
    lat_row = BS((TM, BR), lambda i: (jnp.maximum(i - tiles_ctx, 0), 0))
    return pl.pallas_call(
        functools.partial(_outproj_body, final=final),
        out_shape=SDS((N_TOK, D_MODEL), F32),
        grid=(N_TOK // TM,),
        in_specs=[row(D_MODEL),
                  BS((1, 1, 3 * D_MODEL), lambda i: (_cond_index(i), 0, 0)),
                  ctx_row, lat_row, ctx_row, lat_row, ctx_row, lat_row, row(BR), row(BR), row(2 * BR), ctx_row, lat_row,
                  full(1, BR), full(BR, BR), full(1, BR), full(1, BR), full(D_MODEL, D_MODEL), full(1, D_MODEL)],
        out_specs=row(D_MODEL),
        compiler_params=_cparams("parallel"),
        name="outproj",
    )(x, mod, yp_ctx, yp_lat, dn_ctx[0], dn_lat[0], dn_ctx[1], dn_lat[1], dn_z, s5_y, s5, yf_ctx, yf_lat,
      s5_d, glu_w, glu_b, dn_g, w_out, final_g)


def _permute_w_in(w_in):
    main = jnp.concatenate([w_in[:, 0:1536], w_in[:, 1552:2576]], axis=1)
    ba = jnp.pad(w_in[:, 1536:1552], ((0, 0), (0, 112)))
    return jnp.concatenate([main, ba], axis=1).astype(BF16)


def kernel(x_prompt, x_sample, c, state_delta, state_s5, c_ctx, w_ada, b_ada, norm_g, w_in, pool_w, pool_scale,
           dn_conv, dn_a_log, dn_dt_bias, dn_norm_g, s5_a_re, s5_a_im, s5_log_dt, s5_b_re, s5_b_im, s5_c_re,
           s5_c_im, s5_d, s5_glu_w, s5_glu_b, ft_w, w_out, final_g):
    x = jnp.concatenate([x_prompt.astype(F32).reshape(N_CTX, D_MODEL),
                         x_sample.astype(F32).reshape(N_LAT, D_MODEL)], axis=0)
    cond8 = jnp.concatenate([c_ctx.astype(F32)[None], c.astype(F32),
                             jnp.zeros((8 - N_COND, D_MODEL), F32)], axis=0)
    ada = ada_all(cond8, w_ada, b_ada)
    pm_ctx, inv_ctx, pm_lat, inv_lat = _pool_constants()
    fpos, fch, g1, h2, fch2 = _ft_constants()
    s5_mt, s5_q4, s5_rt, s5_al = s5_prep(s5_a_re, s5_a_im, s5_log_dt, s5_b_re, s5_b_im, s5_c_re, s5_c_im)
    new_dn, new_s5 = [], []
    for l in range(DEPTH):
        mod = ada[l, 0:N_COND].reshape(N_COND, 1, 3 * D_MODEL)
        pool, qkv, dn_z, s5, ft, ba = inproj(x, mod, norm_g[l].reshape(1, D_MODEL), _permute_w_in(w_in[l]))

        w_bd = jax.scipy.linalg.block_diag(*[pool_w[l, g] for g in range(4)]).astype(BF16)
        sc = pool_scale[l].reshape(1, BR)
        yp_ctx = pool_branch(pool, pm_ctx, inv_ctx, w_bd, sc, False)
        yp_lat = pool_branch(pool, pm_lat, inv_lat, w_bd, sc, True)

        ftw = ft_w[l].astype(BF16)
        yf_ctx = ft_ctx(ft, fpos, fch, ftw)
        yf_lat = ft_lat(ft, g1, h2, fch2, ftw)

        s5_y, fin_s5 = s5_branch(s5, s5_mt[l], s5_q4[l], s5_rt[l], s5_al[l], state_s5[:, l].astype(F32))
        new_s5.append(fin_s5)

        conv_w = jnp.pad(dn_conv[l], ((0, 8 - CONV_K), (0, 0)))
        gate_p = jnp.zeros((8, 128), F32)
        gate_p = gate_p.at[0, 8:16].set(dn_a_log[l].reshape(8)).at[1, 8:16].set(dn_dt_bias[l].reshape(8))
        dn_ctx, dn_lat, fin_dn = dn_branch(qkv, ba, conv_w, gate_p, state_delta[:, l].astype(F32))
        new_dn.append(fin_dn)

        x = outproj(x, mod, yp_ctx, yp_lat, dn_ctx, dn_lat, dn_z, s5_y, s5, yf_ctx, yf_lat, s5_d[l].reshape(1, BR),
                    s5_glu_w[l].astype(BF16), s5_glu_b[l].reshape(1, BR),
                    jnp.tile(dn_norm_g[l], DN_HEADS).reshape(1, BR), w_out[l].astype(BF16),
                    final_g.reshape(1, D_MODEL), l == DEPTH - 1)

    y_prompt = x[:N_CTX].reshape(BATCH, SEQ, D_MODEL).astype(x_prompt.dtype)
    y_sample = x[N_CTX:].reshape(DEC_BATCH, DEC_SEQ, D_MODEL).astype(x_sample.dtype)
    new_state_delta = jnp.stack(new_dn, axis=1).astype(state_delta.dtype)
    new_state_s5 = jnp.stack(new_s5, axis=1).astype(state_s5.dtype)
    return (y_prompt, y_sample, new_state_delta, new_state_s5)
```

```python
import functools
import math

import numpy as np
import jax
import jax.numpy as jnp
from jax import lax
from jax.experimental import pallas as pl
from jax.experimental.pallas import tpu as pltpu

F32 = jnp.float32
BF16 = jnp.bfloat16

D_MODEL = 1024
BATCH = 16
SEQ = 256
DEPTH = 4
DEC_BATCH = 2
DEC_SEQ = 4096
GRID_W = 64
GRID_H = DEC_SEQ // GRID_W
BR = 256
POOL_WINDOWS = (2, 4, 8, 16)
POOL_GD = 64
DN_HEADS = 4
DN_HD = 64
CONV_K = 5
CHUNK = 64
S5_P = 16
S5_G = 16
S5_N = 64
S5_L = 16
FT_HD = 64
EPS = 1e-6

N_CTX = BATCH * SEQ
N_LAT = DEC_BATCH * DEC_SEQ
N_TOK = N_CTX + N_LAT
N_COND = 1 + DEC_BATCH
TM = 512
W_IN_COLS = 2688
VMEM_LIMIT = 56 * 1024 * 1024

SDS = jax.ShapeDtypeStruct
BS = pl.BlockSpec


def _cparams(*sem):
    return pltpu.CompilerParams(dimension_semantics=sem, vmem_limit_bytes=VMEM_LIMIT)


def bdot(a, b):
    return jnp.dot(a.astype(BF16), b.astype(BF16), preferred_element_type=F32)


def hdot(a, b):
    return jnp.dot(a, b, preferred_element_type=F32, precision=lax.Precision.HIGHEST)


def silu(x):
    return x * jax.nn.sigmoid(x)


def _cond_index(i):
    tiles_ctx = N_CTX // TM
    tiles_seq = DEC_SEQ // TM
    return jnp.where(i < tiles_ctx, 0, 1 + (i - tiles_ctx) // tiles_seq)


def _ada_body(c_ref, w_ref, b_ref, o_ref):
    o_ref[0] = hdot(silu(c_ref[...]), w_ref[0]) + b_ref[0]


def ada_all(cond8, w_ada, b_ada):
    tn = 512
    return pl.pallas_call(
        _ada_body,
        out_shape=SDS((DEPTH, 8, 3 * D_MODEL), F32),
        grid=(DEPTH, 3 * D_MODEL // tn),
        in_specs=[BS((8, D_MODEL), lambda l, j: (0, 0)),
                  BS((1, D_MODEL, tn), lambda l, j: (l, 0, j)),
                  BS((1, 1, tn), lambda l, j: (l, 0, j))],
        out_specs=BS((1, 8, tn), lambda l, j: (l, 0, j)),
        compiler_params=_cparams("parallel", "parallel"),
        name="ada",
    )(cond8, w_ada, b_ada.reshape(DEPTH, 1, 3 * D_MODEL))


def _inproj_body(x_ref, mod_ref, g_ref, w_ref, pool_ref, qkv_ref, dnz_ref, s5_ref, ft_ref, ba_ref):
    x = x_ref[...]
    m = mod_ref[0]
    shift = m[:, 0:D_MODEL]
    scale = m[:, D_MODEL:2 * D_MODEL]
    xn = x * lax.rsqrt(jnp.mean(x * x, axis=-1, keepdims=True) + EPS) * g_ref[...]
    h = (xn * (1.0 + scale) + shift).astype(BF16)

    def proj(lo, hi):
        return jnp.dot(h, w_ref[:, lo:hi], preferred_element_type=F32)

    pool_ref[...] = proj(0, 512)
    qkv_ref[...] = proj(512, 1280)
    dnz_ref[...] = proj(1280, 1536)
    s5_ref[...] = proj(1536, 2048)
    ft_ref[...] = proj(2048, 2560)
    ba_ref[...] = proj(2560, 2688)


def inproj(x, mod, norm_g, w_in_p):
    widths = (512, 768, 256, 512, 512, 128)
    return pl.pallas_call(
        _inproj_body,
        out_shape=[SDS((N_TOK, w), F32) for w in widths],
        grid=(N_TOK // TM,),
        in_specs=[BS((TM, D_MODEL), lambda i: (i, 0)),
                  BS((1, 1, 3 * D_MODEL), lambda i: (_cond_index(i), 0, 0)),
                  BS((1, D_MODEL), lambda i: (0, 0)),
                  BS((D_MODEL, W_IN_COLS), lambda i: (0, 0))],
        out_specs=[BS((TM, w), lambda i: (i, 0)) for w in widths],
        compiler_params=_cparams("parallel"),
        name="inproj",
    )(x, mod, norm_g, w_in_p)


def _pool_body(u_ref, z_ref, pm_ref, inv_ref, w_ref, sc_ref, o_ref, *scratch, two_d):
    nblk = u_ref.shape[0] // 256
    if two_d:
        pad_ref, v_ref = scratch
        halo = 8 * GRID_W
        pad_ref[0:halo, :] = jnp.zeros((halo, BR), F32)
        pad_ref[halo + DEC_SEQ:2 * halo + DEC_SEQ, :] = jnp.zeros((halo, BR), F32)
        pad_ref[halo:halo + DEC_SEQ, :] = u_ref[...]
        lane = lax.broadcasted_iota(jnp.int32, (GRID_W, 128), 1)

        def row_body(r, c):
            base = pl.multiple_of(r * GRID_W, GRID_W)

            def slab(d, lo):
                return pad_ref[pl.ds(base + (8 + d) * GRID_W, GRID_W), lo:lo + 128]

            s2 = slab(-1, 0) + slab(0, 0)
            s4 = s2 + slab(-2, 0) + slab(1, 0)
            v_ref[pl.ds(base, GRID_W), 0:128] = jnp.where(lane < 64, s2, s4)
            s8 = slab(-4, 128)
            for d in (-3, -2, -1, 0, 1, 2, 3):
                s8 = s8 + slab(d, 128)
            s16 = s8
            for d in (-8, -7, -6, -5, 4, 5, 6, 7):
                s16 = s16 + slab(d, 128)
            v_ref[pl.ds(base, GRID_W), 128:256] = jnp.where(lane < 64, s8, s16)
            return c

        lax.fori_loop(0, GRID_H, row_body, 0)
        src = v_ref
    else:
        src = u_ref
    grp = lax.broadcasted_iota(jnp.int32, (256, BR), 1) // POOL_GD

    def blk_body(b, c):
        r0 = pl.multiple_of(b * 256, 256)
        vb = src[pl.ds(r0, 256), :]
        hi = vb.astype(BF16)
        lo = (vb - hi.astype(F32)).astype(BF16)
        res = jnp.zeros((256, BR), F32)
        for g in range(len(POOL_WINDOWS)):
            pg = (jnp.dot(pm_ref[g], hi, preferred_element_type=F32)
                  + jnp.dot(pm_ref[g], lo, preferred_element_type=F32))
            res = jnp.where(grp == g, pg, res)
        pooled = res * inv_ref[pl.ds(r0, 256), :]
        d = pooled - u_ref[pl.ds(r0, 256), :]
        y = bdot(d, w_ref[...]) * sc_ref[...]
        o_ref[pl.ds(r0, 256), :] = (y * silu(z_ref[pl.ds(r0, 256), :])).astype(BF16)
        return c

    lax.fori_loop(0, nblk, blk_body, 0)


def _band_matrices(seg):
    t = np.arange(256)
    out = []
    for w in POOL_WINDOWS:
        lo = t - w // 2
        hi = t - w // 2 + w
        s = t[None, :]
        m = (s >= lo[:, None]) & (s < hi[:, None]) & ((s // seg) == (t[:, None] // seg))
        out.append(m.astype(np.float32))
    return np.stack(out)


def _counts(length, w):
    pos = np.arange(length)
    return (np.clip(pos - w // 2 + w, 0, length) - np.clip(pos - w // 2, 0, length)).astype(np.float64)


def _pool_constants():
    inv_ctx = np.concatenate([np.repeat((1.0 / _counts(SEQ, w))[:, None], POOL_GD, 1) for w in POOL_WINDOWS], 1)
    inv_lat = []
    for w in POOL_WINDOWS:
        c2 = np.outer(_counts(GRID_H, w), _counts(GRID_W, w)).reshape(DEC_SEQ)
        inv_lat.append(np.repeat((1.0 / c2)[:, None], POOL_GD, 1))
    inv_lat = np.concatenate(inv_lat, 1)
    return (jnp.asarray(_band_matrices(SEQ), BF16), jnp.asarray(inv_ctx, F32),
            jnp.asarray(_band_matrices(GRID_W), BF16), jnp.asarray(inv_lat, F32))


def pool_branch(pool, pm, inv, w_bd, scale, two_d):
    if two_d:
        rows, nseq, blk0 = DEC_SEQ, DEC_BATCH, N_CTX // DEC_SEQ
        scratch = [pltpu.VMEM((DEC_SEQ + 16 * GRID_W, BR), F32), pltpu.VMEM((DEC_SEQ, BR), F32)]
    else:
        rows, nseq, blk0 = SEQ, BATCH, 0
        scratch = []
    return pl.pallas_call(
        functools.partial(_pool_body, two_d=two_d),
        out_shape=SDS((nseq * rows, BR), BF16),
        grid=(nseq,),
        in_specs=[BS((rows, BR), lambda i: (blk0 + i, 0)),
                  BS((rows, BR), lambda i: (blk0 + i, 1)),
                  BS((4, 256, 256), lambda i: (0, 0, 0)),
                  BS((rows, BR), lambda i: (0, 0)),
                  BS((BR, BR), lambda i: (0, 0)),
                  BS((1, BR), lambda i: (0, 0))],
        out_specs=BS((rows, BR), lambda i: (i, 0)),
        scratch_shapes=scratch,
        compiler_params=_cparams("parallel"),
        name="pool2d" if two_d else "pool1d",
    )(pool, pool, pm, inv, w_bd, scale)


def _ft_ctx_body(u_ref, z_ref, fpos_ref, fch_ref, w_ref, o_ref):
    uc = bdot(u_ref[...], fch_ref[...])
    st = jnp.concatenate([uc[:, 0:BR], uc[:, BR:2 * BR]], axis=0)
    f = bdot(fpos_ref[...], st)
    o_ref[...] = (bdot(f, w_ref[...]) * silu(z_ref[...])).astype(BF16)


def ft_ctx(ft, fpos, fch, ft_w):
    return pl.pallas_call(
        _ft_ctx_body,
        out_shape=SDS((N_CTX, BR), BF16),
        grid=(BATCH,),
        in_specs=[BS((SEQ, BR), lambda i: (i, 0)),
                  BS((SEQ, BR), lambda i: (i, 1)),
                  BS((SEQ, 2 * SEQ), lambda i: (0, 0)),
                  BS((BR, 2 * BR), lambda i: (0, 0)),
                  BS((BR, BR), lambda i: (0, 0))],
        out_specs=BS((SEQ, BR), lambda i: (i, 0)),
        compiler_params=_cparams("parallel"),
        name="ft_ctx",
    )(ft, ft, fpos, fch, ft_w)


def _ft_lat_body(u_ref, z_ref, g_ref, h_ref, fch_ref, w_ref, o_ref, x_ref, yr_ref, yi_ref):
    for hf in range(2):
        x_ref[hf] = u_ref[:, hf * 128:(hf + 1) * 128]

    def stage1(t2, c):
        xs = jnp.concatenate([x_ref[hf, pl.ds(t2, GRID_H, stride=GRID_W), :] for hf in range(2)], axis=1)
        y = jnp.dot(g_ref[t2], xs.astype(BF16), preferred_element_type=F32)
        r0 = pl.multiple_of(t2 * GRID_W, GRID_W)
        for hf in range(2):
            yr_ref[hf, pl.ds(r0, GRID_W), :] = y[0:64, hf * 128:(hf + 1) * 128]
            yi_ref[hf, pl.ds(r0, GRID_W), :] = y[64:128, hf * 128:(hf + 1) * 128]
        return c

    lax.fori_loop(0, GRID_W, stage1, 0, unroll=8)

    def stage2(kb, c):
        yr = jnp.concatenate([yr_ref[hf, pl.ds(kb, GRID_W, stride=GRID_W), :] for hf in range(2)], axis=1)
        yi = jnp.concatenate([yi_ref[hf, pl.ds(kb, GRID_W, stride=GRID_W), :] for hf in range(2)], axis=1)
        st = jnp.concatenate([yr, yi], axis=0).astype(BF16)
        a = jnp.dot(h_ref[...], st, preferred_element_type=F32)
        for hf in range(2):
            yr_ref[hf, pl.ds(kb, GRID_W, stride=GRID_W), :] = a[0:64, hf * 128:(hf + 1) * 128]
            yi_ref[hf, pl.ds(kb, GRID_W, stride=GRID_W), :] = a[64:128, hf * 128:(hf + 1) * 128]
        return c

    lax.fori_loop(0, GRID_W, stage2, 0, unroll=8)

    def stage3(b, c):
        r0 = pl.multiple_of(b * TM, TM)
        ar = jnp.concatenate([yr_ref[hf, pl.ds(r0, TM), :] for hf in range(2)], axis=1)
        ai = jnp.concatenate([yi_ref[hf, pl.ds(r0, TM), :] for hf in range(2)], axis=1)
        f = bdot(ar, fch_ref[0:BR, :]) + bdot(ai, fch_ref[BR:2 * BR, :])
        o_ref[pl.ds(r0, TM), :] = (bdot(f, w_ref[...]) * silu(z_ref[pl.ds(r0, TM), :])).astype(BF16)
        return c

    lax.fori_loop(0, DEC_SEQ // TM, stage3, 0)


def ft_lat(ft, g1, h2, fch2, ft_w):
    blk0 = N_CTX // DEC_SEQ
    return pl.pallas_call(
        _ft_lat_body,
        out_shape=SDS((N_LAT, BR), BF16),
        grid=(DEC_BATCH,),
        in_specs=[BS((DEC_SEQ, BR), lambda i: (blk0 + i, 0)),
                  BS((DEC_SEQ, BR), lambda i: (blk0 + i, 1)),
                  BS((GRID_W, 128, GRID_H), lambda i: (0, 0, 0)),
                  BS((128, 128), lambda i: (0, 0)),
                  BS((2 * BR, BR), lambda i: (0, 0)),
                  BS((BR, BR), lambda i: (0, 0))],
        out_specs=BS((DEC_SEQ, BR), lambda i: (i, 0)),
        scratch_shapes=[pltpu.VMEM((2, DEC_SEQ, 128), F32)] * 3,
        compiler_params=_cparams("parallel"),
        name="ft_lat",
    )(ft, ft, g1, h2, fch2, ft_w)


def _ft_constants():
    c = np.arange(FT_HD)
    ang = 2.0 * np.pi * np.outer(c, c) / FT_HD
    eye4 = np.eye(BR // FT_HD)
    cc = np.kron(eye4, np.cos(ang)) / 8.0
    sc = np.kron(eye4, np.sin(ang)) / 8.0
    t = np.arange(SEQ)
    angt = 2.0 * np.pi * (np.outer(t, t) % SEQ) / SEQ
    fpos = np.concatenate([np.cos(angt), -np.sin(angt)], axis=1) / 16.0
    fch = np.concatenate([cc, sc], axis=1)
    kb = np.arange(GRID_W)[None, :, None]
    t1 = np.arange(GRID_H)[None, None, :]
    t2 = np.arange(GRID_W)[:, None, None]
    a1 = 2.0 * np.pi * ((kb * (GRID_W * t1 + t2)) % DEC_SEQ) / DEC_SEQ
    g1 = np.concatenate([np.cos(a1), -np.sin(a1)], axis=1) / 8.0
    a2 = 2.0 * np.pi * (np.outer(np.arange(GRID_W), np.arange(GRID_W)) % GRID_W) / GRID_W
    c2, s2 = np.cos(a2) / 8.0, np.sin(a2) / 8.0
    h2 = np.block([[c2, s2], [-s2, c2]])
    fch2 = np.concatenate([cc, sc], axis=0)
    as_bf = lambda a: jnp.asarray(a, F32).astype(BF16)
    return as_bf(fpos), as_bf(fch), as_bf(g1), as_bf(h2), as_bf(fch2)


S5_ROWS = N_TOK // S5_L
S5_TM = 1024
S5_TR = S5_TM // S5_L
S5_CTX_C = SEQ // S5_L
S5_LAT_C = DEC_SEQ // S5_L
S5_LT = 2 * S5_G


def _s5_prep_body(ar_ref, ai_ref, ldt_ref, br_ref, bi_ref, btr_ref, bti_ref, cr_ref, ci_ref,
                  mt_ref, q4_ref, rt_ref, al_ref):
    L = S5_L
    m = lax.broadcasted_iota(jnp.int32, (2 * L, S5_N), 0).astype(F32)
    rts, qs, ds, als = [], [], [], []
    for d in range(2):
        a_re, a_im = ar_ref[0, d, 0], ai_ref[0, d, 0]
        dt = jnp.exp(ldt_ref[0, d, 0])
        xr, xi = a_re * dt, a_im * dt
        mag = jnp.exp(xr)
        ab_re, ab_im = mag * jnp.cos(xi), mag * jnp.sin(xi)
        den = a_re * a_re + a_im * a_im
        nr = ab_re - 1.0
        coef_re = (nr * a_re + ab_im * a_im) / den
        coef_im = (ab_im * a_re - nr * a_im) / den
        pw_re = jnp.exp(m * xr) * jnp.cos(m * xi)
        pw_im = jnp.exp(m * xr) * jnp.sin(m * xi)
        cq_re = pw_re * coef_re - pw_im * coef_im
        cq_im = pw_re * coef_im + pw_im * coef_re
        c_re, c_im = cr_ref[0, d, 0], ci_ref[0, d, 0]
        bt_re, bt_im = btr_ref[0, d, 0], bti_ref[0, d, 0]
        row = lambda x, e: x[e:e + 1, :]
        order = range(L) if d == 0 else range(L - 1, -1, -1)
        cp_re = jnp.concatenate([c_re * row(cq_re, e) - c_im * row(cq_im, e) for e in order], axis=0)
        cp_im = jnp.concatenate([c_re * row(cq_im, e) + c_im * row(cq_re, e) for e in order], axis=0)
        rts.append(hdot(cp_re, br_ref[0, d, 0]) - hdot(cp_im, bi_ref[0, d, 0]))
        inj = [L - 1 - i for i in range(L)] if d == 0 else list(range(L))
        q_re = jnp.concatenate([bt_re * row(cq_re, e) - bt_im * row(cq_im, e) for e in inj], axis=0)
        q_im = jnp.concatenate([bt_im * row(cq_re, e) + bt_re * row(cq_im, e) for e in inj], axis=0)
        qs.append((q_re, q_im))
        out = [j + 1 for j in range(L)] if d == 0 else [L - j for j in range(L)]
        d_re = jnp.concatenate([c_re * row(pw_re, e) - c_im * row(pw_im, e) for e in out], axis=0)
        d_im = jnp.concatenate([c_re * row(pw_im, e) + c_im * row(pw_re, e) for e in out], axis=0)
        ds.append((d_re, d_im))
        als.append((row(pw_re, L), row(pw_im, L)))
    pad = jnp.zeros(((L - 1) * S5_P, S5_P), F32)
    z = jnp.concatenate([pad, rts[0]], axis=0) + jnp.concatenate([rts[1], pad], axis=0)
    mt = jnp.concatenate([z[(L - 1 - i) * S5_P:(L - 1 - i) * S5_P + L * S5_P, :] for i in range(L)], axis=1)
    mt_ref[0, 0] = mt.astype(BF16)
    q4_ref[0, 0] = jnp.concatenate([qs[0][0], qs[1][0], qs[0][1], qs[1][1]], axis=1).astype(BF16)
    rt_ref[0, 0] = jnp.concatenate([ds[0][0], ds[1][0], -ds[0][1], -ds[1][1]], axis=1).astype(BF16)
    al_ref[0, 0] = jnp.concatenate([jnp.concatenate([als[0][0], als[1][0]], axis=1),
                                    jnp.concatenate([als[0][1], als[1][1]], axis=1)], axis=0)


def s5_prep(a_re, a_im, log_dt, b_re, b_im, c_re, c_im):
    vec = lambda x: x.reshape(DEPTH, 2, S5_G, 1, S5_N)
    ldt = jnp.broadcast_to(log_dt[..., None, None], (DEPTH, 2, S5_G, 1, S5_N))
    bt = lambda x: x.transpose(0, 1, 2, 4, 3)
    vspec = BS((1, 2, 1, 1, S5_N), lambda i: (i // S5_G, 0, i % S5_G, 0, 0))
    bspec = BS((1, 2, 1, S5_N, S5_P), lambda i: (i // S5_G, 0, i % S5_G, 0, 0))
    cspec = BS((1, 2, 1, S5_P, S5_N), lambda i: (i // S5_G, 0, i % S5_G, 0, 0))
    mat = SDS((DEPTH, S5_G, 256, 256), BF16)
    mspec = BS((1, 1, 256, 256), lambda i: (i // S5_G, i % S5_G, 0, 0))
    return pl.pallas_call(
        _s5_prep_body,
        out_shape=[mat, mat, mat, SDS((DEPTH, S5_G, 2, 128), F32)],
        grid=(DEPTH * S5_G,),
        in_specs=[vspec, vspec, vspec, bspec, bspec, cspec, cspec, cspec, cspec],
        out_specs=[mspec, mspec, mspec, BS((1, 1, 2, 128), lambda i: (i // S5_G, i % S5_G, 0, 0))],
        compiler_params=_cparams("parallel"),
        name="s5_prep",
    )(vec(a_re), vec(a_im), ldt, b_re, b_im, bt(b_re), bt(b_im), c_re, c_im)


def _block_transpose(arrs):
    blk = lax.broadcasted_iota(jnp.int32, arrs[0].shape, 1) // 16
    cur = list(arrs)
    for b in range(4):
        s = 16 << b
        hi = ((blk >> b) & 1) == 1
        nxt = list(cur)
        for x in range(16):
            if (x >> b) & 1:
                continue
            y = x | (1 << b)
            nxt[x] = jnp.where(hi, pltpu.roll(cur[y], s, 1), cur[x])
            nxt[y] = jnp.where(hi, cur[y], pltpu.roll(cur[x], 256 - s, 1))
        cur = nxt
    return cur


def _s5_in_body(s5_ref, mt_ref, q4_ref, y_ref, e_ref, x_ref):
    for hf in range(2):
        x_ref[hf] = s5_ref[:, hf * 128:(hf + 1) * 128]
    xs = [jnp.concatenate([x_ref[hf, pl.ds(i, S5_TR, stride=S5_L), :] for hf in range(2)], axis=1)
          for i in range(S5_L)]
    us = _block_transpose(xs)
    for g in range(S5_G):
        ub = us[g].astype(BF16)
        y_ref[:, g * 256:(g + 1) * 256] = lax.dot_general(ub, mt_ref[g], (((1,), (1,)), ((), ())),
                                                          preferred_element_type=F32)
        e = jnp.dot(ub, q4_ref[g], preferred_element_type=F32)
        e_ref[2 * g] = e[:, 0:128]
        e_ref[2 * g + 1] = e[:, 128:256]


def s5_in(s5, mt, q4):
    wspec = BS((S5_G, 256, 256), lambda t: (0, 0, 0))
    return pl.pallas_call(
        _s5_in_body,
        out_shape=[SDS((S5_ROWS, S5_G * 256), F32), SDS((S5_LT, S5_ROWS, 128), F32)],
        grid=(N_TOK // S5_TM,),
        in_specs=[BS((S5_TM, BR), lambda t: (t, 0)), wspec, wspec],
        out_specs=[BS((S5_TR, S5_G * 256), lambda t: (t, 0)), BS((S5_LT, S5_TR, 128), lambda t: (0, t, 0))],
        scratch_shapes=[pltpu.VMEM((2, S5_TM, 128), F32)],
        compiler_params=_cparams("parallel"),
        name="s5_in",
    )(s5, mt, q4)


def _s5_chunk_scan_body(e_ref, al_ref, s0_ref, spf_ref, spb_ref, fin_ref):
    def update(g, s_re, s_im, e_re, e_im):
        a_re, a_im = al_ref[g, 0:1, :], al_ref[g, 1:2, :]
        return a_re * s_re - a_im * s_im + e_re, a_re * s_im + a_im * s_re + e_im

    fwd_c = lax.broadcasted_iota(jnp.int32, (BATCH, 128), 1) < S5_N

    def ctx_group(g, carry):
        s_re = jnp.zeros((BATCH, 128), F32)
        s_im = jnp.zeros((BATCH, 128), F32)
        for c in range(S5_CTX_C):
            rf = pl.ds(c, BATCH, stride=S5_CTX_C)
            rb = pl.ds(S5_CTX_C - 1 - c, BATCH, stride=S5_CTX_C)
            spf_ref[2 * g, rf, :] = s_re
            spf_ref[2 * g + 1, rf, :] = s_im
            spb_ref[2 * g, rb, :] = s_re
            spb_ref[2 * g + 1, rb, :] = s_im
            e_re = jnp.where(fwd_c, e_ref[2 * g, rf, :], e_ref[2 * g, rb, :])
            e_im = jnp.where(fwd_c, e_ref[2 * g + 1, rf, :], e_ref[2 * g + 1, rb, :])
            s_re, s_im = update(g, s_re, s_im, e_re, e_im)
        fin_ref[2 * g] = s_re
        fin_ref[2 * g + 1] = s_im
        return carry

    lax.fori_loop(0, S5_G, ctx_group, 0)

    row0 = BATCH * S5_CTX_C
    fwd_l = lax.broadcasted_iota(jnp.int32, (DEC_BATCH, 128), 1) < S5_N

    def lat_step(c, state):
        rf = pl.ds(row0 + c, DEC_BATCH, stride=S5_LAT_C)
        rb = pl.ds(row0 + S5_LAT_C - 1 - c, DEC_BATCH, stride=S5_LAT_C)
        new = []
        for g in range(S5_G):
            s_re, s_im = state[2 * g], state[2 * g + 1]
            spf_ref[2 * g, rf, :] = s_re
            spf_ref[2 * g + 1, rf, :] = s_im
            spb_ref[2 * g, rb, :] = s_re
            spb_ref[2 * g + 1, rb, :] = s_im
            e_re = jnp.where(fwd_l, e_ref[2 * g, rf, :], e_ref[2 * g, rb, :])
            e_im = jnp.where(fwd_l, e_ref[2 * g + 1, rf, :], e_ref[2 * g + 1, rb, :])
            new.extend(update(g, s_re, s_im, e_re, e_im))
        return tuple(new)

    lax.fori_loop(0, S5_LAT_C, lat_step, tuple(s0_ref[t] for t in range(S5_LT)))


def s5_chunk_scan(e3, al, s0_lat):
    sp = SDS((S5_LT, S5_ROWS, 128), F32)
    return pl.pallas_call(
        _s5_chunk_scan_body,
        out_shape=[sp, sp, SDS((S5_LT, BATCH, 128), F32)],
        compiler_params=pltpu.CompilerParams(vmem_limit_bytes=VMEM_LIMIT),
        name="s5_chunk_scan",
    )(e3, al, s0_lat)


def _s5_fin_body(y_ref, spf_ref, spb_ref, rt_ref, o_ref, t_ref):
    fwd = lax.broadcasted_iota(jnp.int32, (S5_TR, 128), 1) < S5_N
    ys = []
    for g in range(S5_G):
        s_re = jnp.where(fwd, spf_ref[2 * g], spb_ref[2 * g])
        s_im = jnp.where(fwd, spf_ref[2 * g + 1], spb_ref[2 * g + 1])
        sp = jnp.concatenate([s_re, s_im], axis=1).astype(BF16)
        ys.append(y_ref[:, g * 256:(g + 1) * 256]
                  + lax.dot_general(sp, rt_ref[g], (((1,), (1,)), ((), ())), preferred_element_type=F32))
    xs = _block_transpose(ys)
    for j in range(S5_L):
        for hf in range(2):
            t_ref[hf, pl.ds(j, S5_TR, stride=S5_L), :] = xs[j][:, hf * 128:(hf + 1) * 128]
    o_ref[...] = jnp.concatenate([t_ref[0], t_ref[1]], axis=1)


def s5_fin(yi, spf, spb, rt):
    lt = BS((S5_LT, S5_TR, 128), lambda t: (0, t, 0))
    return pl.pallas_call(
        _s5_fin_body,
        out_shape=SDS((N_TOK, BR), F32),
        grid=(N_TOK // S5_TM,),
        in_specs=[BS((S5_TR, S5_G * 256), lambda t: (t, 0)), lt, lt, BS((S5_G, 256, 256), lambda t: (0, 0, 0))],
        out_specs=BS((S5_TM, BR), lambda t: (t, 0)),
        scratch_shapes=[pltpu.VMEM((2, S5_TM, 128), F32)],
        compiler_params=_cparams("parallel"),
        name="s5_fin",
    )(yi, spf, spb, rt)


def s5_branch(s5, mt, q4, rt, al, state_l):
    yi, e3 = s5_in(s5, mt, q4)
    s0 = state_l.transpose(3, 2, 0, 1, 4).reshape(S5_LT, DEC_BATCH, 2 * S5_N)
    spf, spb, fin = s5_chunk_scan(e3, al, s0)
    y = s5_fin(yi, spf, spb, rt)
    fin = fin.reshape(S5_G, 2, BATCH, 2, S5_N).transpose(2, 3, 1, 0, 4)
    return y, fin


DN_TILE = 256
DN_HALO = 8


N_SEG = N_TOK // DN_TILE
assert N_CTX == DEC_SEQ


def _split3(x):
    x1 = x.astype(BF16)
    r1 = x - x1.astype(F32)
    x2 = r1.astype(BF16)
    return x1, x2, (r1 - x2.astype(F32)).astype(BF16)


def dot3(a, b, exact):
    if exact == 'b':
        return sum(jnp.dot(p, b.astype(BF16), preferred_element_type=F32) for p in _split3(a))
    return sum(jnp.dot(a.astype(BF16), p, preferred_element_type=F32) for p in _split3(b))


def _dn_conv_body(x_ref, prev_ref, next_ref, ba_ref, w_ref, gp_ref, ex_ref, o_ref, gb_ref, pad_ref):
    i = pl.program_id(0)
    tiles_ctx = N_CTX // DN_TILE
    tiles_seq = DEC_SEQ // DN_TILE
    j = (i - tiles_ctx) % tiles_seq
    first = jnp.logical_or(i < tiles_ctx, j == 0)
    last = jnp.logical_or(i < tiles_ctx, j == tiles_seq - 1)
    pad_ref[0:DN_HALO, :] = jnp.where(first, 0.0, prev_ref[...])
    pad_ref[DN_HALO:DN_HALO + DN_TILE, :] = x_ref[...]
    pad_ref[DN_HALO + DN_TILE:2 * DN_HALO + DN_TILE, :] = jnp.where(last, 0.0, next_ref[...])
    for r0 in range(0, DN_TILE, 128):
        for c0 in range(0, 3 * BR, 128):
            acc = jnp.zeros((128, 128), F32)
            for t in range(CONV_K):
                acc = acc + (pad_ref[pl.ds(r0 + DN_HALO - CONV_K // 2 + t, 128), c0:c0 + 128]
                             * w_ref[t:t + 1, c0:c0 + 128])
            o_ref[r0:r0 + 128, c0:c0 + 128] = silu(acc)
    raw = ba_ref[...]
    lane = lax.broadcasted_iota(jnp.int32, raw.shape, 1)
    xa = raw + gp_ref[1:2, :]
    sp = jnp.maximum(xa, 0.0) + jnp.log1p(jnp.exp(-jnp.abs(xa)))
    gates = jnp.where(lane < 2 * DN_HEADS, jax.nn.sigmoid(raw), -jnp.exp(gp_ref[0:1, :]) * sp)
    r = lax.broadcasted_iota(jnp.int32, (DN_TILE, DN_TILE), 0)
    c = lax.broadcasted_iota(jnp.int32, (DN_TILE, DN_TILE), 1)
    same = (r // CHUNK) == (c // CHUNK)
    ex = dot3(gates, ex_ref[...], 'b')
    gb_ref[:, 0:2 * BR] = ex[:, 0:2 * BR]
    gb_ref[:, 2 * BR:3 * BR] = dot3(jnp.logical_and(same, c <= r), ex[:, 2 * BR:3 * BR], 'a')
    gb_ref[:, 3 * BR:4 * BR] = dot3(jnp.logical_and(same, c >= r), ex[:, 3 * BR:4 * BR], 'a')


def _gate_expand():
    e = np.zeros((128, 4 * BR), np.float32)
    for blk in range(4):
        for h in range(DN_HEADS):
            e[blk * DN_HEADS + h, blk * BR + h * DN_HD:blk * BR + (h + 1) * DN_HD] = 1.0
    return jnp.asarray(e, BF16)


def _head_block_mask():
    r = lax.broadcasted_iota(jnp.int32, (BR, BR), 0) // DN_HD
    c = lax.broadcasted_iota(jnp.int32, (BR, BR), 1) // DN_HD
    return r == c


def _split2(x):
    hi = x.astype(BF16)
    return hi, (x - hi.astype(F32)).astype(BF16)


def _dn_local_body(qkv_ref, gb_ref, *out_refs):
    f_refs, b_refs = out_refs[0:6], out_refs[6:12]
    ncb = DN_TILE // CHUNK
    nb = 2 * ncb
    bmask = _head_block_mask()
    ri = lax.broadcasted_iota(jnp.int32, (nb, CHUNK, BR), 1)
    cj = lax.broadcasted_iota(jnp.int32, (nb, CHUNK, BR), 2) % DN_HD
    bwd = lax.broadcasted_iota(jnp.int32, (nb, CHUNK, BR), 0) >= ncb
    eye = cj == ri
    incl = jnp.logical_or(jnp.logical_and(bwd, cj >= ri), jnp.logical_and(jnp.logical_not(bwd), cj <= ri))
    strict = jnp.logical_and(incl, jnp.logical_not(eye))
    ones_blk = bmask.astype(BF16)

    def chunks(x):
        return x.reshape(ncb, CHUNK, BR)

    def both(x):
        return jnp.concatenate([x, x], axis=0)

    def head_sum(x):
        return dot3(x.reshape(-1, BR), ones_blk, 'b').reshape(x.shape)

    def bd(x):
        return jnp.where(bmask, jnp.concatenate([x] * DN_HEADS, axis=1), jnp.zeros((), x.dtype))

    def bmm(a, b):
        return jnp.einsum('bij,bjk->bik', a, b, preferred_element_type=F32)

    q = chunks(qkv_ref[:, 0:BR])
    k = chunks(qkv_ref[:, BR:2 * BR])
    v = both(chunks(qkv_ref[:, 2 * BR:3 * BR]))
    q = q * lax.rsqrt(head_sum(q * q) + EPS) * (DN_HD ** -0.5)
    k = k * lax.rsqrt(head_sum(k * k) + EPS)
    kq = jnp.einsum('bik,bjk->bij', jnp.concatenate([k, q], axis=1).astype(BF16), bd(k.astype(BF16)),
                    preferred_element_type=F32)
    kk, qk = both(kq[:, 0:CHUNK]), both(kq[:, CHUNK:2 * CHUNK])
    q, k = both(q), both(k)
    beta = jnp.concatenate([chunks(gb_ref[:, 0:BR]), chunks(gb_ref[:, BR:2 * BR])], axis=0)
    gc = jnp.concatenate([chunks(gb_ref[:, 2 * BR:3 * BR]), chunks(gb_ref[:, 3 * BR:4 * BR])], axis=0)
    crow = jnp.sum(jnp.where(eye, gc, 0.0), axis=1, keepdims=True)
    decay = jnp.where(incl, jnp.exp(jnp.where(incl, gc - crow, 0.0)), 0.0)
    a = jnp.where(strict, kk * decay * beta, 0.0)
    tinv = jnp.where(eye, 1.0, 0.0) - a
    pw = a
    pw_bd = bd(pw.astype(BF16))
    for _ in range(5):
        pw = bmm(pw.astype(BF16), pw_bd)
        pw_bd = bd(pw.astype(BF16))
        tinv = tinv + bmm(tinv.astype(BF16), pw_bd)
    egc = jnp.exp(gc)
    t_hi, t_lo = _split2(tinv)

    def solve(rhs):
        r_hi, r_lo = _split2(rhs)
        r_hi, r_lo = bd(r_hi), bd(r_lo)
        return bmm(t_hi, r_hi) + bmm(t_hi, r_lo) + bmm(t_lo, r_hi)

    w = bmm(t_hi, bd((k * (beta * egc)).astype(BF16)))

    bwd_row = lax.broadcasted_iota(jnp.int32, (nb, 1, BR), 0) >= ncb
    g_last = jnp.where(bwd_row, gc[:, 0:1], gc[:, CHUNK - 1:CHUNK])
    eg = jnp.exp(g_last)
    outs = (w, solve(v * beta), q * egc, k * jnp.exp(g_last - gc),
            jnp.where(incl, qk * decay, 0.0))
    for d, refs in enumerate((f_refs, b_refs)):
        for ref, x in zip(refs[0:5], outs):
            ref[...] = x[d * ncb:(d + 1) * ncb].reshape(DN_TILE, BR).astype(ref.dtype)
        refs[5][0] = jnp.concatenate([eg[d * ncb:(d + 1) * ncb, 0], jnp.zeros((8 - ncb, BR), F32)], axis=0)


_DN_LOCAL_DTYPES = (BF16, F32, BF16, BF16, BF16)


def _dn_prep_body(x_ref, prev_ref, next_ref, ba_ref, w_ref, gp_ref, ex_ref, *rest):
    out_refs, (pad_ref, qkv_ref, gb_ref) = rest[0:12], rest[12:15]
    _dn_conv_body(x_ref, prev_ref, next_ref, ba_ref, w_ref, gp_ref, ex_ref, qkv_ref, gb_ref, pad_ref)
    _dn_local_body(qkv_ref, gb_ref, *out_refs)


def dn_local(qkv, ba, conv_w, gate_p):
    per = DN_TILE // DN_HALO
    nhb = N_TOK // DN_HALO
    tok = BS((DN_TILE, BR), lambda i: (i, 0))
    shapes = [SDS((N_TOK, BR), dt) for dt in _DN_LOCAL_DTYPES] + [SDS((N_SEG, 8, BR), F32)]
    specs = [tok] * len(_DN_LOCAL_DTYPES) + [BS((1, 8, BR), lambda i: (i, 0, 0))]
    outs = pl.pallas_call(
        _dn_prep_body,
        out_shape=shapes * 2,
        grid=(N_SEG,),
        in_specs=[BS((DN_TILE, 3 * BR), lambda i: (i, 0)),
                  BS((DN_HALO, 3 * BR), lambda i: (jnp.maximum(i * per - 1, 0), 0)),
                  BS((DN_HALO, 3 * BR), lambda i: (jnp.minimum((i + 1) * per, nhb - 1), 0)),
                  BS((DN_TILE, 128), lambda i: (i, 0)),
                  BS((8, 3 * BR), lambda i: (0, 0)),
                  BS((8, 128), lambda i: (0, 0)),
                  BS((128, 4 * BR), lambda i: (0, 0))],
        out_specs=specs * 2,
        scratch_shapes=[pltpu.VMEM((DN_TILE + 2 * DN_HALO, 3 * BR), F32),
                        pltpu.VMEM((DN_TILE, 3 * BR), F32),
                        pltpu.VMEM((DN_TILE, 4 * BR), F32)],
        compiler_params=_cparams("parallel"),
        name="dn_local",
    )(qkv, qkv, qkv, ba, conv_w, gate_p, _gate_expand())
    return outs[0:6], outs[6:12]


def _dn_advance(chains, s, bmask):
    ncb = DN_TILE // CHUNK
    bmm = lambda a, b: jnp.einsum('bij,bjk->bik', a, b, preferred_element_type=F32)
    steps = []
    for t in range(ncb):
        cs = [ncb - 1 - t if rev else t for _, rev in chains]
        w, u, qt, kt, aqk, eg = [jnp.stack([load(k, c) for (load, _), c in zip(chains, cs)]) for k in range(6)]
        sb = s.astype(BF16)
        v_new = u - bmm(w, sb)
        vb = v_new.astype(BF16)
        v_bd = jnp.where(bmask, jnp.concatenate([vb] * DN_HEADS, axis=1), jnp.zeros((), BF16))
        o = bmm(qt, sb) + bmm(aqk, v_bd)
        upd = jnp.einsum('btk,btv->bkv', kt, vb, preferred_element_type=F32)
        s = s * eg + jnp.where(bmask, upd, 0.0)
        steps.append((cs, o))
    return s, steps


def _dn_seq_ctx_body(*refs):
    nseq = 4
    f_in, b_in = refs[0:6], refs[6:12]
    of_ref, ob_ref, finf_ref, finb_ref = refs[12:16]
    bmask = _head_block_mask()

    def loader(in_refs, q):
        def load(k, c):
            if k == 5:
                return in_refs[5][q, c:c + 1, :]
            return in_refs[k][q * DN_TILE + c * CHUNK:q * DN_TILE + (c + 1) * CHUNK, :]
        return load

    chains = [(loader(f_in, q), False) for q in range(nseq)] + [(loader(b_in, q), True) for q in range(nseq)]
    s, steps = _dn_advance(chains, jnp.zeros((2 * nseq, BR, BR), F32), bmask)
    for cs, o in steps:
        for i, c in enumerate(cs):
            o_ref, q = (of_ref, i) if i < nseq else (ob_ref, i - nseq)
            o_ref[q * DN_TILE + c * CHUNK:q * DN_TILE + (c + 1) * CHUNK, :] = o[i]
    finf_ref[...] = s[0:nseq]
    finb_ref[...] = s[nseq:2 * nseq]


def dn_seq_ctx(loc_f, loc_b):
    nseq = 4
    tok = BS((nseq * DN_TILE, BR), lambda i: (i, 0))
    specs = [tok] * len(_DN_LOCAL_DTYPES) + [BS((nseq, 8, BR), lambda i: (i, 0, 0))]
    st = BS((nseq, BR, BR), lambda i: (i, 0, 0))
    return pl.pallas_call(
        _dn_seq_ctx_body,
        out_shape=[SDS((N_CTX, BR), F32), SDS((N_CTX, BR), F32), SDS((BATCH, BR, BR), F32), SDS((BATCH, BR, BR), F32)],
        grid=(BATCH // nseq,),
        in_specs=specs * 2,
        out_specs=[tok, tok, st, st],
        compiler_params=_cparams("parallel"),
        name="dn_seq_ctx",
    )(*loc_f, *loc_b)


def _dn_seq_lat_body(*refs):
    f_in, b_in = refs[0:6], refs[6:12]
    s0_ref, of_ref, ob_ref, s_ref = refs[12:16]

    @pl.when(pl.program_id(0) == 0)
    def _():
        s_ref[...] = s0_ref[...]

    def loader(in_refs, q):
        def load(k, c):
            if k == 5:
                return in_refs[5][1 + q, 0, c:c + 1, :]
            return in_refs[k][1 + q, c * CHUNK:(c + 1) * CHUNK, :]
        return load

    chains = ([(loader(f_in, q), False) for q in range(DEC_BATCH)]
              + [(loader(b_in, q), True) for q in range(DEC_BATCH)])
    s, steps = _dn_advance(chains, s_ref[...], _head_block_mask())
    s_ref[...] = s
    for cs, o in steps:
        for i, c in enumerate(cs):
            o_ref, q = (of_ref, i) if i < DEC_BATCH else (ob_ref, i - DEC_BATCH)
            o_ref[q, c * CHUNK:(c + 1) * CHUNK, :] = o[i]


def dn_seq_lat(loc_f, loc_b, s0):
    nseg = DEC_SEQ // DN_TILE
    grp = lambda a: a.reshape((N_COND, nseg) + a.shape[1:])

    def specs(m):
        return ([BS((N_COND, DN_TILE, BR), lambda g: (0, m(g), 0))] * len(_DN_LOCAL_DTYPES)
                + [BS((N_COND, 1, 8, BR), lambda g: (0, m(g), 0, 0))])

    fwd = lambda g: g
    bwd = lambda g: nseg - 1 - g
    views = lambda loc: [a.reshape(N_COND, DEC_SEQ, BR) for a in loc[0:5]] + [grp(loc[5])]
    out = SDS((DEC_BATCH, DEC_SEQ, BR), F32)
    return pl.pallas_call(
        _dn_seq_lat_body,
        out_shape=[out, out],
        grid=(nseg,),
        in_specs=specs(fwd) + specs(bwd) + [BS((2 * DEC_BATCH, BR, BR), lambda g: (0, 0, 0))],
        out_specs=[BS((DEC_BATCH, DN_TILE, BR), lambda g: (0, fwd(g), 0)),
                   BS((DEC_BATCH, DN_TILE, BR), lambda g: (0, bwd(g), 0))],
        scratch_shapes=[pltpu.VMEM((2 * DEC_BATCH, BR, BR), F32)],
        compiler_params=_cparams("arbitrary"),
        name="dn_seq_lat",
    )(*views(loc_f), *views(loc_b), s0)


def dn_branch(qkv, ba, conv_w, gate_p, s0_lat):
    loc_f, loc_b = dn_local(qkv, ba, conv_w, gate_p)
    eye_h = jnp.eye(DN_HEADS, dtype=F32)
    s0 = s0_lat.transpose(1, 0, 2, 3, 4).reshape(2 * DEC_BATCH, DN_HEADS, DN_HD, DN_HD)
    s0 = jnp.einsum('shkv,hg->shkgv', s0, eye_h).reshape(2 * DEC_BATCH, BR, BR)
    of_c, ob_c, fin_f, fin_b = dn_seq_ctx(loc_f, loc_b)
    of_l, ob_l = dn_seq_lat(loc_f, loc_b, s0)
    unblock = lambda s: jnp.einsum('shkgv,hg->shkv', s.reshape(BATCH, DN_HEADS, DN_HD, DN_HEADS, DN_HD), eye_h)
    fin = jnp.stack([unblock(fin_f), unblock(fin_b)], axis=1)
    return (of_c, ob_c), (of_l.reshape(N_LAT, BR), ob_l.reshape(N_LAT, BR)), fin


def _outproj_body(x_ref, mod_ref, ypc_ref, ypl_ref, dofc_ref, dofl_ref, dobc_ref, dobl_ref, dnz_ref, s5y_ref, s5_ref,
                  yfc_ref, yfl_ref, d_ref, gw_ref, gb_ref, dng_ref, w_ref, fg_ref, o_ref, *, final):
    is_ctx = pl.program_id(0) < N_CTX // TM
    y_pool = jnp.where(is_ctx, ypc_ref[...], ypl_ref[...])
    y_ft = jnp.where(is_ctx, yfc_ref[...], yfl_ref[...])
    gate = mod_ref[0][:, 2 * D_MODEL:3 * D_MODEL]
    o = jnp.where(is_ctx, dofc_ref[...] + dobc_ref[...], dofl_ref[...] + dobl_ref[...])
    head_mean = jnp.where(_head_block_mask(), 1.0 / DN_HD, 0.0)
    y_dn = o * lax.rsqrt(dot3(o * o, head_mean, 'b') + EPS) * dng_ref[...] * silu(dnz_ref[...])
    y = s5y_ref[...] + d_ref[...] * s5_ref[:, 0:BR]
    y = jax.nn.gelu(y)
    y = y * jax.nn.sigmoid(bdot(y, gw_ref[...]) + gb_ref[...])
    y_s5 = y * silu(s5_ref[:, BR:2 * BR])
    acc = bdot(y_pool, w_ref[0:BR, :])
    acc = acc + bdot(y_dn, w_ref[BR:2 * BR, :])
    acc = acc + bdot(y_s5, w_ref[2 * BR:3 * BR, :])
    acc = acc + bdot(y_ft, w_ref[3 * BR:4 * BR, :])
    xn = x_ref[...] + gate * acc
    if final:
        xn = xn * lax.rsqrt(jnp.mean(xn * xn, axis=-1, keepdims=True) + EPS) * fg_ref[...]
    o_ref[...] = xn


def outproj(x, mod, yp_ctx, yp_lat, dn_ctx, dn_lat, dn_z, s5_y, s5, yf_ctx, yf_lat, s5_d, glu_w, glu_b, dn_g, w_out,
            final_g, final):
    row = lambda w: BS((TM, w), lambda i: (i, 0))
    full = lambda a, b: BS((a, b), lambda i: (0, 0))
    tiles_ctx = N_CTX // TM
    ctx_row = BS((TM, BR), lambda i: (jnp.minimum(i, tiles_ctx - 1), 0))
    lat_row = BS((TM, BR), lambda i: (jnp.maximum(i - tiles_ctx, 0), 0))
    return pl.pallas_call(
        functools.partial(_outproj_body, final=final),
        out_shape=SDS((N_TOK, D_MODEL), F32),
        grid=(N_TOK // TM,),
        in_specs=[row(D_MODEL),
                  BS((1, 1, 3 * D_MODEL), lambda i: (_cond_index(i), 0, 0)),
                  ctx_row, lat_row, ctx_row, lat_row, ctx_row, lat_row, row(BR), row(BR), row(2 * BR), ctx_row, lat_row,
                  full(1, BR), full(BR, BR), full(1, BR), full(1, BR), full(D_MODEL, D_MODEL), full(1, D_MODEL)],
        out_specs=row(D_MODEL),
        compiler_params=_cparams("parallel"),
        name="outproj",
    )(x, mod, yp_ctx, yp_lat, dn_ctx[0], dn_lat[0], dn_ctx[1], dn_lat[1], dn_z, s5_y, s5, yf_ctx, yf_lat,
      s5_d, glu_w, glu_b, dn_g, w_out, final_g)


def _permute_w_in(w_in):
    main = jnp.concatenate([w_in[:, 0:1536], w_in[:, 1552:2576]], axis=1)
    ba = jnp.pad(w_in[:, 1536:1552], ((0, 0), (0, 112)))
    return jnp.concatenate([main, ba], axis=1).astype(BF16)


def kernel(x_prompt, x_sample, c, state_delta, state_s5, c_ctx, w_ada, b_ada, norm_g, w_in, pool_w, pool_scale,
           dn_conv, dn_a_log, dn_dt_bias, dn_norm_g, s5_a_re, s5_a_im, s5_log_dt, s5_b_re, s5_b_im, s5_c_re,
           s5_c_im, s5_d, s5_glu_w, s5_glu_b, ft_w, w_out, final_g):
    x = jnp.concatenate([x_prompt.astype(F32).reshape(N_CTX, D_MODEL),
                         x_sample.astype(F32).reshape(N_LAT, D_MODEL)], axis=0)
    cond8 = jnp.concatenate([c_ctx.astype(F32)[None], c.astype(F32),
                             jnp.zeros((8 - N_COND, D_MODEL), F32)], axis=0)
    ada = ada_all(cond8, w_ada, b_ada)
    pm_ctx, inv_ctx, pm_lat, inv_lat = _pool_constants()
    fpos, fch, g1, h2, fch2 = _ft_constants()
    s5_mt, s5_q4, s5_rt, s5_al = s5_prep(s5_a_re, s5_a_im, s5_log_dt, s5_b_re, s5_b_im, s5_c_re, s5_c_im)
    new_dn, new_s5 = [], []
    for l in range(DEPTH):
        mod = ada[l, 0:N_COND].reshape(N_COND, 1, 3 * D_MODEL)
        pool, qkv, dn_z, s5, ft, ba = inproj(x, mod, norm_g[l].reshape(1, D_MODEL), _permute_w_in(w_in[l]))

        w_bd = jax.scipy.linalg.block_diag(*[pool_w[l, g] for g in range(4)]).astype(BF16)
        sc = pool_scale[l].reshape(1, BR)
        yp_ctx = pool_branch(pool, pm_ctx, inv_ctx, w_bd, sc, False)
        yp_lat = pool_branch(pool, pm_lat, inv_lat, w_bd, sc, True)

        ftw = ft_w[l].astype(BF16)
        yf_ctx = ft_ctx(ft, fpos, fch, ftw)
        yf_lat = ft_lat(ft, g1, h2, fch2, ftw)

        s5_y, fin_s5 = s5_branch(s5, s5_mt[l], s5_q4[l], s5_rt[l], s5_al[l], state_s5[:, l].astype(F32))
        new_s5.append(fin_s5)

        conv_w = jnp.pad(dn_conv[l], ((0, 8 - CONV_K), (0, 0)))
        gate_p = jnp.zeros((8, 128), F32)
        gate_p = gate_p.at[0, 8:16].set(dn_a_log[l].reshape(8)).at[1, 8:16].set(dn_dt_bias[l].reshape(8))
        dn_ctx, dn_lat, fin_dn = dn_branch(qkv, ba, conv_w, gate_p, state_delta[:, l].astype(F32))
        new_dn.append(fin_dn)

        x = outproj(x, mod, yp_ctx, yp_lat, dn_ctx, dn_lat, dn_z, s5_y, s5, yf_ctx, yf_lat, s5_d[l].reshape(1, BR),
                    s5_glu_w[l].astype(BF16), s5_glu_b[l].reshape(1, BR),
                    jnp.tile(dn_norm_g[l], DN_HEADS).reshape(1, BR), w_out[l].astype(BF16),
                    final_g.reshape(1, D_MODEL), l == DEPTH - 1)

    y_prompt = x[:N_CTX].reshape(BATCH, SEQ, D_MODEL).astype(x_prompt.dtype)
    y_sample = x[N_CTX:].reshape(DEC_BATCH, DEC_SEQ, D_MODEL).astype(x_sample.dtype)
    new_state_delta = jnp.stack(new_dn, axis=1).astype(state_delta.dtype)
    new_state_s5 = jnp.stack(new_s5, axis=1).astype(state_s5.dtype)
    return (y_prompt, y_sample, new_state_delta, new_state_s5)
```

```python
import functools
import math

import numpy as np
import jax
import jax.numpy as jnp
from jax import lax
from jax.experimental import pallas as pl
from jax.experimental.pallas import tpu as pltpu

F32 = jnp.float32
BF16 = jnp.bfloat16

D_MODEL = 1024
BATCH = 16
SEQ = 256
DEPTH = 4
DEC_BATCH = 2
DEC_SEQ = 4096
GRID_W = 64
GRID_H = DEC_SEQ // GRID_W
BR = 256
POOL_WINDOWS = (2, 4, 8, 16)
POOL_GD = 64
DN_HEADS = 4
DN_HD = 64
CONV_K = 5
CHUNK = 64
S5_P = 16
S5_G = 16
S5_N = 64
S5_L = 16
FT_HD = 64
EPS = 1e-6

N_CTX = BATCH * SEQ
N_LAT = DEC_BATCH * DEC_SEQ
N_TOK = N_CTX + N_LAT
N_COND = 1 + DEC_BATCH
TM = 512
W_IN_COLS = 2688
VMEM_LIMIT = 56 * 1024 * 1024

SDS = jax.ShapeDtypeStruct
BS = pl.BlockSpec


def _cparams(*sem):
    return pltpu.CompilerParams(dimension_semantics=sem, vmem_limit_bytes=VMEM_LIMIT)


def bdot(a, b):
    return jnp.dot(a.astype(BF16), b.astype(BF16), preferred_element_type=F32)


def hdot(a, b):
    return jnp.dot(a, b, preferred_element_type=F32, precision=lax.Precision.HIGHEST)


def silu(x):
    return x * jax.nn.sigmoid(x)


def _cond_index(i):
    tiles_ctx = N_CTX // TM
    tiles_seq = DEC_SEQ // TM
    return jnp.where(i < tiles_ctx, 0, 1 + (i - tiles_ctx) // tiles_seq)


def _ada_body(c_ref, w_ref, b_ref, o_ref):
    o_ref[0] = hdot(silu(c_ref[...]), w_ref[0]) + b_ref[0]


def ada_all(cond8, w_ada, b_ada):
    tn = 512
    return pl.pallas_call(
        _ada_body,
        out_shape=SDS((DEPTH, 8, 3 * D_MODEL), F32),
        grid=(DEPTH, 3 * D_MODEL // tn),
        in_specs=[BS((8, D_MODEL), lambda l, j: (0, 0)),
                  BS((1, D_MODEL, tn), lambda l, j: (l, 0, j)),
                  BS((1, 1, tn), lambda l, j: (l, 0, j))],
        out_specs=BS((1, 8, tn), lambda l, j: (l, 0, j)),
        compiler_params=_cparams("parallel", "parallel"),
        name="ada",
    )(cond8, w_ada, b_ada.reshape(DEPTH, 1, 3 * D_MODEL))


def _inproj_body(x_ref, mod_ref, g_ref, w_ref, pool_ref, qkv_ref, dnz_ref, s5_ref, ft_ref, ba_ref):
    x = x_ref[...]
    m = mod_ref[0]
    shift = m[:, 0:D_MODEL]
    scale = m[:, D_MODEL:2 * D_MODEL]
    xn = x * lax.rsqrt(jnp.mean(x * x, axis=-1, keepdims=True) + EPS) * g_ref[...]
    h = (xn * (1.0 + scale) + shift).astype(BF16)

    def proj(lo, hi):
        return jnp.dot(h, w_ref[:, lo:hi], preferred_element_type=F32)

    pool_ref[...] = proj(0, 512)
    qkv_ref[...] = proj(512, 1280)
    dnz_ref[...] = proj(1280, 1536)
    s5_ref[...] = proj(1536, 2048)
    ft_ref[...] = proj(2048, 2560)
    ba_ref[...] = proj(2560, 2688)


def inproj(x, mod, norm_g, w_in_p):
    widths = (512, 768, 256, 512, 512, 128)
    return pl.pallas_call(
        _inproj_body,
        out_shape=[SDS((N_TOK, w), F32) for w in widths],
        grid=(N_TOK // TM,),
        in_specs=[BS((TM, D_MODEL), lambda i: (i, 0)),
                  BS((1, 1, 3 * D_MODEL), lambda i: (_cond_index(i), 0, 0)),
                  BS((1, D_MODEL), lambda i: (0, 0)),
                  BS((D_MODEL, W_IN_COLS), lambda i: (0, 0))],
        out_specs=[BS((TM, w), lambda i: (i, 0)) for w in widths],
        compiler_params=_cparams("parallel"),
        name="inproj",
    )(x, mod, norm_g, w_in_p)


def _pool_body(u_ref, z_ref, pm_ref, inv_ref, w_ref, sc_ref, o_ref, *scratch, two_d):
    nblk = u_ref.shape[0] // 256
    if two_d:
        pad_ref, v_ref = scratch
        halo = 8 * GRID_W
        pad_ref[0:halo, :] = jnp.zeros((halo, BR), F32)
        pad_ref[halo + DEC_SEQ:2 * halo + DEC_SEQ, :] = jnp.zeros((halo, BR), F32)
        pad_ref[halo:halo + DEC_SEQ, :] = u_ref[...]
        lane = lax.broadcasted_iota(jnp.int32, (GRID_W, 128), 1)

        def row_body(r, c):
            base = pl.multiple_of(r * GRID_W, GRID_W)

            def slab(d, lo):
                return pad_ref[pl.ds(base + (8 + d) * GRID_W, GRID_W), lo:lo + 128]

            s2 = slab(-1, 0) + slab(0, 0)
            s4 = s2 + slab(-2, 0) + slab(1, 0)
            v_ref[pl.ds(base, GRID_W), 0:128] = jnp.where(lane < 64, s2, s4)
            s8 = slab(-4, 128)
            for d in (-3, -2, -1, 0, 1, 2, 3):
                s8 = s8 + slab(d, 128)
            s16 = s8
            for d in (-8, -7, -6, -5, 4, 5, 6, 7):
                s16 = s16 + slab(d, 128)
            v_ref[pl.ds(base, GRID_W), 128:256] = jnp.where(lane < 64, s8, s16)
            return c

        lax.fori_loop(0, GRID_H, row_body, 0)
        src = v_ref
    else:
        src = u_ref
    grp = lax.broadcasted_iota(jnp.int32, (256, BR), 1) // POOL_GD

    def blk_body(b, c):
        r0 = pl.multiple_of(b * 256, 256)
        vb = src[pl.ds(r0, 256), :]
        hi = vb.astype(BF16)
        lo = (vb - hi.astype(F32)).astype(BF16)
        res = jnp.zeros((256, BR), F32)
        for g in range(len(POOL_WINDOWS)):
            pg = (jnp.dot(pm_ref[g], hi, preferred_element_type=F32)
                  + jnp.dot(pm_ref[g], lo, preferred_element_type=F32))
            res = jnp.where(grp == g, pg, res)
        pooled = res * inv_ref[pl.ds(r0, 256), :]
        d = pooled - u_ref[pl.ds(r0, 256), :]
        y = bdot(d, w_ref[...]) * sc_ref[...]
        o_ref[pl.ds(r0, 256), :] = (y * silu(z_ref[pl.ds(r0, 256), :])).astype(BF16)
        return c

    lax.fori_loop(0, nblk, blk_body, 0)


def _band_matrices(seg):
    t = np.arange(256)
    out = []
    for w in POOL_WINDOWS:
        lo = t - w // 2
        hi = t - w // 2 + w
        s = t[None, :]
        m = (s >= lo[:, None]) & (s < hi[:, None]) & ((s // seg) == (t[:, None] // seg))
        out.append(m.astype(np.float32))
    return np.stack(out)


def _counts(length, w):
    pos = np.arange(length)
    return (np.clip(pos - w // 2 + w, 0, length) - np.clip(pos - w // 2, 0, length)).astype(np.float64)


def _pool_constants():
    inv_ctx = np.concatenate([np.repeat((1.0 / _counts(SEQ, w))[:, None], POOL_GD, 1) for w in POOL_WINDOWS], 1)
    inv_lat = []
    for w in POOL_WINDOWS:
        c2 = np.outer(_counts(GRID_H, w), _counts(GRID_W, w)).reshape(DEC_SEQ)
        inv_lat.append(np.repeat((1.0 / c2)[:, None], POOL_GD, 1))
    inv_lat = np.concatenate(inv_lat, 1)
    return (jnp.asarray(_band_matrices(SEQ), BF16), jnp.asarray(inv_ctx, F32),
            jnp.asarray(_band_matrices(GRID_W), BF16), jnp.asarray(inv_lat, F32))


def pool_branch(pool, pm, inv, w_bd, scale, two_d):
    if two_d:
        rows, nseq, blk0 = DEC_SEQ, DEC_BATCH, N_CTX // DEC_SEQ
        scratch = [pltpu.VMEM((DEC_SEQ + 16 * GRID_W, BR), F32), pltpu.VMEM((DEC_SEQ, BR), F32)]
    else:
        rows, nseq, blk0 = SEQ, BATCH, 0
        scratch = []
    return pl.pallas_call(
        functools.partial(_pool_body, two_d=two_d),
        out_shape=SDS((nseq * rows, BR), BF16),
        grid=(nseq,),
        in_specs=[BS((rows, BR), lambda i: (blk0 + i, 0)),
                  BS((rows, BR), lambda i: (blk0 + i, 1)),
                  BS((4, 256, 256), lambda i: (0, 0, 0)),
                  BS((rows, BR), lambda i: (0, 0)),
                  BS((BR, BR), lambda i: (0, 0)),
                  BS((1, BR), lambda i: (0, 0))],
        out_specs=BS((rows, BR), lambda i: (i, 0)),
        scratch_shapes=scratch,
        compiler_params=_cparams("parallel"),
        name="pool2d" if two_d else "pool1d",
    )(pool, pool, pm, inv, w_bd, scale)


def _ft_ctx_body(u_ref, z_ref, fpos_ref, fch_ref, w_ref, o_ref):
    uc = bdot(u_ref[...], fch_ref[...])
    st = jnp.concatenate([uc[:, 0:BR], uc[:, BR:2 * BR]], axis=0)
    f = bdot(fpos_ref[...], st)
    o_ref[...] = (bdot(f, w_ref[...]) * silu(z_ref[...])).astype(BF16)


def ft_ctx(ft, fpos, fch, ft_w):
    return pl.pallas_call(
        _ft_ctx_body,
        out_shape=SDS((N_CTX, BR), BF16),
        grid=(BATCH,),
        in_specs=[BS((SEQ, BR), lambda i: (i, 0)),
                  BS((SEQ, BR), lambda i: (i, 1)),
                  BS((SEQ, 2 * SEQ), lambda i: (0, 0)),
                  BS((BR, 2 * BR), lambda i: (0, 0)),
                  BS((BR, BR), lambda i: (0, 0))],
        out_specs=BS((SEQ, BR), lambda i: (i, 0)),
        compiler_params=_cparams("parallel"),
        name="ft_ctx",
    )(ft, ft, fpos, fch, ft_w)


def _ft_lat_body(u_ref, z_ref, g_ref, h_ref, fch_ref, w_ref, o_ref, x_ref, yr_ref, yi_ref):
    for hf in range(2):
        x_ref[hf] = u_ref[:, hf * 128:(hf + 1) * 128]

    def stage1(t2, c):
        xs = jnp.concatenate([x_ref[hf, pl.ds(t2, GRID_H, stride=GRID_W), :] for hf in range(2)], axis=1)
        y = jnp.dot(g_ref[t2], xs.astype(BF16), preferred_element_type=F32)
        r0 = pl.multiple_of(t2 * GRID_W, GRID_W)
        for hf in range(2):
            yr_ref[hf, pl.ds(r0, GRID_W), :] = y[0:64, hf * 128:(hf + 1) * 128]
            yi_ref[hf, pl.ds(r0, GRID_W), :] = y[64:128, hf * 128:(hf + 1) * 128]
        return c

    lax.fori_loop(0, GRID_W, stage1, 0, unroll=8)

    def stage2(kb, c):
        yr = jnp.concatenate([yr_ref[hf, pl.ds(kb, GRID_W, stride=GRID_W), :] for hf in range(2)], axis=1)
        yi = jnp.concatenate([yi_ref[hf, pl.ds(kb, GRID_W, stride=GRID_W), :] for hf in range(2)], axis=1)
        st = jnp.concatenate([yr, yi], axis=0).astype(BF16)
        a = jnp.dot(h_ref[...], st, preferred_element_type=F32)
        for hf in range(2):
            yr_ref[hf, pl.ds(kb, GRID_W, stride=GRID_W), :] = a[0:64, hf * 128:(hf + 1) * 128]
            yi_ref[hf, pl.ds(kb, GRID_W, stride=GRID_W), :] = a[64:128, hf * 128:(hf + 1) * 128]
        return c

    lax.fori_loop(0, GRID_W, stage2, 0, unroll=8)

    def stage3(b, c):
        r0 = pl.multiple_of(b * TM, TM)
        ar = jnp.concatenate([yr_ref[hf, pl.ds(r0, TM), :] for hf in range(2)], axis=1)
        ai = jnp.concatenate([yi_ref[hf, pl.ds(r0, TM), :] for hf in range(2)], axis=1)
        f = bdot(ar, fch_ref[0:BR, :]) + bdot(ai, fch_ref[BR:2 * BR, :])
        o_ref[pl.ds(r0, TM), :] = (bdot(f, w_ref[...]) * silu(z_ref[pl.ds(r0, TM), :])).astype(BF16)
        return c

    lax.fori_loop(0, DEC_SEQ // TM, stage3, 0)


def ft_lat(ft, g1, h2, fch2, ft_w):
    blk0 = N_CTX // DEC_SEQ
    return pl.pallas_call(
        _ft_lat_body,
        out_shape=SDS((N_LAT, BR), BF16),
        grid=(DEC_BATCH,),
        in_specs=[BS((DEC_SEQ, BR), lambda i: (blk0 + i, 0)),
                  BS((DEC_SEQ, BR), lambda i: (blk0 + i, 1)),
                  BS((GRID_W, 128, GRID_H), lambda i: (0, 0, 0)),
                  BS((128, 128), lambda i: (0, 0)),
                  BS((2 * BR, BR), lambda i: (0, 0)),
                  BS((BR, BR), lambda i: (0, 0))],
        out_specs=BS((DEC_SEQ, BR), lambda i: (i, 0)),
        scratch_shapes=[pltpu.VMEM((2, DEC_SEQ, 128), F32)] * 3,
        compiler_params=_cparams("parallel"),
        name="ft_lat",
    )(ft, ft, g1, h2, fch2, ft_w)


def _ft_constants():
    c = np.arange(FT_HD)
    ang = 2.0 * np.pi * np.outer(c, c) / FT_HD
    eye4 = np.eye(BR // FT_HD)
    cc = np.kron(eye4, np.cos(ang)) / 8.0
    sc = np.kron(eye4, np.sin(ang)) / 8.0
    t = np.arange(SEQ)
    angt = 2.0 * np.pi * (np.outer(t, t) % SEQ) / SEQ
    fpos = np.concatenate([np.cos(angt), -np.sin(angt)], axis=1) / 16.0
    fch = np.concatenate([cc, sc], axis=1)
    kb = np.arange(GRID_W)[None, :, None]
    t1 = np.arange(GRID_H)[None, None, :]
    t2 = np.arange(GRID_W)[:, None, None]
    a1 = 2.0 * np.pi * ((kb * (GRID_W * t1 + t2)) % DEC_SEQ) / DEC_SEQ
    g1 = np.concatenate([np.cos(a1), -np.sin(a1)], axis=1) / 8.0
    a2 = 2.0 * np.pi * (np.outer(np.arange(GRID_W), np.arange(GRID_W)) % GRID_W) / GRID_W
    c2, s2 = np.cos(a2) / 8.0, np.sin(a2) / 8.0
    h2 = np.block([[c2, s2], [-s2, c2]])
    fch2 = np.concatenate([cc, sc], axis=0)
    as_bf = lambda a: jnp.asarray(a, F32).astype(BF16)
    return as_bf(fpos), as_bf(fch), as_bf(g1), as_bf(h2), as_bf(fch2)


S5_ROWS = N_TOK // S5_L
S5_TM = 1024
S5_TR = S5_TM // S5_L
S5_CTX_C = SEQ // S5_L
S5_LAT_C = DEC_SEQ // S5_L
S5_LT = 2 * S5_G


def _s5_prep_body(ar_ref, ai_ref, ldt_ref, br_ref, bi_ref, btr_ref, bti_ref, cr_ref, ci_ref,
                  mt_ref, q4_ref, rt_ref, al_ref):
    L = S5_L
    m = lax.broadcasted_iota(jnp.int32, (2 * L, S5_N), 0).astype(F32)
    rts, qs, ds, als = [], [], [], []
    for d in range(2):
        a_re, a_im = ar_ref[0, d, 0], ai_ref[0, d, 0]
        dt = jnp.exp(ldt_ref[0, d, 0])
        xr, xi = a_re * dt, a_im * dt
        mag = jnp.exp(xr)
        ab_re, ab_im = mag * jnp.cos(xi), mag * jnp.sin(xi)
        den = a_re * a_re + a_im * a_im
        nr = ab_re - 1.0
        coef_re = (nr * a_re + ab_im * a_im) / den
        coef_im = (ab_im * a_re - nr * a_im) / den
        pw_re = jnp.exp(m * xr) * jnp.cos(m * xi)
        pw_im = jnp.exp(m * xr) * jnp.sin(m * xi)
        cq_re = pw_re * coef_re - pw_im * coef_im
        cq_im = pw_re * coef_im + pw_im * coef_re
        c_re, c_im = cr_ref[0, d, 0], ci_ref[0, d, 0]
        bt_re, bt_im = btr_ref[0, d, 0], bti_ref[0, d, 0]
        row = lambda x, e: x[e:e + 1, :]
        order = range(L) if d == 0 else range(L - 1, -1, -1)
        cp_re = jnp.concatenate([c_re * row(cq_re, e) - c_im * row(cq_im, e) for e in order], axis=0)
        cp_im = jnp.concatenate([c_re * row(cq_im, e) + c_im * row(cq_re, e) for e in order], axis=0)
        rts.append(hdot(cp_re, br_ref[0, d, 0]) - hdot(cp_im, bi_ref[0, d, 0]))
        inj = [L - 1 - i for i in range(L)] if d == 0 else list(range(L))
        q_re = jnp.concatenate([bt_re * row(cq_re, e) - bt_im * row(cq_im, e) for e in inj], axis=0)
        q_im = jnp.concatenate([bt_im * row(cq_re, e) + bt_re * row(cq_im, e) for e in inj], axis=0)
        qs.append((q_re, q_im))
        out = [j + 1 for j in range(L)] if d == 0 else [L - j for j in range(L)]
        d_re = jnp.concatenate([c_re * row(pw_re, e) - c_im * row(pw_im, e) for e in out], axis=0)
        d_im = jnp.concatenate([c_re * row(pw_im, e) + c_im * row(pw_re, e) for e in out], axis=0)
        ds.append((d_re, d_im))
        als.append((row(pw_re, L), row(pw_im, L)))
    pad = jnp.zeros(((L - 1) * S5_P, S5_P), F32)
    z = jnp.concatenate([pad, rts[0]], axis=0) + jnp.concatenate([rts[1], pad], axis=0)
    mt = jnp.concatenate([z[(L - 1 - i) * S5_P:(L - 1 - i) * S5_P + L * S5_P, :] for i in range(L)], axis=1)
    mt_ref[0, 0] = mt.astype(BF16)
    q4_ref[0, 0] = jnp.concatenate([qs[0][0], qs[1][0], qs[0][1], qs[1][1]], axis=1).astype(BF16)
    rt_ref[0, 0] = jnp.concatenate([ds[0][0], ds[1][0], -ds[0][1], -ds[1][1]], axis=1).astype(BF16)
    al_ref[0, 0] = jnp.concatenate([jnp.concatenate([als[0][0], als[1][0]], axis=1),
                                    jnp.concatenate([als[0][1], als[1][1]], axis=1)], axis=0)


def s5_prep(a_re, a_im, log_dt, b_re, b_im, c_re, c_im):
    vec = lambda x: x.reshape(DEPTH, 2, S5_G, 1, S5_N)
    ldt = jnp.broadcast_to(log_dt[..., None, None], (DEPTH, 2, S5_G, 1, S5_N))
    bt = lambda x: x.transpose(0, 1, 2, 4, 3)
    vspec = BS((1, 2, 1, 1, S5_N), lambda i: (i // S5_G, 0, i % S5_G, 0, 0))
    bspec = BS((1, 2, 1, S5_N, S5_P), lambda i: (i // S5_G, 0, i % S5_G, 0, 0))
    cspec = BS((1, 2, 1, S5_P, S5_N), lambda i: (i // S5_G, 0, i % S5_G, 0, 0))
    mat = SDS((DEPTH, S5_G, 256, 256), BF16)
    mspec = BS((1, 1, 256, 256), lambda i: (i // S5_G, i % S5_G, 0, 0))
    return pl.pallas_call(
        _s5_prep_body,
        out_shape=[mat, mat, mat, SDS((DEPTH, S5_G, 2, 128), F32)],
        grid=(DEPTH * S5_G,),
        in_specs=[vspec, vspec, vspec, bspec, bspec, cspec, cspec, cspec, cspec],
        out_specs=[mspec, mspec, mspec, BS((1, 1, 2, 128), lambda i: (i // S5_G, i % S5_G, 0, 0))],
        compiler_params=_cparams("parallel"),
        name="s5_prep",
    )(vec(a_re), vec(a_im), ldt, b_re, b_im, bt(b_re), bt(b_im), c_re, c_im)


def _block_transpose(arrs):
    blk = lax.broadcasted_iota(jnp.int32, arrs[0].shape, 1) // 16
    cur = list(arrs)
    for b in range(4):
        s = 16 << b
        hi = ((blk >> b) & 1) == 1
        nxt = list(cur)
        for x in range(16):
            if (x >> b) & 1:
                continue
            y = x | (1 << b)
            nxt[x] = jnp.where(hi, pltpu.roll(cur[y], s, 1), cur[x])
            nxt[y] = jnp.where(hi, cur[y], pltpu.roll(cur[x], 256 - s, 1))
        cur = nxt
    return cur


def _s5_in_body(s5_ref, mt_ref, q4_ref, y_ref, e_ref, x_ref):
    for hf in range(2):
        x_ref[hf] = s5_ref[:, hf * 128:(hf + 1) * 128]
    xs = [jnp.concatenate([x_ref[hf, pl.ds(i, S5_TR, stride=S5_L), :] for hf in range(2)], axis=1)
          for i in range(S5_L)]
    us = _block_transpose(xs)
    for g in range(S5_G):
        ub = us[g].astype(BF16)
        y_ref[:, g * 256:(g + 1) * 256] = lax.dot_general(ub, mt_ref[g], (((1,), (1,)), ((), ())),
                                                          preferred_element_type=F32)
        e = jnp.dot(ub, q4_ref[g], preferred_element_type=F32)
        e_ref[2 * g] = e[:, 0:128]
        e_ref[2 * g + 1] = e[:, 128:256]


def s5_in(s5, mt, q4):
    wspec = BS((S5_G, 256, 256), lambda t: (0, 0, 0))
    return pl.pallas_call(
        _s5_in_body,
        out_shape=[SDS((S5_ROWS, S5_G * 256), F32), SDS((S5_LT, S5_ROWS, 128), F32)],
        grid=(N_TOK // S5_TM,),
        in_specs=[BS((S5_TM, BR), lambda t: (t, 0)), wspec, wspec],
        out_specs=[BS((S5_TR, S5_G * 256), lambda t: (t, 0)), BS((S5_LT, S5_TR, 128), lambda t: (0, t, 0))],
        scratch_shapes=[pltpu.VMEM((2, S5_TM, 128), F32)],
        compiler_params=_cparams("parallel"),
        name="s5_in",
    )(s5, mt, q4)


def _s5_chunk_scan_body(e_ref, al_ref, s0_ref, spf_ref, spb_ref, fin_ref):
    def update(g, s_re, s_im, e_re, e_im):
        a_re, a_im = al_ref[g, 0:1, :], al_ref[g, 1:2, :]
        return a_re * s_re - a_im * s_im + e_re, a_re * s_im + a_im * s_re + e_im

    fwd_c = lax.broadcasted_iota(jnp.int32, (BATCH, 128), 1) < S5_N

    def ctx_group(g, carry):
        s_re = jnp.zeros((BATCH, 128), F32)
        s_im = jnp.zeros((BATCH, 128), F32)
        for c in range(S5_CTX_C):
            rf = pl.ds(c, BATCH, stride=S5_CTX_C)
            rb = pl.ds(S5_CTX_C - 1 - c, BATCH, stride=S5_CTX_C)
            spf_ref[2 * g, rf, :] = s_re
            spf_ref[2 * g + 1, rf, :] = s_im
            spb_ref[2 * g, rb, :] = s_re
            spb_ref[2 * g + 1, rb, :] = s_im
            e_re = jnp.where(fwd_c, e_ref[2 * g, rf, :], e_ref[2 * g, rb, :])
            e_im = jnp.where(fwd_c, e_ref[2 * g + 1, rf, :], e_ref[2 * g + 1, rb, :])
            s_re, s_im = update(g, s_re, s_im, e_re, e_im)
        fin_ref[2 * g] = s_re
        fin_ref[2 * g + 1] = s_im
        return carry

    lax.fori_loop(0, S5_G, ctx_group, 0)

    row0 = BATCH * S5_CTX_C
    fwd_l = lax.broadcasted_iota(jnp.int32, (DEC_BATCH, 128), 1) < S5_N

    def lat_step(c, state):
        rf = pl.ds(row0 + c, DEC_BATCH, stride=S5_LAT_C)
        rb = pl.ds(row0 + S5_LAT_C - 1 - c, DEC_BATCH, stride=S5_LAT_C)
        new = []
        for g in range(S5_G):
            s_re, s_im = state[2 * g], state[2 * g + 1]
            spf_ref[2 * g, rf, :] = s_re
            spf_ref[2 * g + 1, rf, :] = s_im
            spb_ref[2 * g, rb, :] = s_re
            spb_ref[2 * g + 1, rb, :] = s_im
            e_re = jnp.where(fwd_l, e_ref[2 * g, rf, :], e_ref[2 * g, rb, :])
            e_im = jnp.where(fwd_l, e_ref[2 * g + 1, rf, :], e_ref[2 * g + 1, rb, :])
            new.extend(update(g, s_re, s_im, e_re, e_im))
        return tuple(new)

    lax.fori_loop(0, S5_LAT_C, lat_step, tuple(s0_ref[t] for t in range(S5_LT)))


def s5_chunk_scan(e3, al, s0_lat):
    sp = SDS((S5_LT, S5_ROWS, 128), F32)
    return pl.pallas_call(
        _s5_chunk_scan_body,
        out_shape=[sp, sp, SDS((S5_LT, BATCH, 128), F32)],
        compiler_params=pltpu.CompilerParams(vmem_limit_bytes=VMEM_LIMIT),
        name="s5_chunk_scan",
    )(e3, al, s0_lat)


def _s5_fin_body(y_ref, spf_ref, spb_ref, rt_ref, o_ref, t_ref):
    fwd = lax.broadcasted_iota(jnp.int32, (S5_TR, 128), 1) < S5_N
    ys = []
    for g in range(S5_G):
        s_re = jnp.where(fwd, spf_ref[2 * g], spb_ref[2 * g])
        s_im = jnp.where(fwd, spf_ref[2 * g + 1], spb_ref[2 * g + 1])
        sp = jnp.concatenate([s_re, s_im], axis=1).astype(BF16)
        ys.append(y_ref[:, g * 256:(g + 1) * 256]
                  + lax.dot_general(sp, rt_ref[g], (((1,), (1,)), ((), ())), preferred_element_type=F32))
    xs = _block_transpose(ys)
    for j in range(S5_L):
        for hf in range(2):
            t_ref[hf, pl.ds(j, S5_TR, stride=S5_L), :] = xs[j][:, hf * 128:(hf + 1) * 128]
    o_ref[...] = jnp.concatenate([t_ref[0], t_ref[1]], axis=1)


def s5_fin(yi, spf, spb, rt):
    lt = BS((S5_LT, S5_TR, 128), lambda t: (0, t, 0))
    return pl.pallas_call(
        _s5_fin_body,
        out_shape=SDS((N_TOK, BR), F32),
        grid=(N_TOK // S5_TM,),
        in_specs=[BS((S5_TR, S5_G * 256), lambda t: (t, 0)), lt, lt, BS((S5_G, 256, 256), lambda t: (0, 0, 0))],
        out_specs=BS((S5_TM, BR), lambda t: (t, 0)),
        scratch_shapes=[pltpu.VMEM((2, S5_TM, 128), F32)],
        compiler_params=_cparams("parallel"),
        name="s5_fin",
    )(yi, spf, spb, rt)


def s5_branch(s5, mt, q4, rt, al, state_l):
    yi, e3 = s5_in(s5, mt, q4)
    s0 = state_l.transpose(3, 2, 0, 1, 4).reshape(S5_LT, DEC_BATCH, 2 * S5_N)
    spf, spb, fin = s5_chunk_scan(e3, al, s0)
    y = s5_fin(yi, spf, spb, rt)
    fin = fin.reshape(S5_G, 2, BATCH, 2, S5_N).transpose(2, 3, 1, 0, 4)
    return y, fin


DN_TILE = 256
DN_HALO = 8


N_SEG = N_TOK // DN_TILE
assert N_CTX == DEC_SEQ


def _split3(x):
    x1 = x.astype(BF16)
    r1 = x - x1.astype(F32)
    x2 = r1.astype(BF16)
    return x1, x2, (r1 - x2.astype(F32)).astype(BF16)


def dot3(a, b, exact):
    if exact == 'b':
        return sum(jnp.dot(p, b.astype(BF16), preferred_element_type=F32) for p in _split3(a))
    return sum(jnp.dot(a.astype(BF16), p, preferred_element_type=F32) for p in _split3(b))


def _dn_conv_body(x_ref, prev_ref, next_ref, ba_ref, w_ref, gp_ref, ex_ref, o_ref, gb_ref, pad_ref):
    i = pl.program_id(0)
    tiles_ctx = N_CTX // DN_TILE
    tiles_seq = DEC_SEQ // DN_TILE
    j = (i - tiles_ctx) % tiles_seq
    first = jnp.logical_or(i < tiles_ctx, j == 0)
    last = jnp.logical_or(i < tiles_ctx, j == tiles_seq - 1)
    pad_ref[0:DN_HALO, :] = jnp.where(first, 0.0, prev_ref[...])
    pad_ref[DN_HALO:DN_HALO + DN_TILE, :] = x_ref[...]
    pad_ref[DN_HALO + DN_TILE:2 * DN_HALO + DN_TILE, :] = jnp.where(last, 0.0, next_ref[...])
    for r0 in range(0, DN_TILE, 128):
        for c0 in range(0, 3 * BR, 128):
            acc = jnp.zeros((128, 128), F32)
            for t in range(CONV_K):
                acc = acc + (pad_ref[pl.ds(r0 + DN_HALO - CONV_K // 2 + t, 128), c0:c0 + 128]
                             * w_ref[t:t + 1, c0:c0 + 128])
            o_ref[r0:r0 + 128, c0:c0 + 128] = silu(acc)
    raw = ba_ref[...]
    lane = lax.broadcasted_iota(jnp.int32, raw.shape, 1)
    xa = raw + gp_ref[1:2, :]
    sp = jnp.maximum(xa, 0.0) + jnp.log1p(jnp.exp(-jnp.abs(xa)))
    gates = jnp.where(lane < 2 * DN_HEADS, jax.nn.sigmoid(raw), -jnp.exp(gp_ref[0:1, :]) * sp)
    r = lax.broadcasted_iota(jnp.int32, (DN_TILE, DN_TILE), 0)
    c = lax.broadcasted_iota(jnp.int32, (DN_TILE, DN_TILE), 1)
    same = (r // CHUNK) == (c // CHUNK)
    ex = dot3(gates, ex_ref[...], 'b')
    gb_ref[:, 0:2 * BR] = ex[:, 0:2 * BR]
    gb_ref[:, 2 * BR:3 * BR] = dot3(jnp.logical_and(same, c <= r), ex[:, 2 * BR:3 * BR], 'a')
    gb_ref[:, 3 * BR:4 * BR] = dot3(jnp.logical_and(same, c >= r), ex[:, 3 * BR:4 * BR], 'a')


def _gate_expand():
    e = np.zeros((128, 4 * BR), np.float32)
    for blk in range(4):
        for h in range(DN_HEADS):
            e[blk * DN_HEADS + h, blk * BR + h * DN_HD:blk * BR + (h + 1) * DN_HD] = 1.0
    return jnp.asarray(e, BF16)


def _head_block_mask():
    r = lax.broadcasted_iota(jnp.int32, (BR, BR), 0) // DN_HD
    c = lax.broadcasted_iota(jnp.int32, (BR, BR), 1) // DN_HD
    return r == c


def _split2(x):
    hi = x.astype(BF16)
    return hi, (x - hi.astype(F32)).astype(BF16)


def _dn_local_body(qkv_ref, gb_ref, *out_refs):
    f_refs, b_refs = out_refs[0:6], out_refs[6:12]
    ncb = DN_TILE // CHUNK
    nb = 2 * ncb
    bmask = _head_block_mask()
    ri = lax.broadcasted_iota(jnp.int32, (nb, CHUNK, BR), 1)
    cj = lax.broadcasted_iota(jnp.int32, (nb, CHUNK, BR), 2) % DN_HD
    bwd = lax.broadcasted_iota(jnp.int32, (nb, CHUNK, BR), 0) >= ncb
    eye = cj == ri
    incl = jnp.logical_or(jnp.logical_and(bwd, cj >= ri), jnp.logical_and(jnp.logical_not(bwd), cj <= ri))
    strict = jnp.logical_and(incl, jnp.logical_not(eye))
    ones_blk = bmask.astype(BF16)

    def chunks(x):
        return x.reshape(ncb, CHUNK, BR)

    def both(x):
        return jnp.concatenate([x, x], axis=0)

    def head_sum(x):
        return dot3(x.reshape(-1, BR), ones_blk, 'b').reshape(x.shape)

    def bd(x):
        return jnp.where(bmask, jnp.concatenate([x] * DN_HEADS, axis=1), jnp.zeros((), x.dtype))

    def bmm(a, b):
        return jnp.einsum('bij,bjk->bik', a, b, preferred_element_type=F32)

    q = chunks(qkv_ref[:, 0:BR])
    k = chunks(qkv_ref[:, BR:2 * BR])
    v = both(chunks(qkv_ref[:, 2 * BR:3 * BR]))
    q = q * lax.rsqrt(head_sum(q * q) + EPS) * (DN_HD ** -0.5)
    k = k * lax.rsqrt(head_sum(k * k) + EPS)
    kq = jnp.einsum('bik,bjk->bij', jnp.concatenate([k, q], axis=1).astype(BF16), bd(k.astype(BF16)),
                    preferred_element_type=F32)
    kk, qk = both(kq[:, 0:CHUNK]), both(kq[:, CHUNK:2 * CHUNK])
    q, k = both(q), both(k)
    beta = jnp.concatenate([chunks(gb_ref[:, 0:BR]), chunks(gb_ref[:, BR:2 * BR])], axis=0)
    gc = jnp.concatenate([chunks(gb_ref[:, 2 * BR:3 * BR]), chunks(gb_ref[:, 3 * BR:4 * BR])], axis=0)
    crow = jnp.sum(jnp.where(eye, gc, 0.0), axis=1, keepdims=True)
    decay = jnp.where(incl, jnp.exp(jnp.where(incl, gc - crow, 0.0)), 0.0)
    a = jnp.where(strict, kk * decay * beta, 0.0)
    tinv = jnp.where(eye, 1.0, 0.0) - a
    pw = a
    pw_bd = bd(pw.astype(BF16))
    for _ in range(5):
        pw = bmm(pw.astype(BF16), pw_bd)
        pw_bd = bd(pw.astype(BF16))
        tinv = tinv + bmm(tinv.astype(BF16), pw_bd)
    egc = jnp.exp(gc)
    t_hi, t_lo = _split2(tinv)

    def solve(rhs):
        r_hi, r_lo = _split2(rhs)
        r_hi, r_lo = bd(r_hi), bd(r_lo)
        return bmm(t_hi, r_hi) + bmm(t_hi, r_lo) + bmm(t_lo, r_hi)

    w = bmm(t_hi, bd((k * (beta * egc)).astype(BF16)))

    bwd_row = lax.broadcasted_iota(jnp.int32, (nb, 1, BR), 0) >= ncb
    g_last = jnp.where(bwd_row, gc[:, 0:1], gc[:, CHUNK - 1:CHUNK])
    eg = jnp.exp(g_last)
    outs = (w, solve(v * beta), q * egc, k * jnp.exp(g_last - gc),
            jnp.where(incl, qk * decay, 0.0))
    for d, refs in enumerate((f_refs, b_refs)):
        for ref, x in zip(refs[0:5], outs):
            ref[...] = x[d * ncb:(d + 1) * ncb].reshape(DN_TILE, BR).astype(ref.dtype)
        refs[5][0] = jnp.concatenate([eg[d * ncb:(d + 1) * ncb, 0], jnp.zeros((8 - ncb, BR), F32)], axis=0)


_DN_LOCAL_DTYPES = (BF16, F32, BF16, BF16, BF16)


def _dn_prep_body(x_ref, prev_ref, next_ref, ba_ref, w_ref, gp_ref, ex_ref, *rest):
    out_refs, (pad_ref, qkv_ref, gb_ref) = rest[0:12], rest[12:15]
    _dn_conv_body(x_ref, prev_ref, next_ref, ba_ref, w_ref, gp_ref, ex_ref, qkv_ref, gb_ref, pad_ref)
    _dn_local_body(qkv_ref, gb_ref, *out_refs)


def dn_local(qkv, ba, conv_w, gate_p):
    per = DN_TILE // DN_HALO
    nhb = N_TOK // DN_HALO
    tok = BS((DN_TILE, BR), lambda i: (i, 0))
    shapes = [SDS((N_TOK, BR), dt) for dt in _DN_LOCAL_DTYPES] + [SDS((N_SEG, 8, BR), F32)]
    specs = [tok] * len(_DN_LOCAL_DTYPES) + [BS((1, 8, BR), lambda i: (i, 0, 0))]
    outs = pl.pallas_call(
        _dn_prep_body,
        out_shape=shapes * 2,
        grid=(N_SEG,),
        in_specs=[BS((DN_TILE, 3 * BR), lambda i: (i, 0)),
                  BS((DN_HALO, 3 * BR), lambda i: (jnp.maximum(i * per - 1, 0), 0)),
                  BS((DN_HALO, 3 * BR), lambda i: (jnp.minimum((i + 1) * per, nhb - 1), 0)),
                  BS((DN_TILE, 128), lambda i: (i, 0)),
                  BS((8, 3 * BR), lambda i: (0, 0)),
                  BS((8, 128), lambda i: (0, 0)),
                  BS((128, 4 * BR), lambda i: (0, 0))],
        out_specs=specs * 2,
        scratch_shapes=[pltpu.VMEM((DN_TILE + 2 * DN_HALO, 3 * BR), F32),
                        pltpu.VMEM((DN_TILE, 3 * BR), F32),
                        pltpu.VMEM((DN_TILE, 4 * BR), F32)],
        compiler_params=_cparams("parallel"),
        name="dn_local",
    )(qkv, qkv, qkv, ba, conv_w, gate_p, _gate_expand())
    return outs[0:6], outs[6:12]


def _dn_advance(chains, s, bmask):
    ncb = DN_TILE // CHUNK
    bmm = lambda a, b: jnp.einsum('bij,bjk->bik', a, b, preferred_element_type=F32)
    steps = []
    for t in range(ncb):
        cs = [ncb - 1 - t if rev else t for _, rev in chains]
        w, u, qt, kt, aqk, eg = [jnp.stack([load(k, c) for (load, _), c in zip(chains, cs)]) for k in range(6)]
        sb = s.astype(BF16)
        v_new = u - bmm(w, sb)
        vb = v_new.astype(BF16)
        v_bd = jnp.where(bmask, jnp.concatenate([vb] * DN_HEADS, axis=1), jnp.zeros((), BF16))
        o = bmm(qt, sb) + bmm(aqk, v_bd)
        upd = jnp.einsum('btk,btv->bkv', kt, vb, preferred_element_type=F32)
        s = s * eg + jnp.where(bmask, upd, 0.0)
        steps.append((cs, o))
    return s, steps


assert BATCH == DEC_SEQ // DN_TILE


def _dn_seq_body(*refs):
    f_in, b_in = refs[0:6], refs[6:12]
    s0_ref, of_ref, ob_ref, finf_ref, finb_ref, s_ref = refs[12:18]

    @pl.when(pl.program_id(0) == 0)
    def _():
        s_ref[...] = s0_ref[...]

    def loader(in_refs, q):
        def load(k, c):
            if k == 5:
                return in_refs[5][q, 0, c:c + 1, :]
            return in_refs[k][q, c * CHUNK:(c + 1) * CHUNK, :]
        return load

    groups = [1, 2, 1, 2, 0, 0]
    outs = [of_ref, of_ref, ob_ref, ob_ref, of_ref, ob_ref]
    chains = [(loader(b_in if o is ob_ref else f_in, q), o is ob_ref) for q, o in zip(groups, outs)]
    n_lat = 2 * DEC_BATCH
    s = jnp.concatenate([s_ref[...], jnp.zeros((2, BR, BR), F32)], axis=0)
    s, steps = _dn_advance(chains, s, _head_block_mask())
    s_ref[...] = s[0:n_lat]
    finf_ref[0] = s[n_lat]
    finb_ref[0] = s[n_lat + 1]
    for cs, o in steps:
        for i, c in enumerate(cs):
            outs[i][groups[i], c * CHUNK:(c + 1) * CHUNK, :] = o[i]


def dn_seq(loc_f, loc_b, s0):
    nseg = DEC_SEQ // DN_TILE
    grp = lambda a: a.reshape((N_COND, nseg) + a.shape[1:])

    def specs(m):
        return ([BS((N_COND, DN_TILE, BR), lambda g: (0, m(g), 0))] * len(_DN_LOCAL_DTYPES)
                + [BS((N_COND, 1, 8, BR), lambda g: (0, m(g), 0, 0))])

    fwd = lambda g: g
    bwd = lambda g: nseg - 1 - g
    views = lambda loc: [a.reshape(N_COND, DEC_SEQ, BR) for a in loc[0:5]] + [grp(loc[5])]
    out = SDS((N_COND, DEC_SEQ, BR), F32)
    fin = SDS((BATCH, BR, BR), F32)
    o_f, o_b, fin_f, fin_b = pl.pallas_call(
        _dn_seq_body,
        out_shape=[out, out, fin, fin],
        grid=(nseg,),
        in_specs=specs(fwd) + specs(bwd) + [BS((2 * DEC_BATCH, BR, BR), lambda g: (0, 0, 0))],
        out_specs=[BS((N_COND, DN_TILE, BR), lambda g: (0, fwd(g), 0)),
                   BS((N_COND, DN_TILE, BR), lambda g: (0, bwd(g), 0)),
                   BS((1, BR, BR), lambda g: (fwd(g), 0, 0)),
                   BS((1, BR, BR), lambda g: (bwd(g), 0, 0))],
        scratch_shapes=[pltpu.VMEM((2 * DEC_BATCH, BR, BR), F32)],
        compiler_params=_cparams("arbitrary"),
        name="dn_seq",
    )(*views(loc_f), *views(loc_b), s0)
    return o_f.reshape(N_TOK, BR), o_b.reshape(N_TOK, BR), fin_f, fin_b


def dn_branch(qkv, ba, conv_w, gate_p, s0_lat):
    loc_f, loc_b = dn_local(qkv, ba, conv_w, gate_p)
    eye_h = jnp.eye(DN_HEADS, dtype=F32)
    s0 = s0_lat.transpose(1, 0, 2, 3, 4).reshape(2 * DEC_BATCH, DN_HEADS, DN_HD, DN_HD)
    s0 = jnp.einsum('shkv,hg->shkgv', s0, eye_h).reshape(2 * DEC_BATCH, BR, BR)
    o_f, o_b, fin_f, fin_b = dn_seq(loc_f, loc_b, s0)
    unblock = lambda s: jnp.einsum('shkgv,hg->shkv', s.reshape(BATCH, DN_HEADS, DN_HD, DN_HEADS, DN_HD), eye_h)
    fin = jnp.stack([unblock(fin_f), unblock(fin_b)], axis=1)
    return o_f, o_b, fin


def _outproj_body(x_ref, mod_ref, ypc_ref, ypl_ref, dof_ref, dob_ref, dnz_ref, s5y_ref, s5_ref,
                  yfc_ref, yfl_ref, d_ref, gw_ref, gb_ref, dng_ref, w_ref, fg_ref, o_ref, *, final, tile0):
    is_ctx = pl.program_id(0) + tile0 < N_CTX // TM
    y_pool = jnp.where(is_ctx, ypc_ref[...], ypl_ref[...])
    y_ft = jnp.where(is_ctx, yfc_ref[...], yfl_ref[...])
    gate = mod_ref[0][:, 2 * D_MODEL:3 * D_MODEL]
    o = dof_ref[...] + dob_ref[...]
    head_mean = jnp.where(_head_block_mask(), 1.0 / DN_HD, 0.0)
    y_dn = o * lax.rsqrt(dot3(o * o, head_mean, 'b') + EPS) * dng_ref[...] * silu(dnz_ref[...])
    y = s5y_ref[...] + d_ref[...] * s5_ref[:, 0:BR]
    y = jax.nn.gelu(y)
    y = y * jax.nn.sigmoid(bdot(y, gw_ref[...]) + gb_ref[...])
    y_s5 = y * silu(s5_ref[:, BR:2 * BR])
    acc = bdot(y_pool, w_ref[0:BR, :])
    acc = acc + bdot(y_dn, w_ref[BR:2 * BR, :])
    acc = acc + bdot(y_s5, w_ref[2 * BR:3 * BR, :])
    acc = acc + bdot(y_ft, w_ref[3 * BR:4 * BR, :])
    xn = x_ref[...] + gate * acc
    if final:
        xn = xn * lax.rsqrt(jnp.mean(xn * xn, axis=-1, keepdims=True) + EPS) * fg_ref[...]
    o_ref[...] = xn


def outproj(x, mod, yp_ctx, yp_lat, dn_of, dn_ob, dn_z, s5_y, s5, yf_ctx, yf_lat, s5_d, glu_w, glu_b, dn_g, w_out,
            final_g, final, tile0=0, ntiles=N_TOK // TM):
    row = lambda w: BS((TM, w), lambda i: (i + tile0, 0))
    full = lambda a, b: BS((a, b), lambda i: (0, 0))
    tiles_ctx = N_CTX // TM
    ctx_row = BS((TM, BR), lambda i: (jnp.minimum(i + tile0, tiles_ctx - 1), 0))
    lat_row = BS((TM, BR), lambda i: (jnp.maximum(i + tile0 - tiles_ctx, 0), 0))
    return pl.pallas_call(
        functools.partial(_outproj_body, final=final, tile0=tile0),
        out_shape=SDS((ntiles * TM, D_MODEL), F32),
        grid=(ntiles,),
        in_specs=[row(D_MODEL),
                  BS((1, 1, 3 * D_MODEL), lambda i: (_cond_index(i + tile0), 0, 0)),
                  ctx_row, lat_row, row(BR), row(BR), row(BR), row(BR), row(2 * BR), ctx_row, lat_row,
                  full(1, BR), full(BR, BR), full(1, BR), full(1, BR), full(D_MODEL, D_MODEL), full(1, D_MODEL)],
        out_specs=BS((TM, D_MODEL), lambda i: (i, 0)),
        compiler_params=_cparams("parallel"),
        name="outproj",
    )(x, mod, yp_ctx, yp_lat, dn_of, dn_ob, dn_z, s5_y, s5, yf_ctx, yf_lat, s5_d, glu_w, glu_b, dn_g, w_out, final_g)


def _permute_w_in(w_in):
    main = jnp.concatenate([w_in[:, 0:1536], w_in[:, 1552:2576]], axis=1)
    ba = jnp.pad(w_in[:, 1536:1552], ((0, 0), (0, 112)))
    return jnp.concatenate([main, ba], axis=1).astype(BF16)


def kernel(x_prompt, x_sample, c, state_delta, state_s5, c_ctx, w_ada, b_ada, norm_g, w_in, pool_w, pool_scale,
           dn_conv, dn_a_log, dn_dt_bias, dn_norm_g, s5_a_re, s5_a_im, s5_log_dt, s5_b_re, s5_b_im, s5_c_re,
           s5_c_im, s5_d, s5_glu_w, s5_glu_b, ft_w, w_out, final_g):
    x = jnp.concatenate([x_prompt.astype(F32).reshape(N_CTX, D_MODEL),
                         x_sample.astype(F32).reshape(N_LAT, D_MODEL)], axis=0)
    cond8 = jnp.concatenate([c_ctx.astype(F32)[None], c.astype(F32),
                             jnp.zeros((8 - N_COND, D_MODEL), F32)], axis=0)
    ada = ada_all(cond8, w_ada, b_ada)
    pm_ctx, inv_ctx, pm_lat, inv_lat = _pool_constants()
    fpos, fch, g1, h2, fch2 = _ft_constants()
    s5_mt, s5_q4, s5_rt, s5_al = s5_prep(s5_a_re, s5_a_im, s5_log_dt, s5_b_re, s5_b_im, s5_c_re, s5_c_im)
    new_dn, new_s5 = [], []
    for l in range(DEPTH):
        mod = ada[l, 0:N_COND].reshape(N_COND, 1, 3 * D_MODEL)
        pool, qkv, dn_z, s5, ft, ba = inproj(x, mod, norm_g[l].reshape(1, D_MODEL), _permute_w_in(w_in[l]))

        w_bd = jax.scipy.linalg.block_diag(*[pool_w[l, g] for g in range(4)]).astype(BF16)
        sc = pool_scale[l].reshape(1, BR)
        yp_ctx = pool_branch(pool, pm_ctx, inv_ctx, w_bd, sc, False)
        yp_lat = pool_branch(pool, pm_lat, inv_lat, w_bd, sc, True)

        ftw = ft_w[l].astype(BF16)
        yf_ctx = ft_ctx(ft, fpos, fch, ftw)
        yf_lat = ft_lat(ft, g1, h2, fch2, ftw)

        s5_y, fin_s5 = s5_branch(s5, s5_mt[l], s5_q4[l], s5_rt[l], s5_al[l], state_s5[:, l].astype(F32))
        new_s5.append(fin_s5)

        conv_w = jnp.pad(dn_conv[l], ((0, 8 - CONV_K), (0, 0)))
        gate_p = jnp.zeros((8, 128), F32)
        gate_p = gate_p.at[0, 8:16].set(dn_a_log[l].reshape(8)).at[1, 8:16].set(dn_dt_bias[l].reshape(8))
        dn_of, dn_ob, fin_dn = dn_branch(qkv, ba, conv_w, gate_p, state_delta[:, l].astype(F32))
        new_dn.append(fin_dn)

        finish = functools.partial(
            outproj, x, mod, yp_ctx, yp_lat, dn_of, dn_ob, dn_z, s5_y, s5, yf_ctx, yf_lat, s5_d[l].reshape(1, BR),
            s5_glu_w[l].astype(BF16), s5_glu_b[l].reshape(1, BR), jnp.tile(dn_norm_g[l], DN_HEADS).reshape(1, BR),
            w_out[l].astype(BF16), final_g.reshape(1, D_MODEL))
        if l < DEPTH - 1:
            x = finish(False)
        else:
            tiles_ctx = N_CTX // TM
            y_ctx = finish(True, 0, tiles_ctx)
            y_lat = finish(True, tiles_ctx, N_LAT // TM)

    y_prompt = y_ctx.reshape(BATCH, SEQ, D_MODEL).astype(x_prompt.dtype)
    y_sample = y_lat.reshape(DEC_BATCH, DEC_SEQ, D_MODEL).astype(x_sample.dtype)
    new_state_delta = jnp.stack(new_dn, axis=1).astype(state_delta.dtype)
    new_state_s5 = jnp.stack(new_s5, axis=1).astype(state_s5.dtype)
    return (y_prompt, y_sample, new_state_delta, new_state_s5)
```

```python
import functools
import math

import numpy as np
import jax
import jax.numpy as jnp
from jax import lax
from jax.experimental import pallas as pl
from jax.experimental.pallas import tpu as pltpu

F32 = jnp.float32
BF16 = jnp.bfloat16

D_MODEL = 1024
BATCH = 16
SEQ = 256
DEPTH = 4
DEC_BATCH = 2
DEC_SEQ = 4096
GRID_W = 64
GRID_H = DEC_SEQ // GRID_W
BR = 256
POOL_WINDOWS = (2, 4, 8, 16)
POOL_GD = 64
DN_HEADS = 4
DN_HD = 64
CONV_K = 5
CHUNK = 64
S5_P = 16
S5_G = 16
S5_N = 64
S5_L = 16
FT_HD = 64
EPS = 1e-6

N_CTX = BATCH * SEQ
N_LAT = DEC_BATCH * DEC_SEQ
N_TOK = N_CTX + N_LAT
N_COND = 1 + DEC_BATCH
TM = 512
W_IN_COLS = 2688
VMEM_LIMIT = 56 * 1024 * 1024

SDS = jax.ShapeDtypeStruct
BS = pl.BlockSpec


def _cparams(*sem):
    return pltpu.CompilerParams(dimension_semantics=sem, vmem_limit_bytes=VMEM_LIMIT)


def bdot(a, b):
    return jnp.dot(a.astype(BF16), b.astype(BF16), preferred_element_type=F32)


def hdot(a, b):
    return jnp.dot(a, b, preferred_element_type=F32, precision=lax.Precision.HIGHEST)


def silu(x):
    return x * jax.nn.sigmoid(x)


def _cond_index(i):
    tiles_ctx = N_CTX // TM
    tiles_seq = DEC_SEQ // TM
    return jnp.where(i < tiles_ctx, 0, 1 + (i - tiles_ctx) // tiles_seq)


def _ada_body(c_ref, w_ref, b_ref, o_ref):
    o_ref[0] = hdot(silu(c_ref[...]), w_ref[0]) + b_ref[0]


def ada_all(cond8, w_ada, b_ada):
    tn = 512
    return pl.pallas_call(
        _ada_body,
        out_shape=SDS((DEPTH, 8, 3 * D_MODEL), F32),
        grid=(DEPTH, 3 * D_MODEL // tn),
        in_specs=[BS((8, D_MODEL), lambda l, j: (0, 0)),
                  BS((1, D_MODEL, tn), lambda l, j: (l, 0, j)),
                  BS((1, 1, tn), lambda l, j: (l, 0, j))],
        out_specs=BS((1, 8, tn), lambda l, j: (l, 0, j)),
        compiler_params=_cparams("parallel", "parallel"),
        name="ada",
    )(cond8, w_ada, b_ada.reshape(DEPTH, 1, 3 * D_MODEL))


def _inproj_body(x_ref, mod_ref, g_ref, w_ref, pool_ref, qkv_ref, dnz_ref, s5_ref, ft_ref, ba_ref):
    x = x_ref[...]
    m = mod_ref[0]
    shift = m[:, 0:D_MODEL]
    scale = m[:, D_MODEL:2 * D_MODEL]
    xn = x * lax.rsqrt(jnp.mean(x * x, axis=-1, keepdims=True) + EPS) * g_ref[...]
    h = (xn * (1.0 + scale) + shift).astype(BF16)

    def proj(lo, hi):
        return jnp.dot(h, w_ref[:, lo:hi], preferred_element_type=F32)

    pool_ref[...] = proj(0, 512)
    qkv_ref[...] = proj(512, 1280)
    dnz_ref[...] = proj(1280, 1536)
    s5_ref[...] = proj(1536, 2048)
    ft_ref[...] = proj(2048, 2560)
    ba_ref[...] = proj(2560, 2688)


def _layer(l, *block):
    zeros = (0,) * len(block)
    return BS((None,) + block, lambda *_: (l,) + zeros)


def inproj(x, mod, norm_g, w_in_p, l):
    widths = (512, 768, 256, 512, 512, 128)
    return pl.pallas_call(
        _inproj_body,
        out_shape=[SDS((N_TOK, w), F32) for w in widths],
        grid=(N_TOK // TM,),
        in_specs=[BS((TM, D_MODEL), lambda i: (i, 0)),
                  BS((1, 1, 3 * D_MODEL), lambda i: (8 * l + _cond_index(i), 0, 0)),
                  _layer(l, 1, D_MODEL),
                  _layer(l, D_MODEL, W_IN_COLS)],
        out_specs=[BS((TM, w), lambda i: (i, 0)) for w in widths],
        compiler_params=_cparams("parallel"),
        name="inproj",
    )(x, mod, norm_g, w_in_p)


def _pool_body(u_ref, z_ref, pm_ref, inv_ref, w_ref, sc_ref, o_ref, *scratch, two_d):
    nblk = u_ref.shape[0] // 256
    if two_d:
        pad_ref, v_ref = scratch
        halo = 8 * GRID_W
        pad_ref[0:halo, :] = jnp.zeros((halo, BR), F32)
        pad_ref[halo + DEC_SEQ:2 * halo + DEC_SEQ, :] = jnp.zeros((halo, BR), F32)
        pad_ref[halo:halo + DEC_SEQ, :] = u_ref[...]
        lane = lax.broadcasted_iota(jnp.int32, (GRID_W, 128), 1)

        def row_body(r, c):
            base = pl.multiple_of(r * GRID_W, GRID_W)

            def slab(d, lo):
                return pad_ref[pl.ds(base + (8 + d) * GRID_W, GRID_W), lo:lo + 128]

            s2 = slab(-1, 0) + slab(0, 0)
            s4 = s2 + slab(-2, 0) + slab(1, 0)
            v_ref[pl.ds(base, GRID_W), 0:128] = jnp.where(lane < 64, s2, s4)
            s8 = slab(-4, 128)
            for d in (-3, -2, -1, 0, 1, 2, 3):
                s8 = s8 + slab(d, 128)
            s16 = s8
            for d in (-8, -7, -6, -5, 4, 5, 6, 7):
                s16 = s16 + slab(d, 128)
            v_ref[pl.ds(base, GRID_W), 128:256] = jnp.where(lane < 64, s8, s16)
            return c

        lax.fori_loop(0, GRID_H, row_body, 0)
        src = v_ref
    else:
        src = u_ref
    grp = lax.broadcasted_iota(jnp.int32, (256, BR), 1) // POOL_GD

    def blk_body(b, c):
        r0 = pl.multiple_of(b * 256, 256)
        vb = src[pl.ds(r0, 256), :]
        hi = vb.astype(BF16)
        lo = (vb - hi.astype(F32)).astype(BF16)
        res = jnp.zeros((256, BR), F32)
        for g in range(len(POOL_WINDOWS)):
            pg = (jnp.dot(pm_ref[g], hi, preferred_element_type=F32)
                  + jnp.dot(pm_ref[g], lo, preferred_element_type=F32))
            res = jnp.where(grp == g, pg, res)
        pooled = res * inv_ref[pl.ds(r0, 256), :]
        d = pooled - u_ref[pl.ds(r0, 256), :]
        y = bdot(d, w_ref[...]) * sc_ref[...]
        o_ref[pl.ds(r0, 256), :] = (y * silu(z_ref[pl.ds(r0, 256), :])).astype(BF16)
        return c

    lax.fori_loop(0, nblk, blk_body, 0)


def _band_matrices(seg):
    t = np.arange(256)
    out = []
    for w in POOL_WINDOWS:
        lo = t - w // 2
        hi = t - w // 2 + w
        s = t[None, :]
        m = (s >= lo[:, None]) & (s < hi[:, None]) & ((s // seg) == (t[:, None] // seg))
        out.append(m.astype(np.float32))
    return np.stack(out)


def _counts(length, w):
    pos = np.arange(length)
    return (np.clip(pos - w // 2 + w, 0, length) - np.clip(pos - w // 2, 0, length)).astype(np.float64)


def _pool_constants():
    inv_ctx = np.concatenate([np.repeat((1.0 / _counts(SEQ, w))[:, None], POOL_GD, 1) for w in POOL_WINDOWS], 1)
    inv_lat = []
    for w in POOL_WINDOWS:
        c2 = np.outer(_counts(GRID_H, w), _counts(GRID_W, w)).reshape(DEC_SEQ)
        inv_lat.append(np.repeat((1.0 / c2)[:, None], POOL_GD, 1))
    inv_lat = np.concatenate(inv_lat, 1)
    return (jnp.asarray(_band_matrices(SEQ), BF16), jnp.asarray(inv_ctx, F32),
            jnp.asarray(_band_matrices(GRID_W), BF16), jnp.asarray(inv_lat, F32))


def pool_branch(pool, pm, inv, w_bd, scale, two_d, l):
    if two_d:
        rows, nseq, blk0 = DEC_SEQ, DEC_BATCH, N_CTX // DEC_SEQ
        scratch = [pltpu.VMEM((DEC_SEQ + 16 * GRID_W, BR), F32), pltpu.VMEM((DEC_SEQ, BR), F32)]
    else:
        rows, nseq, blk0 = SEQ, BATCH, 0
        scratch = []
    return pl.pallas_call(
        functools.partial(_pool_body, two_d=two_d),
        out_shape=SDS((nseq * rows, BR), BF16),
        grid=(nseq,),
        in_specs=[BS((rows, BR), lambda i: (blk0 + i, 0)),
                  BS((rows, BR), lambda i: (blk0 + i, 1)),
                  BS((4, 256, 256), lambda i: (0, 0, 0)),
                  BS((rows, BR), lambda i: (0, 0)),
                  _layer(l, BR, BR),
                  _layer(l, 1, BR)],
        out_specs=BS((rows, BR), lambda i: (i, 0)),
        scratch_shapes=scratch,
        compiler_params=_cparams("parallel"),
        name="pool2d" if two_d else "pool1d",
    )(pool, pool, pm, inv, w_bd, scale)


def _ft_ctx_body(u_ref, z_ref, fpos_ref, fch_ref, w_ref, o_ref):
    uc = bdot(u_ref[...], fch_ref[...])
    st = jnp.concatenate([uc[:, 0:BR], uc[:, BR:2 * BR]], axis=0)
    f = bdot(fpos_ref[...], st)
    o_ref[...] = (bdot(f, w_ref[...]) * silu(z_ref[...])).astype(BF16)


def ft_ctx(ft, fpos, fch, ft_w, l):
    return pl.pallas_call(
        _ft_ctx_body,
        out_shape=SDS((N_CTX, BR), BF16),
        grid=(BATCH,),
        in_specs=[BS((SEQ, BR), lambda i: (i, 0)),
                  BS((SEQ, BR), lambda i: (i, 1)),
                  BS((SEQ, 2 * SEQ), lambda i: (0, 0)),
                  BS((BR, 2 * BR), lambda i: (0, 0)),
                  _layer(l, BR, BR)],
        out_specs=BS((SEQ, BR), lambda i: (i, 0)),
        compiler_params=_cparams("parallel"),
        name="ft_ctx",
    )(ft, ft, fpos, fch, ft_w)


def _ft_lat_body(u_ref, z_ref, g_ref, h_ref, fch_ref, w_ref, o_ref, x_ref, yr_ref, yi_ref):
    for hf in range(2):
        x_ref[hf] = u_ref[:, hf * 128:(hf + 1) * 128]

    def stage1(t2, c):
        xs = jnp.concatenate([x_ref[hf, pl.ds(t2, GRID_H, stride=GRID_W), :] for hf in range(2)], axis=1)
        y = jnp.dot(g_ref[t2], xs.astype(BF16), preferred_element_type=F32)
        r0 = pl.multiple_of(t2 * GRID_W, GRID_W)
        for hf in range(2):
            yr_ref[hf, pl.ds(r0, GRID_W), :] = y[0:64, hf * 128:(hf + 1) * 128]
            yi_ref[hf, pl.ds(r0, GRID_W), :] = y[64:128, hf * 128:(hf + 1) * 128]
        return c

    lax.fori_loop(0, GRID_W, stage1, 0, unroll=8)

    def stage2(kb, c):
        yr = jnp.concatenate([yr_ref[hf, pl.ds(kb, GRID_W, stride=GRID_W), :] for hf in range(2)], axis=1)
        yi = jnp.concatenate([yi_ref[hf, pl.ds(kb, GRID_W, stride=GRID_W), :] for hf in range(2)], axis=1)
        st = jnp.concatenate([yr, yi], axis=0).astype(BF16)
        a = jnp.dot(h_ref[...], st, preferred_element_type=F32)
        for hf in range(2):
            yr_ref[hf, pl.ds(kb, GRID_W, stride=GRID_W), :] = a[0:64, hf * 128:(hf + 1) * 128]
            yi_ref[hf, pl.ds(kb, GRID_W, stride=GRID_W), :] = a[64:128, hf * 128:(hf + 1) * 128]
        return c

    lax.fori_loop(0, GRID_W, stage2, 0, unroll=8)

    def stage3(b, c):
        r0 = pl.multiple_of(b * TM, TM)
        ar = jnp.concatenate([yr_ref[hf, pl.ds(r0, TM), :] for hf in range(2)], axis=1)
        ai = jnp.concatenate([yi_ref[hf, pl.ds(r0, TM), :] for hf in range(2)], axis=1)
        f = bdot(ar, fch_ref[0:BR, :]) + bdot(ai, fch_ref[BR:2 * BR, :])
        o_ref[pl.ds(r0, TM), :] = (bdot(f, w_ref[...]) * silu(z_ref[pl.ds(r0, TM), :])).astype(BF16)
        return c

    lax.fori_loop(0, DEC_SEQ // TM, stage3, 0)


def ft_lat(ft, g1, h2, fch2, ft_w, l):
    blk0 = N_CTX // DEC_SEQ
    return pl.pallas_call(
        _ft_lat_body,
        out_shape=SDS((N_LAT, BR), BF16),
        grid=(DEC_BATCH,),
        in_specs=[BS((DEC_SEQ, BR), lambda i: (blk0 + i, 0)),
                  BS((DEC_SEQ, BR), lambda i: (blk0 + i, 1)),
                  BS((GRID_W, 128, GRID_H), lambda i: (0, 0, 0)),
                  BS((128, 128), lambda i: (0, 0)),
                  BS((2 * BR, BR), lambda i: (0, 0)),
                  _layer(l, BR, BR)],
        out_specs=BS((DEC_SEQ, BR), lambda i: (i, 0)),
        scratch_shapes=[pltpu.VMEM((2, DEC_SEQ, 128), F32)] * 3,
        compiler_params=_cparams("parallel"),
        name="ft_lat",
    )(ft, ft, g1, h2, fch2, ft_w)


def _ft_constants():
    c = np.arange(FT_HD)
    ang = 2.0 * np.pi * np.outer(c, c) / FT_HD
    eye4 = np.eye(BR // FT_HD)
    cc = np.kron(eye4, np.cos(ang)) / 8.0
    sc = np.kron(eye4, np.sin(ang)) / 8.0
    t = np.arange(SEQ)
    angt = 2.0 * np.pi * (np.outer(t, t) % SEQ) / SEQ
    fpos = np.concatenate([np.cos(angt), -np.sin(angt)], axis=1) / 16.0
    fch = np.concatenate([cc, sc], axis=1)
    kb = np.arange(GRID_W)[None, :, None]
    t1 = np.arange(GRID_H)[None, None, :]
    t2 = np.arange(GRID_W)[:, None, None]
    a1 = 2.0 * np.pi * ((kb * (GRID_W * t1 + t2)) % DEC_SEQ) / DEC_SEQ
    g1 = np.concatenate([np.cos(a1), -np.sin(a1)], axis=1) / 8.0
    a2 = 2.0 * np.pi * (np.outer(np.arange(GRID_W), np.arange(GRID_W)) % GRID_W) / GRID_W
    c2, s2 = np.cos(a2) / 8.0, np.sin(a2) / 8.0
    h2 = np.block([[c2, s2], [-s2, c2]])
    fch2 = np.concatenate([cc, sc], axis=0)
    as_bf = lambda a: jnp.asarray(a, F32).astype(BF16)
    return as_bf(fpos), as_bf(fch), as_bf(g1), as_bf(h2), as_bf(fch2)


S5_ROWS = N_TOK // S5_L
S5_TM = 1024
S5_TR = S5_TM // S5_L
S5_CTX_C = SEQ // S5_L
S5_LAT_C = DEC_SEQ // S5_L
S5_LT = 2 * S5_G


def _s5_prep_body(ar_ref, ai_ref, ldt_ref, br_ref, bi_ref, btr_ref, bti_ref, cr_ref, ci_ref,
                  mt_ref, q4_ref, rt_ref, al_ref):
    L = S5_L
    m = lax.broadcasted_iota(jnp.int32, (2 * L, S5_N), 0).astype(F32)
    rts, qs, ds, als = [], [], [], []
    for d in range(2):
        a_re, a_im = ar_ref[0, d, 0], ai_ref[0, d, 0]
        dt = jnp.exp(ldt_ref[0, d, 0])
        xr, xi = a_re * dt, a_im * dt
        mag = jnp.exp(xr)
        ab_re, ab_im = mag * jnp.cos(xi), mag * jnp.sin(xi)
        den = a_re * a_re + a_im * a_im
        nr = ab_re - 1.0
        coef_re = (nr * a_re + ab_im * a_im) / den
        coef_im = (ab_im * a_re - nr * a_im) / den
        pw_re = jnp.exp(m * xr) * jnp.cos(m * xi)
        pw_im = jnp.exp(m * xr) * jnp.sin(m * xi)
        cq_re = pw_re * coef_re - pw_im * coef_im
        cq_im = pw_re * coef_im + pw_im * coef_re
        c_re, c_im = cr_ref[0, d, 0], ci_ref[0, d, 0]
        bt_re, bt_im = btr_ref[0, d, 0], bti_ref[0, d, 0]
        row = lambda x, e: x[e:e + 1, :]
        order = range(L) if d == 0 else range(L - 1, -1, -1)
        cp_re = jnp.concatenate([c_re * row(cq_re, e) - c_im * row(cq_im, e) for e in order], axis=0)
        cp_im = jnp.concatenate([c_re * row(cq_im, e) + c_im * row(cq_re, e) for e in order], axis=0)
        rts.append(hdot(cp_re, br_ref[0, d, 0]) - hdot(cp_im, bi_ref[0, d, 0]))
        inj = [L - 1 - i for i in range(L)] if d == 0 else list(range(L))
        q_re = jnp.concatenate([bt_re * row(cq_re, e) - bt_im * row(cq_im, e) for e in inj], axis=0)
        q_im = jnp.concatenate([bt_im * row(cq_re, e) + bt_re * row(cq_im, e) for e in inj], axis=0)
        qs.append((q_re, q_im))
        out = [j + 1 for j in range(L)] if d == 0 else [L - j for j in range(L)]
        d_re = jnp.concatenate([c_re * row(pw_re, e) - c_im * row(pw_im, e) for e in out], axis=0)
        d_im = jnp.concatenate([c_re * row(pw_im, e) + c_im * row(pw_re, e) for e in out], axis=0)
        ds.append((d_re, d_im))
        als.append((row(pw_re, L), row(pw_im, L)))
    pad = jnp.zeros(((L - 1) * S5_P, S5_P), F32)
    z = jnp.concatenate([pad, rts[0]], axis=0) + jnp.concatenate([rts[1], pad], axis=0)
    mt = jnp.concatenate([z[(L - 1 - i) * S5_P:(L - 1 - i) * S5_P + L * S5_P, :] for i in range(L)], axis=1)
    mt_ref[0, 0] = mt.astype(BF16)
    q4_ref[0, 0] = jnp.concatenate([qs[0][0], qs[1][0], qs[0][1], qs[1][1]], axis=1).astype(BF16)
    rt_ref[0, 0] = jnp.concatenate([ds[0][0], ds[1][0], -ds[0][1], -ds[1][1]], axis=1).astype(BF16)
    al_ref[0, 0] = jnp.concatenate([jnp.concatenate([als[0][0], als[1][0]], axis=1),
                                    jnp.concatenate([als[0][1], als[1][1]], axis=1)], axis=0)


def s5_prep(a_re, a_im, log_dt, b_re, b_im, c_re, c_im):
    vec = lambda x: x.reshape(DEPTH, 2, S5_G, 1, S5_N)
    ldt = jnp.broadcast_to(log_dt[..., None, None], (DEPTH, 2, S5_G, 1, S5_N))
    bt = lambda x: x.transpose(0, 1, 2, 4, 3)
    vspec = BS((1, 2, 1, 1, S5_N), lambda i: (i // S5_G, 0, i % S5_G, 0, 0))
    bspec = BS((1, 2, 1, S5_N, S5_P), lambda i: (i // S5_G, 0, i % S5_G, 0, 0))
    cspec = BS((1, 2, 1, S5_P, S5_N), lambda i: (i // S5_G, 0, i % S5_G, 0, 0))
    mat = SDS((DEPTH, S5_G, 256, 256), BF16)
    mspec = BS((1, 1, 256, 256), lambda i: (i // S5_G, i % S5_G, 0, 0))
    return pl.pallas_call(
        _s5_prep_body,
        out_shape=[mat, mat, mat, SDS((DEPTH, S5_G, 2, 128), F32)],
        grid=(DEPTH * S5_G,),
        in_specs=[vspec, vspec, vspec, bspec, bspec, cspec, cspec, cspec, cspec],
        out_specs=[mspec, mspec, mspec, BS((1, 1, 2, 128), lambda i: (i // S5_G, i % S5_G, 0, 0))],
        compiler_params=_cparams("parallel"),
        name="s5_prep",
    )(vec(a_re), vec(a_im), ldt, b_re, b_im, bt(b_re), bt(b_im), c_re, c_im)


def _block_transpose(arrs):
    blk = lax.broadcasted_iota(jnp.int32, arrs[0].shape, 1) // 16
    cur = list(arrs)
    for b in range(4):
        s = 16 << b
        hi = ((blk >> b) & 1) == 1
        nxt = list(cur)
        for x in range(16):
            if (x >> b) & 1:
                continue
            y = x | (1 << b)
            nxt[x] = jnp.where(hi, pltpu.roll(cur[y], s, 1), cur[x])
            nxt[y] = jnp.where(hi, cur[y], pltpu.roll(cur[x], 256 - s, 1))
        cur = nxt
    return cur


def _s5_in_body(s5_ref, mt_ref, q4_ref, y_ref, e_ref, x_ref):
    for hf in range(2):
        x_ref[hf] = s5_ref[:, hf * 128:(hf + 1) * 128]
    xs = [jnp.concatenate([x_ref[hf, pl.ds(i, S5_TR, stride=S5_L), :] for hf in range(2)], axis=1)
          for i in range(S5_L)]
    us = _block_transpose(xs)
    for g in range(S5_G):
        ub = us[g].astype(BF16)
        y_ref[:, g * 256:(g + 1) * 256] = lax.dot_general(ub, mt_ref[g], (((1,), (1,)), ((), ())),
                                                          preferred_element_type=F32)
        e = jnp.dot(ub, q4_ref[g], preferred_element_type=F32)
        e_ref[2 * g] = e[:, 0:128]
        e_ref[2 * g + 1] = e[:, 128:256]


def s5_in(s5, mt, q4, l):
    wspec = _layer(l, S5_G, 256, 256)
    return pl.pallas_call(
        _s5_in_body,
        out_shape=[SDS((S5_ROWS, S5_G * 256), F32), SDS((S5_LT, S5_ROWS, 128), F32)],
        grid=(N_TOK // S5_TM,),
        in_specs=[BS((S5_TM, BR), lambda t: (t, 0)), wspec, wspec],
        out_specs=[BS((S5_TR, S5_G * 256), lambda t: (t, 0)), BS((S5_LT, S5_TR, 128), lambda t: (0, t, 0))],
        scratch_shapes=[pltpu.VMEM((2, S5_TM, 128), F32)],
        compiler_params=_cparams("parallel"),
        name="s5_in",
    )(s5, mt, q4)


def _s5_chunk_scan_body(e_ref, al_ref, s0_ref, spf_ref, spb_ref, fin_ref):
    def update(g, s_re, s_im, e_re, e_im):
        a_re, a_im = al_ref[g, 0:1, :], al_ref[g, 1:2, :]
        return a_re * s_re - a_im * s_im + e_re, a_re * s_im + a_im * s_re + e_im

    fwd_c = lax.broadcasted_iota(jnp.int32, (BATCH, 128), 1) < S5_N

    def ctx_group(g, carry):
        s_re = jnp.zeros((BATCH, 128), F32)
        s_im = jnp.zeros((BATCH, 128), F32)
        for c in range(S5_CTX_C):
            rf = pl.ds(c, BATCH, stride=S5_CTX_C)
            rb = pl.ds(S5_CTX_C - 1 - c, BATCH, stride=S5_CTX_C)
            spf_ref[2 * g, rf, :] = s_re
            spf_ref[2 * g + 1, rf, :] = s_im
            spb_ref[2 * g, rb, :] = s_re
            spb_ref[2 * g + 1, rb, :] = s_im
            e_re = jnp.where(fwd_c, e_ref[2 * g, rf, :], e_ref[2 * g, rb, :])
            e_im = jnp.where(fwd_c, e_ref[2 * g + 1, rf, :], e_ref[2 * g + 1, rb, :])
            s_re, s_im = update(g, s_re, s_im, e_re, e_im)
        fin_ref[2 * g] = s_re
        fin_ref[2 * g + 1] = s_im
        return carry

    lax.fori_loop(0, S5_GQ, ctx_group, 0)

    row0 = BATCH * S5_CTX_C
    fwd_l = lax.broadcasted_iota(jnp.int32, (DEC_BATCH, 128), 1) < S5_N

    def lat_step(c, state):
        rf = pl.ds(row0 + c, DEC_BATCH, stride=S5_LAT_C)
        rb = pl.ds(row0 + S5_LAT_C - 1 - c, DEC_BATCH, stride=S5_LAT_C)
        new = []
        for g in range(S5_GQ):
            s_re, s_im = state[2 * g], state[2 * g + 1]
            spf_ref[2 * g, rf, :] = s_re
            spf_ref[2 * g + 1, rf, :] = s_im
            spb_ref[2 * g, rb, :] = s_re
            spb_ref[2 * g + 1, rb, :] = s_im
            e_re = jnp.where(fwd_l, e_ref[2 * g, rf, :], e_ref[2 * g, rb, :])
            e_im = jnp.where(fwd_l, e_ref[2 * g + 1, rf, :], e_ref[2 * g + 1, rb, :])
            new.extend(update(g, s_re, s_im, e_re, e_im))
        return tuple(new)

    lax.fori_loop(0, S5_LAT_C, lat_step, tuple(s0_ref[t] for t in range(2 * S5_GQ)))


S5_GQ = 4


def s5_chunk_scan(e3, al, s0_lat, l):
    sp = SDS((S5_LT, S5_ROWS, 128), F32)
    tiles = lambda rows: BS((2 * S5_GQ, rows, 128), lambda q: (q, 0, 0))
    return pl.pallas_call(
        _s5_chunk_scan_body,
        out_shape=[sp, sp, SDS((S5_LT, BATCH, 128), F32)],
        grid=(S5_G // S5_GQ,),
        in_specs=[tiles(S5_ROWS), BS((None, S5_GQ, 2, 128), lambda q: (l, q, 0, 0)), tiles(DEC_BATCH)],
        out_specs=[tiles(S5_ROWS), tiles(S5_ROWS), tiles(BATCH)],
        compiler_params=_cparams("parallel"),
        name="s5_chunk_scan",
    )(e3, al, s0_lat)


def _s5_fin_body(y_ref, spf_ref, spb_ref, rt_ref, o_ref, t_ref):
    fwd = lax.broadcasted_iota(jnp.int32, (S5_TR, 128), 1) < S5_N
    ys = []
    for g in range(S5_G):
        s_re = jnp.where(fwd, spf_ref[2 * g], spb_ref[2 * g])
        s_im = jnp.where(fwd, spf_ref[2 * g + 1], spb_ref[2 * g + 1])
        sp = jnp.concatenate([s_re, s_im], axis=1).astype(BF16)
        ys.append(y_ref[:, g * 256:(g + 1) * 256]
                  + lax.dot_general(sp, rt_ref[g], (((1,), (1,)), ((), ())), preferred_element_type=F32))
    xs = _block_transpose(ys)
    for j in range(S5_L):
        for hf in range(2):
            t_ref[hf, pl.ds(j, S5_TR, stride=S5_L), :] = xs[j][:, hf * 128:(hf + 1) * 128]
    o_ref[...] = jnp.concatenate([t_ref[0], t_ref[1]], axis=1)


def s5_fin(yi, spf, spb, rt, l):
    lt = BS((S5_LT, S5_TR, 128), lambda t: (0, t, 0))
    return pl.pallas_call(
        _s5_fin_body,
        out_shape=SDS((N_TOK, BR), F32),
        grid=(N_TOK // S5_TM,),
        in_specs=[BS((S5_TR, S5_G * 256), lambda t: (t, 0)), lt, lt, _layer(l, S5_G, 256, 256)],
        out_specs=BS((S5_TM, BR), lambda t: (t, 0)),
        scratch_shapes=[pltpu.VMEM((2, S5_TM, 128), F32)],
        compiler_params=_cparams("parallel"),
        name="s5_fin",
    )(yi, spf, spb, rt)


def s5_branch(s5, mt, q4, rt, al, state_l, l):
    yi, e3 = s5_in(s5, mt, q4, l)
    s0 = state_l.transpose(3, 2, 0, 1, 4).reshape(S5_LT, DEC_BATCH, 2 * S5_N)
    spf, spb, fin = s5_chunk_scan(e3, al, s0, l)
    y = s5_fin(yi, spf, spb, rt, l)
    fin = fin.reshape(S5_G, 2, BATCH, 2, S5_N).transpose(2, 3, 1, 0, 4)
    return y, fin


DN_TILE = 256
DN_HALO = 8


N_SEG = N_TOK // DN_TILE
assert N_CTX == DEC_SEQ


def _split3(x):
    x1 = x.astype(BF16)
    r1 = x - x1.astype(F32)
    x2 = r1.astype(BF16)
    return x1, x2, (r1 - x2.astype(F32)).astype(BF16)


def dot3(a, b, exact):
    if exact == 'b':
        return sum(jnp.dot(p, b.astype(BF16), preferred_element_type=F32) for p in _split3(a))
    return sum(jnp.dot(a.astype(BF16), p, preferred_element_type=F32) for p in _split3(b))


def _dn_conv_body(x_ref, prev_ref, next_ref, ba_ref, w_ref, gp_ref, ex_ref, o_ref, gb_ref, pad_ref):
    i = pl.program_id(0)
    tiles_ctx = N_CTX // DN_TILE
    tiles_seq = DEC_SEQ // DN_TILE
    j = (i - tiles_ctx) % tiles_seq
    first = jnp.logical_or(i < tiles_ctx, j == 0)
    last = jnp.logical_or(i < tiles_ctx, j == tiles_seq - 1)
    pad_ref[0:DN_HALO, :] = jnp.where(first, 0.0, prev_ref[...])
    pad_ref[DN_HALO:DN_HALO + DN_TILE, :] = x_ref[...]
    pad_ref[DN_HALO + DN_TILE:2 * DN_HALO + DN_TILE, :] = jnp.where(last, 0.0, next_ref[...])
    for r0 in range(0, DN_TILE, 128):
        for c0 in range(0, 3 * BR, 128):
            acc = jnp.zeros((128, 128), F32)
            for t in range(CONV_K):
                acc = acc + (pad_ref[pl.ds(r0 + DN_HALO - CONV_K // 2 + t, 128), c0:c0 + 128]
                             * w_ref[t:t + 1, c0:c0 + 128])
            o_ref[r0:r0 + 128, c0:c0 + 128] = silu(acc)
    raw = ba_ref[...]
    lane = lax.broadcasted_iota(jnp.int32, raw.shape, 1)
    xa = raw + gp_ref[1:2, :]
    sp = jnp.maximum(xa, 0.0) + jnp.log1p(jnp.exp(-jnp.abs(xa)))
    gates = jnp.where(lane < 2 * DN_HEADS, jax.nn.sigmoid(raw), -jnp.exp(gp_ref[0:1, :]) * sp)
    r = lax.broadcasted_iota(jnp.int32, (DN_TILE, DN_TILE), 0)
    c = lax.broadcasted_iota(jnp.int32, (DN_TILE, DN_TILE), 1)
    same = (r // CHUNK) == (c // CHUNK)
    ex = dot3(gates, ex_ref[...], 'b')
    gb_ref[:, 0:2 * BR] = ex[:, 0:2 * BR]
    gb_ref[:, 2 * BR:3 * BR] = dot3(jnp.logical_and(same, c <= r), ex[:, 2 * BR:3 * BR], 'a')
    gb_ref[:, 3 * BR:4 * BR] = dot3(jnp.logical_and(same, c >= r), ex[:, 3 * BR:4 * BR], 'a')


def _gate_expand():
    e = np.zeros((128, 4 * BR), np.float32)
    for blk in range(4):
        for h in range(DN_HEADS):
            e[blk * DN_HEADS + h, blk * BR + h * DN_HD:blk * BR + (h + 1) * DN_HD] = 1.0
    return jnp.asarray(e, BF16)


def _head_block_mask():
    r = lax.broadcasted_iota(jnp.int32, (BR, BR), 0) // DN_HD
    c = lax.broadcasted_iota(jnp.int32, (BR, BR), 1) // DN_HD
    return r == c


def _split2(x):
    hi = x.astype(BF16)
    return hi, (x - hi.astype(F32)).astype(BF16)


def _dn_local_body(qkv_ref, gb_ref, *out_refs):
    f_refs, b_refs = out_refs[0:6], out_refs[6:12]
    ncb = DN_TILE // CHUNK
    nb = 2 * ncb
    bmask = _head_block_mask()
    ri = lax.broadcasted_iota(jnp.int32, (nb, CHUNK, BR), 1)
    cj = lax.broadcasted_iota(jnp.int32, (nb, CHUNK, BR), 2) % DN_HD
    bwd = lax.broadcasted_iota(jnp.int32, (nb, CHUNK, BR), 0) >= ncb
    eye = cj == ri
    incl = jnp.logical_or(jnp.logical_and(bwd, cj >= ri), jnp.logical_and(jnp.logical_not(bwd), cj <= ri))
    strict = jnp.logical_and(incl, jnp.logical_not(eye))
    ones_blk = bmask.astype(BF16)

    def chunks(x):
        return x.reshape(ncb, CHUNK, BR)

    def both(x):
        return jnp.concatenate([x, x], axis=0)

    def head_sum(x):
        return dot3(x.reshape(-1, BR), ones_blk, 'b').reshape(x.shape)

    def bd(x):
        return jnp.where(bmask, jnp.concatenate([x] * DN_HEADS, axis=1), jnp.zeros((), x.dtype))

    def bmm(a, b):
        return jnp.einsum('bij,bjk->bik', a, b, preferred_element_type=F32)

    q = chunks(qkv_ref[:, 0:BR])
    k = chunks(qkv_ref[:, BR:2 * BR])
    v = both(chunks(qkv_ref[:, 2 * BR:3 * BR]))
    q = q * lax.rsqrt(head_sum(q * q) + EPS) * (DN_HD ** -0.5)
    k = k * lax.rsqrt(head_sum(k * k) + EPS)
    kq = jnp.einsum('bik,bjk->bij', jnp.concatenate([k, q], axis=1).astype(BF16), bd(k.astype(BF16)),
                    preferred_element_type=F32)
    kk, qk = both(kq[:, 0:CHUNK]), both(kq[:, CHUNK:2 * CHUNK])
    q, k = both(q), both(k)
    beta = jnp.concatenate([chunks(gb_ref[:, 0:BR]), chunks(gb_ref[:, BR:2 * BR])], axis=0)
    gc = jnp.concatenate([chunks(gb_ref[:, 2 * BR:3 * BR]), chunks(gb_ref[:, 3 * BR:4 * BR])], axis=0)
    crow = jnp.sum(jnp.where(eye, gc, 0.0), axis=1, keepdims=True)
    decay = jnp.where(incl, jnp.exp(jnp.where(incl, gc - crow, 0.0)), 0.0)
    a = jnp.where(strict, kk * decay * beta, 0.0)
    tinv = jnp.where(eye, 1.0, 0.0) - a
    pw = a
    pw_bd = bd(pw.astype(BF16))
    for _ in range(5):
        pw = bmm(pw.astype(BF16), pw_bd)
        pw_bd = bd(pw.astype(BF16))
        tinv = tinv + bmm(tinv.astype(BF16), pw_bd)
    egc = jnp.exp(gc)
    t_hi, t_lo = _split2(tinv)

    def solve(rhs):
        r_hi, r_lo = _split2(rhs)
        r_hi, r_lo = bd(r_hi), bd(r_lo)
        return bmm(t_hi, r_hi) + bmm(t_hi, r_lo) + bmm(t_lo, r_hi)

    w = bmm(t_hi, bd((k * (beta * egc)).astype(BF16)))

    bwd_row = lax.broadcasted_iota(jnp.int32, (nb, 1, BR), 0) >= ncb
    g_last = jnp.where(bwd_row, gc[:, 0:1], gc[:, CHUNK - 1:CHUNK])
    eg = jnp.exp(g_last)
    outs = (w, solve(v * beta), q * egc, k * jnp.exp(g_last - gc),
            jnp.where(incl, qk * decay, 0.0))
    for d, refs in enumerate((f_refs, b_refs)):
        for ref, x in zip(refs[0:5], outs):
            ref[...] = x[d * ncb:(d + 1) * ncb].reshape(DN_TILE, BR).astype(ref.dtype)
        refs[5][0] = jnp.concatenate([eg[d * ncb:(d + 1) * ncb, 0], jnp.zeros((8 - ncb, BR), F32)], axis=0)


_DN_LOCAL_DTYPES = (BF16, F32, BF16, BF16, BF16)


def _dn_prep_body(x_ref, prev_ref, next_ref, ba_ref, w_ref, gp_ref, ex_ref, *rest):
    out_refs, (pad_ref, qkv_ref, gb_ref) = rest[0:12], rest[12:15]
    _dn_conv_body(x_ref, prev_ref, next_ref, ba_ref, w_ref, gp_ref, ex_ref, qkv_ref, gb_ref, pad_ref)
    _dn_local_body(qkv_ref, gb_ref, *out_refs)


def dn_local(qkv, ba, conv_w, gate_p, l):
    per = DN_TILE // DN_HALO
    nhb = N_TOK // DN_HALO
    tok = BS((DN_TILE, BR), lambda i: (i, 0))
    shapes = [SDS((N_TOK, BR), dt) for dt in _DN_LOCAL_DTYPES] + [SDS((N_SEG, 8, BR), F32)]
    specs = [tok] * len(_DN_LOCAL_DTYPES) + [BS((1, 8, BR), lambda i: (i, 0, 0))]
    outs = pl.pallas_call(
        _dn_prep_body,
        out_shape=shapes * 2,
        grid=(N_SEG,),
        in_specs=[BS((DN_TILE, 3 * BR), lambda i: (i, 0)),
                  BS((DN_HALO, 3 * BR), lambda i: (jnp.maximum(i * per - 1, 0), 0)),
                  BS((DN_HALO, 3 * BR), lambda i: (jnp.minimum((i + 1) * per, nhb - 1), 0)),
                  BS((DN_TILE, 128), lambda i: (i, 0)),
                  _layer(l, 8, 3 * BR),
                  _layer(l, 8, 128),
                  BS((128, 4 * BR), lambda i: (0, 0))],
        out_specs=specs * 2,
        scratch_shapes=[pltpu.VMEM((DN_TILE + 2 * DN_HALO, 3 * BR), F32),
                        pltpu.VMEM((DN_TILE, 3 * BR), F32),
                        pltpu.VMEM((DN_TILE, 4 * BR), F32)],
        compiler_params=_cparams("parallel"),
        name="dn_local",
    )(qkv, qkv, qkv, ba, conv_w, gate_p, _gate_expand())
    return outs[0:6], outs[6:12]


def _dn_advance(chains, s, bmask):
    ncb = DN_TILE // CHUNK
    bmm = lambda a, b: jnp.einsum('bij,bjk->bik', a, b, preferred_element_type=F32)
    steps = []
    for t in range(ncb):
        cs = [ncb - 1 - t if rev else t for _, rev in chains]
        w, u, qt, kt, aqk, eg = [jnp.stack([load(k, c) for (load, _), c in zip(chains, cs)]) for k in range(6)]
        sb = s.astype(BF16)
        v_new = u - bmm(w, sb)
        vb = v_new.astype(BF16)
        v_bd = jnp.where(bmask, jnp.concatenate([vb] * DN_HEADS, axis=1), jnp.zeros((), BF16))
        o = bmm(qt, sb) + bmm(aqk, v_bd)
        upd = jnp.einsum('btk,btv->bkv', kt, vb, preferred_element_type=F32)
        s = s * eg + jnp.where(bmask, upd, 0.0)
        steps.append((cs, o))
    return s, steps


assert BATCH == DEC_SEQ // DN_TILE


def _dn_seq_body(*refs):
    f_in, b_in = refs[0:6], refs[6:12]
    s0_ref, of_ref, ob_ref, finf_ref, finb_ref, s_ref = refs[12:18]

    @pl.when(pl.program_id(0) == 0)
    def _():
        s_ref[...] = s0_ref[...]

    def loader(in_refs, q):
        def load(k, c):
            if k == 5:
                return in_refs[5][q, 0, c:c + 1, :]
            return in_refs[k][q, c * CHUNK:(c + 1) * CHUNK, :]
        return load

    groups = [1, 2, 1, 2, 0, 0]
    outs = [of_ref, of_ref, ob_ref, ob_ref, of_ref, ob_ref]
    chains = [(loader(b_in if o is ob_ref else f_in, q), o is ob_ref) for q, o in zip(groups, outs)]
    n_lat = 2 * DEC_BATCH
    s = jnp.concatenate([s_ref[...], jnp.zeros((2, BR, BR), F32)], axis=0)
    s, steps = _dn_advance(chains, s, _head_block_mask())
    s_ref[...] = s[0:n_lat]
    finf_ref[0] = s[n_lat]
    finb_ref[0] = s[n_lat + 1]
    for cs, o in steps:
        for i, c in enumerate(cs):
            outs[i][groups[i], c * CHUNK:(c + 1) * CHUNK, :] = o[i]


def dn_seq(loc_f, loc_b, s0):
    nseg = DEC_SEQ // DN_TILE
    grp = lambda a: a.reshape((N_COND, nseg) + a.shape[1:])

    def specs(m):
        return ([BS((N_COND, DN_TILE, BR), lambda g: (0, m(g), 0))] * len(_DN_LOCAL_DTYPES)
                + [BS((N_COND, 1, 8, BR), lambda g: (0, m(g), 0, 0))])

    fwd = lambda g: g
    bwd = lambda g: nseg - 1 - g
    views = lambda loc: [a.reshape(N_COND, DEC_SEQ, BR) for a in loc[0:5]] + [grp(loc[5])]
    out = SDS((N_COND, DEC_SEQ, BR), F32)
    fin = SDS((BATCH, BR, BR), F32)
    o_f, o_b, fin_f, fin_b = pl.pallas_call(
        _dn_seq_body,
        out_shape=[out, out, fin, fin],
        grid=(nseg,),
        in_specs=specs(fwd) + specs(bwd) + [BS((2 * DEC_BATCH, BR, BR), lambda g: (0, 0, 0))],
        out_specs=[BS((N_COND, DN_TILE, BR), lambda g: (0, fwd(g), 0)),
                   BS((N_COND, DN_TILE, BR), lambda g: (0, bwd(g), 0)),
                   BS((1, BR, BR), lambda g: (fwd(g), 0, 0)),
                   BS((1, BR, BR), lambda g: (bwd(g), 0, 0))],
        scratch_shapes=[pltpu.VMEM((2 * DEC_BATCH, BR, BR), F32)],
        compiler_params=_cparams("arbitrary"),
        name="dn_seq",
    )(*views(loc_f), *views(loc_b), s0)
    return o_f.reshape(N_TOK, BR), o_b.reshape(N_TOK, BR), fin_f, fin_b


def dn_branch(qkv, ba, conv_w, gate_p, s0_lat, l):
    loc_f, loc_b = dn_local(qkv, ba, conv_w, gate_p, l)
    eye_h = jnp.eye(DN_HEADS, dtype=F32)
    s0 = s0_lat.transpose(1, 0, 2, 3, 4).reshape(2 * DEC_BATCH, DN_HEADS, DN_HD, DN_HD)
    s0 = jnp.einsum('shkv,hg->shkgv', s0, eye_h).reshape(2 * DEC_BATCH, BR, BR)
    o_f, o_b, fin_f, fin_b = dn_seq(loc_f, loc_b, s0)
    unblock = lambda s: jnp.einsum('shkgv,hg->shkv', s.reshape(BATCH, DN_HEADS, DN_HD, DN_HEADS, DN_HD), eye_h)
    fin = jnp.stack([unblock(fin_f), unblock(fin_b)], axis=1)
    return o_f, o_b, fin


def _outproj_body(x_ref, mod_ref, ypc_ref, ypl_ref, dof_ref, dob_ref, dnz_ref, s5y_ref, s5_ref,
                  yfc_ref, yfl_ref, d_ref, gw_ref, gb_ref, dng_ref, w_ref, fg_ref, o_ref, *, final, tile0):
    is_ctx = pl.program_id(0) + tile0 < N_CTX // TM
    y_pool = jnp.where(is_ctx, ypc_ref[...], ypl_ref[...])
    y_ft = jnp.where(is_ctx, yfc_ref[...], yfl_ref[...])
    gate = mod_ref[0][:, 2 * D_MODEL:3 * D_MODEL]
    o = dof_ref[...] + dob_ref[...]
    head_mean = jnp.where(_head_block_mask(), 1.0 / DN_HD, 0.0)
    y_dn = o * lax.rsqrt(dot3(o * o, head_mean, 'b') + EPS) * dng_ref[...] * silu(dnz_ref[...])
    y = s5y_ref[...] + d_ref[...] * s5_ref[:, 0:BR]
    y = jax.nn.gelu(y)
    y = y * jax.nn.sigmoid(bdot(y, gw_ref[...]) + gb_ref[...])
    y_s5 = y * silu(s5_ref[:, BR:2 * BR])
    acc = bdot(y_pool, w_ref[0:BR, :])
    acc = acc + bdot(y_dn, w_ref[BR:2 * BR, :])
    acc = acc + bdot(y_s5, w_ref[2 * BR:3 * BR, :])
    acc = acc + bdot(y_ft, w_ref[3 * BR:4 * BR, :])
    xn = x_ref[...] + gate * acc
    if final:
        xn = xn * lax.rsqrt(jnp.mean(xn * xn, axis=-1, keepdims=True) + EPS) * fg_ref[...]
    o_ref[...] = xn


def outproj(x, mod, yp_ctx, yp_lat, dn_of, dn_ob, dn_z, s5_y, s5, yf_ctx, yf_lat, s5_d, glu_w, glu_b, dn_g, w_out,
            final_g, l, final, tile0=0, ntiles=N_TOK // TM):
    row = lambda w: BS((TM, w), lambda i: (i + tile0, 0))
    full = lambda a, b: BS((a, b), lambda i: (0, 0))
    tiles_ctx = N_CTX // TM
    ctx_row = BS((TM, BR), lambda i: (jnp.minimum(i + tile0, tiles_ctx - 1), 0))
    lat_row = BS((TM, BR), lambda i: (jnp.maximum(i + tile0 - tiles_ctx, 0), 0))
    return pl.pallas_call(
        functools.partial(_outproj_body, final=final, tile0=tile0),
        out_shape=SDS((ntiles * TM, D_MODEL), F32),
        grid=(ntiles,),
        in_specs=[row(D_MODEL),
                  BS((1, 1, 3 * D_MODEL), lambda i: (8 * l + _cond_index(i + tile0), 0, 0)),
                  ctx_row, lat_row, row(BR), row(BR), row(BR), row(BR), row(2 * BR), ctx_row, lat_row,
                  _layer(l, 1, BR), _layer(l, BR, BR), _layer(l, 1, BR), _layer(l, 1, BR),
                  _layer(l, D_MODEL, D_MODEL), full(1, D_MODEL)],
        out_specs=BS((TM, D_MODEL), lambda i: (i, 0)),
        compiler_params=_cparams("parallel"),
        name="outproj",
    )(x, mod, yp_ctx, yp_lat, dn_of, dn_ob, dn_z, s5_y, s5, yf_ctx, yf_lat, s5_d, glu_w, glu_b, dn_g, w_out, final_g)


def _permute_w_in(w_in):
    main = jnp.concatenate([w_in[..., 0:1536], w_in[..., 1552:2576]], axis=-1)
    ba = jnp.pad(w_in[..., 1536:1552], ((0, 0), (0, 0), (0, 112)))
    return jnp.concatenate([main, ba], axis=-1).astype(BF16)


def kernel(x_prompt, x_sample, c, state_delta, state_s5, c_ctx, w_ada, b_ada, norm_g, w_in, pool_w, pool_scale,
           dn_conv, dn_a_log, dn_dt_bias, dn_norm_g, s5_a_re, s5_a_im, s5_log_dt, s5_b_re, s5_b_im, s5_c_re,
           s5_c_im, s5_d, s5_glu_w, s5_glu_b, ft_w, w_out, final_g):
    x = jnp.concatenate([x_prompt.astype(F32).reshape(N_CTX, D_MODEL),
                         x_sample.astype(F32).reshape(N_LAT, D_MODEL)], axis=0)
    cond8 = jnp.concatenate([c_ctx.astype(F32)[None], c.astype(F32),
                             jnp.zeros((8 - N_COND, D_MODEL), F32)], axis=0)
    ada = ada_all(cond8, w_ada, b_ada)
    pm_ctx, inv_ctx, pm_lat, inv_lat = _pool_constants()
    fpos, fch, g1, h2, fch2 = _ft_constants()
    s5_mt, s5_q4, s5_rt, s5_al = s5_prep(s5_a_re, s5_a_im, s5_log_dt, s5_b_re, s5_b_im, s5_c_re, s5_c_im)
    mod = ada.reshape(DEPTH * 8, 1, 3 * D_MODEL)
    norm_g3 = norm_g.reshape(DEPTH, 1, D_MODEL)
    w_in_p = _permute_w_in(w_in)
    eye_g = jnp.eye(len(POOL_WINDOWS), dtype=F32)
    w_bd = jnp.einsum('lgcd,gh->lgchd', pool_w, eye_g).reshape(DEPTH, BR, BR).astype(BF16)
    sc = pool_scale.reshape(DEPTH, 1, BR)
    ftw = ft_w.astype(BF16)
    conv_w = jnp.pad(dn_conv, ((0, 0), (0, 8 - CONV_K), (0, 0)))
    gate_p = jnp.zeros((DEPTH, 8, 128), F32)
    gate_p = gate_p.at[:, 0, 8:16].set(dn_a_log.reshape(DEPTH, 8)).at[:, 1, 8:16].set(dn_dt_bias.reshape(DEPTH, 8))
    out_params = (s5_d.reshape(DEPTH, 1, BR), s5_glu_w.astype(BF16), s5_glu_b.reshape(DEPTH, 1, BR),
                  jnp.tile(dn_norm_g, (1, DN_HEADS)).reshape(DEPTH, 1, BR), w_out.astype(BF16),
                  final_g.reshape(1, D_MODEL))
    new_dn, new_s5 = [], []
    for l in range(DEPTH):
        pool, qkv, dn_z, s5, ft, ba = inproj(x, mod, norm_g3, w_in_p, l)

        yp_ctx = pool_branch(pool, pm_ctx, inv_ctx, w_bd, sc, False, l)
        yp_lat = pool_branch(pool, pm_lat, inv_lat, w_bd, sc, True, l)

        yf_ctx = ft_ctx(ft, fpos, fch, ftw, l)
        yf_lat = ft_lat(ft, g1, h2, fch2, ftw, l)

        s5_y, fin_s5 = s5_branch(s5, s5_mt, s5_q4, s5_rt, s5_al, state_s5[:, l].astype(F32), l)
        new_s5.append(fin_s5)

        dn_of, dn_ob, fin_dn = dn_branch(qkv, ba, conv_w, gate_p, state_delta[:, l].astype(F32), l)
        new_dn.append(fin_dn)

        finish = functools.partial(outproj, x, mod, yp_ctx, yp_lat, dn_of, dn_ob, dn_z, s5_y, s5, yf_ctx, yf_lat,
                                   *out_params, l)
        if l < DEPTH - 1:
            x = finish(False)
        else:
            tiles_ctx = N_CTX // TM
            y_ctx = finish(True, 0, tiles_ctx)
            y_lat = finish(True, tiles_ctx, N_LAT // TM)

    y_prompt = y_ctx.reshape(BATCH, SEQ, D_MODEL).astype(x_prompt.dtype)
    y_sample = y_lat.reshape(DEC_BATCH, DEC_SEQ, D_MODEL).astype(x_sample.dtype)
    new_state_delta = jnp.stack(new_dn, axis=1).astype(state_delta.dtype)
    new_state_s5 = jnp.stack(new_s5, axis=1).astype(state_s5.dtype)
    return (y_prompt, y_sample, new_state_delta, new_state_s5)
```

```python
import functools
import math

import numpy as np
import jax
import jax.numpy as jnp
from jax import lax
from jax.experimental import pallas as pl
from jax.experimental.pallas import tpu as pltpu

F32 = jnp.float32
BF16 = jnp.bfloat16

D_MODEL = 1024
BATCH = 16
SEQ = 256
DEPTH = 4
DEC_BATCH = 2
DEC_SEQ = 4096
GRID_W = 64
GRID_H = DEC_SEQ // GRID_W
BR = 256
POOL_WINDOWS = (2, 4, 8, 16)
POOL_GD = 64
DN_HEADS = 4
DN_HD = 64
CONV_K = 5
CHUNK = 64
S5_P = 16
S5_G = 16
S5_N = 64
S5_L = 16
FT_HD = 64
EPS = 1e-6

N_CTX = BATCH * SEQ
N_LAT = DEC_BATCH * DEC_SEQ
N_TOK = N_CTX + N_LAT
N_COND = 1 + DEC_BATCH
TM = 512
W_IN_COLS = 2688
VMEM_LIMIT = 56 * 1024 * 1024

SDS = jax.ShapeDtypeStruct
BS = pl.BlockSpec


def _cparams(*sem):
    return pltpu.CompilerParams(dimension_semantics=sem, vmem_limit_bytes=VMEM_LIMIT)


def bdot(a, b):
    return jnp.dot(a.astype(BF16), b.astype(BF16), preferred_element_type=F32)


def hdot(a, b):
    return jnp.dot(a, b, preferred_element_type=F32, precision=lax.Precision.HIGHEST)


def silu(x):
    return x * jax.nn.sigmoid(x)


def _cond_index(i):
    tiles_ctx = N_CTX // TM
    tiles_seq = DEC_SEQ // TM
    return jnp.where(i < tiles_ctx, 0, 1 + (i - tiles_ctx) // tiles_seq)


def _ada_body(c_ref, w_ref, b_ref, o_ref):
    o_ref[0] = hdot(silu(c_ref[...]), w_ref[0]) + b_ref[0]


def ada_all(cond8, w_ada, b_ada):
    tn = 512
    return pl.pallas_call(
        _ada_body,
        out_shape=SDS((DEPTH, 8, 3 * D_MODEL), F32),
        grid=(DEPTH, 3 * D_MODEL // tn),
        in_specs=[BS((8, D_MODEL), lambda l, j: (0, 0)),
                  BS((1, D_MODEL, tn), lambda l, j: (l, 0, j)),
                  BS((1, 1, tn), lambda l, j: (l, 0, j))],
        out_specs=BS((1, 8, tn), lambda l, j: (l, 0, j)),
        compiler_params=_cparams("parallel", "parallel"),
        name="ada",
    )(cond8, w_ada, b_ada.reshape(DEPTH, 1, 3 * D_MODEL))


def _inproj_body(x_ref, mod_ref, g_ref, w_ref, pool_ref, qkv_ref, dnz_ref, s5_ref, ft_ref, ba_ref):
    x = x_ref[...]
    m = mod_ref[0]
    shift = m[:, 0:D_MODEL]
    scale = m[:, D_MODEL:2 * D_MODEL]
    xn = x * lax.rsqrt(jnp.mean(x * x, axis=-1, keepdims=True) + EPS) * g_ref[...]
    h = (xn * (1.0 + scale) + shift).astype(BF16)

    def proj(lo, hi):
        return jnp.dot(h, w_ref[:, lo:hi], preferred_element_type=F32)

    pool_ref[...] = proj(0, 512)
    qkv_ref[...] = proj(512, 1280)
    dnz_ref[...] = proj(1280, 1536)
    s5_ref[...] = proj(1536, 2048)
    ft_ref[...] = proj(2048, 2560)
    ba_ref[...] = proj(2560, 2688)


def _layer(l, *block):
    zeros = (0,) * len(block)
    return BS((None,) + block, lambda *_: (l,) + zeros)


def inproj(x, mod, norm_g, w_in_p, l):
    widths = (512, 768, 256, 512, 512, 128)
    return pl.pallas_call(
        _inproj_body,
        out_shape=[SDS((N_TOK, w), F32) for w in widths],
        grid=(N_TOK // TM,),
        in_specs=[BS((TM, D_MODEL), lambda i: (i, 0)),
                  BS((1, 1, 3 * D_MODEL), lambda i: (8 * l + _cond_index(i), 0, 0)),
                  _layer(l, 1, D_MODEL),
                  _layer(l, D_MODEL, W_IN_COLS)],
        out_specs=[BS((TM, w), lambda i: (i, 0)) for w in widths],
        compiler_params=_cparams("parallel"),
        name="inproj",
    )(x, mod, norm_g, w_in_p)


def _pool_body(u_ref, z_ref, pm_ref, inv_ref, w_ref, sc_ref, o_ref, *scratch, two_d):
    nblk = u_ref.shape[0] // 256
    if two_d:
        pad_ref, v_ref = scratch
        halo = 8 * GRID_W
        pad_ref[0:halo, :] = jnp.zeros((halo, BR), F32)
        pad_ref[halo + DEC_SEQ:2 * halo + DEC_SEQ, :] = jnp.zeros((halo, BR), F32)
        pad_ref[halo:halo + DEC_SEQ, :] = u_ref[...]
        lane = lax.broadcasted_iota(jnp.int32, (GRID_W, 128), 1)

        def row_body(r, c):
            base = pl.multiple_of(r * GRID_W, GRID_W)

            def slab(d, lo):
                return pad_ref[pl.ds(base + (8 + d) * GRID_W, GRID_W), lo:lo + 128]

            s2 = slab(-1, 0) + slab(0, 0)
            s4 = s2 + slab(-2, 0) + slab(1, 0)
            v_ref[pl.ds(base, GRID_W), 0:128] = jnp.where(lane < 64, s2, s4)
            s8 = slab(-4, 128)
            for d in (-3, -2, -1, 0, 1, 2, 3):
                s8 = s8 + slab(d, 128)
            s16 = s8
            for d in (-8, -7, -6, -5, 4, 5, 6, 7):
                s16 = s16 + slab(d, 128)
            v_ref[pl.ds(base, GRID_W), 128:256] = jnp.where(lane < 64, s8, s16)
            return c

        lax.fori_loop(0, GRID_H, row_body, 0)
        src = v_ref
    else:
        src = u_ref
    grp = lax.broadcasted_iota(jnp.int32, (256, BR), 1) // POOL_GD

    def blk_body(b, c):
        r0 = pl.multiple_of(b * 256, 256)
        vb = src[pl.ds(r0, 256), :]
        hi = vb.astype(BF16)
        lo = (vb - hi.astype(F32)).astype(BF16)
        res = jnp.zeros((256, BR), F32)
        for g in range(len(POOL_WINDOWS)):
            pg = (jnp.dot(pm_ref[g], hi, preferred_element_type=F32)
                  + jnp.dot(pm_ref[g], lo, preferred_element_type=F32))
            res = jnp.where(grp == g, pg, res)
        pooled = res * inv_ref[pl.ds(r0, 256), :]
        d = pooled - u_ref[pl.ds(r0, 256), :]
        y = bdot(d, w_ref[...]) * sc_ref[...]
        o_ref[pl.ds(r0, 256), :] = (y * silu(z_ref[pl.ds(r0, 256), :])).astype(BF16)
        return c

    lax.fori_loop(0, nblk, blk_body, 0)


def _band_matrices(seg):
    t = np.arange(256)
    out = []
    for w in POOL_WINDOWS:
        lo = t - w // 2
        hi = t - w // 2 + w
        s = t[None, :]
        m = (s >= lo[:, None]) & (s < hi[:, None]) & ((s // seg) == (t[:, None] // seg))
        out.append(m.astype(np.float32))
    return np.stack(out)


def _counts(length, w):
    pos = np.arange(length)
    return (np.clip(pos - w // 2 + w, 0, length) - np.clip(pos - w // 2, 0, length)).astype(np.float64)


def _pool_constants():
    inv_ctx = np.concatenate([np.repeat((1.0 / _counts(SEQ, w))[:, None], POOL_GD, 1) for w in POOL_WINDOWS], 1)
    inv_lat = []
    for w in POOL_WINDOWS:
        c2 = np.outer(_counts(GRID_H, w), _counts(GRID_W, w)).reshape(DEC_SEQ)
        inv_lat.append(np.repeat((1.0 / c2)[:, None], POOL_GD, 1))
    inv_lat = np.concatenate(inv_lat, 1)
    return (jnp.asarray(_band_matrices(SEQ), BF16), jnp.asarray(inv_ctx, F32),
            jnp.asarray(_band_matrices(GRID_W), BF16), jnp.asarray(inv_lat, F32))


def pool_branch(pool, pm, inv, w_bd, scale, two_d, l):
    if two_d:
        rows, nseq, blk0 = DEC_SEQ, DEC_BATCH, N_CTX // DEC_SEQ
        scratch = [pltpu.VMEM((DEC_SEQ + 16 * GRID_W, BR), F32), pltpu.VMEM((DEC_SEQ, BR), F32)]
    else:
        rows, nseq, blk0 = SEQ, BATCH, 0
        scratch = []
    return pl.pallas_call(
        functools.partial(_pool_body, two_d=two_d),
        out_shape=SDS((nseq * rows, BR), BF16),
        grid=(nseq,),
        in_specs=[BS((rows, BR), lambda i: (blk0 + i, 0)),
                  BS((rows, BR), lambda i: (blk0 + i, 1)),
                  BS((4, 256, 256), lambda i: (0, 0, 0)),
                  BS((rows, BR), lambda i: (0, 0)),
                  _layer(l, BR, BR),
                  _layer(l, 1, BR)],
        out_specs=BS((rows, BR), lambda i: (i, 0)),
        scratch_shapes=scratch,
        compiler_params=_cparams("parallel"),
        name="pool2d" if two_d else "pool1d",
    )(pool, pool, pm, inv, w_bd, scale)


def _ft_ctx_body(u_ref, z_ref, fpos_ref, fch_ref, w_ref, o_ref):
    uc = bdot(u_ref[...], fch_ref[...])
    st = jnp.concatenate([uc[:, 0:BR], uc[:, BR:2 * BR]], axis=0)
    f = bdot(fpos_ref[...], st)
    o_ref[...] = (bdot(f, w_ref[...]) * silu(z_ref[...])).astype(BF16)


def ft_ctx(ft, fpos, fch, ft_w, l):
    return pl.pallas_call(
        _ft_ctx_body,
        out_shape=SDS((N_CTX, BR), BF16),
        grid=(BATCH,),
        in_specs=[BS((SEQ, BR), lambda i: (i, 0)),
                  BS((SEQ, BR), lambda i: (i, 1)),
                  BS((SEQ, 2 * SEQ), lambda i: (0, 0)),
                  BS((BR, 2 * BR), lambda i: (0, 0)),
                  _layer(l, BR, BR)],
        out_specs=BS((SEQ, BR), lambda i: (i, 0)),
        compiler_params=_cparams("parallel"),
        name="ft_ctx",
    )(ft, ft, fpos, fch, ft_w)


def _ft_lat_body(u_ref, z_ref, g_ref, h_ref, fch_ref, w_ref, o_ref, x_ref, yr_ref, yi_ref):
    for hf in range(2):
        x_ref[hf] = u_ref[:, hf * 128:(hf + 1) * 128]

    def stage1(t2, c):
        xs = jnp.concatenate([x_ref[hf, pl.ds(t2, GRID_H, stride=GRID_W), :] for hf in range(2)], axis=1)
        y = jnp.dot(g_ref[t2], xs.astype(BF16), preferred_element_type=F32)
        r0 = pl.multiple_of(t2 * GRID_W, GRID_W)
        for hf in range(2):
            yr_ref[hf, pl.ds(r0, GRID_W), :] = y[0:64, hf * 128:(hf + 1) * 128]
            yi_ref[hf, pl.ds(r0, GRID_W), :] = y[64:128, hf * 128:(hf + 1) * 128]
        return c

    lax.fori_loop(0, GRID_W, stage1, 0, unroll=8)

    def stage2(kb, c):
        yr = jnp.concatenate([yr_ref[hf, pl.ds(kb, GRID_W, stride=GRID_W), :] for hf in range(2)], axis=1)
        yi = jnp.concatenate([yi_ref[hf, pl.ds(kb, GRID_W, stride=GRID_W), :] for hf in range(2)], axis=1)
        st = jnp.concatenate([yr, yi], axis=0).astype(BF16)
        a = jnp.dot(h_ref[...], st, preferred_element_type=F32)
        for hf in range(2):
            yr_ref[hf, pl.ds(kb, GRID_W, stride=GRID_W), :] = a[0:64, hf * 128:(hf + 1) * 128]
            yi_ref[hf, pl.ds(kb, GRID_W, stride=GRID_W), :] = a[64:128, hf * 128:(hf + 1) * 128]
        return c

    lax.fori_loop(0, GRID_W, stage2, 0, unroll=8)

    def stage3(b, c):
        r0 = pl.multiple_of(b * TM, TM)
        ar = jnp.concatenate([yr_ref[hf, pl.ds(r0, TM), :] for hf in range(2)], axis=1)
        ai = jnp.concatenate([yi_ref[hf, pl.ds(r0, TM), :] for hf in range(2)], axis=1)
        f = bdot(ar, fch_ref[0:BR, :]) + bdot(ai, fch_ref[BR:2 * BR, :])
        o_ref[pl.ds(r0, TM), :] = (bdot(f, w_ref[...]) * silu(z_ref[pl.ds(r0, TM), :])).astype(BF16)
        return c

    lax.fori_loop(0, DEC_SEQ // TM, stage3, 0)


def ft_lat(ft, g1, h2, fch2, ft_w, l):
    blk0 = N_CTX // DEC_SEQ
    return pl.pallas_call(
        _ft_lat_body,
        out_shape=SDS((N_LAT, BR), BF16),
        grid=(DEC_BATCH,),
        in_specs=[BS((DEC_SEQ, BR), lambda i: (blk0 + i, 0)),
                  BS((DEC_SEQ, BR), lambda i: (blk0 + i, 1)),
                  BS((GRID_W, 128, GRID_H), lambda i: (0, 0, 0)),
                  BS((128, 128), lambda i: (0, 0)),
                  BS((2 * BR, BR), lambda i: (0, 0)),
                  _layer(l, BR, BR)],
        out_specs=BS((DEC_SEQ, BR), lambda i: (i, 0)),
        scratch_shapes=[pltpu.VMEM((2, DEC_SEQ, 128), F32)] * 3,
        compiler_params=_cparams("parallel"),
        name="ft_lat",
    )(ft, ft, g1, h2, fch2, ft_w)


def _ft_constants():
    c = np.arange(FT_HD)
    ang = 2.0 * np.pi * np.outer(c, c) / FT_HD
    eye4 = np.eye(BR // FT_HD)
    cc = np.kron(eye4, np.cos(ang)) / 8.0
    sc = np.kron(eye4, np.sin(ang)) / 8.0
    t = np.arange(SEQ)
    angt = 2.0 * np.pi * (np.outer(t, t) % SEQ) / SEQ
    fpos = np.concatenate([np.cos(angt), -np.sin(angt)], axis=1) / 16.0
    fch = np.concatenate([cc, sc], axis=1)
    kb = np.arange(GRID_W)[None, :, None]
    t1 = np.arange(GRID_H)[None, None, :]
    t2 = np.arange(GRID_W)[:, None, None]
    a1 = 2.0 * np.pi * ((kb * (GRID_W * t1 + t2)) % DEC_SEQ) / DEC_SEQ
    g1 = np.concatenate([np.cos(a1), -np.sin(a1)], axis=1) / 8.0
    a2 = 2.0 * np.pi * (np.outer(np.arange(GRID_W), np.arange(GRID_W)) % GRID_W) / GRID_W
    c2, s2 = np.cos(a2) / 8.0, np.sin(a2) / 8.0
    h2 = np.block([[c2, s2], [-s2, c2]])
    fch2 = np.concatenate([cc, sc], axis=0)
    as_bf = lambda a: jnp.asarray(a, F32).astype(BF16)
    return as_bf(fpos), as_bf(fch), as_bf(g1), as_bf(h2), as_bf(fch2)


S5_ROWS = N_TOK // S5_L
S5_TM = 1024
S5_TR = S5_TM // S5_L
S5_CTX_C = SEQ // S5_L
S5_LAT_C = DEC_SEQ // S5_L
S5_LT = 2 * S5_G


S5_PG = 4


def _s5_prep_body(*refs):
    for gi in range(S5_PG):
        _s5_prep_group(gi, *refs)


def _s5_prep_group(gi, ar_ref, ai_ref, ldt_ref, br_ref, bi_ref, btr_ref, bti_ref, cr_ref, ci_ref,
                   mt_ref, q4_ref, rt_ref, al_ref):
    L = S5_L
    m = lax.broadcasted_iota(jnp.int32, (2 * L, S5_N), 0).astype(F32)
    rts, qs, ds, als = [], [], [], []
    for d in range(2):
        a_re, a_im = ar_ref[0, d, gi], ai_ref[0, d, gi]
        dt = jnp.exp(ldt_ref[0, d, gi])
        xr, xi = a_re * dt, a_im * dt
        mag = jnp.exp(xr)
        ab_re, ab_im = mag * jnp.cos(xi), mag * jnp.sin(xi)
        den = a_re * a_re + a_im * a_im
        nr = ab_re - 1.0
        coef_re = (nr * a_re + ab_im * a_im) / den
        coef_im = (ab_im * a_re - nr * a_im) / den
        pw_re = jnp.exp(m * xr) * jnp.cos(m * xi)
        pw_im = jnp.exp(m * xr) * jnp.sin(m * xi)
        cq_re = pw_re * coef_re - pw_im * coef_im
        cq_im = pw_re * coef_im + pw_im * coef_re
        c_re, c_im = cr_ref[0, d, gi], ci_ref[0, d, gi]
        bt_re, bt_im = btr_ref[0, d, gi], bti_ref[0, d, gi]
        row = lambda x, e: x[e:e + 1, :]
        order = range(L) if d == 0 else range(L - 1, -1, -1)
        cp_re = jnp.concatenate([c_re * row(cq_re, e) - c_im * row(cq_im, e) for e in order], axis=0)
        cp_im = jnp.concatenate([c_re * row(cq_im, e) + c_im * row(cq_re, e) for e in order], axis=0)
        rts.append(hdot(cp_re, br_ref[0, d, gi]) - hdot(cp_im, bi_ref[0, d, gi]))
        inj = [L - 1 - i for i in range(L)] if d == 0 else list(range(L))
        q_re = jnp.concatenate([bt_re * row(cq_re, e) - bt_im * row(cq_im, e) for e in inj], axis=0)
        q_im = jnp.concatenate([bt_im * row(cq_re, e) + bt_re * row(cq_im, e) for e in inj], axis=0)
        qs.append((q_re, q_im))
        out = [j + 1 for j in range(L)] if d == 0 else [L - j for j in range(L)]
        d_re = jnp.concatenate([c_re * row(pw_re, e) - c_im * row(pw_im, e) for e in out], axis=0)
        d_im = jnp.concatenate([c_re * row(pw_im, e) + c_im * row(pw_re, e) for e in out], axis=0)
        ds.append((d_re, d_im))
        als.append((row(pw_re, L), row(pw_im, L)))
    pad = jnp.zeros(((L - 1) * S5_P, S5_P), F32)
    z = jnp.concatenate([pad, rts[0]], axis=0) + jnp.concatenate([rts[1], pad], axis=0)
    mt = jnp.concatenate([z[(L - 1 - i) * S5_P:(L - 1 - i) * S5_P + L * S5_P, :] for i in range(L)], axis=1)
    mt_ref[0, gi] = mt.astype(BF16)
    q4_ref[0, gi] = jnp.concatenate([qs[0][0], qs[1][0], qs[0][1], qs[1][1]], axis=1).astype(BF16)
    rt_ref[0, gi] = jnp.concatenate([ds[0][0], ds[1][0], -ds[0][1], -ds[1][1]], axis=1).astype(BF16)
    al_ref[0, gi] = jnp.concatenate([jnp.concatenate([als[0][0], als[1][0]], axis=1),
                                    jnp.concatenate([als[0][1], als[1][1]], axis=1)], axis=0)


def s5_prep(a_re, a_im, log_dt, b_re, b_im, c_re, c_im):
    vec = lambda x: x.reshape(DEPTH, 2, S5_G, 1, S5_N)
    ldt = jnp.broadcast_to(log_dt[..., None, None], (DEPTH, 2, S5_G, 1, S5_N))
    bt = lambda x: x.transpose(0, 1, 2, 4, 3)
    per = S5_G // S5_PG
    vspec = BS((1, 2, S5_PG, 1, S5_N), lambda i: (i // per, 0, i % per, 0, 0))
    bspec = BS((1, 2, S5_PG, S5_N, S5_P), lambda i: (i // per, 0, i % per, 0, 0))
    cspec = BS((1, 2, S5_PG, S5_P, S5_N), lambda i: (i // per, 0, i % per, 0, 0))
    mat = SDS((DEPTH, S5_G, 256, 256), BF16)
    mspec = BS((1, S5_PG, 256, 256), lambda i: (i // per, i % per, 0, 0))
    return pl.pallas_call(
        _s5_prep_body,
        out_shape=[mat, mat, mat, SDS((DEPTH, S5_G, 2, 128), F32)],
        grid=(DEPTH * per,),
        in_specs=[vspec, vspec, vspec, bspec, bspec, cspec, cspec, cspec, cspec],
        out_specs=[mspec, mspec, mspec, BS((1, S5_PG, 2, 128), lambda i: (i // per, i % per, 0, 0))],
        compiler_params=_cparams("parallel"),
        name="s5_prep",
    )(vec(a_re), vec(a_im), ldt, b_re, b_im, bt(b_re), bt(b_im), c_re, c_im)


def _block_transpose(arrs):
    blk = lax.broadcasted_iota(jnp.int32, arrs[0].shape, 1) // 16
    cur = list(arrs)
    for b in range(4):
        s = 16 << b
        hi = ((blk >> b) & 1) == 1
        nxt = list(cur)
        for x in range(16):
            if (x >> b) & 1:
                continue
            y = x | (1 << b)
            nxt[x] = jnp.where(hi, pltpu.roll(cur[y], s, 1), cur[x])
            nxt[y] = jnp.where(hi, cur[y], pltpu.roll(cur[x], 256 - s, 1))
        cur = nxt
    return cur


def _s5_in_body(s5_ref, mt_ref, q4_ref, y_ref, e_ref, x_ref):
    for hf in range(2):
        x_ref[hf] = s5_ref[:, hf * 128:(hf + 1) * 128]
    xs = [jnp.concatenate([x_ref[hf, pl.ds(i, S5_TR, stride=S5_L), :] for hf in range(2)], axis=1)
          for i in range(S5_L)]
    us = _block_transpose(xs)
    for g in range(S5_G):
        ub = us[g].astype(BF16)
        y_ref[:, g * 256:(g + 1) * 256] = lax.dot_general(ub, mt_ref[g], (((1,), (1,)), ((), ())),
                                                          preferred_element_type=F32)
        e = jnp.dot(ub, q4_ref[g], preferred_element_type=F32)
        e_ref[2 * g] = e[:, 0:128]
        e_ref[2 * g + 1] = e[:, 128:256]


def s5_in(s5, mt, q4, l):
    wspec = _layer(l, S5_G, 256, 256)
    return pl.pallas_call(
        _s5_in_body,
        out_shape=[SDS((S5_ROWS, S5_G * 256), F32), SDS((S5_LT, S5_ROWS, 128), F32)],
        grid=(N_TOK // S5_TM,),
        in_specs=[BS((S5_TM, BR), lambda t: (t, 0)), wspec, wspec],
        out_specs=[BS((S5_TR, S5_G * 256), lambda t: (t, 0)), BS((S5_LT, S5_TR, 128), lambda t: (0, t, 0))],
        scratch_shapes=[pltpu.VMEM((2, S5_TM, 128), F32)],
        compiler_params=_cparams("parallel"),
        name="s5_in",
    )(s5, mt, q4)


def _s5_chunk_scan_body(e_ref, al_ref, s0_ref, spf_ref, spb_ref, fin_ref):
    def update(g, s_re, s_im, e_re, e_im):
        a_re, a_im = al_ref[g, 0:1, :], al_ref[g, 1:2, :]
        return a_re * s_re - a_im * s_im + e_re, a_re * s_im + a_im * s_re + e_im

    fwd_c = lax.broadcasted_iota(jnp.int32, (BATCH, 128), 1) < S5_N

    def ctx_group(g, carry):
        s_re = jnp.zeros((BATCH, 128), F32)
        s_im = jnp.zeros((BATCH, 128), F32)
        for c in range(S5_CTX_C):
            rf = pl.ds(c, BATCH, stride=S5_CTX_C)
            rb = pl.ds(S5_CTX_C - 1 - c, BATCH, stride=S5_CTX_C)
            spf_ref[2 * g, rf, :] = s_re
            spf_ref[2 * g + 1, rf, :] = s_im
            spb_ref[2 * g, rb, :] = s_re
            spb_ref[2 * g + 1, rb, :] = s_im
            e_re = jnp.where(fwd_c, e_ref[2 * g, rf, :], e_ref[2 * g, rb, :])
            e_im = jnp.where(fwd_c, e_ref[2 * g + 1, rf, :], e_ref[2 * g + 1, rb, :])
            s_re, s_im = update(g, s_re, s_im, e_re, e_im)
        fin_ref[2 * g] = s_re
        fin_ref[2 * g + 1] = s_im
        return carry

    lax.fori_loop(0, S5_GQ, ctx_group, 0)

    row0 = BATCH * S5_CTX_C
    fwd_l = lax.broadcasted_iota(jnp.int32, (DEC_BATCH, 128), 1) < S5_N

    def lat_step(c, state):
        rf = pl.ds(row0 + c, DEC_BATCH, stride=S5_LAT_C)
        rb = pl.ds(row0 + S5_LAT_C - 1 - c, DEC_BATCH, stride=S5_LAT_C)
        new = []
        for g in range(S5_GQ):
            s_re, s_im = state[2 * g], state[2 * g + 1]
            spf_ref[2 * g, rf, :] = s_re
            spf_ref[2 * g + 1, rf, :] = s_im
            spb_ref[2 * g, rb, :] = s_re
            spb_ref[2 * g + 1, rb, :] = s_im
            e_re = jnp.where(fwd_l, e_ref[2 * g, rf, :], e_ref[2 * g, rb, :])
            e_im = jnp.where(fwd_l, e_ref[2 * g + 1, rf, :], e_ref[2 * g + 1, rb, :])
            new.extend(update(g, s_re, s_im, e_re, e_im))
        return tuple(new)

    lax.fori_loop(0, S5_LAT_C, lat_step, tuple(s0_ref[t] for t in range(2 * S5_GQ)))


S5_GQ = 4


def s5_chunk_scan(e3, al, s0_lat, l):
    sp = SDS((S5_LT, S5_ROWS, 128), F32)
    tiles = lambda rows: BS((2 * S5_GQ, rows, 128), lambda q: (q, 0, 0))
    return pl.pallas_call(
        _s5_chunk_scan_body,
        out_shape=[sp, sp, SDS((S5_LT, BATCH, 128), F32)],
        grid=(S5_G // S5_GQ,),
        in_specs=[tiles(S5_ROWS), BS((None, S5_GQ, 2, 128), lambda q: (l, q, 0, 0)),
                  BS((None, 2 * S5_GQ, DEC_BATCH, 128), lambda q: (l, q, 0, 0))],
        out_specs=[tiles(S5_ROWS), tiles(S5_ROWS), tiles(BATCH)],
        compiler_params=_cparams("parallel"),
        name="s5_chunk_scan",
    )(e3, al, s0_lat)


def _s5_fin_body(y_ref, spf_ref, spb_ref, rt_ref, o_ref, t_ref):
    fwd = lax.broadcasted_iota(jnp.int32, (S5_TR, 128), 1) < S5_N
    ys = []
    for g in range(S5_G):
        s_re = jnp.where(fwd, spf_ref[2 * g], spb_ref[2 * g])
        s_im = jnp.where(fwd, spf_ref[2 * g + 1], spb_ref[2 * g + 1])
        sp = jnp.concatenate([s_re, s_im], axis=1).astype(BF16)
        ys.append(y_ref[:, g * 256:(g + 1) * 256]
                  + lax.dot_general(sp, rt_ref[g], (((1,), (1,)), ((), ())), preferred_element_type=F32))
    xs = _block_transpose(ys)
    for j in range(S5_L):
        for hf in range(2):
            t_ref[hf, pl.ds(j, S5_TR, stride=S5_L), :] = xs[j][:, hf * 128:(hf + 1) * 128]
    o_ref[...] = jnp.concatenate([t_ref[0], t_ref[1]], axis=1)


def s5_fin(yi, spf, spb, rt, l):
    lt = BS((S5_LT, S5_TR, 128), lambda t: (0, t, 0))
    return pl.pallas_call(
        _s5_fin_body,
        out_shape=SDS((N_TOK, BR), F32),
        grid=(N_TOK // S5_TM,),
        in_specs=[BS((S5_TR, S5_G * 256), lambda t: (t, 0)), lt, lt, _layer(l, S5_G, 256, 256)],
        out_specs=BS((S5_TM, BR), lambda t: (t, 0)),
        scratch_shapes=[pltpu.VMEM((2, S5_TM, 128), F32)],
        compiler_params=_cparams("parallel"),
        name="s5_fin",
    )(yi, spf, spb, rt)


def s5_branch(s5, mt, q4, rt, al, s0, l):
    yi, e3 = s5_in(s5, mt, q4, l)
    spf, spb, fin = s5_chunk_scan(e3, al, s0, l)
    return s5_fin(yi, spf, spb, rt, l), fin


DN_TILE = 256
DN_HALO = 8


N_SEG = N_TOK // DN_TILE
assert N_CTX == DEC_SEQ


def _split3(x):
    x1 = x.astype(BF16)
    r1 = x - x1.astype(F32)
    x2 = r1.astype(BF16)
    return x1, x2, (r1 - x2.astype(F32)).astype(BF16)


def dot3(a, b, exact):
    if exact == 'b':
        return sum(jnp.dot(p, b.astype(BF16), preferred_element_type=F32) for p in _split3(a))
    return sum(jnp.dot(a.astype(BF16), p, preferred_element_type=F32) for p in _split3(b))


def _dn_conv_body(x_ref, prev_ref, next_ref, ba_ref, w_ref, gp_ref, ex_ref, o_ref, gb_ref, pad_ref):
    i = pl.program_id(0)
    tiles_ctx = N_CTX // DN_TILE
    tiles_seq = DEC_SEQ // DN_TILE
    j = (i - tiles_ctx) % tiles_seq
    first = jnp.logical_or(i < tiles_ctx, j == 0)
    last = jnp.logical_or(i < tiles_ctx, j == tiles_seq - 1)
    pad_ref[0:DN_HALO, :] = jnp.where(first, 0.0, prev_ref[...])
    pad_ref[DN_HALO:DN_HALO + DN_TILE, :] = x_ref[...]
    pad_ref[DN_HALO + DN_TILE:2 * DN_HALO + DN_TILE, :] = jnp.where(last, 0.0, next_ref[...])
    for r0 in range(0, DN_TILE, 128):
        for c0 in range(0, 3 * BR, 128):
            acc = jnp.zeros((128, 128), F32)
            for t in range(CONV_K):
                acc = acc + (pad_ref[pl.ds(r0 + DN_HALO - CONV_K // 2 + t, 128), c0:c0 + 128]
                             * w_ref[t:t + 1, c0:c0 + 128])
            o_ref[r0:r0 + 128, c0:c0 + 128] = silu(acc)
    raw = ba_ref[...]
    lane = lax.broadcasted_iota(jnp.int32, raw.shape, 1)
    xa = raw + gp_ref[1:2, :]
    sp = jnp.maximum(xa, 0.0) + jnp.log1p(jnp.exp(-jnp.abs(xa)))
    gates = jnp.where(lane < 2 * DN_HEADS, jax.nn.sigmoid(raw), -jnp.exp(gp_ref[0:1, :]) * sp)
    r = lax.broadcasted_iota(jnp.int32, (DN_TILE, DN_TILE), 0)
    c = lax.broadcasted_iota(jnp.int32, (DN_TILE, DN_TILE), 1)
    same = (r // CHUNK) == (c // CHUNK)
    ex = dot3(gates, ex_ref[...], 'b')
    gb_ref[:, 0:2 * BR] = ex[:, 0:2 * BR]
    gb_ref[:, 2 * BR:3 * BR] = dot3(jnp.logical_and(same, c <= r), ex[:, 2 * BR:3 * BR], 'a')
    gb_ref[:, 3 * BR:4 * BR] = dot3(jnp.logical_and(same, c >= r), ex[:, 3 * BR:4 * BR], 'a')


def _gate_expand():
    e = np.zeros((128, 4 * BR), np.float32)
    for blk in range(4):
        for h in range(DN_HEADS):
            e[blk * DN_HEADS + h, blk * BR + h * DN_HD:blk * BR + (h + 1) * DN_HD] = 1.0
    return jnp.asarray(e, BF16)


def _head_block_mask():
    r = lax.broadcasted_iota(jnp.int32, (BR, BR), 0) // DN_HD
    c = lax.broadcasted_iota(jnp.int32, (BR, BR), 1) // DN_HD
    return r == c


def _split2(x):
    hi = x.astype(BF16)
    return hi, (x - hi.astype(F32)).astype(BF16)


def _dn_local_body(qkv_ref, gb_ref, *out_refs):
    f_refs, b_refs = out_refs[0:6], out_refs[6:12]
    ncb = DN_TILE // CHUNK
    nb = 2 * ncb
    bmask = _head_block_mask()
    ri = lax.broadcasted_iota(jnp.int32, (nb, CHUNK, BR), 1)
    cj = lax.broadcasted_iota(jnp.int32, (nb, CHUNK, BR), 2) % DN_HD
    bwd = lax.broadcasted_iota(jnp.int32, (nb, CHUNK, BR), 0) >= ncb
    eye = cj == ri
    incl = jnp.logical_or(jnp.logical_and(bwd, cj >= ri), jnp.logical_and(jnp.logical_not(bwd), cj <= ri))
    strict = jnp.logical_and(incl, jnp.logical_not(eye))
    ones_blk = bmask.astype(BF16)

    def chunks(x):
        return x.reshape(ncb, CHUNK, BR)

    def both(x):
        return jnp.concatenate([x, x], axis=0)

    def head_sum(x):
        return dot3(x.reshape(-1, BR), ones_blk, 'b').reshape(x.shape)

    def bd(x):
        return jnp.where(bmask, jnp.concatenate([x] * DN_HEADS, axis=1), jnp.zeros((), x.dtype))

    def bmm(a, b):
        return jnp.einsum('bij,bjk->bik', a, b, preferred_element_type=F32)

    q = chunks(qkv_ref[:, 0:BR])
    k = chunks(qkv_ref[:, BR:2 * BR])
    v = both(chunks(qkv_ref[:, 2 * BR:3 * BR]))
    q = q * lax.rsqrt(head_sum(q * q) + EPS) * (DN_HD ** -0.5)
    k = k * lax.rsqrt(head_sum(k * k) + EPS)
    kq = jnp.einsum('bik,bjk->bij', jnp.concatenate([k, q], axis=1).astype(BF16), bd(k.astype(BF16)),
                    preferred_element_type=F32)
    kk, qk = both(kq[:, 0:CHUNK]), both(kq[:, CHUNK:2 * CHUNK])
    q, k = both(q), both(k)
    beta = jnp.concatenate([chunks(gb_ref[:, 0:BR]), chunks(gb_ref[:, BR:2 * BR])], axis=0)
    gc = jnp.concatenate([chunks(gb_ref[:, 2 * BR:3 * BR]), chunks(gb_ref[:, 3 * BR:4 * BR])], axis=0)
    crow = jnp.sum(jnp.where(eye, gc, 0.0), axis=1, keepdims=True)
    decay = jnp.where(incl, jnp.exp(jnp.where(incl, gc - crow, 0.0)), 0.0)
    a = jnp.where(strict, kk * decay * beta, 0.0)
    tinv = jnp.where(eye, 1.0, 0.0) - a
    pw = a
    pw_bd = bd(pw.astype(BF16))
    for _ in range(5):
        pw = bmm(pw.astype(BF16), pw_bd)
        pw_bd = bd(pw.astype(BF16))
        tinv = tinv + bmm(tinv.astype(BF16), pw_bd)
    egc = jnp.exp(gc)
    t_hi, t_lo = _split2(tinv)

    def solve(rhs):
        r_hi, r_lo = _split2(rhs)
        r_hi, r_lo = bd(r_hi), bd(r_lo)
        return bmm(t_hi, r_hi) + bmm(t_hi, r_lo) + bmm(t_lo, r_hi)

    w = bmm(t_hi, bd((k * (beta * egc)).astype(BF16)))

    bwd_row = lax.broadcasted_iota(jnp.int32, (nb, 1, BR), 0) >= ncb
    g_last = jnp.where(bwd_row, gc[:, 0:1], gc[:, CHUNK - 1:CHUNK])
    eg = jnp.exp(g_last)
    outs = (w, solve(v * beta), q * egc, k * jnp.exp(g_last - gc),
            jnp.where(incl, qk * decay, 0.0))
    for d, refs in enumerate((f_refs, b_refs)):
        for ref, x in zip(refs[0:5], outs):
            ref[...] = x[d * ncb:(d + 1) * ncb].reshape(DN_TILE, BR).astype(ref.dtype)
        refs[5][0] = jnp.concatenate([eg[d * ncb:(d + 1) * ncb, 0], jnp.zeros((8 - ncb, BR), F32)], axis=0)


_DN_LOCAL_DTYPES = (BF16, F32, BF16, BF16, BF16)


def _dn_prep_body(x_ref, prev_ref, next_ref, ba_ref, w_ref, gp_ref, ex_ref, *rest):
    out_refs, (pad_ref, qkv_ref, gb_ref) = rest[0:12], rest[12:15]
    _dn_conv_body(x_ref, prev_ref, next_ref, ba_ref, w_ref, gp_ref, ex_ref, qkv_ref, gb_ref, pad_ref)
    _dn_local_body(qkv_ref, gb_ref, *out_refs)


def dn_local(qkv, ba, conv_w, gate_p, l):
    per = DN_TILE // DN_HALO
    nhb = N_TOK // DN_HALO
    tok = BS((DN_TILE, BR), lambda i: (i, 0))
    shapes = [SDS((N_TOK, BR), dt) for dt in _DN_LOCAL_DTYPES] + [SDS((N_SEG, 8, BR), F32)]
    specs = [tok] * len(_DN_LOCAL_DTYPES) + [BS((1, 8, BR), lambda i: (i, 0, 0))]
    outs = pl.pallas_call(
        _dn_prep_body,
        out_shape=shapes * 2,
        grid=(N_SEG,),
        in_specs=[BS((DN_TILE, 3 * BR), lambda i: (i, 0)),
                  BS((DN_HALO, 3 * BR), lambda i: (jnp.maximum(i * per - 1, 0), 0)),
                  BS((DN_HALO, 3 * BR), lambda i: (jnp.minimum((i + 1) * per, nhb - 1), 0)),
                  BS((DN_TILE, 128), lambda i: (i, 0)),
                  _layer(l, 8, 3 * BR),
                  _layer(l, 8, 128),
                  BS((128, 4 * BR), lambda i: (0, 0))],
        out_specs=specs * 2,
        scratch_shapes=[pltpu.VMEM((DN_TILE + 2 * DN_HALO, 3 * BR), F32),
                        pltpu.VMEM((DN_TILE, 3 * BR), F32),
                        pltpu.VMEM((DN_TILE, 4 * BR), F32)],
        compiler_params=_cparams("parallel"),
        name="dn_local",
    )(qkv, qkv, qkv, ba, conv_w, gate_p, _gate_expand())
    return outs[0:6], outs[6:12]


def _dn_advance(chains, s, bmask):
    ncb = DN_TILE // CHUNK
    bmm = lambda a, b: jnp.einsum('bij,bjk->bik', a, b, preferred_element_type=F32)
    steps = []
    for t in range(ncb):
        cs = [ncb - 1 - t if rev else t for _, rev in chains]
        w, u, qt, kt, aqk, eg = [jnp.stack([load(k, c) for (load, _), c in zip(chains, cs)]) for k in range(6)]
        sb = s.astype(BF16)
        v_new = u - bmm(w, sb)
        vb = v_new.astype(BF16)
        v_bd = jnp.where(bmask, jnp.concatenate([vb] * DN_HEADS, axis=1), jnp.zeros((), BF16))
        o = bmm(qt, sb) + bmm(aqk, v_bd)
        upd = jnp.einsum('btk,btv->bkv', kt, vb, preferred_element_type=F32)
        s = s * eg + jnp.where(bmask, upd, 0.0)
        steps.append((cs, o))
    return s, steps


assert BATCH == DEC_SEQ // DN_TILE


def _dn_seq_body(*refs):
    f_in, b_in = refs[0:6], refs[6:12]
    s0_ref, of_ref, ob_ref, finf_ref, finb_ref, s_ref = refs[12:18]

    @pl.when(pl.program_id(0) == 0)
    def _():
        s_ref[...] = s0_ref[...]

    def loader(in_refs, q):
        def load(k, c):
            if k == 5:
                return in_refs[5][q, 0, c:c + 1, :]
            return in_refs[k][q, c * CHUNK:(c + 1) * CHUNK, :]
        return load

    groups = [1, 2, 1, 2, 0, 0]
    outs = [of_ref, of_ref, ob_ref, ob_ref, of_ref, ob_ref]
    chains = [(loader(b_in if o is ob_ref else f_in, q), o is ob_ref) for q, o in zip(groups, outs)]
    n_lat = 2 * DEC_BATCH
    s = jnp.concatenate([s_ref[...], jnp.zeros((2, BR, BR), F32)], axis=0)
    s, steps = _dn_advance(chains, s, _head_block_mask())
    s_ref[...] = s[0:n_lat]
    finf_ref[0] = s[n_lat]
    finb_ref[0] = s[n_lat + 1]
    for cs, o in steps:
        for i, c in enumerate(cs):
            outs[i][groups[i], c * CHUNK:(c + 1) * CHUNK, :] = o[i]


def dn_seq(loc_f, loc_b, s0, l):
    nseg = DEC_SEQ // DN_TILE
    grp = lambda a: a.reshape((N_COND, nseg) + a.shape[1:])

    def specs(m):
        return ([BS((N_COND, DN_TILE, BR), lambda g: (0, m(g), 0))] * len(_DN_LOCAL_DTYPES)
                + [BS((N_COND, 1, 8, BR), lambda g: (0, m(g), 0, 0))])

    fwd = lambda g: g
    bwd = lambda g: nseg - 1 - g
    views = lambda loc: [a.reshape(N_COND, DEC_SEQ, BR) for a in loc[0:5]] + [grp(loc[5])]
    out = SDS((N_COND, DEC_SEQ, BR), F32)
    fin = SDS((BATCH, BR, BR), F32)
    o_f, o_b, fin_f, fin_b = pl.pallas_call(
        _dn_seq_body,
        out_shape=[out, out, fin, fin],
        grid=(nseg,),
        in_specs=specs(fwd) + specs(bwd) + [_layer(l, 2 * DEC_BATCH, BR, BR)],
        out_specs=[BS((N_COND, DN_TILE, BR), lambda g: (0, fwd(g), 0)),
                   BS((N_COND, DN_TILE, BR), lambda g: (0, bwd(g), 0)),
                   BS((1, BR, BR), lambda g: (fwd(g), 0, 0)),
                   BS((1, BR, BR), lambda g: (bwd(g), 0, 0))],
        scratch_shapes=[pltpu.VMEM((2 * DEC_BATCH, BR, BR), F32)],
        compiler_params=_cparams("arbitrary"),
        name="dn_seq",
    )(*views(loc_f), *views(loc_b), s0)
    return o_f.reshape(N_TOK, BR), o_b.reshape(N_TOK, BR), fin_f, fin_b


def dn_branch(qkv, ba, conv_w, gate_p, s0, l):
    loc_f, loc_b = dn_local(qkv, ba, conv_w, gate_p, l)
    return dn_seq(loc_f, loc_b, s0, l)


def _outproj_body(x_ref, mod_ref, ypc_ref, ypl_ref, dof_ref, dob_ref, dnz_ref, s5y_ref, s5_ref,
                  yfc_ref, yfl_ref, d_ref, gw_ref, gb_ref, dng_ref, w_ref, fg_ref, o_ref, *, final, tile0):
    is_ctx = pl.program_id(0) + tile0 < N_CTX // TM
    y_pool = jnp.where(is_ctx, ypc_ref[...], ypl_ref[...])
    y_ft = jnp.where(is_ctx, yfc_ref[...], yfl_ref[...])
    gate = mod_ref[0][:, 2 * D_MODEL:3 * D_MODEL]
    o = dof_ref[...] + dob_ref[...]
    head_mean = jnp.where(_head_block_mask(), 1.0 / DN_HD, 0.0)
    y_dn = o * lax.rsqrt(dot3(o * o, head_mean, 'b') + EPS) * dng_ref[...] * silu(dnz_ref[...])
    y = s5y_ref[...] + d_ref[...] * s5_ref[:, 0:BR]
    y = jax.nn.gelu(y)
    y = y * jax.nn.sigmoid(bdot(y, gw_ref[...]) + gb_ref[...])
    y_s5 = y * silu(s5_ref[:, BR:2 * BR])
    acc = bdot(y_pool, w_ref[0:BR, :])
    acc = acc + bdot(y_dn, w_ref[BR:2 * BR, :])
    acc = acc + bdot(y_s5, w_ref[2 * BR:3 * BR, :])
    acc = acc + bdot(y_ft, w_ref[3 * BR:4 * BR, :])
    xn = x_ref[...] + gate * acc
    if final:
        xn = xn * lax.rsqrt(jnp.mean(xn * xn, axis=-1, keepdims=True) + EPS) * fg_ref[...]
    o_ref[...] = xn


def outproj(x, mod, yp_ctx, yp_lat, dn_of, dn_ob, dn_z, s5_y, s5, yf_ctx, yf_lat, s5_d, glu_w, glu_b, dn_g, w_out,
            final_g, l, final, tile0=0, ntiles=N_TOK // TM):
    row = lambda w: BS((TM, w), lambda i: (i + tile0, 0))
    full = lambda a, b: BS((a, b), lambda i: (0, 0))
    tiles_ctx = N_CTX // TM
    ctx_row = BS((TM, BR), lambda i: (jnp.minimum(i + tile0, tiles_ctx - 1), 0))
    lat_row = BS((TM, BR), lambda i: (jnp.maximum(i + tile0 - tiles_ctx, 0), 0))
    return pl.pallas_call(
        functools.partial(_outproj_body, final=final, tile0=tile0),
        out_shape=SDS((ntiles * TM, D_MODEL), F32),
        grid=(ntiles,),
        in_specs=[row(D_MODEL),
                  BS((1, 1, 3 * D_MODEL), lambda i: (8 * l + _cond_index(i + tile0), 0, 0)),
                  ctx_row, lat_row, row(BR), row(BR), row(BR), row(BR), row(2 * BR), ctx_row, lat_row,
                  _layer(l, 1, BR), _layer(l, BR, BR), _layer(l, 1, BR), _layer(l, 1, BR),
                  _layer(l, D_MODEL, D_MODEL), full(1, D_MODEL)],
        out_specs=BS((TM, D_MODEL), lambda i: (i, 0)),
        compiler_params=_cparams("parallel"),
        name="outproj",
    )(x, mod, yp_ctx, yp_lat, dn_of, dn_ob, dn_z, s5_y, s5, yf_ctx, yf_lat, s5_d, glu_w, glu_b, dn_g, w_out, final_g)


def _permute_w_in(w_in):
    main = jnp.concatenate([w_in[..., 0:1536], w_in[..., 1552:2576]], axis=-1)
    ba = jnp.pad(w_in[..., 1536:1552], ((0, 0), (0, 0), (0, 112)))
    return jnp.concatenate([main, ba], axis=-1).astype(BF16)


def kernel(x_prompt, x_sample, c, state_delta, state_s5, c_ctx, w_ada, b_ada, norm_g, w_in, pool_w, pool_scale,
           dn_conv, dn_a_log, dn_dt_bias, dn_norm_g, s5_a_re, s5_a_im, s5_log_dt, s5_b_re, s5_b_im, s5_c_re,
           s5_c_im, s5_d, s5_glu_w, s5_glu_b, ft_w, w_out, final_g):
    x = jnp.concatenate([x_prompt.astype(F32).reshape(N_CTX, D_MODEL),
                         x_sample.astype(F32).reshape(N_LAT, D_MODEL)], axis=0)
    cond8 = jnp.concatenate([c_ctx.astype(F32)[None], c.astype(F32),
                             jnp.zeros((8 - N_COND, D_MODEL), F32)], axis=0)
    ada = ada_all(cond8, w_ada, b_ada)
    pm_ctx, inv_ctx, pm_lat, inv_lat = _pool_constants()
    fpos, fch, g1, h2, fch2 = _ft_constants()
    s5_mt, s5_q4, s5_rt, s5_al = s5_prep(s5_a_re, s5_a_im, s5_log_dt, s5_b_re, s5_b_im, s5_c_re, s5_c_im)
    mod = ada.reshape(DEPTH * 8, 1, 3 * D_MODEL)
    norm_g3 = norm_g.reshape(DEPTH, 1, D_MODEL)
    w_in_p = _permute_w_in(w_in)
    eye_g = jnp.eye(len(POOL_WINDOWS), dtype=F32)
    w_bd = jnp.einsum('lgcd,gh->lgchd', pool_w, eye_g).reshape(DEPTH, BR, BR).astype(BF16)
    sc = pool_scale.reshape(DEPTH, 1, BR)
    ftw = ft_w.astype(BF16)
    conv_w = jnp.pad(dn_conv, ((0, 0), (0, 8 - CONV_K), (0, 0)))
    gate_p = jnp.zeros((DEPTH, 8, 128), F32)
    gate_p = gate_p.at[:, 0, 8:16].set(dn_a_log.reshape(DEPTH, 8)).at[:, 1, 8:16].set(dn_dt_bias.reshape(DEPTH, 8))
    out_params = (s5_d.reshape(DEPTH, 1, BR), s5_glu_w.astype(BF16), s5_glu_b.reshape(DEPTH, 1, BR),
                  jnp.tile(dn_norm_g, (1, DN_HEADS)).reshape(DEPTH, 1, BR), w_out.astype(BF16),
                  final_g.reshape(1, D_MODEL))
    s5_s0 = state_s5.astype(F32).transpose(1, 4, 3, 0, 2, 5).reshape(DEPTH, S5_LT, DEC_BATCH, 2 * S5_N)
    eye_h = jnp.eye(DN_HEADS, dtype=F32)
    dn_s0 = jnp.einsum('bldhkv,hg->ldbhkgv', state_delta.astype(F32), eye_h).reshape(DEPTH, 2 * DEC_BATCH, BR, BR)
    new_dn, new_s5 = [], []
    for l in range(DEPTH):
        pool, qkv, dn_z, s5, ft, ba = inproj(x, mod, norm_g3, w_in_p, l)

        yp_ctx = pool_branch(pool, pm_ctx, inv_ctx, w_bd, sc, False, l)
        yp_lat = pool_branch(pool, pm_lat, inv_lat, w_bd, sc, True, l)

        yf_ctx = ft_ctx(ft, fpos, fch, ftw, l)
        yf_lat = ft_lat(ft, g1, h2, fch2, ftw, l)

        s5_y, fin_s5 = s5_branch(s5, s5_mt, s5_q4, s5_rt, s5_al, s5_s0, l)
        new_s5.append(fin_s5)

        dn_of, dn_ob, fin_f, fin_b = dn_branch(qkv, ba, conv_w, gate_p, dn_s0, l)
        new_dn.append(jnp.stack([fin_f, fin_b]))

        finish = functools.partial(outproj, x, mod, yp_ctx, yp_lat, dn_of, dn_ob, dn_z, s5_y, s5, yf_ctx, yf_lat,
                                   *out_params, l)
        if l < DEPTH - 1:
            x = finish(False)
        else:
            tiles_ctx = N_CTX // TM
            y_ctx = finish(True, 0, tiles_ctx)
            y_lat = finish(True, tiles_ctx, N_LAT // TM)

    y_prompt = y_ctx.reshape(BATCH, SEQ, D_MODEL).astype(x_prompt.dtype)
    y_sample = y_lat.reshape(DEC_BATCH, DEC_SEQ, D_MODEL).astype(x_sample.dtype)
    fin = jnp.stack(new_dn).reshape(DEPTH, 2, BATCH, DN_HEADS, DN_HD, DN_HEADS, DN_HD)
    new_state_delta = jnp.einsum('ldbhkgv,hg->bldhkv', fin, eye_h).astype(state_delta.dtype)
    fin = jnp.stack(new_s5).reshape(DEPTH, S5_G, 2, BATCH, 2, S5_N)
    new_state_s5 = fin.transpose(3, 0, 4, 2, 1, 5).astype(state_s5.dtype)
    return (y_prompt, y_sample, new_state_delta, new_state_s5)
```

```python
import functools
import math

import numpy as np
import jax
import jax.numpy as jnp
from jax import lax
from jax.experimental import pallas as pl
from jax.experimental.pallas import tpu as pltpu

F32 = jnp.float32
BF16 = jnp.bfloat16

D_MODEL = 1024
BATCH = 16
SEQ = 256
DEPTH = 4
DEC_BATCH = 2
DEC_SEQ = 4096
GRID_W = 64
GRID_H = DEC_SEQ // GRID_W
BR = 256
POOL_WINDOWS = (2, 4, 8, 16)
POOL_GD = 64
DN_HEADS = 4
DN_HD = 64
CONV_K = 5
CHUNK = 64
S5_P = 16
S5_G = 16
S5_N = 64
S5_L = 16
FT_HD = 64
EPS = 1e-6

N_CTX = BATCH * SEQ
N_LAT = DEC_BATCH * DEC_SEQ
N_TOK = N_CTX + N_LAT
N_COND = 1 + DEC_BATCH
TM = 512
W_IN_COLS = 2688
VMEM_LIMIT = 56 * 1024 * 1024

SDS = jax.ShapeDtypeStruct
BS = pl.BlockSpec


def _cparams(*sem):
    return pltpu.CompilerParams(dimension_semantics=sem, vmem_limit_bytes=VMEM_LIMIT)


def bdot(a, b):
    return jnp.dot(a.astype(BF16), b.astype(BF16), preferred_element_type=F32)


def hdot(a, b):
    return jnp.dot(a, b, preferred_element_type=F32, precision=lax.Precision.HIGHEST)


def silu(x):
    return x * jax.nn.sigmoid(x)


def _cond_index(i):
    tiles_ctx = N_CTX // TM
    tiles_seq = DEC_SEQ // TM
    return jnp.where(i < tiles_ctx, 0, 1 + (i - tiles_ctx) // tiles_seq)


def _ada_body(c_ref, w_ref, b_ref, o_ref):
    o_ref[0] = hdot(silu(c_ref[...]), w_ref[0]) + b_ref[0]


def ada_all(cond8, w_ada, b_ada):
    tn = 512
    return pl.pallas_call(
        _ada_body,
        out_shape=SDS((DEPTH, 8, 3 * D_MODEL), F32),
        grid=(DEPTH, 3 * D_MODEL // tn),
        in_specs=[BS((8, D_MODEL), lambda l, j: (0, 0)),
                  BS((1, D_MODEL, tn), lambda l, j: (l, 0, j)),
                  BS((1, 1, tn), lambda l, j: (l, 0, j))],
        out_specs=BS((1, 8, tn), lambda l, j: (l, 0, j)),
        compiler_params=_cparams("parallel", "parallel"),
        name="ada",
    )(cond8, w_ada, b_ada.reshape(DEPTH, 1, 3 * D_MODEL))


def _inproj_body(x_ref, mod_ref, g_ref, w_ref, pool_ref, qkv_ref, dnz_ref, s5_ref, ft_ref, ba_ref):
    x = x_ref[...]
    m = mod_ref[0]
    shift = m[:, 0:D_MODEL]
    scale = m[:, D_MODEL:2 * D_MODEL]
    xn = x * lax.rsqrt(jnp.mean(x * x, axis=-1, keepdims=True) + EPS) * g_ref[...]
    h = (xn * (1.0 + scale) + shift).astype(BF16)

    def proj(lo, hi):
        return jnp.dot(h, w_ref[:, lo:hi], preferred_element_type=F32)

    pool_ref[...] = proj(0, 512)
    qkv_ref[...] = proj(512, 1280)
    dnz_ref[...] = proj(1280, 1536)
    s5_ref[...] = proj(1536, 2048)
    ft_ref[...] = proj(2048, 2560)
    ba_ref[...] = proj(2560, 2688)


def _layer(l, *block):
    zeros = (0,) * len(block)
    return BS((None,) + block, lambda *_: (l,) + zeros)


def inproj(x, mod, norm_g, w_in_p, l):
    widths = (512, 768, 256, 512, 512, 128)
    return pl.pallas_call(
        _inproj_body,
        out_shape=[SDS((N_TOK, w), F32) for w in widths],
        grid=(N_TOK // TM,),
        in_specs=[BS((TM, D_MODEL), lambda i: (i, 0)),
                  BS((1, 1, 3 * D_MODEL), lambda i: (8 * l + _cond_index(i), 0, 0)),
                  _layer(l, 1, D_MODEL),
                  _layer(l, D_MODEL, W_IN_COLS)],
        out_specs=[BS((TM, w), lambda i: (i, 0)) for w in widths],
        compiler_params=_cparams("parallel"),
        name="inproj",
    )(x, mod, norm_g, w_in_p)


def _pool_body(u_ref, z_ref, pm_ref, inv_ref, w_ref, sc_ref, o_ref, *scratch, two_d):
    nblk = u_ref.shape[0] // 256
    if two_d:
        pad_ref, v_ref = scratch
        halo = 8 * GRID_W
        pad_ref[0:halo, :] = jnp.zeros((halo, BR), F32)
        pad_ref[halo + DEC_SEQ:2 * halo + DEC_SEQ, :] = jnp.zeros((halo, BR), F32)
        pad_ref[halo:halo + DEC_SEQ, :] = u_ref[...]
        lane = lax.broadcasted_iota(jnp.int32, (GRID_W, 128), 1)

        def row_body(r, c):
            base = pl.multiple_of(r * GRID_W, GRID_W)

            def slab(d, lo):
                return pad_ref[pl.ds(base + (8 + d) * GRID_W, GRID_W), lo:lo + 128]

            s2 = slab(-1, 0) + slab(0, 0)
            s4 = s2 + slab(-2, 0) + slab(1, 0)
            v_ref[pl.ds(base, GRID_W), 0:128] = jnp.where(lane < 64, s2, s4)
            s8 = slab(-4, 128)
            for d in (-3, -2, -1, 0, 1, 2, 3):
                s8 = s8 + slab(d, 128)
            s16 = s8
            for d in (-8, -7, -6, -5, 4, 5, 6, 7):
                s16 = s16 + slab(d, 128)
            v_ref[pl.ds(base, GRID_W), 128:256] = jnp.where(lane < 64, s8, s16)
            return c

        lax.fori_loop(0, GRID_H, row_body, 0)
        src = v_ref
    else:
        src = u_ref
    grp = lax.broadcasted_iota(jnp.int32, (256, BR), 1) // POOL_GD

    def blk_body(b, c):
        r0 = pl.multiple_of(b * 256, 256)
        vb = src[pl.ds(r0, 256), :]
        hi = vb.astype(BF16)
        lo = (vb - hi.astype(F32)).astype(BF16)
        res = jnp.zeros((256, BR), F32)
        for g in range(len(POOL_WINDOWS)):
            pg = (jnp.dot(pm_ref[g], hi, preferred_element_type=F32)
                  + jnp.dot(pm_ref[g], lo, preferred_element_type=F32))
            res = jnp.where(grp == g, pg, res)
        pooled = res * inv_ref[pl.ds(r0, 256), :]
        d = pooled - u_ref[pl.ds(r0, 256), :]
        y = bdot(d, w_ref[...]) * sc_ref[...]
        o_ref[pl.ds(r0, 256), :] = (y * silu(z_ref[pl.ds(r0, 256), :])).astype(BF16)
        return c

    lax.fori_loop(0, nblk, blk_body, 0)


def _band_matrices(seg):
    t = np.arange(256)
    out = []
    for w in POOL_WINDOWS:
        lo = t - w // 2
        hi = t - w // 2 + w
        s = t[None, :]
        m = (s >= lo[:, None]) & (s < hi[:, None]) & ((s // seg) == (t[:, None] // seg))
        out.append(m.astype(np.float32))
    return np.stack(out)


def _counts(length, w):
    pos = np.arange(length)
    return (np.clip(pos - w // 2 + w, 0, length) - np.clip(pos - w // 2, 0, length)).astype(np.float64)


def _pool_constants():
    inv_ctx = np.concatenate([np.repeat((1.0 / _counts(SEQ, w))[:, None], POOL_GD, 1) for w in POOL_WINDOWS], 1)
    inv_lat = []
    for w in POOL_WINDOWS:
        c2 = np.outer(_counts(GRID_H, w), _counts(GRID_W, w)).reshape(DEC_SEQ)
        inv_lat.append(np.repeat((1.0 / c2)[:, None], POOL_GD, 1))
    inv_lat = np.concatenate(inv_lat, 1)
    return (jnp.asarray(_band_matrices(SEQ), BF16), jnp.asarray(inv_ctx, F32),
            jnp.asarray(_band_matrices(GRID_W), BF16), jnp.asarray(inv_lat, F32))


def pool_branch(pool, pm, inv, w_bd, scale, two_d, l):
    if two_d:
        rows, nseq, blk0 = DEC_SEQ, DEC_BATCH, N_CTX // DEC_SEQ
        scratch = [pltpu.VMEM((DEC_SEQ + 16 * GRID_W, BR), F32), pltpu.VMEM((DEC_SEQ, BR), F32)]
    else:
        rows, nseq, blk0 = SEQ, BATCH, 0
        scratch = []
    return pl.pallas_call(
        functools.partial(_pool_body, two_d=two_d),
        out_shape=SDS((nseq * rows, BR), BF16),
        grid=(nseq,),
        in_specs=[BS((rows, BR), lambda i: (blk0 + i, 0)),
                  BS((rows, BR), lambda i: (blk0 + i, 1)),
                  BS((4, 256, 256), lambda i: (0, 0, 0)),
                  BS((rows, BR), lambda i: (0, 0)),
                  _layer(l, BR, BR),
                  _layer(l, 1, BR)],
        out_specs=BS((rows, BR), lambda i: (i, 0)),
        scratch_shapes=scratch,
        compiler_params=_cparams("parallel"),
        name="pool2d" if two_d else "pool1d",
    )(pool, pool, pm, inv, w_bd, scale)


def _ft_ctx_body(u_ref, z_ref, fpos_ref, fch_ref, w_ref, o_ref):
    uc = bdot(u_ref[...], fch_ref[...])
    st = jnp.concatenate([uc[:, 0:BR], uc[:, BR:2 * BR]], axis=0)
    f = bdot(fpos_ref[...], st)
    o_ref[...] = (bdot(f, w_ref[...]) * silu(z_ref[...])).astype(BF16)


def ft_ctx(ft, fpos, fch, ft_w, l):
    return pl.pallas_call(
        _ft_ctx_body,
        out_shape=SDS((N_CTX, BR), BF16),
        grid=(BATCH,),
        in_specs=[BS((SEQ, BR), lambda i: (i, 0)),
                  BS((SEQ, BR), lambda i: (i, 1)),
                  BS((SEQ, 2 * SEQ), lambda i: (0, 0)),
                  BS((BR, 2 * BR), lambda i: (0, 0)),
                  _layer(l, BR, BR)],
        out_specs=BS((SEQ, BR), lambda i: (i, 0)),
        compiler_params=_cparams("parallel"),
        name="ft_ctx",
    )(ft, ft, fpos, fch, ft_w)


def _ft_lat_body(u_ref, z_ref, g_ref, h_ref, fch_ref, w_ref, o_ref, x_ref, yr_ref, yi_ref):
    for hf in range(2):
        x_ref[hf] = u_ref[:, hf * 128:(hf + 1) * 128]

    def stage1(t2, c):
        xs = jnp.concatenate([x_ref[hf, pl.ds(t2, GRID_H, stride=GRID_W), :] for hf in range(2)], axis=1)
        y = jnp.dot(g_ref[t2], xs.astype(BF16), preferred_element_type=F32)
        r0 = pl.multiple_of(t2 * GRID_W, GRID_W)
        for hf in range(2):
            yr_ref[hf, pl.ds(r0, GRID_W), :] = y[0:64, hf * 128:(hf + 1) * 128]
            yi_ref[hf, pl.ds(r0, GRID_W), :] = y[64:128, hf * 128:(hf + 1) * 128]
        return c

    lax.fori_loop(0, GRID_W, stage1, 0, unroll=8)

    def stage2(kb, c):
        yr = jnp.concatenate([yr_ref[hf, pl.ds(kb, GRID_W, stride=GRID_W), :] for hf in range(2)], axis=1)
        yi = jnp.concatenate([yi_ref[hf, pl.ds(kb, GRID_W, stride=GRID_W), :] for hf in range(2)], axis=1)
        st = jnp.concatenate([yr, yi], axis=0).astype(BF16)
        a = jnp.dot(h_ref[...], st, preferred_element_type=F32)
        for hf in range(2):
            yr_ref[hf, pl.ds(kb, GRID_W, stride=GRID_W), :] = a[0:64, hf * 128:(hf + 1) * 128]
            yi_ref[hf, pl.ds(kb, GRID_W, stride=GRID_W), :] = a[64:128, hf * 128:(hf + 1) * 128]
        return c

    lax.fori_loop(0, GRID_W, stage2, 0, unroll=8)

    def stage3(b, c):
        r0 = pl.multiple_of(b * TM, TM)
        ar = jnp.concatenate([yr_ref[hf, pl.ds(r0, TM), :] for hf in range(2)], axis=1)
        ai = jnp.concatenate([yi_ref[hf, pl.ds(r0, TM), :] for hf in range(2)], axis=1)
        f = bdot(ar, fch_ref[0:BR, :]) + bdot(ai, fch_ref[BR:2 * BR, :])
        o_ref[pl.ds(r0, TM), :] = (bdot(f, w_ref[...]) * silu(z_ref[pl.ds(r0, TM), :])).astype(BF16)
        return c

    lax.fori_loop(0, DEC_SEQ // TM, stage3, 0)


def ft_lat(ft, g1, h2, fch2, ft_w, l):
    blk0 = N_CTX // DEC_SEQ
    return pl.pallas_call(
        _ft_lat_body,
        out_shape=SDS((N_LAT, BR), BF16),
        grid=(DEC_BATCH,),
        in_specs=[BS((DEC_SEQ, BR), lambda i: (blk0 + i, 0)),
                  BS((DEC_SEQ, BR), lambda i: (blk0 + i, 1)),
                  BS((GRID_W, 128, GRID_H), lambda i: (0, 0, 0)),
                  BS((128, 128), lambda i: (0, 0)),
                  BS((2 * BR, BR), lambda i: (0, 0)),
                  _layer(l, BR, BR)],
        out_specs=BS((DEC_SEQ, BR), lambda i: (i, 0)),
        scratch_shapes=[pltpu.VMEM((2, DEC_SEQ, 128), F32)] * 3,
        compiler_params=_cparams("parallel"),
        name="ft_lat",
    )(ft, ft, g1, h2, fch2, ft_w)


def _ft_constants():
    c = np.arange(FT_HD)
    ang = 2.0 * np.pi * np.outer(c, c) / FT_HD
    eye4 = np.eye(BR // FT_HD)
    cc = np.kron(eye4, np.cos(ang)) / 8.0
    sc = np.kron(eye4, np.sin(ang)) / 8.0
    t = np.arange(SEQ)
    angt = 2.0 * np.pi * (np.outer(t, t) % SEQ) / SEQ
    fpos = np.concatenate([np.cos(angt), -np.sin(angt)], axis=1) / 16.0
    fch = np.concatenate([cc, sc], axis=1)
    kb = np.arange(GRID_W)[None, :, None]
    t1 = np.arange(GRID_H)[None, None, :]
    t2 = np.arange(GRID_W)[:, None, None]
    a1 = 2.0 * np.pi * ((kb * (GRID_W * t1 + t2)) % DEC_SEQ) / DEC_SEQ
    g1 = np.concatenate([np.cos(a1), -np.sin(a1)], axis=1) / 8.0
    a2 = 2.0 * np.pi * (np.outer(np.arange(GRID_W), np.arange(GRID_W)) % GRID_W) / GRID_W
    c2, s2 = np.cos(a2) / 8.0, np.sin(a2) / 8.0
    h2 = np.block([[c2, s2], [-s2, c2]])
    fch2 = np.concatenate([cc, sc], axis=0)
    as_bf = lambda a: jnp.asarray(a, F32).astype(BF16)
    return as_bf(fpos), as_bf(fch), as_bf(g1), as_bf(h2), as_bf(fch2)


S5_ROWS = N_TOK // S5_L
S5_TM = 1024
S5_TR = S5_TM // S5_L
S5_CTX_C = SEQ // S5_L
S5_LAT_C = DEC_SEQ // S5_L
S5_LT = 2 * S5_G


S5_PG = 4


def _s5_prep_body(*refs):
    for gi in range(S5_PG):
        _s5_prep_group(gi, *refs)


def _s5_prep_group(gi, ar_ref, ai_ref, ldt_ref, br_ref, bi_ref, btr_ref, bti_ref, cr_ref, ci_ref,
                   mt_ref, q4_ref, rt_ref, al_ref):
    L = S5_L
    m = lax.broadcasted_iota(jnp.int32, (2 * L, S5_N), 0).astype(F32)
    rts, qs, ds, als = [], [], [], []
    for d in range(2):
        a_re, a_im = ar_ref[0, d, gi], ai_ref[0, d, gi]
        dt = jnp.exp(ldt_ref[0, d, gi])
        xr, xi = a_re * dt, a_im * dt
        mag = jnp.exp(xr)
        ab_re, ab_im = mag * jnp.cos(xi), mag * jnp.sin(xi)
        den = a_re * a_re + a_im * a_im
        nr = ab_re - 1.0
        coef_re = (nr * a_re + ab_im * a_im) / den
        coef_im = (ab_im * a_re - nr * a_im) / den
        pw_re = jnp.exp(m * xr) * jnp.cos(m * xi)
        pw_im = jnp.exp(m * xr) * jnp.sin(m * xi)
        cq_re = pw_re * coef_re - pw_im * coef_im
        cq_im = pw_re * coef_im + pw_im * coef_re
        c_re, c_im = cr_ref[0, d, gi], ci_ref[0, d, gi]
        bt_re, bt_im = btr_ref[0, d, gi], bti_ref[0, d, gi]
        row = lambda x, e: x[e:e + 1, :]
        order = range(L) if d == 0 else range(L - 1, -1, -1)
        cp_re = jnp.concatenate([c_re * row(cq_re, e) - c_im * row(cq_im, e) for e in order], axis=0)
        cp_im = jnp.concatenate([c_re * row(cq_im, e) + c_im * row(cq_re, e) for e in order], axis=0)
        rts.append(hdot(cp_re, br_ref[0, d, gi]) - hdot(cp_im, bi_ref[0, d, gi]))
        inj = [L - 1 - i for i in range(L)] if d == 0 else list(range(L))
        q_re = jnp.concatenate([bt_re * row(cq_re, e) - bt_im * row(cq_im, e) for e in inj], axis=0)
        q_im = jnp.concatenate([bt_im * row(cq_re, e) + bt_re * row(cq_im, e) for e in inj], axis=0)
        qs.append((q_re, q_im))
        out = [j + 1 for j in range(L)] if d == 0 else [L - j for j in range(L)]
        d_re = jnp.concatenate([c_re * row(pw_re, e) - c_im * row(pw_im, e) for e in out], axis=0)
        d_im = jnp.concatenate([c_re * row(pw_im, e) + c_im * row(pw_re, e) for e in out], axis=0)
        ds.append((d_re, d_im))
        als.append((row(pw_re, L), row(pw_im, L)))
    pad = jnp.zeros(((L - 1) * S5_P, S5_P), F32)
    z = jnp.concatenate([pad, rts[0]], axis=0) + jnp.concatenate([rts[1], pad], axis=0)
    mt = jnp.concatenate([z[(L - 1 - i) * S5_P:(L - 1 - i) * S5_P + L * S5_P, :] for i in range(L)], axis=1)
    mt_ref[0, gi] = mt.astype(BF16)
    q4_ref[0, gi] = jnp.concatenate([qs[0][0], qs[1][0], qs[0][1], qs[1][1]], axis=1).astype(BF16)
    rt_ref[0, gi] = jnp.concatenate([ds[0][0], ds[1][0], -ds[0][1], -ds[1][1]], axis=1).astype(BF16)
    al_ref[0, gi] = jnp.concatenate([jnp.concatenate([als[0][0], als[1][0]], axis=1),
                                    jnp.concatenate([als[0][1], als[1][1]], axis=1)], axis=0)


def s5_prep(a_re, a_im, log_dt, b_re, b_im, c_re, c_im):
    vec = lambda x: x.reshape(DEPTH, 2, S5_G, 1, S5_N)
    ldt = jnp.broadcast_to(log_dt[..., None, None], (DEPTH, 2, S5_G, 1, S5_N))
    bt = lambda x: x.transpose(0, 1, 2, 4, 3)
    per = S5_G // S5_PG
    vspec = BS((1, 2, S5_PG, 1, S5_N), lambda i: (i // per, 0, i % per, 0, 0))
    bspec = BS((1, 2, S5_PG, S5_N, S5_P), lambda i: (i // per, 0, i % per, 0, 0))
    cspec = BS((1, 2, S5_PG, S5_P, S5_N), lambda i: (i // per, 0, i % per, 0, 0))
    mat = SDS((DEPTH, S5_G, 256, 256), BF16)
    mspec = BS((1, S5_PG, 256, 256), lambda i: (i // per, i % per, 0, 0))
    return pl.pallas_call(
        _s5_prep_body,
        out_shape=[mat, mat, mat, SDS((DEPTH, S5_G, 2, 128), F32)],
        grid=(DEPTH * per,),
        in_specs=[vspec, vspec, vspec, bspec, bspec, cspec, cspec, cspec, cspec],
        out_specs=[mspec, mspec, mspec, BS((1, S5_PG, 2, 128), lambda i: (i // per, i % per, 0, 0))],
        compiler_params=_cparams("parallel"),
        name="s5_prep",
    )(vec(a_re), vec(a_im), ldt, b_re, b_im, bt(b_re), bt(b_im), c_re, c_im)


def _block_transpose(arrs):
    blk = lax.broadcasted_iota(jnp.int32, arrs[0].shape, 1) // 16
    cur = list(arrs)
    for b in range(4):
        s = 16 << b
        hi = ((blk >> b) & 1) == 1
        nxt = list(cur)
        for x in range(16):
            if (x >> b) & 1:
                continue
            y = x | (1 << b)
            nxt[x] = jnp.where(hi, pltpu.roll(cur[y], s, 1), cur[x])
            nxt[y] = jnp.where(hi, cur[y], pltpu.roll(cur[x], 256 - s, 1))
        cur = nxt
    return cur


def _s5_in_body(s5_ref, mt_ref, q4_ref, y_ref, e_ref, x_ref):
    for hf in range(2):
        x_ref[hf] = s5_ref[:, hf * 128:(hf + 1) * 128]
    xs = [jnp.concatenate([x_ref[hf, pl.ds(i, S5_TR, stride=S5_L), :] for hf in range(2)], axis=1)
          for i in range(S5_L)]
    us = _block_transpose(xs)
    for g in range(S5_G):
        ub = us[g].astype(BF16)
        y_ref[:, g * 256:(g + 1) * 256] = lax.dot_general(ub, mt_ref[g], (((1,), (1,)), ((), ())),
                                                          preferred_element_type=F32)
        e = jnp.dot(ub, q4_ref[g], preferred_element_type=F32)
        e_ref[2 * g] = e[:, 0:128]
        e_ref[2 * g + 1] = e[:, 128:256]


def s5_in(s5, mt, q4, l):
    wspec = _layer(l, S5_G, 256, 256)
    return pl.pallas_call(
        _s5_in_body,
        out_shape=[SDS((S5_ROWS, S5_G * 256), F32), SDS((S5_LT, S5_ROWS, 128), F32)],
        grid=(N_TOK // S5_TM,),
        in_specs=[BS((S5_TM, BR), lambda t: (t, 0)), wspec, wspec],
        out_specs=[BS((S5_TR, S5_G * 256), lambda t: (t, 0)), BS((S5_LT, S5_TR, 128), lambda t: (0, t, 0))],
        scratch_shapes=[pltpu.VMEM((2, S5_TM, 128), F32)],
        compiler_params=_cparams("parallel"),
        name="s5_in",
    )(s5, mt, q4)


def _s5_chunk_scan_body(e_ref, al_ref, s0_ref, spf_ref, spb_ref, fin_ref):
    def update(g, s_re, s_im, e_re, e_im):
        a_re, a_im = al_ref[g, 0:1, :], al_ref[g, 1:2, :]
        return a_re * s_re - a_im * s_im + e_re, a_re * s_im + a_im * s_re + e_im

    fwd_c = lax.broadcasted_iota(jnp.int32, (BATCH, 128), 1) < S5_N

    def ctx_group(g, carry):
        s_re = jnp.zeros((BATCH, 128), F32)
        s_im = jnp.zeros((BATCH, 128), F32)
        for c in range(S5_CTX_C):
            rf = pl.ds(c, BATCH, stride=S5_CTX_C)
            rb = pl.ds(S5_CTX_C - 1 - c, BATCH, stride=S5_CTX_C)
            spf_ref[2 * g, rf, :] = s_re
            spf_ref[2 * g + 1, rf, :] = s_im
            spb_ref[2 * g, rb, :] = s_re
            spb_ref[2 * g + 1, rb, :] = s_im
            e_re = jnp.where(fwd_c, e_ref[2 * g, rf, :], e_ref[2 * g, rb, :])
            e_im = jnp.where(fwd_c, e_ref[2 * g + 1, rf, :], e_ref[2 * g + 1, rb, :])
            s_re, s_im = update(g, s_re, s_im, e_re, e_im)
        fin_ref[2 * g] = s_re
        fin_ref[2 * g + 1] = s_im
        return carry

    lax.fori_loop(0, S5_GQ, ctx_group, 0)

    row0 = BATCH * S5_CTX_C
    fwd_l = lax.broadcasted_iota(jnp.int32, (DEC_BATCH, 128), 1) < S5_N

    def lat_step(c, state):
        rf = pl.ds(row0 + c, DEC_BATCH, stride=S5_LAT_C)
        rb = pl.ds(row0 + S5_LAT_C - 1 - c, DEC_BATCH, stride=S5_LAT_C)
        new = []
        for g in range(S5_GQ):
            s_re, s_im = state[2 * g], state[2 * g + 1]
            spf_ref[2 * g, rf, :] = s_re
            spf_ref[2 * g + 1, rf, :] = s_im
            spb_ref[2 * g, rb, :] = s_re
            spb_ref[2 * g + 1, rb, :] = s_im
            e_re = jnp.where(fwd_l, e_ref[2 * g, rf, :], e_ref[2 * g, rb, :])
            e_im = jnp.where(fwd_l, e_ref[2 * g + 1, rf, :], e_ref[2 * g + 1, rb, :])
            new.extend(update(g, s_re, s_im, e_re, e_im))
        return tuple(new)

    lax.fori_loop(0, S5_LAT_C, lat_step, tuple(s0_ref[t] for t in range(2 * S5_GQ)))


S5_GQ = 4


def s5_chunk_scan(e3, al, s0_lat, l):
    sp = SDS((S5_LT, S5_ROWS, 128), F32)
    tiles = lambda rows: BS((2 * S5_GQ, rows, 128), lambda q: (q, 0, 0))
    return pl.pallas_call(
        _s5_chunk_scan_body,
        out_shape=[sp, sp, SDS((S5_LT, BATCH, 128), F32)],
        grid=(S5_G // S5_GQ,),
        in_specs=[tiles(S5_ROWS), BS((None, S5_GQ, 2, 128), lambda q: (l, q, 0, 0)),
                  BS((None, 2 * S5_GQ, DEC_BATCH, 128), lambda q: (l, q, 0, 0))],
        out_specs=[tiles(S5_ROWS), tiles(S5_ROWS), tiles(BATCH)],
        compiler_params=_cparams("parallel"),
        name="s5_chunk_scan",
    )(e3, al, s0_lat)


def _s5_fin_body(y_ref, spf_ref, spb_ref, rt_ref, o_ref, t_ref):
    fwd = lax.broadcasted_iota(jnp.int32, (S5_TR, 128), 1) < S5_N
    ys = []
    for g in range(S5_G):
        s_re = jnp.where(fwd, spf_ref[2 * g], spb_ref[2 * g])
        s_im = jnp.where(fwd, spf_ref[2 * g + 1], spb_ref[2 * g + 1])
        sp = jnp.concatenate([s_re, s_im], axis=1).astype(BF16)
        ys.append(y_ref[:, g * 256:(g + 1) * 256]
                  + lax.dot_general(sp, rt_ref[g], (((1,), (1,)), ((), ())), preferred_element_type=F32))
    xs = _block_transpose(ys)
    for j in range(S5_L):
        for hf in range(2):
            t_ref[hf, pl.ds(j, S5_TR, stride=S5_L), :] = xs[j][:, hf * 128:(hf + 1) * 128]
    o_ref[...] = jnp.concatenate([t_ref[0], t_ref[1]], axis=1)


def s5_fin(yi, spf, spb, rt, l):
    lt = BS((S5_LT, S5_TR, 128), lambda t: (0, t, 0))
    return pl.pallas_call(
        _s5_fin_body,
        out_shape=SDS((N_TOK, BR), F32),
        grid=(N_TOK // S5_TM,),
        in_specs=[BS((S5_TR, S5_G * 256), lambda t: (t, 0)), lt, lt, _layer(l, S5_G, 256, 256)],
        out_specs=BS((S5_TM, BR), lambda t: (t, 0)),
        scratch_shapes=[pltpu.VMEM((2, S5_TM, 128), F32)],
        compiler_params=_cparams("parallel"),
        name="s5_fin",
    )(yi, spf, spb, rt)


def s5_branch(s5, mt, q4, rt, al, s0, l):
    yi, e3 = s5_in(s5, mt, q4, l)
    spf, spb, fin = s5_chunk_scan(e3, al, s0, l)
    return s5_fin(yi, spf, spb, rt, l), fin


DN_TILE = 256
DN_HALO = 8


N_SEG = N_TOK // DN_TILE
assert N_CTX == DEC_SEQ


def _split3(x):
    x1 = x.astype(BF16)
    r1 = x - x1.astype(F32)
    x2 = r1.astype(BF16)
    return x1, x2, (r1 - x2.astype(F32)).astype(BF16)


def dot3(a, b, exact):
    if exact == 'b':
        return sum(jnp.dot(p, b.astype(BF16), preferred_element_type=F32) for p in _split3(a))
    return sum(jnp.dot(a.astype(BF16), p, preferred_element_type=F32) for p in _split3(b))


def _dn_conv_body(x_ref, prev_ref, next_ref, ba_ref, w_ref, gp_ref, ex_ref, o_ref, gb_ref, pad_ref):
    i = pl.program_id(0)
    tiles_ctx = N_CTX // DN_TILE
    tiles_seq = DEC_SEQ // DN_TILE
    j = (i - tiles_ctx) % tiles_seq
    first = jnp.logical_or(i < tiles_ctx, j == 0)
    last = jnp.logical_or(i < tiles_ctx, j == tiles_seq - 1)
    pad_ref[0:DN_HALO, :] = jnp.where(first, 0.0, prev_ref[...])
    pad_ref[DN_HALO:DN_HALO + DN_TILE, :] = x_ref[...]
    pad_ref[DN_HALO + DN_TILE:2 * DN_HALO + DN_TILE, :] = jnp.where(last, 0.0, next_ref[...])
    for r0 in range(0, DN_TILE, 128):
        for c0 in range(0, 3 * BR, 128):
            acc = jnp.zeros((128, 128), F32)
            for t in range(CONV_K):
                acc = acc + (pad_ref[pl.ds(r0 + DN_HALO - CONV_K // 2 + t, 128), c0:c0 + 128]
                             * w_ref[t:t + 1, c0:c0 + 128])
            o_ref[r0:r0 + 128, c0:c0 + 128] = silu(acc)
    raw = ba_ref[...]
    lane = lax.broadcasted_iota(jnp.int32, raw.shape, 1)
    xa = raw + gp_ref[1:2, :]
    sp = jnp.maximum(xa, 0.0) + jnp.log1p(jnp.exp(-jnp.abs(xa)))
    gates = jnp.where(lane < 2 * DN_HEADS, jax.nn.sigmoid(raw), -jnp.exp(gp_ref[0:1, :]) * sp)
    r = lax.broadcasted_iota(jnp.int32, (DN_TILE, DN_TILE), 0)
    c = lax.broadcasted_iota(jnp.int32, (DN_TILE, DN_TILE), 1)
    same = (r // CHUNK) == (c // CHUNK)
    ex = dot3(gates, ex_ref[...], 'b')
    gb_ref[:, 0:2 * BR] = ex[:, 0:2 * BR]
    gb_ref[:, 2 * BR:3 * BR] = dot3(jnp.logical_and(same, c <= r), ex[:, 2 * BR:3 * BR], 'a')
    gb_ref[:, 3 * BR:4 * BR] = dot3(jnp.logical_and(same, c >= r), ex[:, 3 * BR:4 * BR], 'a')


def _gate_expand():
    e = np.zeros((128, 4 * BR), np.float32)
    for blk in range(4):
        for h in range(DN_HEADS):
            e[blk * DN_HEADS + h, blk * BR + h * DN_HD:blk * BR + (h + 1) * DN_HD] = 1.0
    return jnp.asarray(e, BF16)


def _head_block_mask():
    r = lax.broadcasted_iota(jnp.int32, (BR, BR), 0) // DN_HD
    c = lax.broadcasted_iota(jnp.int32, (BR, BR), 1) // DN_HD
    return r == c


def _split2(x):
    hi = x.astype(BF16)
    return hi, (x - hi.astype(F32)).astype(BF16)


def _dn_local_body(qkv_ref, gb_ref, *out_refs):
    f_refs, b_refs = out_refs[0:6], out_refs[6:12]
    ncb = DN_TILE // CHUNK
    nb = 2 * ncb
    bmask = _head_block_mask()
    ri = lax.broadcasted_iota(jnp.int32, (nb, CHUNK, BR), 1)
    cj = lax.broadcasted_iota(jnp.int32, (nb, CHUNK, BR), 2) % DN_HD
    bwd = lax.broadcasted_iota(jnp.int32, (nb, CHUNK, BR), 0) >= ncb
    eye = cj == ri
    incl = jnp.logical_or(jnp.logical_and(bwd, cj >= ri), jnp.logical_and(jnp.logical_not(bwd), cj <= ri))
    strict = jnp.logical_and(incl, jnp.logical_not(eye))
    ones_blk = bmask.astype(BF16)

    def chunks(x):
        return x.reshape(ncb, CHUNK, BR)

    def both(x):
        return jnp.concatenate([x, x], axis=0)

    def head_sum(x):
        return dot3(x.reshape(-1, BR), ones_blk, 'b').reshape(x.shape)

    def bd(x):
        return jnp.where(bmask, jnp.concatenate([x] * DN_HEADS, axis=1), jnp.zeros((), x.dtype))

    def bmm(a, b):
        return jnp.einsum('bij,bjk->bik', a, b, preferred_element_type=F32)

    q = chunks(qkv_ref[:, 0:BR])
    k = chunks(qkv_ref[:, BR:2 * BR])
    v = both(chunks(qkv_ref[:, 2 * BR:3 * BR]))
    q = q * lax.rsqrt(head_sum(q * q) + EPS) * (DN_HD ** -0.5)
    k = k * lax.rsqrt(head_sum(k * k) + EPS)
    kq = jnp.einsum('bik,bjk->bij', jnp.concatenate([k, q], axis=1).astype(BF16), bd(k.astype(BF16)),
                    preferred_element_type=F32)
    kk, qk = both(kq[:, 0:CHUNK]), both(kq[:, CHUNK:2 * CHUNK])
    q, k = both(q), both(k)
    beta = jnp.concatenate([chunks(gb_ref[:, 0:BR]), chunks(gb_ref[:, BR:2 * BR])], axis=0)
    gc = jnp.concatenate([chunks(gb_ref[:, 2 * BR:3 * BR]), chunks(gb_ref[:, 3 * BR:4 * BR])], axis=0)
    crow = jnp.sum(jnp.where(eye, gc, 0.0), axis=1, keepdims=True)
    decay = jnp.where(incl, jnp.exp(jnp.where(incl, gc - crow, 0.0)), 0.0)
    a = jnp.where(strict, kk * decay * beta, 0.0)
    tinv = jnp.where(eye, 1.0, 0.0) - a
    pw = a
    pw_bd = bd(pw.astype(BF16))
    for _ in range(5):
        pw = bmm(pw.astype(BF16), pw_bd)
        pw_bd = bd(pw.astype(BF16))
        tinv = tinv + bmm(tinv.astype(BF16), pw_bd)
    egc = jnp.exp(gc)
    t_hi, t_lo = _split2(tinv)

    def solve(rhs):
        r_hi, r_lo = _split2(rhs)
        r_hi, r_lo = bd(r_hi), bd(r_lo)
        return bmm(t_hi, r_hi) + bmm(t_hi, r_lo) + bmm(t_lo, r_hi)

    w = bmm(t_hi, bd((k * (beta * egc)).astype(BF16)))

    bwd_row = lax.broadcasted_iota(jnp.int32, (nb, 1, BR), 0) >= ncb
    g_last = jnp.where(bwd_row, gc[:, 0:1], gc[:, CHUNK - 1:CHUNK])
    eg = jnp.exp(g_last)
    outs = (w, solve(v * beta), q * egc, k * jnp.exp(g_last - gc),
            jnp.where(incl, qk * decay, 0.0))
    for d, refs in enumerate((f_refs, b_refs)):
        for ref, x in zip(refs[0:5], outs):
            ref[...] = x[d * ncb:(d + 1) * ncb].reshape(DN_TILE, BR).astype(ref.dtype)
        refs[5][0] = jnp.concatenate([eg[d * ncb:(d + 1) * ncb, 0], jnp.zeros((8 - ncb, BR), F32)], axis=0)


_DN_LOCAL_DTYPES = (BF16, F32, BF16, BF16, BF16)


def _dn_prep_body(x_ref, prev_ref, next_ref, ba_ref, w_ref, gp_ref, ex_ref, *rest):
    out_refs, (pad_ref, qkv_ref, gb_ref) = rest[0:12], rest[12:15]
    _dn_conv_body(x_ref, prev_ref, next_ref, ba_ref, w_ref, gp_ref, ex_ref, qkv_ref, gb_ref, pad_ref)
    _dn_local_body(qkv_ref, gb_ref, *out_refs)


def dn_local(qkv, ba, conv_w, gate_p, l):
    per = DN_TILE // DN_HALO
    nhb = N_TOK // DN_HALO
    tok = BS((DN_TILE, BR), lambda i: (i, 0))
    shapes = [SDS((N_TOK, BR), dt) for dt in _DN_LOCAL_DTYPES] + [SDS((N_SEG, 8, BR), F32)]
    specs = [tok] * len(_DN_LOCAL_DTYPES) + [BS((1, 8, BR), lambda i: (i, 0, 0))]
    outs = pl.pallas_call(
        _dn_prep_body,
        out_shape=shapes * 2,
        grid=(N_SEG,),
        in_specs=[BS((DN_TILE, 3 * BR), lambda i: (i, 0)),
                  BS((DN_HALO, 3 * BR), lambda i: (jnp.maximum(i * per - 1, 0), 0)),
                  BS((DN_HALO, 3 * BR), lambda i: (jnp.minimum((i + 1) * per, nhb - 1), 0)),
                  BS((DN_TILE, 128), lambda i: (i, 0)),
                  _layer(l, 8, 3 * BR),
                  _layer(l, 8, 128),
                  BS((128, 4 * BR), lambda i: (0, 0))],
        out_specs=specs * 2,
        scratch_shapes=[pltpu.VMEM((DN_TILE + 2 * DN_HALO, 3 * BR), F32),
                        pltpu.VMEM((DN_TILE, 3 * BR), F32),
                        pltpu.VMEM((DN_TILE, 4 * BR), F32)],
        compiler_params=_cparams("parallel"),
        name="dn_local",
    )(qkv, qkv, qkv, ba, conv_w, gate_p, _gate_expand())
    return outs[0:6], outs[6:12]


def _dn_advance(chains, s, bmask):
    ncb = DN_TILE // CHUNK
    bmm = lambda a, b: jnp.einsum('bij,bjk->bik', a, b, preferred_element_type=F32)
    steps = []
    for t in range(ncb):
        cs = [ncb - 1 - t if rev else t for _, rev in chains]
        w, u, qt, kt, aqk, eg = [jnp.stack([load(k, c) for (load, _), c in zip(chains, cs)]) for k in range(6)]
        sb = s.astype(BF16)
        v_new = u - bmm(w, sb)
        vb = v_new.astype(BF16)
        v_bd = jnp.where(bmask, jnp.concatenate([vb] * DN_HEADS, axis=1), jnp.zeros((), BF16))
        o = bmm(qt, sb) + bmm(aqk, v_bd)
        upd = jnp.einsum('btk,btv->bkv', kt, vb, preferred_element_type=F32)
        s = s * eg + jnp.where(bmask, upd, 0.0)
        steps.append((cs, o))
    return s, steps


assert BATCH == DEC_SEQ // DN_TILE


def _dn_seq_body(*refs):
    f_in, b_in = refs[0:6], refs[6:12]
    s0_ref, of_ref, ob_ref, finf_ref, finb_ref, s_ref = refs[12:18]

    @pl.when(pl.program_id(0) == 0)
    def _():
        s_ref[...] = s0_ref[...]

    def loader(in_refs, q):
        def load(k, c):
            if k == 5:
                return in_refs[5][q, 0, c:c + 1, :]
            return in_refs[k][q, c * CHUNK:(c + 1) * CHUNK, :]
        return load

    groups = [1, 2, 1, 2, 0, 0]
    outs = [of_ref, of_ref, ob_ref, ob_ref, of_ref, ob_ref]
    chains = [(loader(b_in if o is ob_ref else f_in, q), o is ob_ref) for q, o in zip(groups, outs)]
    n_lat = 2 * DEC_BATCH
    s = jnp.concatenate([s_ref[...], jnp.zeros((2, BR, BR), F32)], axis=0)
    s, steps = _dn_advance(chains, s, _head_block_mask())
    s_ref[...] = s[0:n_lat]
    heads = lambda m: jnp.stack([m[h * DN_HD:(h + 1) * DN_HD, h * DN_HD:(h + 1) * DN_HD] for h in range(DN_HEADS)])
    finf_ref[0] = heads(s[n_lat])
    finb_ref[0] = heads(s[n_lat + 1])
    for cs, o in steps:
        for i, c in enumerate(cs):
            outs[i][groups[i], c * CHUNK:(c + 1) * CHUNK, :] = o[i]


def dn_seq(loc_f, loc_b, s0, l):
    nseg = DEC_SEQ // DN_TILE
    grp = lambda a: a.reshape((N_COND, nseg) + a.shape[1:])

    def specs(m):
        return ([BS((N_COND, DN_TILE, BR), lambda g: (0, m(g), 0))] * len(_DN_LOCAL_DTYPES)
                + [BS((N_COND, 1, 8, BR), lambda g: (0, m(g), 0, 0))])

    fwd = lambda g: g
    bwd = lambda g: nseg - 1 - g
    views = lambda loc: [a.reshape(N_COND, DEC_SEQ, BR) for a in loc[0:5]] + [grp(loc[5])]
    out = SDS((N_COND, DEC_SEQ, BR), F32)
    fin = SDS((BATCH, DN_HEADS, DN_HD, DN_HD), F32)
    o_f, o_b, fin_f, fin_b = pl.pallas_call(
        _dn_seq_body,
        out_shape=[out, out, fin, fin],
        grid=(nseg,),
        in_specs=specs(fwd) + specs(bwd) + [_layer(l, 2 * DEC_BATCH, BR, BR)],
        out_specs=[BS((N_COND, DN_TILE, BR), lambda g: (0, fwd(g), 0)),
                   BS((N_COND, DN_TILE, BR), lambda g: (0, bwd(g), 0)),
                   BS((1, DN_HEADS, DN_HD, DN_HD), lambda g: (fwd(g), 0, 0, 0)),
                   BS((1, DN_HEADS, DN_HD, DN_HD), lambda g: (bwd(g), 0, 0, 0))],
        scratch_shapes=[pltpu.VMEM((2 * DEC_BATCH, BR, BR), F32)],
        compiler_params=_cparams("arbitrary"),
        name="dn_seq",
    )(*views(loc_f), *views(loc_b), s0)
    return o_f.reshape(N_TOK, BR), o_b.reshape(N_TOK, BR), fin_f, fin_b


def dn_branch(qkv, ba, conv_w, gate_p, s0, l):
    loc_f, loc_b = dn_local(qkv, ba, conv_w, gate_p, l)
    return dn_seq(loc_f, loc_b, s0, l)


def _outproj_body(x_ref, mod_ref, ypc_ref, ypl_ref, dof_ref, dob_ref, dnz_ref, s5y_ref, s5_ref,
                  yfc_ref, yfl_ref, d_ref, gw_ref, gb_ref, dng_ref, w_ref, fg_ref, o_ref, *, final, tile0):
    is_ctx = pl.program_id(0) + tile0 < N_CTX // TM
    y_pool = jnp.where(is_ctx, ypc_ref[...], ypl_ref[...])
    y_ft = jnp.where(is_ctx, yfc_ref[...], yfl_ref[...])
    gate = mod_ref[0][:, 2 * D_MODEL:3 * D_MODEL]
    o = dof_ref[...] + dob_ref[...]
    head_mean = jnp.where(_head_block_mask(), 1.0 / DN_HD, 0.0)
    y_dn = o * lax.rsqrt(dot3(o * o, head_mean, 'b') + EPS) * dng_ref[...] * silu(dnz_ref[...])
    y = s5y_ref[...] + d_ref[...] * s5_ref[:, 0:BR]
    y = jax.nn.gelu(y)
    y = y * jax.nn.sigmoid(bdot(y, gw_ref[...]) + gb_ref[...])
    y_s5 = y * silu(s5_ref[:, BR:2 * BR])
    acc = bdot(y_pool, w_ref[0:BR, :])
    acc = acc + bdot(y_dn, w_ref[BR:2 * BR, :])
    acc = acc + bdot(y_s5, w_ref[2 * BR:3 * BR, :])
    acc = acc + bdot(y_ft, w_ref[3 * BR:4 * BR, :])
    xn = x_ref[...] + gate * acc
    if final:
        xn = xn * lax.rsqrt(jnp.mean(xn * xn, axis=-1, keepdims=True) + EPS) * fg_ref[...]
    o_ref[...] = xn


def outproj(x, mod, yp_ctx, yp_lat, dn_of, dn_ob, dn_z, s5_y, s5, yf_ctx, yf_lat, s5_d, glu_w, glu_b, dn_g, w_out,
            final_g, l, final, tile0=0, ntiles=N_TOK // TM):
    row = lambda w: BS((TM, w), lambda i: (i + tile0, 0))
    full = lambda a, b: BS((a, b), lambda i: (0, 0))
    tiles_ctx = N_CTX // TM
    ctx_row = BS((TM, BR), lambda i: (jnp.minimum(i + tile0, tiles_ctx - 1), 0))
    lat_row = BS((TM, BR), lambda i: (jnp.maximum(i + tile0 - tiles_ctx, 0), 0))
    return pl.pallas_call(
        functools.partial(_outproj_body, final=final, tile0=tile0),
        out_shape=SDS((ntiles * TM, D_MODEL), F32),
        grid=(ntiles,),
        in_specs=[row(D_MODEL),
                  BS((1, 1, 3 * D_MODEL), lambda i: (8 * l + _cond_index(i + tile0), 0, 0)),
                  ctx_row, lat_row, row(BR), row(BR), row(BR), row(BR), row(2 * BR), ctx_row, lat_row,
                  _layer(l, 1, BR), _layer(l, BR, BR), _layer(l, 1, BR), _layer(l, 1, BR),
                  _layer(l, D_MODEL, D_MODEL), full(1, D_MODEL)],
        out_specs=BS((TM, D_MODEL), lambda i: (i, 0)),
        compiler_params=_cparams("parallel"),
        name="outproj",
    )(x, mod, yp_ctx, yp_lat, dn_of, dn_ob, dn_z, s5_y, s5, yf_ctx, yf_lat, s5_d, glu_w, glu_b, dn_g, w_out, final_g)


def _permute_w_in(w_in):
    main = jnp.concatenate([w_in[..., 0:1536], w_in[..., 1552:2576]], axis=-1)
    ba = jnp.pad(w_in[..., 1536:1552], ((0, 0), (0, 0), (0, 112)))
    return jnp.concatenate([main, ba], axis=-1).astype(BF16)


def kernel(x_prompt, x_sample, c, state_delta, state_s5, c_ctx, w_ada, b_ada, norm_g, w_in, pool_w, pool_scale,
           dn_conv, dn_a_log, dn_dt_bias, dn_norm_g, s5_a_re, s5_a_im, s5_log_dt, s5_b_re, s5_b_im, s5_c_re,
           s5_c_im, s5_d, s5_glu_w, s5_glu_b, ft_w, w_out, final_g):
    x = jnp.concatenate([x_prompt.astype(F32).reshape(N_CTX, D_MODEL),
                         x_sample.astype(F32).reshape(N_LAT, D_MODEL)], axis=0)
    cond8 = jnp.concatenate([c_ctx.astype(F32)[None], c.astype(F32),
                             jnp.zeros((8 - N_COND, D_MODEL), F32)], axis=0)
    ada = ada_all(cond8, w_ada, b_ada)
    pm_ctx, inv_ctx, pm_lat, inv_lat = _pool_constants()
    fpos, fch, g1, h2, fch2 = _ft_constants()
    s5_mt, s5_q4, s5_rt, s5_al = s5_prep(s5_a_re, s5_a_im, s5_log_dt, s5_b_re, s5_b_im, s5_c_re, s5_c_im)
    mod = ada.reshape(DEPTH * 8, 1, 3 * D_MODEL)
    norm_g3 = norm_g.reshape(DEPTH, 1, D_MODEL)
    w_in_p = _permute_w_in(w_in)
    eye_g = jnp.eye(len(POOL_WINDOWS), dtype=F32)
    w_bd = jnp.einsum('lgcd,gh->lgchd', pool_w, eye_g).reshape(DEPTH, BR, BR).astype(BF16)
    sc = pool_scale.reshape(DEPTH, 1, BR)
    ftw = ft_w.astype(BF16)
    conv_w = jnp.pad(dn_conv, ((0, 0), (0, 8 - CONV_K), (0, 0)))
    gate_p = jnp.zeros((DEPTH, 8, 128), F32)
    gate_p = gate_p.at[:, 0, 8:16].set(dn_a_log.reshape(DEPTH, 8)).at[:, 1, 8:16].set(dn_dt_bias.reshape(DEPTH, 8))
    out_params = (s5_d.reshape(DEPTH, 1, BR), s5_glu_w.astype(BF16), s5_glu_b.reshape(DEPTH, 1, BR),
                  jnp.tile(dn_norm_g, (1, DN_HEADS)).reshape(DEPTH, 1, BR), w_out.astype(BF16),
                  final_g.reshape(1, D_MODEL))
    s5_s0 = state_s5.astype(F32).transpose(1, 4, 3, 0, 2, 5).reshape(DEPTH, S5_LT, DEC_BATCH, 2 * S5_N)
    eye_h = jnp.eye(DN_HEADS, dtype=F32)
    dn_s0 = jnp.einsum('bldhkv,hg->ldbhkgv', state_delta.astype(F32), eye_h).reshape(DEPTH, 2 * DEC_BATCH, BR, BR)
    new_dn, new_s5 = [], []
    for l in range(DEPTH):
        pool, qkv, dn_z, s5, ft, ba = inproj(x, mod, norm_g3, w_in_p, l)

        yp_ctx = pool_branch(pool, pm_ctx, inv_ctx, w_bd, sc, False, l)
        yp_lat = pool_branch(pool, pm_lat, inv_lat, w_bd, sc, True, l)

        yf_ctx = ft_ctx(ft, fpos, fch, ftw, l)
        yf_lat = ft_lat(ft, g1, h2, fch2, ftw, l)

        s5_y, fin_s5 = s5_branch(s5, s5_mt, s5_q4, s5_rt, s5_al, s5_s0, l)
        new_s5.append(fin_s5)

        dn_of, dn_ob, fin_f, fin_b = dn_branch(qkv, ba, conv_w, gate_p, dn_s0, l)
        new_dn.append(jnp.stack([fin_f, fin_b]))

        finish = functools.partial(outproj, x, mod, yp_ctx, yp_lat, dn_of, dn_ob, dn_z, s5_y, s5, yf_ctx, yf_lat,
                                   *out_params, l)
        if l < DEPTH - 1:
            x = finish(False)
        else:
            tiles_ctx = N_CTX // TM
            y_ctx = finish(True, 0, tiles_ctx)
            y_lat = finish(True, tiles_ctx, N_LAT // TM)

    y_prompt = y_ctx.reshape(BATCH, SEQ, D_MODEL).astype(x_prompt.dtype)
    y_sample = y_lat.reshape(DEC_BATCH, DEC_SEQ, D_MODEL).astype(x_sample.dtype)
    new_state_delta = jnp.stack(new_dn).transpose(2, 0, 1, 3, 4, 5).astype(state_delta.dtype)
    fin = jnp.stack(new_s5).reshape(DEPTH, S5_G, 2, BATCH, 2, S5_N)
    new_state_s5 = fin.transpose(3, 0, 4, 2, 1, 5).astype(state_s5.dtype)
    return (y_prompt, y_sample, new_state_delta, new_state_s5)
```

```python
import functools
import math

import numpy as np
import jax
import jax.numpy as jnp
from jax import lax
from jax.experimental import pallas as pl
from jax.experimental.pallas import tpu as pltpu

F32 = jnp.float32
BF16 = jnp.bfloat16

D_MODEL = 1024
BATCH = 16
SEQ = 256
DEPTH = 4
DEC_BATCH = 2
DEC_SEQ = 4096
GRID_W = 64
GRID_H = DEC_SEQ // GRID_W
BR = 256
POOL_WINDOWS = (2, 4, 8, 16)
POOL_GD = 64
DN_HEADS = 4
DN_HD = 64
CONV_K = 5
CHUNK = 64
S5_P = 16
S5_G = 16
S5_N = 64
S5_L = 16
FT_HD = 64
EPS = 1e-6

N_CTX = BATCH * SEQ
N_LAT = DEC_BATCH * DEC_SEQ
N_TOK = N_CTX + N_LAT
N_COND = 1 + DEC_BATCH
TM = 512
W_IN_COLS = 2688
VMEM_LIMIT = 56 * 1024 * 1024

SDS = jax.ShapeDtypeStruct
BS = pl.BlockSpec


def _cparams(*sem):
    return pltpu.CompilerParams(dimension_semantics=sem, vmem_limit_bytes=VMEM_LIMIT)


def bdot(a, b):
    return jnp.dot(a.astype(BF16), b.astype(BF16), preferred_element_type=F32)


def hdot(a, b):
    return jnp.dot(a, b, preferred_element_type=F32, precision=lax.Precision.HIGHEST)


def silu(x):
    return x * jax.nn.sigmoid(x)


def _cond_index(i):
    tiles_ctx = N_CTX // TM
    tiles_seq = DEC_SEQ // TM
    return jnp.where(i < tiles_ctx, 0, 1 + (i - tiles_ctx) // tiles_seq)


def _ada_body(c_ref, w_ref, b_ref, o_ref):
    o_ref[0] = hdot(silu(c_ref[...]), w_ref[0]) + b_ref[0]


def ada_all(cond8, w_ada, b_ada):
    tn = 512
    return pl.pallas_call(
        _ada_body,
        out_shape=SDS((DEPTH, 8, 3 * D_MODEL), F32),
        grid=(DEPTH, 3 * D_MODEL // tn),
        in_specs=[BS((8, D_MODEL), lambda l, j: (0, 0)),
                  BS((1, D_MODEL, tn), lambda l, j: (l, 0, j)),
                  BS((1, 1, tn), lambda l, j: (l, 0, j))],
        out_specs=BS((1, 8, tn), lambda l, j: (l, 0, j)),
        compiler_params=_cparams("parallel", "parallel"),
        name="ada",
    )(cond8, w_ada, b_ada.reshape(DEPTH, 1, 3 * D_MODEL))


def _inproj_body(x_ref, mod_ref, g_ref, w_ref, pool_ref, qkv_ref, s5_ref, ft_ref, z_ref, ba_ref):
    x = x_ref[...]
    m = mod_ref[0]
    shift = m[:, 0:D_MODEL]
    scale = m[:, D_MODEL:2 * D_MODEL]
    xn = x * lax.rsqrt(jnp.mean(x * x, axis=-1, keepdims=True) + EPS) * g_ref[...]
    h = (xn * (1.0 + scale) + shift).astype(BF16)

    def proj(lo, hi):
        return jnp.dot(h, w_ref[:, lo:hi], preferred_element_type=F32)

    pool_ref[...] = proj(0, 256)
    qkv_ref[...] = proj(256, 1024)
    s5_ref[...] = proj(1024, 1280)
    ft_ref[...] = proj(1280, 1536).astype(BF16)
    z_ref[...] = proj(1536, 2560).astype(BF16)
    ba_ref[...] = proj(2560, 2688)


def _layer(l, *block):
    zeros = (0,) * len(block)
    return BS((None,) + block, lambda *_: (l,) + zeros)


def inproj(x, mod, norm_g, w_in_p, l):
    widths = (256, 768, 256, 256, 1024, 128)
    dtypes = (F32, F32, F32, BF16, BF16, F32)
    return pl.pallas_call(
        _inproj_body,
        out_shape=[SDS((N_TOK, w), dt) for w, dt in zip(widths, dtypes)],
        grid=(N_TOK // TM,),
        in_specs=[BS((TM, D_MODEL), lambda i: (i, 0)),
                  BS((1, 1, 3 * D_MODEL), lambda i: (8 * l + _cond_index(i), 0, 0)),
                  _layer(l, 1, D_MODEL),
                  _layer(l, D_MODEL, W_IN_COLS)],
        out_specs=[BS((TM, w), lambda i: (i, 0)) for w in widths],
        compiler_params=_cparams("parallel"),
        name="inproj",
    )(x, mod, norm_g, w_in_p)


def _pool_body(u_ref, z_ref, pm_ref, inv_ref, w_ref, sc_ref, o_ref, *scratch, two_d):
    nblk = u_ref.shape[0] // 256
    if two_d:
        pad_ref, v_ref = scratch
        halo = 8 * GRID_W
        pad_ref[0:halo, :] = jnp.zeros((halo, BR), F32)
        pad_ref[halo + DEC_SEQ:2 * halo + DEC_SEQ, :] = jnp.zeros((halo, BR), F32)
        pad_ref[halo:halo + DEC_SEQ, :] = u_ref[...]
        lane = lax.broadcasted_iota(jnp.int32, (GRID_W, 128), 1)

        def row_body(r, c):
            base = pl.multiple_of(r * GRID_W, GRID_W)

            def slab(d, lo):
                return pad_ref[pl.ds(base + (8 + d) * GRID_W, GRID_W), lo:lo + 128]

            s2 = slab(-1, 0) + slab(0, 0)
            s4 = s2 + slab(-2, 0) + slab(1, 0)
            v_ref[pl.ds(base, GRID_W), 0:128] = jnp.where(lane < 64, s2, s4)
            s8 = slab(-4, 128)
            for d in (-3, -2, -1, 0, 1, 2, 3):
                s8 = s8 + slab(d, 128)
            s16 = s8
            for d in (-8, -7, -6, -5, 4, 5, 6, 7):
                s16 = s16 + slab(d, 128)
            v_ref[pl.ds(base, GRID_W), 128:256] = jnp.where(lane < 64, s8, s16)
            return c

        lax.fori_loop(0, GRID_H, row_body, 0)
        src = v_ref
    else:
        src = u_ref
    grp = lax.broadcasted_iota(jnp.int32, (256, BR), 1) // POOL_GD

    def blk_body(b, c):
        r0 = pl.multiple_of(b * 256, 256)
        vb = src[pl.ds(r0, 256), :]
        hi = vb.astype(BF16)
        lo = (vb - hi.astype(F32)).astype(BF16)
        res = jnp.zeros((256, BR), F32)
        for g in range(len(POOL_WINDOWS)):
            pg = (jnp.dot(pm_ref[g], hi, preferred_element_type=F32)
                  + jnp.dot(pm_ref[g], lo, preferred_element_type=F32))
            res = jnp.where(grp == g, pg, res)
        pooled = res * inv_ref[pl.ds(r0, 256), :]
        d = pooled - u_ref[pl.ds(r0, 256), :]
        y = bdot(d, w_ref[...]) * sc_ref[...]
        o_ref[pl.ds(r0, 256), :] = (y * silu(z_ref[pl.ds(r0, 256), :].astype(F32))).astype(BF16)
        return c

    lax.fori_loop(0, nblk, blk_body, 0)


def _band_matrices(seg):
    t = np.arange(256)
    out = []
    for w in POOL_WINDOWS:
        lo = t - w // 2
        hi = t - w // 2 + w
        s = t[None, :]
        m = (s >= lo[:, None]) & (s < hi[:, None]) & ((s // seg) == (t[:, None] // seg))
        out.append(m.astype(np.float32))
    return np.stack(out)


def _counts(length, w):
    pos = np.arange(length)
    return (np.clip(pos - w // 2 + w, 0, length) - np.clip(pos - w // 2, 0, length)).astype(np.float64)


def _pool_constants():
    inv_ctx = np.concatenate([np.repeat((1.0 / _counts(SEQ, w))[:, None], POOL_GD, 1) for w in POOL_WINDOWS], 1)
    inv_lat = []
    for w in POOL_WINDOWS:
        c2 = np.outer(_counts(GRID_H, w), _counts(GRID_W, w)).reshape(DEC_SEQ)
        inv_lat.append(np.repeat((1.0 / c2)[:, None], POOL_GD, 1))
    inv_lat = np.concatenate(inv_lat, 1)
    return (jnp.asarray(_band_matrices(SEQ), BF16), jnp.asarray(inv_ctx, F32),
            jnp.asarray(_band_matrices(GRID_W), BF16), jnp.asarray(inv_lat, F32))


def pool_branch(pool_u, gates, pm, inv, w_bd, scale, two_d, l):
    if two_d:
        rows, nseq, blk0 = DEC_SEQ, DEC_BATCH, N_CTX // DEC_SEQ
        scratch = [pltpu.VMEM((DEC_SEQ + 16 * GRID_W, BR), F32), pltpu.VMEM((DEC_SEQ, BR), F32)]
    else:
        rows, nseq, blk0 = SEQ, BATCH, 0
        scratch = []
    return pl.pallas_call(
        functools.partial(_pool_body, two_d=two_d),
        out_shape=SDS((nseq * rows, BR), BF16),
        grid=(nseq,),
        in_specs=[BS((rows, BR), lambda i: (blk0 + i, 0)),
                  BS((rows, BR), lambda i: (blk0 + i, 0)),
                  BS((4, 256, 256), lambda i: (0, 0, 0)),
                  BS((rows, BR), lambda i: (0, 0)),
                  _layer(l, BR, BR),
                  _layer(l, 1, BR)],
        out_specs=BS((rows, BR), lambda i: (i, 0)),
        scratch_shapes=scratch,
        compiler_params=_cparams("parallel"),
        name="pool2d" if two_d else "pool1d",
    )(pool_u, gates, pm, inv, w_bd, scale)


def _ft_ctx_body(u_ref, z_ref, fpos_ref, fch_ref, w_ref, o_ref):
    uc = bdot(u_ref[...], fch_ref[...])
    st = jnp.concatenate([uc[:, 0:BR], uc[:, BR:2 * BR]], axis=0)
    f = bdot(fpos_ref[...], st)
    o_ref[...] = (bdot(f, w_ref[...]) * silu(z_ref[...].astype(F32))).astype(BF16)


def ft_ctx(ft_u, gates, fpos, fch, ft_w, l):
    return pl.pallas_call(
        _ft_ctx_body,
        out_shape=SDS((N_CTX, BR), BF16),
        grid=(BATCH,),
        in_specs=[BS((SEQ, BR), lambda i: (i, 0)),
                  BS((SEQ, BR), lambda i: (i, 3)),
                  BS((SEQ, 2 * SEQ), lambda i: (0, 0)),
                  BS((BR, 2 * BR), lambda i: (0, 0)),
                  _layer(l, BR, BR)],
        out_specs=BS((SEQ, BR), lambda i: (i, 0)),
        compiler_params=_cparams("parallel"),
        name="ft_ctx",
    )(ft_u, gates, fpos, fch, ft_w)


def _ft_lat_body(u_ref, z_ref, g_ref, h_ref, fch_ref, w_ref, o_ref, x_ref, yr_ref, yi_ref):
    for hf in range(2):
        x_ref[hf] = u_ref[:, hf * 128:(hf + 1) * 128].astype(F32)

    def stage1(t2, c):
        xs = jnp.concatenate([x_ref[hf, pl.ds(t2, GRID_H, stride=GRID_W), :] for hf in range(2)], axis=1)
        y = jnp.dot(g_ref[t2], xs.astype(BF16), preferred_element_type=F32)
        r0 = pl.multiple_of(t2 * GRID_W, GRID_W)
        for hf in range(2):
            yr_ref[hf, pl.ds(r0, GRID_W), :] = y[0:64, hf * 128:(hf + 1) * 128]
            yi_ref[hf, pl.ds(r0, GRID_W), :] = y[64:128, hf * 128:(hf + 1) * 128]
        return c

    lax.fori_loop(0, GRID_W, stage1, 0, unroll=8)

    def stage2(kb, c):
        yr = jnp.concatenate([yr_ref[hf, pl.ds(kb, GRID_W, stride=GRID_W), :] for hf in range(2)], axis=1)
        yi = jnp.concatenate([yi_ref[hf, pl.ds(kb, GRID_W, stride=GRID_W), :] for hf in range(2)], axis=1)
        st = jnp.concatenate([yr, yi], axis=0).astype(BF16)
        a = jnp.dot(h_ref[...], st, preferred_element_type=F32)
        for hf in range(2):
            yr_ref[hf, pl.ds(kb, GRID_W, stride=GRID_W), :] = a[0:64, hf * 128:(hf + 1) * 128]
            yi_ref[hf, pl.ds(kb, GRID_W, stride=GRID_W), :] = a[64:128, hf * 128:(hf + 1) * 128]
        return c

    lax.fori_loop(0, GRID_W, stage2, 0, unroll=8)

    def stage3(b, c):
        r0 = pl.multiple_of(b * TM, TM)
        ar = jnp.concatenate([yr_ref[hf, pl.ds(r0, TM), :] for hf in range(2)], axis=1)
        ai = jnp.concatenate([yi_ref[hf, pl.ds(r0, TM), :] for hf in range(2)], axis=1)
        f = bdot(ar, fch_ref[0:BR, :]) + bdot(ai, fch_ref[BR:2 * BR, :])
        o_ref[pl.ds(r0, TM), :] = (bdot(f, w_ref[...]) * silu(z_ref[pl.ds(r0, TM), :].astype(F32))).astype(BF16)
        return c

    lax.fori_loop(0, DEC_SEQ // TM, stage3, 0)


def ft_lat(ft_u, gates, g1, h2, fch2, ft_w, l):
    blk0 = N_CTX // DEC_SEQ
    return pl.pallas_call(
        _ft_lat_body,
        out_shape=SDS((N_LAT, BR), BF16),
        grid=(DEC_BATCH,),
        in_specs=[BS((DEC_SEQ, BR), lambda i: (blk0 + i, 0)),
                  BS((DEC_SEQ, BR), lambda i: (blk0 + i, 3)),
                  BS((GRID_W, 128, GRID_H), lambda i: (0, 0, 0)),
                  BS((128, 128), lambda i: (0, 0)),
                  BS((2 * BR, BR), lambda i: (0, 0)),
                  _layer(l, BR, BR)],
        out_specs=BS((DEC_SEQ, BR), lambda i: (i, 0)),
        scratch_shapes=[pltpu.VMEM((2, DEC_SEQ, 128), F32)] * 3,
        compiler_params=_cparams("parallel"),
        name="ft_lat",
    )(ft_u, gates, g1, h2, fch2, ft_w)


def _ft_constants():
    c = np.arange(FT_HD)
    ang = 2.0 * np.pi * np.outer(c, c) / FT_HD
    eye4 = np.eye(BR // FT_HD)
    cc = np.kron(eye4, np.cos(ang)) / 8.0
    sc = np.kron(eye4, np.sin(ang)) / 8.0
    t = np.arange(SEQ)
    angt = 2.0 * np.pi * (np.outer(t, t) % SEQ) / SEQ
    fpos = np.concatenate([np.cos(angt), -np.sin(angt)], axis=1) / 16.0
    fch = np.concatenate([cc, sc], axis=1)
    kb = np.arange(GRID_W)[None, :, None]
    t1 = np.arange(GRID_H)[None, None, :]
    t2 = np.arange(GRID_W)[:, None, None]
    a1 = 2.0 * np.pi * ((kb * (GRID_W * t1 + t2)) % DEC_SEQ) / DEC_SEQ
    g1 = np.concatenate([np.cos(a1), -np.sin(a1)], axis=1) / 8.0
    a2 = 2.0 * np.pi * (np.outer(np.arange(GRID_W), np.arange(GRID_W)) % GRID_W) / GRID_W
    c2, s2 = np.cos(a2) / 8.0, np.sin(a2) / 8.0
    h2 = np.block([[c2, s2], [-s2, c2]])
    fch2 = np.concatenate([cc, sc], axis=0)
    as_bf = lambda a: jnp.asarray(a, F32).astype(BF16)
    return as_bf(fpos), as_bf(fch), as_bf(g1), as_bf(h2), as_bf(fch2)


S5_ROWS = N_TOK // S5_L
S5_TM = 1024
S5_TR = S5_TM // S5_L
S5_CTX_C = SEQ // S5_L
S5_LAT_C = DEC_SEQ // S5_L
S5_LT = 2 * S5_G


S5_PG = 4


def _s5_prep_body(*refs):
    for gi in range(S5_PG):
        _s5_prep_group(gi, *refs)


def _s5_prep_group(gi, ar_ref, ai_ref, ldt_ref, br_ref, bi_ref, btr_ref, bti_ref, cr_ref, ci_ref,
                   mt_ref, q4_ref, rt_ref, al_ref):
    L = S5_L
    m = lax.broadcasted_iota(jnp.int32, (2 * L, S5_N), 0).astype(F32)
    rts, qs, ds, als = [], [], [], []
    for d in range(2):
        a_re, a_im = ar_ref[0, d, gi], ai_ref[0, d, gi]
        dt = jnp.exp(ldt_ref[0, d, gi])
        xr, xi = a_re * dt, a_im * dt
        mag = jnp.exp(xr)
        ab_re, ab_im = mag * jnp.cos(xi), mag * jnp.sin(xi)
        den = a_re * a_re + a_im * a_im
        nr = ab_re - 1.0
        coef_re = (nr * a_re + ab_im * a_im) / den
        coef_im = (ab_im * a_re - nr * a_im) / den
        pw_re = jnp.exp(m * xr) * jnp.cos(m * xi)
        pw_im = jnp.exp(m * xr) * jnp.sin(m * xi)
        cq_re = pw_re * coef_re - pw_im * coef_im
        cq_im = pw_re * coef_im + pw_im * coef_re
        c_re, c_im = cr_ref[0, d, gi], ci_ref[0, d, gi]
        bt_re, bt_im = btr_ref[0, d, gi], bti_ref[0, d, gi]
        row = lambda x, e: x[e:e + 1, :]
        order = range(L) if d == 0 else range(L - 1, -1, -1)
        cp_re = jnp.concatenate([c_re * row(cq_re, e) - c_im * row(cq_im, e) for e in order], axis=0)
        cp_im = jnp.concatenate([c_re * row(cq_im, e) + c_im * row(cq_re, e) for e in order], axis=0)
        rts.append(hdot(cp_re, br_ref[0, d, gi]) - hdot(cp_im, bi_ref[0, d, gi]))
        inj = [L - 1 - i for i in range(L)] if d == 0 else list(range(L))
        q_re = jnp.concatenate([bt_re * row(cq_re, e) - bt_im * row(cq_im, e) for e in inj], axis=0)
        q_im = jnp.concatenate([bt_im * row(cq_re, e) + bt_re * row(cq_im, e) for e in inj], axis=0)
        qs.append((q_re, q_im))
        out = [j + 1 for j in range(L)] if d == 0 else [L - j for j in range(L)]
        d_re = jnp.concatenate([c_re * row(pw_re, e) - c_im * row(pw_im, e) for e in out], axis=0)
        d_im = jnp.concatenate([c_re * row(pw_im, e) + c_im * row(pw_re, e) for e in out], axis=0)
        ds.append((d_re, d_im))
        als.append((row(pw_re, L), row(pw_im, L)))
    pad = jnp.zeros(((L - 1) * S5_P, S5_P), F32)
    z = jnp.concatenate([pad, rts[0]], axis=0) + jnp.concatenate([rts[1], pad], axis=0)
    mt = jnp.concatenate([z[(L - 1 - i) * S5_P:(L - 1 - i) * S5_P + L * S5_P, :] for i in range(L)], axis=1)
    mt_ref[0, gi] = mt.astype(BF16)
    q4_ref[0, gi] = jnp.concatenate([qs[0][0], qs[1][0], qs[0][1], qs[1][1]], axis=1).astype(BF16)
    rt_ref[0, gi] = jnp.concatenate([ds[0][0], ds[1][0], -ds[0][1], -ds[1][1]], axis=1).astype(BF16)
    al_ref[0, gi] = jnp.concatenate([jnp.concatenate([als[0][0], als[1][0]], axis=1),
                                    jnp.concatenate([als[0][1], als[1][1]], axis=1)], axis=0)


def s5_prep(a_re, a_im, log_dt, b_re, b_im, c_re, c_im):
    vec = lambda x: x.reshape(DEPTH, 2, S5_G, 1, S5_N)
    ldt = jnp.broadcast_to(log_dt[..., None, None], (DEPTH, 2, S5_G, 1, S5_N))
    bt = lambda x: x.transpose(0, 1, 2, 4, 3)
    per = S5_G // S5_PG
    vspec = BS((1, 2, S5_PG, 1, S5_N), lambda i: (i // per, 0, i % per, 0, 0))
    bspec = BS((1, 2, S5_PG, S5_N, S5_P), lambda i: (i // per, 0, i % per, 0, 0))
    cspec = BS((1, 2, S5_PG, S5_P, S5_N), lambda i: (i // per, 0, i % per, 0, 0))
    mat = SDS((DEPTH, S5_G, 256, 256), BF16)
    mspec = BS((1, S5_PG, 256, 256), lambda i: (i // per, i % per, 0, 0))
    return pl.pallas_call(
        _s5_prep_body,
        out_shape=[mat, mat, mat, SDS((DEPTH, S5_G, 2, 128), F32)],
        grid=(DEPTH * per,),
        in_specs=[vspec, vspec, vspec, bspec, bspec, cspec, cspec, cspec, cspec],
        out_specs=[mspec, mspec, mspec, BS((1, S5_PG, 2, 128), lambda i: (i // per, i % per, 0, 0))],
        compiler_params=_cparams("parallel"),
        name="s5_prep",
    )(vec(a_re), vec(a_im), ldt, b_re, b_im, bt(b_re), bt(b_im), c_re, c_im)


def _block_transpose(arrs):
    blk = lax.broadcasted_iota(jnp.int32, arrs[0].shape, 1) // 16
    cur = list(arrs)
    for b in range(4):
        s = 16 << b
        hi = ((blk >> b) & 1) == 1
        nxt = list(cur)
        for x in range(16):
            if (x >> b) & 1:
                continue
            y = x | (1 << b)
            nxt[x] = jnp.where(hi, pltpu.roll(cur[y], s, 1), cur[x])
            nxt[y] = jnp.where(hi, cur[y], pltpu.roll(cur[x], 256 - s, 1))
        cur = nxt
    return cur


def _s5_in_body(s5_ref, mt_ref, q4_ref, y_ref, e_ref, x_ref):
    for hf in range(2):
        x_ref[hf] = s5_ref[:, hf * 128:(hf + 1) * 128]
    xs = [jnp.concatenate([x_ref[hf, pl.ds(i, S5_TR, stride=S5_L), :] for hf in range(2)], axis=1)
          for i in range(S5_L)]
    us = _block_transpose(xs)
    for g in range(S5_G):
        ub = us[g].astype(BF16)
        y_ref[:, g * 256:(g + 1) * 256] = lax.dot_general(ub, mt_ref[g], (((1,), (1,)), ((), ())),
                                                          preferred_element_type=F32)
        e = jnp.dot(ub, q4_ref[g], preferred_element_type=F32)
        e_ref[2 * g] = e[:, 0:128]
        e_ref[2 * g + 1] = e[:, 128:256]


def s5_in(s5, mt, q4, l):
    wspec = _layer(l, S5_G, 256, 256)
    return pl.pallas_call(
        _s5_in_body,
        out_shape=[SDS((S5_ROWS, S5_G * 256), F32), SDS((S5_LT, S5_ROWS, 128), F32)],
        grid=(N_TOK // S5_TM,),
        in_specs=[BS((S5_TM, BR), lambda t: (t, 0)), wspec, wspec],
        out_specs=[BS((S5_TR, S5_G * 256), lambda t: (t, 0)), BS((S5_LT, S5_TR, 128), lambda t: (0, t, 0))],
        scratch_shapes=[pltpu.VMEM((2, S5_TM, 128), F32)],
        compiler_params=_cparams("parallel"),
        name="s5_in",
    )(s5, mt, q4)


def _s5_chunk_scan_body(e_ref, al_ref, s0_ref, spf_ref, spb_ref, fin_ref):
    def update(g, s_re, s_im, e_re, e_im):
        a_re, a_im = al_ref[g, 0:1, :], al_ref[g, 1:2, :]
        return a_re * s_re - a_im * s_im + e_re, a_re * s_im + a_im * s_re + e_im

    fwd_c = lax.broadcasted_iota(jnp.int32, (BATCH, 128), 1) < S5_N

    def ctx_group(g, carry):
        s_re = jnp.zeros((BATCH, 128), F32)
        s_im = jnp.zeros((BATCH, 128), F32)
        for c in range(S5_CTX_C):
            rf = pl.ds(c, BATCH, stride=S5_CTX_C)
            rb = pl.ds(S5_CTX_C - 1 - c, BATCH, stride=S5_CTX_C)
            spf_ref[2 * g, rf, :] = s_re
            spf_ref[2 * g + 1, rf, :] = s_im
            spb_ref[2 * g, rb, :] = s_re
            spb_ref[2 * g + 1, rb, :] = s_im
            e_re = jnp.where(fwd_c, e_ref[2 * g, rf, :], e_ref[2 * g, rb, :])
            e_im = jnp.where(fwd_c, e_ref[2 * g + 1, rf, :], e_ref[2 * g + 1, rb, :])
            s_re, s_im = update(g, s_re, s_im, e_re, e_im)
        fin_ref[2 * g] = s_re
        fin_ref[2 * g + 1] = s_im
        return carry

    lax.fori_loop(0, S5_GQ, ctx_group, 0)

    row0 = BATCH * S5_CTX_C
    fwd_l = lax.broadcasted_iota(jnp.int32, (DEC_BATCH, 128), 1) < S5_N

    def lat_step(c, state):
        rf = pl.ds(row0 + c, DEC_BATCH, stride=S5_LAT_C)
        rb = pl.ds(row0 + S5_LAT_C - 1 - c, DEC_BATCH, stride=S5_LAT_C)
        new = []
        for g in range(S5_GQ):
            s_re, s_im = state[2 * g], state[2 * g + 1]
            spf_ref[2 * g, rf, :] = s_re
            spf_ref[2 * g + 1, rf, :] = s_im
            spb_ref[2 * g, rb, :] = s_re
            spb_ref[2 * g + 1, rb, :] = s_im
            e_re = jnp.where(fwd_l, e_ref[2 * g, rf, :], e_ref[2 * g, rb, :])
            e_im = jnp.where(fwd_l, e_ref[2 * g + 1, rf, :], e_ref[2 * g + 1, rb, :])
            new.extend(update(g, s_re, s_im, e_re, e_im))
        return tuple(new)

    lax.fori_loop(0, S5_LAT_C, lat_step, tuple(s0_ref[t] for t in range(2 * S5_GQ)))


S5_GQ = 4


def s5_chunk_scan(e3, al, s0_lat, l):
    sp = SDS((S5_LT, S5_ROWS, 128), F32)
    tiles = lambda rows: BS((2 * S5_GQ, rows, 128), lambda q: (q, 0, 0))
    return pl.pallas_call(
        _s5_chunk_scan_body,
        out_shape=[sp, sp, SDS((S5_LT, BATCH, 128), F32)],
        grid=(S5_G // S5_GQ,),
        in_specs=[tiles(S5_ROWS), BS((None, S5_GQ, 2, 128), lambda q: (l, q, 0, 0)),
                  BS((None, 2 * S5_GQ, DEC_BATCH, 128), lambda q: (l, q, 0, 0))],
        out_specs=[tiles(S5_ROWS), tiles(S5_ROWS), tiles(BATCH)],
        compiler_params=_cparams("parallel"),
        name="s5_chunk_scan",
    )(e3, al, s0_lat)


def _s5_fin_body(y_ref, spf_ref, spb_ref, rt_ref, o_ref, t_ref):
    fwd = lax.broadcasted_iota(jnp.int32, (S5_TR, 128), 1) < S5_N
    ys = []
    for g in range(S5_G):
        s_re = jnp.where(fwd, spf_ref[2 * g], spb_ref[2 * g])
        s_im = jnp.where(fwd, spf_ref[2 * g + 1], spb_ref[2 * g + 1])
        sp = jnp.concatenate([s_re, s_im], axis=1).astype(BF16)
        ys.append(y_ref[:, g * 256:(g + 1) * 256]
                  + lax.dot_general(sp, rt_ref[g], (((1,), (1,)), ((), ())), preferred_element_type=F32))
    xs = _block_transpose(ys)
    for j in range(S5_L):
        for hf in range(2):
            t_ref[hf, pl.ds(j, S5_TR, stride=S5_L), :] = xs[j][:, hf * 128:(hf + 1) * 128]
    o_ref[...] = jnp.concatenate([t_ref[0], t_ref[1]], axis=1)


def s5_fin(yi, spf, spb, rt, l):
    lt = BS((S5_LT, S5_TR, 128), lambda t: (0, t, 0))
    return pl.pallas_call(
        _s5_fin_body,
        out_shape=SDS((N_TOK, BR), F32),
        grid=(N_TOK // S5_TM,),
        in_specs=[BS((S5_TR, S5_G * 256), lambda t: (t, 0)), lt, lt, _layer(l, S5_G, 256, 256)],
        out_specs=BS((S5_TM, BR), lambda t: (t, 0)),
        scratch_shapes=[pltpu.VMEM((2, S5_TM, 128), F32)],
        compiler_params=_cparams("parallel"),
        name="s5_fin",
    )(yi, spf, spb, rt)


def s5_branch(s5, mt, q4, rt, al, s0, l):
    yi, e3 = s5_in(s5, mt, q4, l)
    spf, spb, fin = s5_chunk_scan(e3, al, s0, l)
    return s5_fin(yi, spf, spb, rt, l), fin


DN_TILE = 256
DN_HALO = 8


N_SEG = N_TOK // DN_TILE
assert N_CTX == DEC_SEQ


def _split3(x):
    x1 = x.astype(BF16)
    r1 = x - x1.astype(F32)
    x2 = r1.astype(BF16)
    return x1, x2, (r1 - x2.astype(F32)).astype(BF16)


def dot3(a, b, exact):
    if exact == 'b':
        return sum(jnp.dot(p, b.astype(BF16), preferred_element_type=F32) for p in _split3(a))
    return sum(jnp.dot(a.astype(BF16), p, preferred_element_type=F32) for p in _split3(b))


def _dn_conv_body(x_ref, prev_ref, next_ref, ba_ref, w_ref, gp_ref, ex_ref, o_ref, gb_ref, pad_ref):
    i = pl.program_id(0)
    tiles_ctx = N_CTX // DN_TILE
    tiles_seq = DEC_SEQ // DN_TILE
    j = (i - tiles_ctx) % tiles_seq
    first = jnp.logical_or(i < tiles_ctx, j == 0)
    last = jnp.logical_or(i < tiles_ctx, j == tiles_seq - 1)
    pad_ref[0:DN_HALO, :] = jnp.where(first, 0.0, prev_ref[...])
    pad_ref[DN_HALO:DN_HALO + DN_TILE, :] = x_ref[...]
    pad_ref[DN_HALO + DN_TILE:2 * DN_HALO + DN_TILE, :] = jnp.where(last, 0.0, next_ref[...])
    for r0 in range(0, DN_TILE, 128):
        for c0 in range(0, 3 * BR, 128):
            acc = jnp.zeros((128, 128), F32)
            for t in range(CONV_K):
                acc = acc + (pad_ref[pl.ds(r0 + DN_HALO - CONV_K // 2 + t, 128), c0:c0 + 128]
                             * w_ref[t:t + 1, c0:c0 + 128])
            o_ref[r0:r0 + 128, c0:c0 + 128] = silu(acc)
    raw = ba_ref[...]
    lane = lax.broadcasted_iota(jnp.int32, raw.shape, 1)
    xa = raw + gp_ref[1:2, :]
    sp = jnp.maximum(xa, 0.0) + jnp.log1p(jnp.exp(-jnp.abs(xa)))
    gates = jnp.where(lane < 2 * DN_HEADS, jax.nn.sigmoid(raw), -jnp.exp(gp_ref[0:1, :]) * sp)
    r = lax.broadcasted_iota(jnp.int32, (DN_TILE, DN_TILE), 0)
    c = lax.broadcasted_iota(jnp.int32, (DN_TILE, DN_TILE), 1)
    same = (r // CHUNK) == (c // CHUNK)
    ex = dot3(gates, ex_ref[...], 'b')
    gb_ref[:, 0:2 * BR] = ex[:, 0:2 * BR]
    gb_ref[:, 2 * BR:3 * BR] = dot3(jnp.logical_and(same, c <= r), ex[:, 2 * BR:3 * BR], 'a')
    gb_ref[:, 3 * BR:4 * BR] = dot3(jnp.logical_and(same, c >= r), ex[:, 3 * BR:4 * BR], 'a')


def _gate_expand():
    e = np.zeros((128, 4 * BR), np.float32)
    for blk in range(4):
        for h in range(DN_HEADS):
            e[blk * DN_HEADS + h, blk * BR + h * DN_HD:blk * BR + (h + 1) * DN_HD] = 1.0
    return jnp.asarray(e, BF16)


def _head_block_mask():
    r = lax.broadcasted_iota(jnp.int32, (BR, BR), 0) // DN_HD
    c = lax.broadcasted_iota(jnp.int32, (BR, BR), 1) // DN_HD
    return r == c


def _split2(x):
    hi = x.astype(BF16)
    return hi, (x - hi.astype(F32)).astype(BF16)


def _dn_local_body(qkv_ref, gb_ref, *out_refs):
    f_refs, b_refs = out_refs[0:6], out_refs[6:12]
    ncb = DN_TILE // CHUNK
    nb = 2 * ncb
    bmask = _head_block_mask()
    ri = lax.broadcasted_iota(jnp.int32, (nb, CHUNK, BR), 1)
    cj = lax.broadcasted_iota(jnp.int32, (nb, CHUNK, BR), 2) % DN_HD
    bwd = lax.broadcasted_iota(jnp.int32, (nb, CHUNK, BR), 0) >= ncb
    eye = cj == ri
    incl = jnp.logical_or(jnp.logical_and(bwd, cj >= ri), jnp.logical_and(jnp.logical_not(bwd), cj <= ri))
    strict = jnp.logical_and(incl, jnp.logical_not(eye))
    ones_blk = bmask.astype(BF16)

    def chunks(x):
        return x.reshape(ncb, CHUNK, BR)

    def both(x):
        return jnp.concatenate([x, x], axis=0)

    def head_sum(x):
        return dot3(x.reshape(-1, BR), ones_blk, 'b').reshape(x.shape)

    def bd(x):
        return jnp.where(bmask, jnp.concatenate([x] * DN_HEADS, axis=1), jnp.zeros((), x.dtype))

    def bmm(a, b):
        return jnp.einsum('bij,bjk->bik', a, b, preferred_element_type=F32)

    q = chunks(qkv_ref[:, 0:BR])
    k = chunks(qkv_ref[:, BR:2 * BR])
    v = both(chunks(qkv_ref[:, 2 * BR:3 * BR]))
    q = q * lax.rsqrt(head_sum(q * q) + EPS) * (DN_HD ** -0.5)
    k = k * lax.rsqrt(head_sum(k * k) + EPS)
    kq = jnp.einsum('bik,bjk->bij', jnp.concatenate([k, q], axis=1).astype(BF16), bd(k.astype(BF16)),
                    preferred_element_type=F32)
    kk, qk = both(kq[:, 0:CHUNK]), both(kq[:, CHUNK:2 * CHUNK])
    q, k = both(q), both(k)
    beta = jnp.concatenate([chunks(gb_ref[:, 0:BR]), chunks(gb_ref[:, BR:2 * BR])], axis=0)
    gc = jnp.concatenate([chunks(gb_ref[:, 2 * BR:3 * BR]), chunks(gb_ref[:, 3 * BR:4 * BR])], axis=0)
    crow = jnp.sum(jnp.where(eye, gc, 0.0), axis=1, keepdims=True)
    decay = jnp.where(incl, jnp.exp(jnp.where(incl, gc - crow, 0.0)), 0.0)
    a = jnp.where(strict, kk * decay * beta, 0.0)
    tinv = jnp.where(eye, 1.0, 0.0) - a
    pw = a
    pw_bd = bd(pw.astype(BF16))
    for _ in range(5):
        pw = bmm(pw.astype(BF16), pw_bd)
        pw_bd = bd(pw.astype(BF16))
        tinv = tinv + bmm(tinv.astype(BF16), pw_bd)
    egc = jnp.exp(gc)
    t_hi, t_lo = _split2(tinv)

    def solve(rhs):
        r_hi, r_lo = _split2(rhs)
        r_hi, r_lo = bd(r_hi), bd(r_lo)
        return bmm(t_hi, r_hi) + bmm(t_hi, r_lo) + bmm(t_lo, r_hi)

    w = bmm(t_hi, bd((k * (beta * egc)).astype(BF16)))

    bwd_row = lax.broadcasted_iota(jnp.int32, (nb, 1, BR), 0) >= ncb
    g_last = jnp.where(bwd_row, gc[:, 0:1], gc[:, CHUNK - 1:CHUNK])
    eg = jnp.exp(g_last)
    outs = (w, solve(v * beta), q * egc, k * jnp.exp(g_last - gc),
            jnp.where(incl, qk * decay, 0.0))
    for d, refs in enumerate((f_refs, b_refs)):
        for ref, x in zip(refs[0:5], outs):
            ref[...] = x[d * ncb:(d + 1) * ncb].reshape(DN_TILE, BR).astype(ref.dtype)
        refs[5][0] = jnp.concatenate([eg[d * ncb:(d + 1) * ncb, 0], jnp.zeros((8 - ncb, BR), F32)], axis=0)


_DN_LOCAL_DTYPES = (BF16, F32, BF16, BF16, BF16)


def _dn_prep_body(x_ref, prev_ref, next_ref, ba_ref, w_ref, gp_ref, ex_ref, *rest):
    out_refs, (pad_ref, qkv_ref, gb_ref) = rest[0:12], rest[12:15]
    _dn_conv_body(x_ref, prev_ref, next_ref, ba_ref, w_ref, gp_ref, ex_ref, qkv_ref, gb_ref, pad_ref)
    _dn_local_body(qkv_ref, gb_ref, *out_refs)


def dn_local(qkv, ba, conv_w, gate_p, l):
    per = DN_TILE // DN_HALO
    nhb = N_TOK // DN_HALO
    tok = BS((DN_TILE, BR), lambda i: (i, 0))
    shapes = [SDS((N_TOK, BR), dt) for dt in _DN_LOCAL_DTYPES] + [SDS((N_SEG, 8, BR), F32)]
    specs = [tok] * len(_DN_LOCAL_DTYPES) + [BS((1, 8, BR), lambda i: (i, 0, 0))]
    outs = pl.pallas_call(
        _dn_prep_body,
        out_shape=shapes * 2,
        grid=(N_SEG,),
        in_specs=[BS((DN_TILE, 3 * BR), lambda i: (i, 0)),
                  BS((DN_HALO, 3 * BR), lambda i: (jnp.maximum(i * per - 1, 0), 0)),
                  BS((DN_HALO, 3 * BR), lambda i: (jnp.minimum((i + 1) * per, nhb - 1), 0)),
                  BS((DN_TILE, 128), lambda i: (i, 0)),
                  _layer(l, 8, 3 * BR),
                  _layer(l, 8, 128),
                  BS((128, 4 * BR), lambda i: (0, 0))],
        out_specs=specs * 2,
        scratch_shapes=[pltpu.VMEM((DN_TILE + 2 * DN_HALO, 3 * BR), F32),
                        pltpu.VMEM((DN_TILE, 3 * BR), F32),
                        pltpu.VMEM((DN_TILE, 4 * BR), F32)],
        compiler_params=_cparams("parallel"),
        name="dn_local",
    )(qkv, qkv, qkv, ba, conv_w, gate_p, _gate_expand())
    return outs[0:6], outs[6:12]


def _dn_advance(chains, s, bmask):
    ncb = DN_TILE // CHUNK
    bmm = lambda a, b: jnp.einsum('bij,bjk->bik', a, b, preferred_element_type=F32)
    steps = []
    for t in range(ncb):
        cs = [ncb - 1 - t if rev else t for _, rev in chains]
        w, u, qt, kt, aqk, eg = [jnp.stack([load(k, c) for (load, _), c in zip(chains, cs)]) for k in range(6)]
        sb = s.astype(BF16)
        v_new = u - bmm(w, sb)
        vb = v_new.astype(BF16)
        v_bd = jnp.where(bmask, jnp.concatenate([vb] * DN_HEADS, axis=1), jnp.zeros((), BF16))
        o = bmm(qt, sb) + bmm(aqk, v_bd)
        upd = jnp.einsum('btk,btv->bkv', kt, vb, preferred_element_type=F32)
        s = s * eg + jnp.where(bmask, upd, 0.0)
        steps.append((cs, o))
    return s, steps


assert BATCH == DEC_SEQ // DN_TILE


def _dn_seq_body(*refs):
    f_in, b_in = refs[0:6], refs[6:12]
    s0_ref, of_ref, ob_ref, finf_ref, finb_ref, s_ref = refs[12:18]

    @pl.when(pl.program_id(0) == 0)
    def _():
        s_ref[...] = s0_ref[...]

    def loader(in_refs, q):
        def load(k, c):
            if k == 5:
                return in_refs[5][q, 0, c:c + 1, :]
            return in_refs[k][q, c * CHUNK:(c + 1) * CHUNK, :]
        return load

    groups = [1, 2, 1, 2, 0, 0]
    outs = [of_ref, of_ref, ob_ref, ob_ref, of_ref, ob_ref]
    chains = [(loader(b_in if o is ob_ref else f_in, q), o is ob_ref) for q, o in zip(groups, outs)]
    n_lat = 2 * DEC_BATCH
    s = jnp.concatenate([s_ref[...], jnp.zeros((2, BR, BR), F32)], axis=0)
    s, steps = _dn_advance(chains, s, _head_block_mask())
    s_ref[...] = s[0:n_lat]
    heads = lambda m: jnp.stack([m[h * DN_HD:(h + 1) * DN_HD, h * DN_HD:(h + 1) * DN_HD] for h in range(DN_HEADS)])
    finf_ref[0] = heads(s[n_lat])
    finb_ref[0] = heads(s[n_lat + 1])
    for cs, o in steps:
        for i, c in enumerate(cs):
            outs[i][groups[i], c * CHUNK:(c + 1) * CHUNK, :] = o[i]


def dn_seq(loc_f, loc_b, s0, l):
    nseg = DEC_SEQ // DN_TILE
    grp = lambda a: a.reshape((N_COND, nseg) + a.shape[1:])

    def specs(m):
        return ([BS((N_COND, DN_TILE, BR), lambda g: (0, m(g), 0))] * len(_DN_LOCAL_DTYPES)
                + [BS((N_COND, 1, 8, BR), lambda g: (0, m(g), 0, 0))])

    fwd = lambda g: g
    bwd = lambda g: nseg - 1 - g
    views = lambda loc: [a.reshape(N_COND, DEC_SEQ, BR) for a in loc[0:5]] + [grp(loc[5])]
    out = SDS((N_COND, DEC_SEQ, BR), F32)
    fin = SDS((BATCH, DN_HEADS, DN_HD, DN_HD), F32)
    o_f, o_b, fin_f, fin_b = pl.pallas_call(
        _dn_seq_body,
        out_shape=[out, out, fin, fin],
        grid=(nseg,),
        in_specs=specs(fwd) + specs(bwd) + [_layer(l, 2 * DEC_BATCH, BR, BR)],
        out_specs=[BS((N_COND, DN_TILE, BR), lambda g: (0, fwd(g), 0)),
                   BS((N_COND, DN_TILE, BR), lambda g: (0, bwd(g), 0)),
                   BS((1, DN_HEADS, DN_HD, DN_HD), lambda g: (fwd(g), 0, 0, 0)),
                   BS((1, DN_HEADS, DN_HD, DN_HD), lambda g: (bwd(g), 0, 0, 0))],
        scratch_shapes=[pltpu.VMEM((2 * DEC_BATCH, BR, BR), F32)],
        compiler_params=_cparams("arbitrary"),
        name="dn_seq",
    )(*views(loc_f), *views(loc_b), s0)
    return o_f.reshape(N_TOK, BR), o_b.reshape(N_TOK, BR), fin_f, fin_b


def dn_branch(qkv, ba, conv_w, gate_p, s0, l):
    loc_f, loc_b = dn_local(qkv, ba, conv_w, gate_p, l)
    return dn_seq(loc_f, loc_b, s0, l)


def _outproj_body(x_ref, mod_ref, ypc_ref, ypl_ref, dof_ref, dob_ref, dnz_ref, s5y_ref, s5u_ref, s5z_ref,
                  yfc_ref, yfl_ref, d_ref, gw_ref, gb_ref, dng_ref, w_ref, fg_ref, o_ref, *, final, tile0):
    is_ctx = pl.program_id(0) + tile0 < N_CTX // TM
    y_pool = jnp.where(is_ctx, ypc_ref[...], ypl_ref[...])
    y_ft = jnp.where(is_ctx, yfc_ref[...], yfl_ref[...])
    gate = mod_ref[0][:, 2 * D_MODEL:3 * D_MODEL]
    o = dof_ref[...] + dob_ref[...]
    head_mean = jnp.where(_head_block_mask(), 1.0 / DN_HD, 0.0)
    y_dn = o * lax.rsqrt(dot3(o * o, head_mean, 'b') + EPS) * dng_ref[...] * silu(dnz_ref[...].astype(F32))
    y = s5y_ref[...] + d_ref[...] * s5u_ref[...]
    y = jax.nn.gelu(y)
    y = y * jax.nn.sigmoid(bdot(y, gw_ref[...]) + gb_ref[...])
    y_s5 = y * silu(s5z_ref[...].astype(F32))
    acc = bdot(y_pool, w_ref[0:BR, :])
    acc = acc + bdot(y_dn, w_ref[BR:2 * BR, :])
    acc = acc + bdot(y_s5, w_ref[2 * BR:3 * BR, :])
    acc = acc + bdot(y_ft, w_ref[3 * BR:4 * BR, :])
    xn = x_ref[...] + gate * acc
    if final:
        xn = xn * lax.rsqrt(jnp.mean(xn * xn, axis=-1, keepdims=True) + EPS) * fg_ref[...]
    o_ref[...] = xn


def outproj(x, mod, yp_ctx, yp_lat, dn_of, dn_ob, gates, s5_y, s5_u, yf_ctx, yf_lat, s5_d, glu_w, glu_b, dn_g, w_out,
            final_g, l, final, tile0=0, ntiles=N_TOK // TM):
    row = lambda w: BS((TM, w), lambda i: (i + tile0, 0))
    gate = lambda c: BS((TM, BR), lambda i: (i + tile0, c))
    full = lambda a, b: BS((a, b), lambda i: (0, 0))
    tiles_ctx = N_CTX // TM
    ctx_row = BS((TM, BR), lambda i: (jnp.minimum(i + tile0, tiles_ctx - 1), 0))
    lat_row = BS((TM, BR), lambda i: (jnp.maximum(i + tile0 - tiles_ctx, 0), 0))
    return pl.pallas_call(
        functools.partial(_outproj_body, final=final, tile0=tile0),
        out_shape=SDS((ntiles * TM, D_MODEL), F32),
        grid=(ntiles,),
        in_specs=[row(D_MODEL),
                  BS((1, 1, 3 * D_MODEL), lambda i: (8 * l + _cond_index(i + tile0), 0, 0)),
                  ctx_row, lat_row, row(BR), row(BR), gate(1), row(BR), row(BR), gate(2), ctx_row, lat_row,
                  _layer(l, 1, BR), _layer(l, BR, BR), _layer(l, 1, BR), _layer(l, 1, BR),
                  _layer(l, D_MODEL, D_MODEL), full(1, D_MODEL)],
        out_specs=BS((TM, D_MODEL), lambda i: (i, 0)),
        compiler_params=_cparams("parallel"),
        name="outproj",
    )(x, mod, yp_ctx, yp_lat, dn_of, dn_ob, gates, s5_y, s5_u, gates, yf_ctx, yf_lat, s5_d, glu_w, glu_b, dn_g, w_out,
      final_g)


def _permute_w_in(w_in):
    cols = lambda lo, hi: w_in[..., lo:hi]
    data = [cols(0, 256), cols(512, 1280), cols(1552, 1808), cols(2064, 2320)]
    gates = [cols(256, 512), cols(1280, 1536), cols(1808, 2064), cols(2320, 2576)]
    ba = jnp.pad(cols(1536, 1552), ((0, 0), (0, 0), (0, 112)))
    return jnp.concatenate(data + gates + [ba], axis=-1).astype(BF16)


def kernel(x_prompt, x_sample, c, state_delta, state_s5, c_ctx, w_ada, b_ada, norm_g, w_in, pool_w, pool_scale,
           dn_conv, dn_a_log, dn_dt_bias, dn_norm_g, s5_a_re, s5_a_im, s5_log_dt, s5_b_re, s5_b_im, s5_c_re,
           s5_c_im, s5_d, s5_glu_w, s5_glu_b, ft_w, w_out, final_g):
    x = jnp.concatenate([x_prompt.astype(F32).reshape(N_CTX, D_MODEL),
                         x_sample.astype(F32).reshape(N_LAT, D_MODEL)], axis=0)
    cond8 = jnp.concatenate([c_ctx.astype(F32)[None], c.astype(F32),
                             jnp.zeros((8 - N_COND, D_MODEL), F32)], axis=0)
    ada = ada_all(cond8, w_ada, b_ada)
    pm_ctx, inv_ctx, pm_lat, inv_lat = _pool_constants()
    fpos, fch, g1, h2, fch2 = _ft_constants()
    s5_mt, s5_q4, s5_rt, s5_al = s5_prep(s5_a_re, s5_a_im, s5_log_dt, s5_b_re, s5_b_im, s5_c_re, s5_c_im)
    mod = ada.reshape(DEPTH * 8, 1, 3 * D_MODEL)
    norm_g3 = norm_g.reshape(DEPTH, 1, D_MODEL)
    w_in_p = _permute_w_in(w_in)
    eye_g = jnp.eye(len(POOL_WINDOWS), dtype=F32)
    w_bd = jnp.einsum('lgcd,gh->lgchd', pool_w, eye_g).reshape(DEPTH, BR, BR).astype(BF16)
    sc = pool_scale.reshape(DEPTH, 1, BR)
    ftw = ft_w.astype(BF16)
    conv_w = jnp.pad(dn_conv, ((0, 0), (0, 8 - CONV_K), (0, 0)))
    gate_p = jnp.zeros((DEPTH, 8, 128), F32)
    gate_p = gate_p.at[:, 0, 8:16].set(dn_a_log.reshape(DEPTH, 8)).at[:, 1, 8:16].set(dn_dt_bias.reshape(DEPTH, 8))
    out_params = (s5_d.reshape(DEPTH, 1, BR), s5_glu_w.astype(BF16), s5_glu_b.reshape(DEPTH, 1, BR),
                  jnp.tile(dn_norm_g, (1, DN_HEADS)).reshape(DEPTH, 1, BR), w_out.astype(BF16),
                  final_g.reshape(1, D_MODEL))
    s5_s0 = state_s5.astype(F32).transpose(1, 4, 3, 0, 2, 5).reshape(DEPTH, S5_LT, DEC_BATCH, 2 * S5_N)
    eye_h = jnp.eye(DN_HEADS, dtype=F32)
    dn_s0 = jnp.einsum('bldhkv,hg->ldbhkgv', state_delta.astype(F32), eye_h).reshape(DEPTH, 2 * DEC_BATCH, BR, BR)
    new_dn, new_s5 = [], []
    for l in range(DEPTH):
        pool_u, qkv, s5_u, ft_u, gates, ba = inproj(x, mod, norm_g3, w_in_p, l)

        yp_ctx = pool_branch(pool_u, gates, pm_ctx, inv_ctx, w_bd, sc, False, l)
        yp_lat = pool_branch(pool_u, gates, pm_lat, inv_lat, w_bd, sc, True, l)

        yf_ctx = ft_ctx(ft_u, gates, fpos, fch, ftw, l)
        yf_lat = ft_lat(ft_u, gates, g1, h2, fch2, ftw, l)

        s5_y, fin_s5 = s5_branch(s5_u, s5_mt, s5_q4, s5_rt, s5_al, s5_s0, l)
        new_s5.append(fin_s5)

        dn_of, dn_ob, fin_f, fin_b = dn_branch(qkv, ba, conv_w, gate_p, dn_s0, l)
        new_dn.append(jnp.stack([fin_f, fin_b]))

        finish = functools.partial(outproj, x, mod, yp_ctx, yp_lat, dn_of, dn_ob, gates, s5_y, s5_u, yf_ctx, yf_lat,
                                   *out_params, l)
        if l < DEPTH - 1:
            x = finish(False)
        else:
            tiles_ctx = N_CTX // TM
            y_ctx = finish(True, 0, tiles_ctx)
            y_lat = finish(True, tiles_ctx, N_LAT // TM)

    y_prompt = y_ctx.reshape(BATCH, SEQ, D_MODEL).astype(x_prompt.dtype)
    y_sample = y_lat.reshape(DEC_BATCH, DEC_SEQ, D_MODEL).astype(x_sample.dtype)
    new_state_delta = jnp.stack(new_dn).transpose(2, 0, 1, 3, 4, 5).astype(state_delta.dtype)
    fin = jnp.stack(new_s5).reshape(DEPTH, S5_G, 2, BATCH, 2, S5_N)
    new_state_s5 = fin.transpose(3, 0, 4, 2, 1, 5).astype(state_s5.dtype)
    return (y_prompt, y_sample, new_state_delta, new_state_s5)
```

```python
import functools
import math

import numpy as np
import jax
import jax.numpy as jnp
from jax import lax
from jax.experimental import pallas as pl
from jax.experimental.pallas import tpu as pltpu

F32 = jnp.float32
BF16 = jnp.bfloat16

D_MODEL = 1024
BATCH = 16
SEQ = 256
DEPTH = 4
DEC_BATCH = 2
DEC_SEQ = 4096
GRID_W = 64
GRID_H = DEC_SEQ // GRID_W
BR = 256
POOL_WINDOWS = (2, 4, 8, 16)
POOL_GD = 64
DN_HEADS = 4
DN_HD = 64
CONV_K = 5
CHUNK = 64
S5_P = 16
S5_G = 16
S5_N = 64
S5_L = 16
FT_HD = 64
EPS = 1e-6

N_CTX = BATCH * SEQ
N_LAT = DEC_BATCH * DEC_SEQ
N_TOK = N_CTX + N_LAT
N_COND = 1 + DEC_BATCH
TM = 512
W_IN_COLS = 2688
VMEM_LIMIT = 56 * 1024 * 1024

SDS = jax.ShapeDtypeStruct
BS = pl.BlockSpec


def _cparams(*sem):
    return pltpu.CompilerParams(dimension_semantics=sem, vmem_limit_bytes=VMEM_LIMIT)


def bdot(a, b):
    return jnp.dot(a.astype(BF16), b.astype(BF16), preferred_element_type=F32)


def hdot(a, b):
    return jnp.dot(a, b, preferred_element_type=F32, precision=lax.Precision.HIGHEST)


def silu(x):
    return x * jax.nn.sigmoid(x)


def _cond_index(i):
    tiles_ctx = N_CTX // TM
    tiles_seq = DEC_SEQ // TM
    return jnp.where(i < tiles_ctx, 0, 1 + (i - tiles_ctx) // tiles_seq)


def _ada_body(c_ref, w_ref, b_ref, o_ref):
    o_ref[0] = hdot(silu(c_ref[...]), w_ref[0]) + b_ref[0]


def ada_all(cond8, w_ada, b_ada):
    tn = 512
    return pl.pallas_call(
        _ada_body,
        out_shape=SDS((DEPTH, 8, 3 * D_MODEL), F32),
        grid=(DEPTH, 3 * D_MODEL // tn),
        in_specs=[BS((8, D_MODEL), lambda l, j: (0, 0)),
                  BS((1, D_MODEL, tn), lambda l, j: (l, 0, j)),
                  BS((1, 1, tn), lambda l, j: (l, 0, j))],
        out_specs=BS((1, 8, tn), lambda l, j: (l, 0, j)),
        compiler_params=_cparams("parallel", "parallel"),
        name="ada",
    )(cond8, w_ada, b_ada.reshape(DEPTH, 1, 3 * D_MODEL))


def _inproj_body(x_ref, mod_ref, g_ref, w_ref, pool_ref, qkv_ref, s5_ref, ft_ref, z_ref, ba_ref):
    x = x_ref[...]
    m = mod_ref[0]
    shift = m[:, 0:D_MODEL]
    scale = m[:, D_MODEL:2 * D_MODEL]
    xn = x * lax.rsqrt(jnp.mean(x * x, axis=-1, keepdims=True) + EPS) * g_ref[...]
    h = (xn * (1.0 + scale) + shift).astype(BF16)

    def proj(lo, hi):
        return jnp.dot(h, w_ref[:, lo:hi], preferred_element_type=F32)

    pool_ref[...] = proj(0, 256)
    qkv_ref[...] = proj(256, 1024)
    s5_ref[...] = proj(1024, 1280)
    ft_ref[...] = proj(1280, 1536).astype(BF16)
    z_ref[...] = proj(1536, 2560).astype(BF16)
    ba_ref[...] = proj(2560, 2688)


def _layer(l, *block):
    zeros = (0,) * len(block)
    return BS((None,) + block, lambda *_: (l,) + zeros)


def inproj(x, mod, norm_g, w_in_p, l):
    widths = (256, 768, 256, 256, 1024, 128)
    dtypes = (F32, F32, F32, BF16, BF16, F32)
    return pl.pallas_call(
        _inproj_body,
        out_shape=[SDS((N_TOK, w), dt) for w, dt in zip(widths, dtypes)],
        grid=(N_TOK // TM,),
        in_specs=[BS((TM, D_MODEL), lambda i: (i, 0)),
                  BS((1, 1, 3 * D_MODEL), lambda i: (8 * l + _cond_index(i), 0, 0)),
                  _layer(l, 1, D_MODEL),
                  _layer(l, D_MODEL, W_IN_COLS)],
        out_specs=[BS((TM, w), lambda i: (i, 0)) for w in widths],
        compiler_params=_cparams("parallel"),
        name="inproj",
    )(x, mod, norm_g, w_in_p)


def _pool_body(u_ref, z_ref, pm_ref, inv_ref, w_ref, sc_ref, o_ref, *scratch, two_d):
    nblk = u_ref.shape[0] // 256
    if two_d:
        pad_ref, v_ref = scratch
        halo = 8 * GRID_W
        pad_ref[0:halo, :] = jnp.zeros((halo, BR), F32)
        pad_ref[halo + DEC_SEQ:2 * halo + DEC_SEQ, :] = jnp.zeros((halo, BR), F32)
        pad_ref[halo:halo + DEC_SEQ, :] = u_ref[...]
        lane = lax.broadcasted_iota(jnp.int32, (GRID_W, 128), 1)

        def row_body(r, c):
            base = pl.multiple_of(r * GRID_W, GRID_W)

            def slab(d, lo):
                return pad_ref[pl.ds(base + (8 + d) * GRID_W, GRID_W), lo:lo + 128]

            s2 = slab(-1, 0) + slab(0, 0)
            s4 = s2 + slab(-2, 0) + slab(1, 0)
            v_ref[pl.ds(base, GRID_W), 0:128] = jnp.where(lane < 64, s2, s4)
            s8 = slab(-4, 128)
            for d in (-3, -2, -1, 0, 1, 2, 3):
                s8 = s8 + slab(d, 128)
            s16 = s8
            for d in (-8, -7, -6, -5, 4, 5, 6, 7):
                s16 = s16 + slab(d, 128)
            v_ref[pl.ds(base, GRID_W), 128:256] = jnp.where(lane < 64, s8, s16)
            return c

        lax.fori_loop(0, GRID_H, row_body, 0)
        src = v_ref
    else:
        src = u_ref
    grp = lax.broadcasted_iota(jnp.int32, (256, BR), 1) // POOL_GD

    def blk_body(b, c):
        r0 = pl.multiple_of(b * 256, 256)
        vb = src[pl.ds(r0, 256), :]
        hi = vb.astype(BF16)
        lo = (vb - hi.astype(F32)).astype(BF16)
        res = jnp.zeros((256, BR), F32)
        for g in range(len(POOL_WINDOWS)):
            pg = (jnp.dot(pm_ref[g], hi, preferred_element_type=F32)
                  + jnp.dot(pm_ref[g], lo, preferred_element_type=F32))
            res = jnp.where(grp == g, pg, res)
        pooled = res * inv_ref[pl.ds(r0, 256), :]
        d = pooled - u_ref[pl.ds(r0, 256), :]
        y = bdot(d, w_ref[...]) * sc_ref[...]
        o_ref[pl.ds(r0, 256), :] = (y * silu(z_ref[pl.ds(r0, 256), :].astype(F32))).astype(BF16)
        return c

    lax.fori_loop(0, nblk, blk_body, 0)


def _band_matrices(seg):
    t = np.arange(256)
    out = []
    for w in POOL_WINDOWS:
        lo = t - w // 2
        hi = t - w // 2 + w
        s = t[None, :]
        m = (s >= lo[:, None]) & (s < hi[:, None]) & ((s // seg) == (t[:, None] // seg))
        out.append(m.astype(np.float32))
    return np.stack(out)


def _counts(length, w):
    pos = np.arange(length)
    return (np.clip(pos - w // 2 + w, 0, length) - np.clip(pos - w // 2, 0, length)).astype(np.float64)


def _pool_constants():
    inv_ctx = np.concatenate([np.repeat((1.0 / _counts(SEQ, w))[:, None], POOL_GD, 1) for w in POOL_WINDOWS], 1)
    inv_lat = []
    for w in POOL_WINDOWS:
        c2 = np.outer(_counts(GRID_H, w), _counts(GRID_W, w)).reshape(DEC_SEQ)
        inv_lat.append(np.repeat((1.0 / c2)[:, None], POOL_GD, 1))
    inv_lat = np.concatenate(inv_lat, 1)
    return (jnp.asarray(_band_matrices(SEQ), BF16), jnp.asarray(inv_ctx, F32),
            jnp.asarray(_band_matrices(GRID_W), BF16), jnp.asarray(inv_lat, F32))


def pool_branch(pool_u, gates, pm, inv, w_bd, scale, two_d, l):
    if two_d:
        rows, nseq, blk0 = DEC_SEQ, DEC_BATCH, N_CTX // DEC_SEQ
        scratch = [pltpu.VMEM((DEC_SEQ + 16 * GRID_W, BR), F32), pltpu.VMEM((DEC_SEQ, BR), F32)]
    else:
        rows, nseq, blk0 = SEQ, BATCH, 0
        scratch = []
    return pl.pallas_call(
        functools.partial(_pool_body, two_d=two_d),
        out_shape=SDS((nseq * rows, BR), BF16),
        grid=(nseq,),
        in_specs=[BS((rows, BR), lambda i: (blk0 + i, 0)),
                  BS((rows, BR), lambda i: (blk0 + i, 0)),
                  BS((4, 256, 256), lambda i: (0, 0, 0)),
                  BS((rows, BR), lambda i: (0, 0)),
                  _layer(l, BR, BR),
                  _layer(l, 1, BR)],
        out_specs=BS((rows, BR), lambda i: (i, 0)),
        scratch_shapes=scratch,
        compiler_params=_cparams("parallel"),
        name="pool2d" if two_d else "pool1d",
    )(pool_u, gates, pm, inv, w_bd, scale)


def _ft_ctx_body(u_ref, z_ref, fpos_ref, fch_ref, w_ref, o_ref):
    uc = bdot(u_ref[...], fch_ref[...])
    st = jnp.concatenate([uc[:, 0:BR], uc[:, BR:2 * BR]], axis=0)
    f = bdot(fpos_ref[...], st)
    o_ref[...] = (bdot(f, w_ref[...]) * silu(z_ref[...].astype(F32))).astype(BF16)


def ft_ctx(ft_u, gates, fpos, fch, ft_w, l):
    return pl.pallas_call(
        _ft_ctx_body,
        out_shape=SDS((N_CTX, BR), BF16),
        grid=(BATCH,),
        in_specs=[BS((SEQ, BR), lambda i: (i, 0)),
                  BS((SEQ, BR), lambda i: (i, 3)),
                  BS((SEQ, 2 * SEQ), lambda i: (0, 0)),
                  BS((BR, 2 * BR), lambda i: (0, 0)),
                  _layer(l, BR, BR)],
        out_specs=BS((SEQ, BR), lambda i: (i, 0)),
        compiler_params=_cparams("parallel"),
        name="ft_ctx",
    )(ft_u, gates, fpos, fch, ft_w)


def _ft_lat_body(u_ref, z_ref, g_ref, h_ref, fch_ref, w_ref, o_ref, x_ref, yr_ref, yi_ref):
    for hf in range(2):
        x_ref[hf] = u_ref[:, hf * 128:(hf + 1) * 128].astype(F32)

    def stage1(t2, c):
        xs = jnp.concatenate([x_ref[hf, pl.ds(t2, GRID_H, stride=GRID_W), :] for hf in range(2)], axis=1)
        y = jnp.dot(g_ref[t2], xs.astype(BF16), preferred_element_type=F32)
        r0 = pl.multiple_of(t2 * GRID_W, GRID_W)
        for hf in range(2):
            yr_ref[hf, pl.ds(r0, GRID_W), :] = y[0:64, hf * 128:(hf + 1) * 128]
            yi_ref[hf, pl.ds(r0, GRID_W), :] = y[64:128, hf * 128:(hf + 1) * 128]
        return c

    lax.fori_loop(0, GRID_W, stage1, 0, unroll=8)

    def stage2(kb, c):
        yr = jnp.concatenate([yr_ref[hf, pl.ds(kb, GRID_W, stride=GRID_W), :] for hf in range(2)], axis=1)
        yi = jnp.concatenate([yi_ref[hf, pl.ds(kb, GRID_W, stride=GRID_W), :] for hf in range(2)], axis=1)
        st = jnp.concatenate([yr, yi], axis=0).astype(BF16)
        a = jnp.dot(h_ref[...], st, preferred_element_type=F32)
        for hf in range(2):
            yr_ref[hf, pl.ds(kb, GRID_W, stride=GRID_W), :] = a[0:64, hf * 128:(hf + 1) * 128]
            yi_ref[hf, pl.ds(kb, GRID_W, stride=GRID_W), :] = a[64:128, hf * 128:(hf + 1) * 128]
        return c

    lax.fori_loop(0, GRID_W, stage2, 0, unroll=8)

    def stage3(b, c):
        r0 = pl.multiple_of(b * TM, TM)
        ar = jnp.concatenate([yr_ref[hf, pl.ds(r0, TM), :] for hf in range(2)], axis=1)
        ai = jnp.concatenate([yi_ref[hf, pl.ds(r0, TM), :] for hf in range(2)], axis=1)
        f = bdot(ar, fch_ref[0:BR, :]) + bdot(ai, fch_ref[BR:2 * BR, :])
        o_ref[pl.ds(r0, TM), :] = (bdot(f, w_ref[...]) * silu(z_ref[pl.ds(r0, TM), :].astype(F32))).astype(BF16)
        return c

    lax.fori_loop(0, DEC_SEQ // TM, stage3, 0)


def ft_lat(ft_u, gates, g1, h2, fch2, ft_w, l):
    blk0 = N_CTX // DEC_SEQ
    return pl.pallas_call(
        _ft_lat_body,
        out_shape=SDS((N_LAT, BR), BF16),
        grid=(DEC_BATCH,),
        in_specs=[BS((DEC_SEQ, BR), lambda i: (blk0 + i, 0)),
                  BS((DEC_SEQ, BR), lambda i: (blk0 + i, 3)),
                  BS((GRID_W, 128, GRID_H), lambda i: (0, 0, 0)),
                  BS((128, 128), lambda i: (0, 0)),
                  BS((2 * BR, BR), lambda i: (0, 0)),
                  _layer(l, BR, BR)],
        out_specs=BS((DEC_SEQ, BR), lambda i: (i, 0)),
        scratch_shapes=[pltpu.VMEM((2, DEC_SEQ, 128), F32)] * 3,
        compiler_params=_cparams("parallel"),
        name="ft_lat",
    )(ft_u, gates, g1, h2, fch2, ft_w)


def _ft_constants():
    c = np.arange(FT_HD)
    ang = 2.0 * np.pi * np.outer(c, c) / FT_HD
    eye4 = np.eye(BR // FT_HD)
    cc = np.kron(eye4, np.cos(ang)) / 8.0
    sc = np.kron(eye4, np.sin(ang)) / 8.0
    t = np.arange(SEQ)
    angt = 2.0 * np.pi * (np.outer(t, t) % SEQ) / SEQ
    fpos = np.concatenate([np.cos(angt), -np.sin(angt)], axis=1) / 16.0
    fch = np.concatenate([cc, sc], axis=1)
    kb = np.arange(GRID_W)[None, :, None]
    t1 = np.arange(GRID_H)[None, None, :]
    t2 = np.arange(GRID_W)[:, None, None]
    a1 = 2.0 * np.pi * ((kb * (GRID_W * t1 + t2)) % DEC_SEQ) / DEC_SEQ
    g1 = np.concatenate([np.cos(a1), -np.sin(a1)], axis=1) / 8.0
    a2 = 2.0 * np.pi * (np.outer(np.arange(GRID_W), np.arange(GRID_W)) % GRID_W) / GRID_W
    c2, s2 = np.cos(a2) / 8.0, np.sin(a2) / 8.0
    h2 = np.block([[c2, s2], [-s2, c2]])
    fch2 = np.concatenate([cc, sc], axis=0)
    as_bf = lambda a: jnp.asarray(a, F32).astype(BF16)
    return as_bf(fpos), as_bf(fch), as_bf(g1), as_bf(h2), as_bf(fch2)


S5_ROWS = N_TOK // S5_L
S5_TM = 1024
S5_TR = S5_TM // S5_L
S5_CTX_C = SEQ // S5_L
S5_LAT_C = DEC_SEQ // S5_L
S5_LT = 2 * S5_G


S5_PG = 4


def _s5_prep_body(*refs):
    for gi in range(S5_PG):
        _s5_prep_group(gi, *refs)


def _s5_prep_group(gi, ar_ref, ai_ref, ldt_ref, br_ref, bi_ref, btr_ref, bti_ref, cr_ref, ci_ref,
                   mt_ref, q4_ref, rt_ref, al_ref):
    L = S5_L
    m = lax.broadcasted_iota(jnp.int32, (2 * L, S5_N), 0).astype(F32)
    rts, qs, ds, als = [], [], [], []
    for d in range(2):
        a_re, a_im = ar_ref[0, d, gi], ai_ref[0, d, gi]
        dt = jnp.exp(ldt_ref[0, d, gi])
        xr, xi = a_re * dt, a_im * dt
        mag = jnp.exp(xr)
        ab_re, ab_im = mag * jnp.cos(xi), mag * jnp.sin(xi)
        den = a_re * a_re + a_im * a_im
        nr = ab_re - 1.0
        coef_re = (nr * a_re + ab_im * a_im) / den
        coef_im = (ab_im * a_re - nr * a_im) / den
        pw_re = jnp.exp(m * xr) * jnp.cos(m * xi)
        pw_im = jnp.exp(m * xr) * jnp.sin(m * xi)
        cq_re = pw_re * coef_re - pw_im * coef_im
        cq_im = pw_re * coef_im + pw_im * coef_re
        c_re, c_im = cr_ref[0, d, gi], ci_ref[0, d, gi]
        bt_re, bt_im = btr_ref[0, d, gi], bti_ref[0, d, gi]
        row = lambda x, e: x[e:e + 1, :]
        order = range(L) if d == 0 else range(L - 1, -1, -1)
        cp_re = jnp.concatenate([c_re * row(cq_re, e) - c_im * row(cq_im, e) for e in order], axis=0)
        cp_im = jnp.concatenate([c_re * row(cq_im, e) + c_im * row(cq_re, e) for e in order], axis=0)
        rts.append(hdot(cp_re, br_ref[0, d, gi]) - hdot(cp_im, bi_ref[0, d, gi]))
        inj = [L - 1 - i for i in range(L)] if d == 0 else list(range(L))
        q_re = jnp.concatenate([bt_re * row(cq_re, e) - bt_im * row(cq_im, e) for e in inj], axis=0)
        q_im = jnp.concatenate([bt_im * row(cq_re, e) + bt_re * row(cq_im, e) for e in inj], axis=0)
        qs.append((q_re, q_im))
        out = [j + 1 for j in range(L)] if d == 0 else [L - j for j in range(L)]
        d_re = jnp.concatenate([c_re * row(pw_re, e) - c_im * row(pw_im, e) for e in out], axis=0)
        d_im = jnp.concatenate([c_re * row(pw_im, e) + c_im * row(pw_re, e) for e in out], axis=0)
        ds.append((d_re, d_im))
        als.append((row(pw_re, L), row(pw_im, L)))
    pad = jnp.zeros(((L - 1) * S5_P, S5_P), F32)
    z = jnp.concatenate([pad, rts[0]], axis=0) + jnp.concatenate([rts[1], pad], axis=0)
    mt = jnp.concatenate([z[(L - 1 - i) * S5_P:(L - 1 - i) * S5_P + L * S5_P, :] for i in range(L)], axis=1)
    mt_ref[0, gi] = mt.astype(BF16)
    q4_ref[0, gi] = jnp.concatenate([qs[0][0], qs[1][0], qs[0][1], qs[1][1]], axis=1).astype(BF16)
    rt_ref[0, gi] = jnp.concatenate([ds[0][0], ds[1][0], -ds[0][1], -ds[1][1]], axis=1).astype(BF16)
    al_ref[0, gi] = jnp.concatenate([jnp.concatenate([als[0][0], als[1][0]], axis=1),
                                    jnp.concatenate([als[0][1], als[1][1]], axis=1)], axis=0)


def s5_prep(a_re, a_im, log_dt, b_re, b_im, c_re, c_im):
    vec = lambda x: x.reshape(DEPTH, 2, S5_G, 1, S5_N)
    ldt = jnp.broadcast_to(log_dt[..., None, None], (DEPTH, 2, S5_G, 1, S5_N))
    bt = lambda x: x.transpose(0, 1, 2, 4, 3)
    per = S5_G // S5_PG
    vspec = BS((1, 2, S5_PG, 1, S5_N), lambda i: (i // per, 0, i % per, 0, 0))
    bspec = BS((1, 2, S5_PG, S5_N, S5_P), lambda i: (i // per, 0, i % per, 0, 0))
    cspec = BS((1, 2, S5_PG, S5_P, S5_N), lambda i: (i // per, 0, i % per, 0, 0))
    mat = SDS((DEPTH, S5_G, 256, 256), BF16)
    mspec = BS((1, S5_PG, 256, 256), lambda i: (i // per, i % per, 0, 0))
    return pl.pallas_call(
        _s5_prep_body,
        out_shape=[mat, mat, mat, SDS((DEPTH, S5_G, 2, 128), F32)],
        grid=(DEPTH * per,),
        in_specs=[vspec, vspec, vspec, bspec, bspec, cspec, cspec, cspec, cspec],
        out_specs=[mspec, mspec, mspec, BS((1, S5_PG, 2, 128), lambda i: (i // per, i % per, 0, 0))],
        compiler_params=_cparams("parallel"),
        name="s5_prep",
    )(vec(a_re), vec(a_im), ldt, b_re, b_im, bt(b_re), bt(b_im), c_re, c_im)


def _block_transpose(arrs):
    blk = lax.broadcasted_iota(jnp.int32, arrs[0].shape, 1) // 16
    cur = list(arrs)
    for b in range(4):
        s = 16 << b
        hi = ((blk >> b) & 1) == 1
        nxt = list(cur)
        for x in range(16):
            if (x >> b) & 1:
                continue
            y = x | (1 << b)
            nxt[x] = jnp.where(hi, pltpu.roll(cur[y], s, 1), cur[x])
            nxt[y] = jnp.where(hi, cur[y], pltpu.roll(cur[x], 256 - s, 1))
        cur = nxt
    return cur


def _s5_in_body(s5_ref, mt_ref, q4_ref, y_ref, e_ref, x_ref):
    for hf in range(2):
        x_ref[hf] = s5_ref[:, hf * 128:(hf + 1) * 128]
    xs = [jnp.concatenate([x_ref[hf, pl.ds(i, S5_TR, stride=S5_L), :] for hf in range(2)], axis=1)
          for i in range(S5_L)]
    us = _block_transpose(xs)
    for g in range(S5_G):
        ub = us[g].astype(BF16)
        y_ref[:, g * 256:(g + 1) * 256] = lax.dot_general(ub, mt_ref[g], (((1,), (1,)), ((), ())),
                                                          preferred_element_type=F32)
        e = jnp.dot(ub, q4_ref[g], preferred_element_type=F32)
        e_ref[2 * g] = e[:, 0:128]
        e_ref[2 * g + 1] = e[:, 128:256]


def s5_in(s5, mt, q4, l):
    wspec = _layer(l, S5_G, 256, 256)
    return pl.pallas_call(
        _s5_in_body,
        out_shape=[SDS((S5_ROWS, S5_G * 256), F32), SDS((S5_LT, S5_ROWS, 128), F32)],
        grid=(N_TOK // S5_TM,),
        in_specs=[BS((S5_TM, BR), lambda t: (t, 0)), wspec, wspec],
        out_specs=[BS((S5_TR, S5_G * 256), lambda t: (t, 0)), BS((S5_LT, S5_TR, 128), lambda t: (0, t, 0))],
        scratch_shapes=[pltpu.VMEM((2, S5_TM, 128), F32)],
        compiler_params=_cparams("parallel"),
        name="s5_in",
    )(s5, mt, q4)


def _s5_chunk_scan_body(e_ref, al_ref, s0_ref, spf_ref, spb_ref, fin_ref):
    def update(g, s_re, s_im, e_re, e_im):
        a_re, a_im = al_ref[g, 0:1, :], al_ref[g, 1:2, :]
        return a_re * s_re - a_im * s_im + e_re, a_re * s_im + a_im * s_re + e_im

    fwd_c = lax.broadcasted_iota(jnp.int32, (BATCH, 128), 1) < S5_N

    def ctx_group(g, carry):
        s_re = jnp.zeros((BATCH, 128), F32)
        s_im = jnp.zeros((BATCH, 128), F32)
        for c in range(S5_CTX_C):
            rf = pl.ds(c, BATCH, stride=S5_CTX_C)
            rb = pl.ds(S5_CTX_C - 1 - c, BATCH, stride=S5_CTX_C)
            spf_ref[2 * g, rf, :] = s_re
            spf_ref[2 * g + 1, rf, :] = s_im
            spb_ref[2 * g, rb, :] = s_re
            spb_ref[2 * g + 1, rb, :] = s_im
            e_re = jnp.where(fwd_c, e_ref[2 * g, rf, :], e_ref[2 * g, rb, :])
            e_im = jnp.where(fwd_c, e_ref[2 * g + 1, rf, :], e_ref[2 * g + 1, rb, :])
            s_re, s_im = update(g, s_re, s_im, e_re, e_im)
        fin_ref[2 * g] = s_re
        fin_ref[2 * g + 1] = s_im
        return carry

    lax.fori_loop(0, S5_GQ, ctx_group, 0)

    row0 = BATCH * S5_CTX_C
    fwd_l = lax.broadcasted_iota(jnp.int32, (DEC_BATCH, 128), 1) < S5_N

    def lat_step(c, state):
        rf = pl.ds(row0 + c, DEC_BATCH, stride=S5_LAT_C)
        rb = pl.ds(row0 + S5_LAT_C - 1 - c, DEC_BATCH, stride=S5_LAT_C)
        new = []
        for g in range(S5_GQ):
            s_re, s_im = state[2 * g], state[2 * g + 1]
            spf_ref[2 * g, rf, :] = s_re
            spf_ref[2 * g + 1, rf, :] = s_im
            spb_ref[2 * g, rb, :] = s_re
            spb_ref[2 * g + 1, rb, :] = s_im
            e_re = jnp.where(fwd_l, e_ref[2 * g, rf, :], e_ref[2 * g, rb, :])
            e_im = jnp.where(fwd_l, e_ref[2 * g + 1, rf, :], e_ref[2 * g + 1, rb, :])
            new.extend(update(g, s_re, s_im, e_re, e_im))
        return tuple(new)

    lax.fori_loop(0, S5_LAT_C, lat_step, tuple(s0_ref[t] for t in range(2 * S5_GQ)))


S5_GQ = 4


def s5_chunk_scan(e3, al, s0_lat, l):
    sp = SDS((S5_LT, S5_ROWS, 128), F32)
    tiles = lambda rows: BS((2 * S5_GQ, rows, 128), lambda q: (q, 0, 0))
    return pl.pallas_call(
        _s5_chunk_scan_body,
        out_shape=[sp, sp, SDS((S5_LT, BATCH, 128), F32)],
        grid=(S5_G // S5_GQ,),
        in_specs=[tiles(S5_ROWS), BS((None, S5_GQ, 2, 128), lambda q: (l, q, 0, 0)),
                  BS((None, 2 * S5_GQ, DEC_BATCH, 128), lambda q: (l, q, 0, 0))],
        out_specs=[tiles(S5_ROWS), tiles(S5_ROWS), tiles(BATCH)],
        compiler_params=_cparams("parallel"),
        name="s5_chunk_scan",
    )(e3, al, s0_lat)


def _s5_fin_body(y_ref, spf_ref, spb_ref, rt_ref, o_ref, t_ref):
    fwd = lax.broadcasted_iota(jnp.int32, (S5_TR, 128), 1) < S5_N
    ys = []
    for g in range(S5_G):
        s_re = jnp.where(fwd, spf_ref[2 * g], spb_ref[2 * g])
        s_im = jnp.where(fwd, spf_ref[2 * g + 1], spb_ref[2 * g + 1])
        sp = jnp.concatenate([s_re, s_im], axis=1).astype(BF16)
        ys.append(y_ref[:, g * 256:(g + 1) * 256]
                  + lax.dot_general(sp, rt_ref[g], (((1,), (1,)), ((), ())), preferred_element_type=F32))
    xs = _block_transpose(ys)
    for j in range(S5_L):
        for hf in range(2):
            t_ref[hf, pl.ds(j, S5_TR, stride=S5_L), :] = xs[j][:, hf * 128:(hf + 1) * 128]
    o_ref[...] = jnp.concatenate([t_ref[0], t_ref[1]], axis=1)


def s5_fin(yi, spf, spb, rt, l):
    lt = BS((S5_LT, S5_TR, 128), lambda t: (0, t, 0))
    return pl.pallas_call(
        _s5_fin_body,
        out_shape=SDS((N_TOK, BR), F32),
        grid=(N_TOK // S5_TM,),
        in_specs=[BS((S5_TR, S5_G * 256), lambda t: (t, 0)), lt, lt, _layer(l, S5_G, 256, 256)],
        out_specs=BS((S5_TM, BR), lambda t: (t, 0)),
        scratch_shapes=[pltpu.VMEM((2, S5_TM, 128), F32)],
        compiler_params=_cparams("parallel"),
        name="s5_fin",
    )(yi, spf, spb, rt)


def s5_branch(s5, mt, q4, rt, al, s0, l):
    yi, e3 = s5_in(s5, mt, q4, l)
    spf, spb, fin = s5_chunk_scan(e3, al, s0, l)
    return s5_fin(yi, spf, spb, rt, l), fin


DN_TILE = 256
DN_HALO = 8


N_SEG = N_TOK // DN_TILE
assert N_CTX == DEC_SEQ


def _split3(x):
    x1 = x.astype(BF16)
    r1 = x - x1.astype(F32)
    x2 = r1.astype(BF16)
    return x1, x2, (r1 - x2.astype(F32)).astype(BF16)


def dot3(a, b, exact):
    if exact == 'b':
        return sum(jnp.dot(p, b.astype(BF16), preferred_element_type=F32) for p in _split3(a))
    return sum(jnp.dot(a.astype(BF16), p, preferred_element_type=F32) for p in _split3(b))


def _dn_conv_body(x_ref, prev_ref, next_ref, ba_ref, w_ref, gp_ref, o_ref, gb_ref, pad_ref):
    i = pl.program_id(0)
    tiles_ctx = N_CTX // DN_TILE
    tiles_seq = DEC_SEQ // DN_TILE
    j = (i - tiles_ctx) % tiles_seq
    first = jnp.logical_or(i < tiles_ctx, j == 0)
    last = jnp.logical_or(i < tiles_ctx, j == tiles_seq - 1)
    pad_ref[0:DN_HALO, :] = jnp.where(first, 0.0, prev_ref[...])
    pad_ref[DN_HALO:DN_HALO + DN_TILE, :] = x_ref[...]
    pad_ref[DN_HALO + DN_TILE:2 * DN_HALO + DN_TILE, :] = jnp.where(last, 0.0, next_ref[...])
    for r0 in range(0, DN_TILE, 128):
        for c0 in range(0, 3 * BR, 128):
            acc = jnp.zeros((128, 128), F32)
            for t in range(CONV_K):
                acc = acc + (pad_ref[pl.ds(r0 + DN_HALO - CONV_K // 2 + t, 128), c0:c0 + 128]
                             * w_ref[t:t + 1, c0:c0 + 128])
            o_ref[r0:r0 + 128, c0:c0 + 128] = silu(acc)
    raw = ba_ref[...]
    lane = lax.broadcasted_iota(jnp.int32, raw.shape, 1)
    xa = raw + gp_ref[1:2, :]
    sp = jnp.maximum(xa, 0.0) + jnp.log1p(jnp.exp(-jnp.abs(xa)))
    gates = jnp.where(lane < 2 * DN_HEADS, jax.nn.sigmoid(raw), -jnp.exp(gp_ref[0:1, :]) * sp)
    pos = lax.broadcasted_iota(jnp.int32, raw.shape, 0) % CHUNK
    gc_f, gc_b = gates, gates
    for s in (1, 2, 4, 8, 16, 32):
        gc_f = gc_f + jnp.where(pos >= s, pltpu.roll(gc_f, s, 0), 0.0)
        gc_b = gc_b + jnp.where(pos < CHUNK - s, pltpu.roll(gc_b, DN_TILE - s, 0), 0.0)
    comb = jnp.where(lane < 2 * DN_HEADS, gates, jnp.where(lane < 3 * DN_HEADS, gc_f, gc_b))
    for blk in range(4):
        cols = [jnp.broadcast_to(comb[:, blk * DN_HEADS + h:blk * DN_HEADS + h + 1], (DN_TILE, DN_HD))
                for h in range(DN_HEADS)]
        gb_ref[:, blk * BR:(blk + 1) * BR] = jnp.concatenate(cols, axis=1)


def _head_block_mask():
    r = lax.broadcasted_iota(jnp.int32, (BR, BR), 0) // DN_HD
    c = lax.broadcasted_iota(jnp.int32, (BR, BR), 1) // DN_HD
    return r == c


def _split2(x):
    hi = x.astype(BF16)
    return hi, (x - hi.astype(F32)).astype(BF16)


def _dn_local_body(qkv_ref, gb_ref, *out_refs):
    f_refs, b_refs = out_refs[0:6], out_refs[6:12]
    ncb = DN_TILE // CHUNK
    nb = 2 * ncb
    bmask = _head_block_mask()
    ri = lax.broadcasted_iota(jnp.int32, (nb, CHUNK, BR), 1)
    cj = lax.broadcasted_iota(jnp.int32, (nb, CHUNK, BR), 2) % DN_HD
    bwd = lax.broadcasted_iota(jnp.int32, (nb, CHUNK, BR), 0) >= ncb
    eye = cj == ri
    incl = jnp.logical_or(jnp.logical_and(bwd, cj >= ri), jnp.logical_and(jnp.logical_not(bwd), cj <= ri))
    strict = jnp.logical_and(incl, jnp.logical_not(eye))
    ones_blk = bmask.astype(BF16)

    def chunks(x):
        return x.reshape(ncb, CHUNK, BR)

    def both(x):
        return jnp.concatenate([x, x], axis=0)

    def head_sum(x):
        return dot3(x.reshape(-1, BR), ones_blk, 'b').reshape(x.shape)

    def bd(x):
        return jnp.where(bmask, jnp.concatenate([x] * DN_HEADS, axis=1), jnp.zeros((), x.dtype))

    def bmm(a, b):
        return jnp.einsum('bij,bjk->bik', a, b, preferred_element_type=F32)

    q = chunks(qkv_ref[:, 0:BR])
    k = chunks(qkv_ref[:, BR:2 * BR])
    v = both(chunks(qkv_ref[:, 2 * BR:3 * BR]))
    q = q * lax.rsqrt(head_sum(q * q) + EPS) * (DN_HD ** -0.5)
    k = k * lax.rsqrt(head_sum(k * k) + EPS)
    kq = jnp.einsum('bik,bjk->bij', jnp.concatenate([k, q], axis=1).astype(BF16), bd(k.astype(BF16)),
                    preferred_element_type=F32)
    kk, qk = both(kq[:, 0:CHUNK]), both(kq[:, CHUNK:2 * CHUNK])
    q, k = both(q), both(k)
    beta = jnp.concatenate([chunks(gb_ref[:, 0:BR]), chunks(gb_ref[:, BR:2 * BR])], axis=0)
    gc = jnp.concatenate([chunks(gb_ref[:, 2 * BR:3 * BR]), chunks(gb_ref[:, 3 * BR:4 * BR])], axis=0)
    crow = jnp.sum(jnp.where(eye, gc, 0.0), axis=1, keepdims=True)
    decay = jnp.where(incl, jnp.exp(jnp.where(incl, gc - crow, 0.0)), 0.0)
    a = jnp.where(strict, kk * decay * beta, 0.0)
    tinv = jnp.where(eye, 1.0, 0.0) - a
    pw = a
    pw_bd = bd(pw.astype(BF16))
    for _ in range(5):
        pw = bmm(pw.astype(BF16), pw_bd)
        pw_bd = bd(pw.astype(BF16))
        tinv = tinv + bmm(tinv.astype(BF16), pw_bd)
    egc = jnp.exp(gc)
    t_hi, t_lo = _split2(tinv)

    def solve(rhs):
        r_hi, r_lo = _split2(rhs)
        r_hi, r_lo = bd(r_hi), bd(r_lo)
        return bmm(t_hi, r_hi) + bmm(t_hi, r_lo) + bmm(t_lo, r_hi)

    w = bmm(t_hi, bd((k * (beta * egc)).astype(BF16)))

    bwd_row = lax.broadcasted_iota(jnp.int32, (nb, 1, BR), 0) >= ncb
    g_last = jnp.where(bwd_row, gc[:, 0:1], gc[:, CHUNK - 1:CHUNK])
    eg = jnp.exp(g_last)
    outs = (w, solve(v * beta), q * egc, k * jnp.exp(g_last - gc),
            jnp.where(incl, qk * decay, 0.0))
    for d, refs in enumerate((f_refs, b_refs)):
        for ref, x in zip(refs[0:5], outs):
            ref[...] = x[d * ncb:(d + 1) * ncb].reshape(DN_TILE, BR).astype(ref.dtype)
        refs[5][0] = jnp.concatenate([eg[d * ncb:(d + 1) * ncb, 0], jnp.zeros((8 - ncb, BR), F32)], axis=0)


_DN_LOCAL_DTYPES = (BF16, F32, BF16, BF16, BF16)


def _dn_prep_body(x_ref, prev_ref, next_ref, ba_ref, w_ref, gp_ref, *rest):
    out_refs, (pad_ref, qkv_ref, gb_ref) = rest[0:12], rest[12:15]
    _dn_conv_body(x_ref, prev_ref, next_ref, ba_ref, w_ref, gp_ref, qkv_ref, gb_ref, pad_ref)
    _dn_local_body(qkv_ref, gb_ref, *out_refs)


def dn_local(qkv, ba, conv_w, gate_p, l):
    per = DN_TILE // DN_HALO
    nhb = N_TOK // DN_HALO
    tok = BS((DN_TILE, BR), lambda i: (i, 0))
    shapes = [SDS((N_TOK, BR), dt) for dt in _DN_LOCAL_DTYPES] + [SDS((N_SEG, 8, BR), F32)]
    specs = [tok] * len(_DN_LOCAL_DTYPES) + [BS((1, 8, BR), lambda i: (i, 0, 0))]
    outs = pl.pallas_call(
        _dn_prep_body,
        out_shape=shapes * 2,
        grid=(N_SEG,),
        in_specs=[BS((DN_TILE, 3 * BR), lambda i: (i, 0)),
                  BS((DN_HALO, 3 * BR), lambda i: (jnp.maximum(i * per - 1, 0), 0)),
                  BS((DN_HALO, 3 * BR), lambda i: (jnp.minimum((i + 1) * per, nhb - 1), 0)),
                  BS((DN_TILE, 128), lambda i: (i, 0)),
                  _layer(l, 8, 3 * BR),
                  _layer(l, 8, 128)],
        out_specs=specs * 2,
        scratch_shapes=[pltpu.VMEM((DN_TILE + 2 * DN_HALO, 3 * BR), F32),
                        pltpu.VMEM((DN_TILE, 3 * BR), F32),
                        pltpu.VMEM((DN_TILE, 4 * BR), F32)],
        compiler_params=_cparams("parallel"),
        name="dn_local",
    )(qkv, qkv, qkv, ba, conv_w, gate_p)
    return outs[0:6], outs[6:12]


def _dn_advance(chains, s, bmask):
    ncb = DN_TILE // CHUNK
    bmm = lambda a, b: jnp.einsum('bij,bjk->bik', a, b, preferred_element_type=F32)
    steps = []
    for t in range(ncb):
        cs = [ncb - 1 - t if rev else t for _, rev in chains]
        w, u, qt, kt, aqk, eg = [jnp.stack([load(k, c) for (load, _), c in zip(chains, cs)]) for k in range(6)]
        sb = s.astype(BF16)
        v_new = u - bmm(w, sb)
        vb = v_new.astype(BF16)
        v_bd = jnp.where(bmask, jnp.concatenate([vb] * DN_HEADS, axis=1), jnp.zeros((), BF16))
        o = bmm(qt, sb) + bmm(aqk, v_bd)
        upd = jnp.einsum('btk,btv->bkv', kt, vb, preferred_element_type=F32)
        s = s * eg + jnp.where(bmask, upd, 0.0)
        steps.append((cs, o))
    return s, steps


assert BATCH == DEC_SEQ // DN_TILE


def _dn_seq_body(*refs):
    f_in, b_in = refs[0:6], refs[6:12]
    s0_ref, of_ref, ob_ref, finf_ref, finb_ref, s_ref = refs[12:18]

    @pl.when(pl.program_id(0) == 0)
    def _():
        s_ref[...] = s0_ref[...]

    def loader(in_refs, q):
        def load(k, c):
            if k == 5:
                return in_refs[5][q, 0, c:c + 1, :]
            return in_refs[k][q, c * CHUNK:(c + 1) * CHUNK, :]
        return load

    groups = [1, 2, 1, 2, 0, 0]
    outs = [of_ref, of_ref, ob_ref, ob_ref, of_ref, ob_ref]
    chains = [(loader(b_in if o is ob_ref else f_in, q), o is ob_ref) for q, o in zip(groups, outs)]
    n_lat = 2 * DEC_BATCH
    s = jnp.concatenate([s_ref[...], jnp.zeros((2, BR, BR), F32)], axis=0)
    s, steps = _dn_advance(chains, s, _head_block_mask())
    s_ref[...] = s[0:n_lat]
    heads = lambda m: jnp.stack([m[h * DN_HD:(h + 1) * DN_HD, h * DN_HD:(h + 1) * DN_HD] for h in range(DN_HEADS)])
    finf_ref[0] = heads(s[n_lat])
    finb_ref[0] = heads(s[n_lat + 1])
    for cs, o in steps:
        for i, c in enumerate(cs):
            outs[i][groups[i], c * CHUNK:(c + 1) * CHUNK, :] = o[i]


def dn_seq(loc_f, loc_b, s0, l):
    nseg = DEC_SEQ // DN_TILE
    grp = lambda a: a.reshape((N_COND, nseg) + a.shape[1:])

    def specs(m):
        return ([BS((N_COND, DN_TILE, BR), lambda g: (0, m(g), 0))] * len(_DN_LOCAL_DTYPES)
                + [BS((N_COND, 1, 8, BR), lambda g: (0, m(g), 0, 0))])

    fwd = lambda g: g
    bwd = lambda g: nseg - 1 - g
    views = lambda loc: [a.reshape(N_COND, DEC_SEQ, BR) for a in loc[0:5]] + [grp(loc[5])]
    out = SDS((N_COND, DEC_SEQ, BR), F32)
    fin = SDS((BATCH, DN_HEADS, DN_HD, DN_HD), F32)
    o_f, o_b, fin_f, fin_b = pl.pallas_call(
        _dn_seq_body,
        out_shape=[out, out, fin, fin],
        grid=(nseg,),
        in_specs=specs(fwd) + specs(bwd) + [_layer(l, 2 * DEC_BATCH, BR, BR)],
        out_specs=[BS((N_COND, DN_TILE, BR), lambda g: (0, fwd(g), 0)),
                   BS((N_COND, DN_TILE, BR), lambda g: (0, bwd(g), 0)),
                   BS((1, DN_HEADS, DN_HD, DN_HD), lambda g: (fwd(g), 0, 0, 0)),
                   BS((1, DN_HEADS, DN_HD, DN_HD), lambda g: (bwd(g), 0, 0, 0))],
        scratch_shapes=[pltpu.VMEM((2 * DEC_BATCH, BR, BR), F32)],
        compiler_params=_cparams("arbitrary"),
        name="dn_seq",
    )(*views(loc_f), *views(loc_b), s0)
    return o_f.reshape(N_TOK, BR), o_b.reshape(N_TOK, BR), fin_f, fin_b


def dn_branch(qkv, ba, conv_w, gate_p, s0, l):
    loc_f, loc_b = dn_local(qkv, ba, conv_w, gate_p, l)
    return dn_seq(loc_f, loc_b, s0, l)


def _outproj_body(x_ref, mod_ref, ypc_ref, ypl_ref, dof_ref, dob_ref, dnz_ref, s5y_ref, s5u_ref, s5z_ref,
                  yfc_ref, yfl_ref, d_ref, gw_ref, gb_ref, dng_ref, w_ref, fg_ref, o_ref, *, final, tile0):
    is_ctx = pl.program_id(0) + tile0 < N_CTX // TM
    y_pool = jnp.where(is_ctx, ypc_ref[...], ypl_ref[...])
    y_ft = jnp.where(is_ctx, yfc_ref[...], yfl_ref[...])
    gate = mod_ref[0][:, 2 * D_MODEL:3 * D_MODEL]
    o = dof_ref[...] + dob_ref[...]
    head_mean = jnp.where(_head_block_mask(), 1.0 / DN_HD, 0.0)
    y_dn = o * lax.rsqrt(dot3(o * o, head_mean, 'b') + EPS) * dng_ref[...] * silu(dnz_ref[...].astype(F32))
    y = s5y_ref[...] + d_ref[...] * s5u_ref[...]
    y = jax.nn.gelu(y)
    y = y * jax.nn.sigmoid(bdot(y, gw_ref[...]) + gb_ref[...])
    y_s5 = y * silu(s5z_ref[...].astype(F32))
    acc = bdot(y_pool, w_ref[0:BR, :])
    acc = acc + bdot(y_dn, w_ref[BR:2 * BR, :])
    acc = acc + bdot(y_s5, w_ref[2 * BR:3 * BR, :])
    acc = acc + bdot(y_ft, w_ref[3 * BR:4 * BR, :])
    xn = x_ref[...] + gate * acc
    if final:
        xn = xn * lax.rsqrt(jnp.mean(xn * xn, axis=-1, keepdims=True) + EPS) * fg_ref[...]
    o_ref[...] = xn


def outproj(x, mod, yp_ctx, yp_lat, dn_of, dn_ob, gates, s5_y, s5_u, yf_ctx, yf_lat, s5_d, glu_w, glu_b, dn_g, w_out,
            final_g, l, final, tile0=0, ntiles=N_TOK // TM):
    row = lambda w: BS((TM, w), lambda i: (i + tile0, 0))
    gate = lambda c: BS((TM, BR), lambda i: (i + tile0, c))
    full = lambda a, b: BS((a, b), lambda i: (0, 0))
    tiles_ctx = N_CTX // TM
    ctx_row = BS((TM, BR), lambda i: (jnp.minimum(i + tile0, tiles_ctx - 1), 0))
    lat_row = BS((TM, BR), lambda i: (jnp.maximum(i + tile0 - tiles_ctx, 0), 0))
    return pl.pallas_call(
        functools.partial(_outproj_body, final=final, tile0=tile0),
        out_shape=SDS((ntiles * TM, D_MODEL), F32),
        grid=(ntiles,),
        in_specs=[row(D_MODEL),
                  BS((1, 1, 3 * D_MODEL), lambda i: (8 * l + _cond_index(i + tile0), 0, 0)),
                  ctx_row, lat_row, row(BR), row(BR), gate(1), row(BR), row(BR), gate(2), ctx_row, lat_row,
                  _layer(l, 1, BR), _layer(l, BR, BR), _layer(l, 1, BR), _layer(l, 1, BR),
                  _layer(l, D_MODEL, D_MODEL), full(1, D_MODEL)],
        out_specs=BS((TM, D_MODEL), lambda i: (i, 0)),
        compiler_params=_cparams("parallel"),
        name="outproj",
    )(x, mod, yp_ctx, yp_lat, dn_of, dn_ob, gates, s5_y, s5_u, gates, yf_ctx, yf_lat, s5_d, glu_w, glu_b, dn_g, w_out,
      final_g)


def _permute_w_in(w_in):
    cols = lambda lo, hi: w_in[..., lo:hi]
    data = [cols(0, 256), cols(512, 1280), cols(1552, 1808), cols(2064, 2320)]
    gates = [cols(256, 512), cols(1280, 1536), cols(1808, 2064), cols(2320, 2576)]
    ba = jnp.pad(cols(1536, 1552), ((0, 0), (0, 0), (0, 112)))
    return jnp.concatenate(data + gates + [ba], axis=-1).astype(BF16)


def kernel(x_prompt, x_sample, c, state_delta, state_s5, c_ctx, w_ada, b_ada, norm_g, w_in, pool_w, pool_scale,
           dn_conv, dn_a_log, dn_dt_bias, dn_norm_g, s5_a_re, s5_a_im, s5_log_dt, s5_b_re, s5_b_im, s5_c_re,
           s5_c_im, s5_d, s5_glu_w, s5_glu_b, ft_w, w_out, final_g):
    x = jnp.concatenate([x_prompt.astype(F32).reshape(N_CTX, D_MODEL),
                         x_sample.astype(F32).reshape(N_LAT, D_MODEL)], axis=0)
    cond8 = jnp.concatenate([c_ctx.astype(F32)[None], c.astype(F32),
                             jnp.zeros((8 - N_COND, D_MODEL), F32)], axis=0)
    ada = ada_all(cond8, w_ada, b_ada)
    pm_ctx, inv_ctx, pm_lat, inv_lat = _pool_constants()
    fpos, fch, g1, h2, fch2 = _ft_constants()
    s5_mt, s5_q4, s5_rt, s5_al = s5_prep(s5_a_re, s5_a_im, s5_log_dt, s5_b_re, s5_b_im, s5_c_re, s5_c_im)
    mod = ada.reshape(DEPTH * 8, 1, 3 * D_MODEL)
    norm_g3 = norm_g.reshape(DEPTH, 1, D_MODEL)
    w_in_p = _permute_w_in(w_in)
    eye_g = jnp.eye(len(POOL_WINDOWS), dtype=F32)
    w_bd = jnp.einsum('lgcd,gh->lgchd', pool_w, eye_g).reshape(DEPTH, BR, BR).astype(BF16)
    sc = pool_scale.reshape(DEPTH, 1, BR)
    ftw = ft_w.astype(BF16)
    conv_w = jnp.pad(dn_conv, ((0, 0), (0, 8 - CONV_K), (0, 0)))
    gate_p = jnp.zeros((DEPTH, 8, 128), F32)
    gate_p = gate_p.at[:, 0, 8:16].set(dn_a_log.reshape(DEPTH, 8)).at[:, 1, 8:16].set(dn_dt_bias.reshape(DEPTH, 8))
    out_params = (s5_d.reshape(DEPTH, 1, BR), s5_glu_w.astype(BF16), s5_glu_b.reshape(DEPTH, 1, BR),
                  jnp.tile(dn_norm_g, (1, DN_HEADS)).reshape(DEPTH, 1, BR), w_out.astype(BF16),
                  final_g.reshape(1, D_MODEL))
    s5_s0 = state_s5.astype(F32).transpose(1, 4, 3, 0, 2, 5).reshape(DEPTH, S5_LT, DEC_BATCH, 2 * S5_N)
    eye_h = jnp.eye(DN_HEADS, dtype=F32)
    dn_s0 = jnp.einsum('bldhkv,hg->ldbhkgv', state_delta.astype(F32), eye_h).reshape(DEPTH, 2 * DEC_BATCH, BR, BR)
    new_dn, new_s5 = [], []
    for l in range(DEPTH):
        pool_u, qkv, s5_u, ft_u, gates, ba = inproj(x, mod, norm_g3, w_in_p, l)

        yp_ctx = pool_branch(pool_u, gates, pm_ctx, inv_ctx, w_bd, sc, False, l)
        yp_lat = pool_branch(pool_u, gates, pm_lat, inv_lat, w_bd, sc, True, l)

        yf_ctx = ft_ctx(ft_u, gates, fpos, fch, ftw, l)
        yf_lat = ft_lat(ft_u, gates, g1, h2, fch2, ftw, l)

        s5_y, fin_s5 = s5_branch(s5_u, s5_mt, s5_q4, s5_rt, s5_al, s5_s0, l)
        new_s5.append(fin_s5)

        dn_of, dn_ob, fin_f, fin_b = dn_branch(qkv, ba, conv_w, gate_p, dn_s0, l)
        new_dn.append(jnp.stack([fin_f, fin_b]))

        finish = functools.partial(outproj, x, mod, yp_ctx, yp_lat, dn_of, dn_ob, gates, s5_y, s5_u, yf_ctx, yf_lat,
                                   *out_params, l)
        if l < DEPTH - 1:
            x = finish(False)
        else:
            tiles_ctx = N_CTX // TM
            y_ctx = finish(True, 0, tiles_ctx)
            y_lat = finish(True, tiles_ctx, N_LAT // TM)

    y_prompt = y_ctx.reshape(BATCH, SEQ, D_MODEL).astype(x_prompt.dtype)
    y_sample = y_lat.reshape(DEC_BATCH, DEC_SEQ, D_MODEL).astype(x_sample.dtype)
    new_state_delta = jnp.stack(new_dn).transpose(2, 0, 1, 3, 4, 5).astype(state_delta.dtype)
    fin = jnp.stack(new_s5).reshape(DEPTH, S5_G, 2, BATCH, 2, S5_N)
    new_state_s5 = fin.transpose(3, 0, 4, 2, 1, 5).astype(state_s5.dtype)
    return (y_prompt, y_sample, new_state_delta, new_state_s5)
```

```python
import functools
import math

import numpy as np
import jax
import jax.numpy as jnp
from jax import lax
from jax.experimental import pallas as pl
from jax.experimental.pallas import tpu as pltpu

F32 = jnp.float32
BF16 = jnp.bfloat16

D_MODEL = 1024
BATCH = 16
SEQ = 256
DEPTH = 4
DEC_BATCH = 2
DEC_SEQ = 4096
GRID_W = 64
GRID_H = DEC_SEQ // GRID_W
BR = 256
POOL_WINDOWS = (2, 4, 8, 16)
POOL_GD = 64
DN_HEADS = 4
DN_HD = 64
CONV_K = 5
CHUNK = 64
S5_P = 16
S5_G = 16
S5_N = 64
S5_L = 16
FT_HD = 64
EPS = 1e-6

N_CTX = BATCH * SEQ
N_LAT = DEC_BATCH * DEC_SEQ
N_TOK = N_CTX + N_LAT
N_COND = 1 + DEC_BATCH
TM = 512
W_IN_COLS = 2688
VMEM_LIMIT = 56 * 1024 * 1024

SDS = jax.ShapeDtypeStruct
BS = pl.BlockSpec


def _cparams(*sem):
    return pltpu.CompilerParams(dimension_semantics=sem, vmem_limit_bytes=VMEM_LIMIT)


def bdot(a, b):
    return jnp.dot(a.astype(BF16), b.astype(BF16), preferred_element_type=F32)


def hdot(a, b):
    return jnp.dot(a, b, preferred_element_type=F32, precision=lax.Precision.HIGHEST)


def silu(x):
    return x * jax.nn.sigmoid(x)


def _cond_index(i):
    tiles_ctx = N_CTX // TM
    tiles_seq = DEC_SEQ // TM
    return jnp.where(i < tiles_ctx, 0, 1 + (i - tiles_ctx) // tiles_seq)


def _ada_body(c_ref, w_ref, b_ref, o_ref):
    o_ref[0] = hdot(silu(c_ref[...]), w_ref[0]) + b_ref[0]


def ada_all(cond8, w_ada, b_ada):
    tn = 512
    return pl.pallas_call(
        _ada_body,
        out_shape=SDS((DEPTH, 8, 3 * D_MODEL), F32),
        grid=(DEPTH, 3 * D_MODEL // tn),
        in_specs=[BS((8, D_MODEL), lambda l, j: (0, 0)),
                  BS((1, D_MODEL, tn), lambda l, j: (l, 0, j)),
                  BS((1, 1, tn), lambda l, j: (l, 0, j))],
        out_specs=BS((1, 8, tn), lambda l, j: (l, 0, j)),
        compiler_params=_cparams("parallel", "parallel"),
        name="ada",
    )(cond8, w_ada, b_ada.reshape(DEPTH, 1, 3 * D_MODEL))


X_SLOTS = 3


def _inproj_body(x_hbm, mod_ref, g_ref, w_ref, pool_ref, qkv_ref, s5_ref, ft_ref, z_ref, ba_ref, xbuf_ref, sem):
    s = pl.program_id(0)

    def tile_copy(t):
        slot = t % X_SLOTS
        return pltpu.make_async_copy(x_hbm.at[pl.ds(t * TM, TM)], xbuf_ref.at[slot], sem.at[slot])

    @pl.when(s == 0)
    def _():
        for t in range(X_SLOTS - 1):
            tile_copy(t).start()

    @pl.when(s + X_SLOTS - 1 < pl.num_programs(0))
    def _():
        tile_copy(s + X_SLOTS - 1).start()

    tile_copy(s).wait()
    x = xbuf_ref[s % X_SLOTS]
    m = mod_ref[0]
    shift = m[:, 0:D_MODEL]
    scale = m[:, D_MODEL:2 * D_MODEL]
    xn = x * lax.rsqrt(jnp.mean(x * x, axis=-1, keepdims=True) + EPS) * g_ref[...]
    h = (xn * (1.0 + scale) + shift).astype(BF16)

    def proj(lo, hi):
        return jnp.dot(h, w_ref[:, lo:hi], preferred_element_type=F32)

    pool_ref[...] = proj(0, 256)
    qkv_ref[...] = proj(256, 1024)
    s5_ref[...] = proj(1024, 1280)
    ft_ref[...] = proj(1280, 1536).astype(BF16)
    z_ref[...] = proj(1536, 2560).astype(BF16)
    ba_ref[...] = proj(2560, 2688)


def _layer(l, *block):
    zeros = (0,) * len(block)
    return BS((None,) + block, lambda *_: (l,) + zeros)


def inproj(x, mod, norm_g, w_in_p, l):
    widths = (256, 768, 256, 256, 1024, 128)
    dtypes = (F32, F32, F32, BF16, BF16, F32)
    return pl.pallas_call(
        _inproj_body,
        out_shape=[SDS((N_TOK, w), dt) for w, dt in zip(widths, dtypes)],
        grid=(N_TOK // TM,),
        in_specs=[BS(memory_space=pl.ANY),
                  BS((1, 1, 3 * D_MODEL), lambda i: (8 * l + _cond_index(i), 0, 0)),
                  _layer(l, 1, D_MODEL),
                  _layer(l, D_MODEL, W_IN_COLS)],
        out_specs=[BS((TM, w), lambda i: (i, 0)) for w in widths],
        scratch_shapes=[pltpu.VMEM((X_SLOTS, TM, D_MODEL), F32), pltpu.SemaphoreType.DMA((X_SLOTS,))],
        compiler_params=_cparams("arbitrary"),
        name="inproj",
    )(x, mod, norm_g, w_in_p)


def _pool_body(u_ref, z_ref, pm_ref, inv_ref, w_ref, sc_ref, o_ref, *scratch, two_d):
    nblk = u_ref.shape[0] // 256
    if two_d:
        pad_ref, v_ref = scratch
        halo = 8 * GRID_W
        pad_ref[0:halo, :] = jnp.zeros((halo, BR), F32)
        pad_ref[halo + DEC_SEQ:2 * halo + DEC_SEQ, :] = jnp.zeros((halo, BR), F32)
        pad_ref[halo:halo + DEC_SEQ, :] = u_ref[...]
        lane = lax.broadcasted_iota(jnp.int32, (GRID_W, 128), 1)

        def row_body(r, c):
            base = pl.multiple_of(r * GRID_W, GRID_W)

            def slab(d, lo):
                return pad_ref[pl.ds(base + (8 + d) * GRID_W, GRID_W), lo:lo + 128]

            s2 = slab(-1, 0) + slab(0, 0)
            s4 = s2 + slab(-2, 0) + slab(1, 0)
            v_ref[pl.ds(base, GRID_W), 0:128] = jnp.where(lane < 64, s2, s4)
            s8 = slab(-4, 128)
            for d in (-3, -2, -1, 0, 1, 2, 3):
                s8 = s8 + slab(d, 128)
            s16 = s8
            for d in (-8, -7, -6, -5, 4, 5, 6, 7):
                s16 = s16 + slab(d, 128)
            v_ref[pl.ds(base, GRID_W), 128:256] = jnp.where(lane < 64, s8, s16)
            return c

        lax.fori_loop(0, GRID_H, row_body, 0)
        src = v_ref
    else:
        src = u_ref
    grp = lax.broadcasted_iota(jnp.int32, (256, BR), 1) // POOL_GD

    def blk_body(b, c):
        r0 = pl.multiple_of(b * 256, 256)
        vb = src[pl.ds(r0, 256), :]
        hi = vb.astype(BF16)
        lo = (vb - hi.astype(F32)).astype(BF16)
        res = jnp.zeros((256, BR), F32)
        for g in range(len(POOL_WINDOWS)):
            pg = (jnp.dot(pm_ref[g], hi, preferred_element_type=F32)
                  + jnp.dot(pm_ref[g], lo, preferred_element_type=F32))
            res = jnp.where(grp == g, pg, res)
        pooled = res * inv_ref[pl.ds(r0, 256), :]
        d = pooled - u_ref[pl.ds(r0, 256), :]
        y = bdot(d, w_ref[...]) * sc_ref[...]
        o_ref[pl.ds(r0, 256), :] = (y * silu(z_ref[pl.ds(r0, 256), :].astype(F32))).astype(BF16)
        return c

    lax.fori_loop(0, nblk, blk_body, 0)


def _band_matrices(seg):
    t = np.arange(256)
    out = []
    for w in POOL_WINDOWS:
        lo = t - w // 2
        hi = t - w // 2 + w
        s = t[None, :]
        m = (s >= lo[:, None]) & (s < hi[:, None]) & ((s // seg) == (t[:, None] // seg))
        out.append(m.astype(np.float32))
    return np.stack(out)


def _counts(length, w):
    pos = np.arange(length)
    return (np.clip(pos - w // 2 + w, 0, length) - np.clip(pos - w // 2, 0, length)).astype(np.float64)


def _pool_constants():
    inv_ctx = np.concatenate([np.repeat((1.0 / _counts(SEQ, w))[:, None], POOL_GD, 1) for w in POOL_WINDOWS], 1)
    inv_lat = []
    for w in POOL_WINDOWS:
        c2 = np.outer(_counts(GRID_H, w), _counts(GRID_W, w)).reshape(DEC_SEQ)
        inv_lat.append(np.repeat((1.0 / c2)[:, None], POOL_GD, 1))
    inv_lat = np.concatenate(inv_lat, 1)
    return (jnp.asarray(_band_matrices(SEQ), BF16), jnp.asarray(inv_ctx, F32),
            jnp.asarray(_band_matrices(GRID_W), BF16), jnp.asarray(inv_lat, F32))


def pool_branch(pool_u, gates, pm, inv, w_bd, scale, two_d, l):
    if two_d:
        rows, nseq, blk0 = DEC_SEQ, DEC_BATCH, N_CTX // DEC_SEQ
        scratch = [pltpu.VMEM((DEC_SEQ + 16 * GRID_W, BR), F32), pltpu.VMEM((DEC_SEQ, BR), F32)]
    else:
        rows, nseq, blk0 = SEQ, BATCH, 0
        scratch = []
    return pl.pallas_call(
        functools.partial(_pool_body, two_d=two_d),
        out_shape=SDS((nseq * rows, BR), BF16),
        grid=(nseq,),
        in_specs=[BS((rows, BR), lambda i: (blk0 + i, 0)),
                  BS((rows, BR), lambda i: (blk0 + i, 0)),
                  BS((4, 256, 256), lambda i: (0, 0, 0)),
                  BS((rows, BR), lambda i: (0, 0)),
                  _layer(l, BR, BR),
                  _layer(l, 1, BR)],
        out_specs=BS((rows, BR), lambda i: (i, 0)),
        scratch_shapes=scratch,
        compiler_params=_cparams("parallel"),
        name="pool2d" if two_d else "pool1d",
    )(pool_u, gates, pm, inv, w_bd, scale)


def _ft_ctx_body(u_ref, z_ref, fpos_ref, fch_ref, w_ref, o_ref):
    uc = bdot(u_ref[...], fch_ref[...])
    st = jnp.concatenate([uc[:, 0:BR], uc[:, BR:2 * BR]], axis=0)
    f = bdot(fpos_ref[...], st)
    o_ref[...] = (bdot(f, w_ref[...]) * silu(z_ref[...].astype(F32))).astype(BF16)


def ft_ctx(ft_u, gates, fpos, fch, ft_w, l):
    return pl.pallas_call(
        _ft_ctx_body,
        out_shape=SDS((N_CTX, BR), BF16),
        grid=(BATCH,),
        in_specs=[BS((SEQ, BR), lambda i: (i, 0)),
                  BS((SEQ, BR), lambda i: (i, 3)),
                  BS((SEQ, 2 * SEQ), lambda i: (0, 0)),
                  BS((BR, 2 * BR), lambda i: (0, 0)),
                  _layer(l, BR, BR)],
        out_specs=BS((SEQ, BR), lambda i: (i, 0)),
        compiler_params=_cparams("parallel"),
        name="ft_ctx",
    )(ft_u, gates, fpos, fch, ft_w)


def _ft_lat_body(u_ref, z_ref, g_ref, h_ref, fch_ref, w_ref, o_ref, x_ref, yr_ref, yi_ref):
    for hf in range(2):
        x_ref[hf] = u_ref[:, hf * 128:(hf + 1) * 128].astype(F32)

    def stage1(t2, c):
        xs = jnp.concatenate([x_ref[hf, pl.ds(t2, GRID_H, stride=GRID_W), :] for hf in range(2)], axis=1)
        y = jnp.dot(g_ref[t2], xs.astype(BF16), preferred_element_type=F32)
        r0 = pl.multiple_of(t2 * GRID_W, GRID_W)
        for hf in range(2):
            yr_ref[hf, pl.ds(r0, GRID_W), :] = y[0:64, hf * 128:(hf + 1) * 128]
            yi_ref[hf, pl.ds(r0, GRID_W), :] = y[64:128, hf * 128:(hf + 1) * 128]
        return c

    lax.fori_loop(0, GRID_W, stage1, 0, unroll=8)

    def stage2(kb, c):
        yr = jnp.concatenate([yr_ref[hf, pl.ds(kb, GRID_W, stride=GRID_W), :] for hf in range(2)], axis=1)
        yi = jnp.concatenate([yi_ref[hf, pl.ds(kb, GRID_W, stride=GRID_W), :] for hf in range(2)], axis=1)
        st = jnp.concatenate([yr, yi], axis=0).astype(BF16)
        a = jnp.dot(h_ref[...], st, preferred_element_type=F32)
        for hf in range(2):
            yr_ref[hf, pl.ds(kb, GRID_W, stride=GRID_W), :] = a[0:64, hf * 128:(hf + 1) * 128]
            yi_ref[hf, pl.ds(kb, GRID_W, stride=GRID_W), :] = a[64:128, hf * 128:(hf + 1) * 128]
        return c

    lax.fori_loop(0, GRID_W, stage2, 0, unroll=8)

    def stage3(b, c):
        r0 = pl.multiple_of(b * TM, TM)
        ar = jnp.concatenate([yr_ref[hf, pl.ds(r0, TM), :] for hf in range(2)], axis=1)
        ai = jnp.concatenate([yi_ref[hf, pl.ds(r0, TM), :] for hf in range(2)], axis=1)
        f = bdot(ar, fch_ref[0:BR, :]) + bdot(ai, fch_ref[BR:2 * BR, :])
        o_ref[pl.ds(r0, TM), :] = (bdot(f, w_ref[...]) * silu(z_ref[pl.ds(r0, TM), :].astype(F32))).astype(BF16)
        return c

    lax.fori_loop(0, DEC_SEQ // TM, stage3, 0)


def ft_lat(ft_u, gates, g1, h2, fch2, ft_w, l):
    blk0 = N_CTX // DEC_SEQ
    return pl.pallas_call(
        _ft_lat_body,
        out_shape=SDS((N_LAT, BR), BF16),
        grid=(DEC_BATCH,),
        in_specs=[BS((DEC_SEQ, BR), lambda i: (blk0 + i, 0)),
                  BS((DEC_SEQ, BR), lambda i: (blk0 + i, 3)),
                  BS((GRID_W, 128, GRID_H), lambda i: (0, 0, 0)),
                  BS((128, 128), lambda i: (0, 0)),
                  BS((2 * BR, BR), lambda i: (0, 0)),
                  _layer(l, BR, BR)],
        out_specs=BS((DEC_SEQ, BR), lambda i: (i, 0)),
        scratch_shapes=[pltpu.VMEM((2, DEC_SEQ, 128), F32)] * 3,
        compiler_params=_cparams("parallel"),
        name="ft_lat",
    )(ft_u, gates, g1, h2, fch2, ft_w)


def _ft_constants():
    c = np.arange(FT_HD)
    ang = 2.0 * np.pi * np.outer(c, c) / FT_HD
    eye4 = np.eye(BR // FT_HD)
    cc = np.kron(eye4, np.cos(ang)) / 8.0
    sc = np.kron(eye4, np.sin(ang)) / 8.0
    t = np.arange(SEQ)
    angt = 2.0 * np.pi * (np.outer(t, t) % SEQ) / SEQ
    fpos = np.concatenate([np.cos(angt), -np.sin(angt)], axis=1) / 16.0
    fch = np.concatenate([cc, sc], axis=1)
    kb = np.arange(GRID_W)[None, :, None]
    t1 = np.arange(GRID_H)[None, None, :]
    t2 = np.arange(GRID_W)[:, None, None]
    a1 = 2.0 * np.pi * ((kb * (GRID_W * t1 + t2)) % DEC_SEQ) / DEC_SEQ
    g1 = np.concatenate([np.cos(a1), -np.sin(a1)], axis=1) / 8.0
    a2 = 2.0 * np.pi * (np.outer(np.arange(GRID_W), np.arange(GRID_W)) % GRID_W) / GRID_W
    c2, s2 = np.cos(a2) / 8.0, np.sin(a2) / 8.0
    h2 = np.block([[c2, s2], [-s2, c2]])
    fch2 = np.concatenate([cc, sc], axis=0)
    as_bf = lambda a: jnp.asarray(a, F32).astype(BF16)
    return as_bf(fpos), as_bf(fch), as_bf(g1), as_bf(h2), as_bf(fch2)


S5_ROWS = N_TOK // S5_L
S5_TM = 1024
S5_TR = S5_TM // S5_L
S5_CTX_C = SEQ // S5_L
S5_LAT_C = DEC_SEQ // S5_L
S5_LT = 2 * S5_G


S5_PG = 4


def _s5_prep_body(*refs):
    for gi in range(S5_PG):
        _s5_prep_group(gi, *refs)


def _s5_prep_group(gi, ar_ref, ai_ref, ldt_ref, br_ref, bi_ref, btr_ref, bti_ref, cr_ref, ci_ref,
                   mt_ref, q4_ref, rt_ref, al_ref):
    L = S5_L
    m = lax.broadcasted_iota(jnp.int32, (2 * L, S5_N), 0).astype(F32)
    rts, qs, ds, als = [], [], [], []
    for d in range(2):
        a_re, a_im = ar_ref[0, d, gi], ai_ref[0, d, gi]
        dt = jnp.exp(ldt_ref[0, d, gi])
        xr, xi = a_re * dt, a_im * dt
        mag = jnp.exp(xr)
        ab_re, ab_im = mag * jnp.cos(xi), mag * jnp.sin(xi)
        den = a_re * a_re + a_im * a_im
        nr = ab_re - 1.0
        coef_re = (nr * a_re + ab_im * a_im) / den
        coef_im = (ab_im * a_re - nr * a_im) / den
        pw_re = jnp.exp(m * xr) * jnp.cos(m * xi)
        pw_im = jnp.exp(m * xr) * jnp.sin(m * xi)
        cq_re = pw_re * coef_re - pw_im * coef_im
        cq_im = pw_re * coef_im + pw_im * coef_re
        c_re, c_im = cr_ref[0, d, gi], ci_ref[0, d, gi]
        bt_re, bt_im = btr_ref[0, d, gi], bti_ref[0, d, gi]
        row = lambda x, e: x[e:e + 1, :]
        order = range(L) if d == 0 else range(L - 1, -1, -1)
        cp_re = jnp.concatenate([c_re * row(cq_re, e) - c_im * row(cq_im, e) for e in order], axis=0)
        cp_im = jnp.concatenate([c_re * row(cq_im, e) + c_im * row(cq_re, e) for e in order], axis=0)
        rts.append(hdot(cp_re, br_ref[0, d, gi]) - hdot(cp_im, bi_ref[0, d, gi]))
        inj = [L - 1 - i for i in range(L)] if d == 0 else list(range(L))
        q_re = jnp.concatenate([bt_re * row(cq_re, e) - bt_im * row(cq_im, e) for e in inj], axis=0)
        q_im = jnp.concatenate([bt_im * row(cq_re, e) + bt_re * row(cq_im, e) for e in inj], axis=0)
        qs.append((q_re, q_im))
        out = [j + 1 for j in range(L)] if d == 0 else [L - j for j in range(L)]
        d_re = jnp.concatenate([c_re * row(pw_re, e) - c_im * row(pw_im, e) for e in out], axis=0)
        d_im = jnp.concatenate([c_re * row(pw_im, e) + c_im * row(pw_re, e) for e in out], axis=0)
        ds.append((d_re, d_im))
        als.append((row(pw_re, L), row(pw_im, L)))
    pad = jnp.zeros(((L - 1) * S5_P, S5_P), F32)
    z = jnp.concatenate([pad, rts[0]], axis=0) + jnp.concatenate([rts[1], pad], axis=0)
    mt = jnp.concatenate([z[(L - 1 - i) * S5_P:(L - 1 - i) * S5_P + L * S5_P, :] for i in range(L)], axis=1)
    mt_ref[0, gi] = mt.astype(BF16)
    q4_ref[0, gi] = jnp.concatenate([qs[0][0], qs[1][0], qs[0][1], qs[1][1]], axis=1).astype(BF16)
    rt_ref[0, gi] = jnp.concatenate([ds[0][0], ds[1][0], -ds[0][1], -ds[1][1]], axis=1).astype(BF16)
    al_ref[0, gi] = jnp.concatenate([jnp.concatenate([als[0][0], als[1][0]], axis=1),
                                    jnp.concatenate([als[0][1], als[1][1]], axis=1)], axis=0)


def s5_prep(a_re, a_im, log_dt, b_re, b_im, c_re, c_im):
    vec = lambda x: x.reshape(DEPTH, 2, S5_G, 1, S5_N)
    ldt = jnp.broadcast_to(log_dt[..., None, None], (DEPTH, 2, S5_G, 1, S5_N))
    bt = lambda x: x.transpose(0, 1, 2, 4, 3)
    per = S5_G // S5_PG
    vspec = BS((1, 2, S5_PG, 1, S5_N), lambda i: (i // per, 0, i % per, 0, 0))
    bspec = BS((1, 2, S5_PG, S5_N, S5_P), lambda i: (i // per, 0, i % per, 0, 0))
    cspec = BS((1, 2, S5_PG, S5_P, S5_N), lambda i: (i // per, 0, i % per, 0, 0))
    mat = SDS((DEPTH, S5_G, 256, 256), BF16)
    mspec = BS((1, S5_PG, 256, 256), lambda i: (i // per, i % per, 0, 0))
    return pl.pallas_call(
        _s5_prep_body,
        out_shape=[mat, mat, mat, SDS((DEPTH, S5_G, 2, 128), F32)],
        grid=(DEPTH * per,),
        in_specs=[vspec, vspec, vspec, bspec, bspec, cspec, cspec, cspec, cspec],
        out_specs=[mspec, mspec, mspec, BS((1, S5_PG, 2, 128), lambda i: (i // per, i % per, 0, 0))],
        compiler_params=_cparams("parallel"),
        name="s5_prep",
    )(vec(a_re), vec(a_im), ldt, b_re, b_im, bt(b_re), bt(b_im), c_re, c_im)


def _block_transpose(arrs):
    blk = lax.broadcasted_iota(jnp.int32, arrs[0].shape, 1) // 16
    cur = list(arrs)
    for b in range(4):
        s = 16 << b
        hi = ((blk >> b) & 1) == 1
        nxt = list(cur)
        for x in range(16):
            if (x >> b) & 1:
                continue
            y = x | (1 << b)
            nxt[x] = jnp.where(hi, pltpu.roll(cur[y], s, 1), cur[x])
            nxt[y] = jnp.where(hi, cur[y], pltpu.roll(cur[x], 256 - s, 1))
        cur = nxt
    return cur


def _s5_in_body(s5_ref, mt_ref, q4_ref, y_ref, e_ref, x_ref):
    for hf in range(2):
        x_ref[hf] = s5_ref[:, hf * 128:(hf + 1) * 128]
    xs = [jnp.concatenate([x_ref[hf, pl.ds(i, S5_TR, stride=S5_L), :] for hf in range(2)], axis=1)
          for i in range(S5_L)]
    us = _block_transpose(xs)
    for g in range(S5_G):
        ub = us[g].astype(BF16)
        y_ref[:, g * 256:(g + 1) * 256] = lax.dot_general(ub, mt_ref[g], (((1,), (1,)), ((), ())),
                                                          preferred_element_type=F32)
        e = jnp.dot(ub, q4_ref[g], preferred_element_type=F32)
        e_ref[2 * g] = e[:, 0:128]
        e_ref[2 * g + 1] = e[:, 128:256]


def s5_in(s5, mt, q4, l):
    wspec = _layer(l, S5_G, 256, 256)
    return pl.pallas_call(
        _s5_in_body,
        out_shape=[SDS((S5_ROWS, S5_G * 256), F32), SDS((S5_LT, S5_ROWS, 128), F32)],
        grid=(N_TOK // S5_TM,),
        in_specs=[BS((S5_TM, BR), lambda t: (t, 0)), wspec, wspec],
        out_specs=[BS((S5_TR, S5_G * 256), lambda t: (t, 0)), BS((S5_LT, S5_TR, 128), lambda t: (0, t, 0))],
        scratch_shapes=[pltpu.VMEM((2, S5_TM, 128), F32)],
        compiler_params=_cparams("parallel"),
        name="s5_in",
    )(s5, mt, q4)


def _s5_chunk_scan_body(e_ref, al_ref, s0_ref, spf_ref, spb_ref, fin_ref):
    def update(g, s_re, s_im, e_re, e_im):
        a_re, a_im = al_ref[g, 0:1, :], al_ref[g, 1:2, :]
        return a_re * s_re - a_im * s_im + e_re, a_re * s_im + a_im * s_re + e_im

    fwd_c = lax.broadcasted_iota(jnp.int32, (BATCH, 128), 1) < S5_N

    def ctx_group(g, carry):
        s_re = jnp.zeros((BATCH, 128), F32)
        s_im = jnp.zeros((BATCH, 128), F32)
        for c in range(S5_CTX_C):
            rf = pl.ds(c, BATCH, stride=S5_CTX_C)
            rb = pl.ds(S5_CTX_C - 1 - c, BATCH, stride=S5_CTX_C)
            spf_ref[2 * g, rf, :] = s_re
            spf_ref[2 * g + 1, rf, :] = s_im
            spb_ref[2 * g, rb, :] = s_re
            spb_ref[2 * g + 1, rb, :] = s_im
            e_re = jnp.where(fwd_c, e_ref[2 * g, rf, :], e_ref[2 * g, rb, :])
            e_im = jnp.where(fwd_c, e_ref[2 * g + 1, rf, :], e_ref[2 * g + 1, rb, :])
            s_re, s_im = update(g, s_re, s_im, e_re, e_im)
        fin_ref[2 * g] = s_re
        fin_ref[2 * g + 1] = s_im
        return carry

    lax.fori_loop(0, S5_GQ, ctx_group, 0)

    row0 = BATCH * S5_CTX_C
    fwd_l = lax.broadcasted_iota(jnp.int32, (DEC_BATCH, 128), 1) < S5_N

    def lat_step(c, state):
        rf = pl.ds(row0 + c, DEC_BATCH, stride=S5_LAT_C)
        rb = pl.ds(row0 + S5_LAT_C - 1 - c, DEC_BATCH, stride=S5_LAT_C)
        new = []
        for g in range(S5_GQ):
            s_re, s_im = state[2 * g], state[2 * g + 1]
            spf_ref[2 * g, rf, :] = s_re
            spf_ref[2 * g + 1, rf, :] = s_im
            spb_ref[2 * g, rb, :] = s_re
            spb_ref[2 * g + 1, rb, :] = s_im
            e_re = jnp.where(fwd_l, e_ref[2 * g, rf, :], e_ref[2 * g, rb, :])
            e_im = jnp.where(fwd_l, e_ref[2 * g + 1, rf, :], e_ref[2 * g + 1, rb, :])
            new.extend(update(g, s_re, s_im, e_re, e_im))
        return tuple(new)

    lax.fori_loop(0, S5_LAT_C, lat_step, tuple(s0_ref[t] for t in range(2 * S5_GQ)))


S5_GQ = 4


def s5_chunk_scan(e3, al, s0_lat, l):
    sp = SDS((S5_LT, S5_ROWS, 128), F32)
    tiles = lambda rows: BS((2 * S5_GQ, rows, 128), lambda q: (q, 0, 0))
    return pl.pallas_call(
        _s5_chunk_scan_body,
        out_shape=[sp, sp, SDS((S5_LT, BATCH, 128), F32)],
        grid=(S5_G // S5_GQ,),
        in_specs=[tiles(S5_ROWS), BS((None, S5_GQ, 2, 128), lambda q: (l, q, 0, 0)),
                  BS((None, 2 * S5_GQ, DEC_BATCH, 128), lambda q: (l, q, 0, 0))],
        out_specs=[tiles(S5_ROWS), tiles(S5_ROWS), tiles(BATCH)],
        compiler_params=_cparams("parallel"),
        name="s5_chunk_scan",
    )(e3, al, s0_lat)


def _s5_fin_body(y_ref, spf_ref, spb_ref, rt_ref, o_ref, t_ref):
    fwd = lax.broadcasted_iota(jnp.int32, (S5_TR, 128), 1) < S5_N
    ys = []
    for g in range(S5_G):
        s_re = jnp.where(fwd, spf_ref[2 * g], spb_ref[2 * g])
        s_im = jnp.where(fwd, spf_ref[2 * g + 1], spb_ref[2 * g + 1])
        sp = jnp.concatenate([s_re, s_im], axis=1).astype(BF16)
        ys.append(y_ref[:, g * 256:(g + 1) * 256]
                  + lax.dot_general(sp, rt_ref[g], (((1,), (1,)), ((), ())), preferred_element_type=F32))
    xs = _block_transpose(ys)
    for j in range(S5_L):
        for hf in range(2):
            t_ref[hf, pl.ds(j, S5_TR, stride=S5_L), :] = xs[j][:, hf * 128:(hf + 1) * 128]
    o_ref[...] = jnp.concatenate([t_ref[0], t_ref[1]], axis=1)


def s5_fin(yi, spf, spb, rt, l):
    lt = BS((S5_LT, S5_TR, 128), lambda t: (0, t, 0))
    return pl.pallas_call(
        _s5_fin_body,
        out_shape=SDS((N_TOK, BR), F32),
        grid=(N_TOK // S5_TM,),
        in_specs=[BS((S5_TR, S5_G * 256), lambda t: (t, 0)), lt, lt, _layer(l, S5_G, 256, 256)],
        out_specs=BS((S5_TM, BR), lambda t: (t, 0)),
        scratch_shapes=[pltpu.VMEM((2, S5_TM, 128), F32)],
        compiler_params=_cparams("parallel"),
        name="s5_fin",
    )(yi, spf, spb, rt)


def s5_branch(s5, mt, q4, rt, al, s0, l):
    yi, e3 = s5_in(s5, mt, q4, l)
    spf, spb, fin = s5_chunk_scan(e3, al, s0, l)
    return s5_fin(yi, spf, spb, rt, l), fin


DN_TILE = 256
DN_HALO = 8


N_SEG = N_TOK // DN_TILE
assert N_CTX == DEC_SEQ


def _split3(x):
    x1 = x.astype(BF16)
    r1 = x - x1.astype(F32)
    x2 = r1.astype(BF16)
    return x1, x2, (r1 - x2.astype(F32)).astype(BF16)


def dot3(a, b, exact):
    if exact == 'b':
        return sum(jnp.dot(p, b.astype(BF16), preferred_element_type=F32) for p in _split3(a))
    return sum(jnp.dot(a.astype(BF16), p, preferred_element_type=F32) for p in _split3(b))


def _dn_conv_body(x_ref, prev_ref, next_ref, ba_ref, w_ref, gp_ref, ex_ref, o_ref, gb_ref, pad_ref):
    i = pl.program_id(0)
    tiles_ctx = N_CTX // DN_TILE
    tiles_seq = DEC_SEQ // DN_TILE
    j = (i - tiles_ctx) % tiles_seq
    first = jnp.logical_or(i < tiles_ctx, j == 0)
    last = jnp.logical_or(i < tiles_ctx, j == tiles_seq - 1)
    pad_ref[0:DN_HALO, :] = jnp.where(first, 0.0, prev_ref[...])
    pad_ref[DN_HALO:DN_HALO + DN_TILE, :] = x_ref[...]
    pad_ref[DN_HALO + DN_TILE:2 * DN_HALO + DN_TILE, :] = jnp.where(last, 0.0, next_ref[...])
    for r0 in range(0, DN_TILE, 128):
        for c0 in range(0, 3 * BR, 128):
            acc = jnp.zeros((128, 128), F32)
            for t in range(CONV_K):
                acc = acc + (pad_ref[pl.ds(r0 + DN_HALO - CONV_K // 2 + t, 128), c0:c0 + 128]
                             * w_ref[t:t + 1, c0:c0 + 128])
            o_ref[r0:r0 + 128, c0:c0 + 128] = silu(acc)
    raw = ba_ref[...]
    lane = lax.broadcasted_iota(jnp.int32, raw.shape, 1)
    xa = raw + gp_ref[1:2, :]
    sp = jnp.maximum(xa, 0.0) + jnp.log1p(jnp.exp(-jnp.abs(xa)))
    gates = jnp.where(lane < 2 * DN_HEADS, jax.nn.sigmoid(raw), -jnp.exp(gp_ref[0:1, :]) * sp)
    r = lax.broadcasted_iota(jnp.int32, (DN_TILE, DN_TILE), 0)
    c = lax.broadcasted_iota(jnp.int32, (DN_TILE, DN_TILE), 1)
    same = (r // CHUNK) == (c // CHUNK)
    ex = dot3(gates, ex_ref[...], 'b')
    gb_ref[:, 0:2 * BR] = ex[:, 0:2 * BR]
    gb_ref[:, 2 * BR:3 * BR] = dot3(jnp.logical_and(same, c <= r), ex[:, 2 * BR:3 * BR], 'a')
    gb_ref[:, 3 * BR:4 * BR] = dot3(jnp.logical_and(same, c >= r), ex[:, 3 * BR:4 * BR], 'a')


def _gate_expand():
    e = np.zeros((128, 4 * BR), np.float32)
    for blk in range(4):
        for h in range(DN_HEADS):
            e[blk * DN_HEADS + h, blk * BR + h * DN_HD:blk * BR + (h + 1) * DN_HD] = 1.0
    return jnp.asarray(e, BF16)


def _head_block_mask():
    r = lax.broadcasted_iota(jnp.int32, (BR, BR), 0) // DN_HD
    c = lax.broadcasted_iota(jnp.int32, (BR, BR), 1) // DN_HD
    return r == c


def _split2(x):
    hi = x.astype(BF16)
    return hi, (x - hi.astype(F32)).astype(BF16)


def _dn_local_body(qkv_ref, gb_ref, *out_refs):
    f_refs, b_refs = out_refs[0:6], out_refs[6:12]
    ncb = DN_TILE // CHUNK
    nb = 2 * ncb
    bmask = _head_block_mask()
    ri = lax.broadcasted_iota(jnp.int32, (nb, CHUNK, BR), 1)
    cj = lax.broadcasted_iota(jnp.int32, (nb, CHUNK, BR), 2) % DN_HD
    bwd = lax.broadcasted_iota(jnp.int32, (nb, CHUNK, BR), 0) >= ncb
    eye = cj == ri
    incl = jnp.logical_or(jnp.logical_and(bwd, cj >= ri), jnp.logical_and(jnp.logical_not(bwd), cj <= ri))
    strict = jnp.logical_and(incl, jnp.logical_not(eye))
    ones_blk = bmask.astype(BF16)

    def chunks(x):
        return x.reshape(ncb, CHUNK, BR)

    def both(x):
        return jnp.concatenate([x, x], axis=0)

    def head_sum(x):
        return dot3(x.reshape(-1, BR), ones_blk, 'b').reshape(x.shape)

    def bd(x):
        return jnp.where(bmask, jnp.concatenate([x] * DN_HEADS, axis=1), jnp.zeros((), x.dtype))

    def bmm(a, b):
        return jnp.einsum('bij,bjk->bik', a, b, preferred_element_type=F32)

    q = chunks(qkv_ref[:, 0:BR])
    k = chunks(qkv_ref[:, BR:2 * BR])
    v = both(chunks(qkv_ref[:, 2 * BR:3 * BR]))
    q = q * lax.rsqrt(head_sum(q * q) + EPS) * (DN_HD ** -0.5)
    k = k * lax.rsqrt(head_sum(k * k) + EPS)
    kq = jnp.einsum('bik,bjk->bij', jnp.concatenate([k, q], axis=1).astype(BF16), bd(k.astype(BF16)),
                    preferred_element_type=F32)
    kk, qk = both(kq[:, 0:CHUNK]), both(kq[:, CHUNK:2 * CHUNK])
    q, k = both(q), both(k)
    beta = jnp.concatenate([chunks(gb_ref[:, 0:BR]), chunks(gb_ref[:, BR:2 * BR])], axis=0)
    gc = jnp.concatenate([chunks(gb_ref[:, 2 * BR:3 * BR]), chunks(gb_ref[:, 3 * BR:4 * BR])], axis=0)
    crow = jnp.sum(jnp.where(eye, gc, 0.0), axis=1, keepdims=True)
    decay = jnp.where(incl, jnp.exp(jnp.where(incl, gc - crow, 0.0)), 0.0)
    a = jnp.where(strict, kk * decay * beta, 0.0)
    tinv = jnp.where(eye, 1.0, 0.0) - a
    pw = a
    pw_bd = bd(pw.astype(BF16))
    for _ in range(5):
        pw = bmm(pw.astype(BF16), pw_bd)
        pw_bd = bd(pw.astype(BF16))
        tinv = tinv + bmm(tinv.astype(BF16), pw_bd)
    egc = jnp.exp(gc)
    t_hi, t_lo = _split2(tinv)

    def solve(rhs):
        r_hi, r_lo = _split2(rhs)
        r_hi, r_lo = bd(r_hi), bd(r_lo)
        return bmm(t_hi, r_hi) + bmm(t_hi, r_lo) + bmm(t_lo, r_hi)

    w = bmm(t_hi, bd((k * (beta * egc)).astype(BF16)))

    bwd_row = lax.broadcasted_iota(jnp.int32, (nb, 1, BR), 0) >= ncb
    g_last = jnp.where(bwd_row, gc[:, 0:1], gc[:, CHUNK - 1:CHUNK])
    eg = jnp.exp(g_last)
    outs = (w, solve(v * beta), q * egc, k * jnp.exp(g_last - gc),
            jnp.where(incl, qk * decay, 0.0))
    for d, refs in enumerate((f_refs, b_refs)):
        for ref, x in zip(refs[0:5], outs):
            ref[...] = x[d * ncb:(d + 1) * ncb].reshape(DN_TILE, BR).astype(ref.dtype)
        refs[5][0] = jnp.concatenate([eg[d * ncb:(d + 1) * ncb, 0], jnp.zeros((8 - ncb, BR), F32)], axis=0)


_DN_LOCAL_DTYPES = (BF16, F32, BF16, BF16, BF16)


def _dn_prep_body(x_ref, prev_ref, next_ref, ba_ref, w_ref, gp_ref, ex_ref, *rest):
    out_refs, (pad_ref, qkv_ref, gb_ref) = rest[0:12], rest[12:15]
    _dn_conv_body(x_ref, prev_ref, next_ref, ba_ref, w_ref, gp_ref, ex_ref, qkv_ref, gb_ref, pad_ref)
    _dn_local_body(qkv_ref, gb_ref, *out_refs)


def dn_local(qkv, ba, conv_w, gate_p, l):
    per = DN_TILE // DN_HALO
    nhb = N_TOK // DN_HALO
    tok = BS((DN_TILE, BR), lambda i: (i, 0))
    shapes = [SDS((N_TOK, BR), dt) for dt in _DN_LOCAL_DTYPES] + [SDS((N_SEG, 8, BR), F32)]
    specs = [tok] * len(_DN_LOCAL_DTYPES) + [BS((1, 8, BR), lambda i: (i, 0, 0))]
    outs = pl.pallas_call(
        _dn_prep_body,
        out_shape=shapes * 2,
        grid=(N_SEG,),
        in_specs=[BS((DN_TILE, 3 * BR), lambda i: (i, 0)),
                  BS((DN_HALO, 3 * BR), lambda i: (jnp.maximum(i * per - 1, 0), 0)),
                  BS((DN_HALO, 3 * BR), lambda i: (jnp.minimum((i + 1) * per, nhb - 1), 0)),
                  BS((DN_TILE, 128), lambda i: (i, 0)),
                  _layer(l, 8, 3 * BR),
                  _layer(l, 8, 128),
                  BS((128, 4 * BR), lambda i: (0, 0))],
        out_specs=specs * 2,
        scratch_shapes=[pltpu.VMEM((DN_TILE + 2 * DN_HALO, 3 * BR), F32),
                        pltpu.VMEM((DN_TILE, 3 * BR), F32),
                        pltpu.VMEM((DN_TILE, 4 * BR), F32)],
        compiler_params=_cparams("parallel"),
        name="dn_local",
    )(qkv, qkv, qkv, ba, conv_w, gate_p, _gate_expand())
    return outs[0:6], outs[6:12]


def _dn_advance(chains, s, bmask):
    ncb = DN_TILE // CHUNK
    bmm = lambda a, b: jnp.einsum('bij,bjk->bik', a, b, preferred_element_type=F32)
    steps = []
    for t in range(ncb):
        cs = [ncb - 1 - t if rev else t for _, rev in chains]
        w, u, qt, kt, aqk, eg = [jnp.stack([load(k, c) for (load, _), c in zip(chains, cs)]) for k in range(6)]
        sb = s.astype(BF16)
        v_new = u - bmm(w, sb)
        vb = v_new.astype(BF16)
        v_bd = jnp.where(bmask, jnp.concatenate([vb] * DN_HEADS, axis=1), jnp.zeros((), BF16))
        o = bmm(qt, sb) + bmm(aqk, v_bd)
        upd = jnp.einsum('btk,btv->bkv', kt, vb, preferred_element_type=F32)
        s = s * eg + jnp.where(bmask, upd, 0.0)
        steps.append((cs, o))
    return s, steps


assert BATCH == DEC_SEQ // DN_TILE


def _dn_seq_body(*refs):
    f_in, b_in = refs[0:6], refs[6:12]
    s0_ref, of_ref, ob_ref, finf_ref, finb_ref, s_ref = refs[12:18]

    @pl.when(pl.program_id(0) == 0)
    def _():
        s_ref[...] = s0_ref[...]

    def loader(in_refs, q):
        def load(k, c):
            if k == 5:
                return in_refs[5][q, 0, c:c + 1, :]
            return in_refs[k][q, c * CHUNK:(c + 1) * CHUNK, :]
        return load

    groups = [1, 2, 1, 2, 0, 0]
    outs = [of_ref, of_ref, ob_ref, ob_ref, of_ref, ob_ref]
    chains = [(loader(b_in if o is ob_ref else f_in, q), o is ob_ref) for q, o in zip(groups, outs)]
    n_lat = 2 * DEC_BATCH
    s = jnp.concatenate([s_ref[...], jnp.zeros((2, BR, BR), F32)], axis=0)
    s, steps = _dn_advance(chains, s, _head_block_mask())
    s_ref[...] = s[0:n_lat]
    heads = lambda m: jnp.stack([m[h * DN_HD:(h + 1) * DN_HD, h * DN_HD:(h + 1) * DN_HD] for h in range(DN_HEADS)])
    finf_ref[0] = heads(s[n_lat])
    finb_ref[0] = heads(s[n_lat + 1])
    for cs, o in steps:
        for i, c in enumerate(cs):
            outs[i][groups[i], c * CHUNK:(c + 1) * CHUNK, :] = o[i]


def dn_seq(loc_f, loc_b, s0, l):
    nseg = DEC_SEQ // DN_TILE
    grp = lambda a: a.reshape((N_COND, nseg) + a.shape[1:])

    def specs(m):
        return ([BS((N_COND, DN_TILE, BR), lambda g: (0, m(g), 0))] * len(_DN_LOCAL_DTYPES)
                + [BS((N_COND, 1, 8, BR), lambda g: (0, m(g), 0, 0))])

    fwd = lambda g: g
    bwd = lambda g: nseg - 1 - g
    views = lambda loc: [a.reshape(N_COND, DEC_SEQ, BR) for a in loc[0:5]] + [grp(loc[5])]
    out = SDS((N_COND, DEC_SEQ, BR), F32)
    fin = SDS((BATCH, DN_HEADS, DN_HD, DN_HD), F32)
    o_f, o_b, fin_f, fin_b = pl.pallas_call(
        _dn_seq_body,
        out_shape=[out, out, fin, fin],
        grid=(nseg,),
        in_specs=specs(fwd) + specs(bwd) + [_layer(l, 2 * DEC_BATCH, BR, BR)],
        out_specs=[BS((N_COND, DN_TILE, BR), lambda g: (0, fwd(g), 0)),
                   BS((N_COND, DN_TILE, BR), lambda g: (0, bwd(g), 0)),
                   BS((1, DN_HEADS, DN_HD, DN_HD), lambda g: (fwd(g), 0, 0, 0)),
                   BS((1, DN_HEADS, DN_HD, DN_HD), lambda g: (bwd(g), 0, 0, 0))],
        scratch_shapes=[pltpu.VMEM((2 * DEC_BATCH, BR, BR), F32)],
        compiler_params=_cparams("arbitrary"),
        name="dn_seq",
    )(*views(loc_f), *views(loc_b), s0)
    return o_f.reshape(N_TOK, BR), o_b.reshape(N_TOK, BR), fin_f, fin_b


def dn_branch(qkv, ba, conv_w, gate_p, s0, l):
    loc_f, loc_b = dn_local(qkv, ba, conv_w, gate_p, l)
    return dn_seq(loc_f, loc_b, s0, l)


def _outproj_body(x_ref, mod_ref, ypc_ref, ypl_ref, dof_ref, dob_ref, dnz_ref, s5y_ref, s5u_ref, s5z_ref,
                  yfc_ref, yfl_ref, d_ref, gw_ref, gb_ref, dng_ref, w_ref, fg_ref, o_ref, *, final, tile0):
    is_ctx = pl.program_id(0) + tile0 < N_CTX // TM
    y_pool = jnp.where(is_ctx, ypc_ref[...], ypl_ref[...])
    y_ft = jnp.where(is_ctx, yfc_ref[...], yfl_ref[...])
    gate = mod_ref[0][:, 2 * D_MODEL:3 * D_MODEL]
    o = dof_ref[...] + dob_ref[...]
    head_mean = jnp.where(_head_block_mask(), 1.0 / DN_HD, 0.0)
    y_dn = o * lax.rsqrt(dot3(o * o, head_mean, 'b') + EPS) * dng_ref[...] * silu(dnz_ref[...].astype(F32))
    y = s5y_ref[...] + d_ref[...] * s5u_ref[...]
    y = jax.nn.gelu(y)
    y = y * jax.nn.sigmoid(bdot(y, gw_ref[...]) + gb_ref[...])
    y_s5 = y * silu(s5z_ref[...].astype(F32))
    acc = bdot(y_pool, w_ref[0:BR, :])
    acc = acc + bdot(y_dn, w_ref[BR:2 * BR, :])
    acc = acc + bdot(y_s5, w_ref[2 * BR:3 * BR, :])
    acc = acc + bdot(y_ft, w_ref[3 * BR:4 * BR, :])
    xn = x_ref[...] + gate * acc
    if final:
        xn = xn * lax.rsqrt(jnp.mean(xn * xn, axis=-1, keepdims=True) + EPS) * fg_ref[...]
    o_ref[...] = xn


def outproj(x, mod, yp_ctx, yp_lat, dn_of, dn_ob, gates, s5_y, s5_u, yf_ctx, yf_lat, s5_d, glu_w, glu_b, dn_g, w_out,
            final_g, l, final, tile0=0, ntiles=N_TOK // TM):
    row = lambda w: BS((TM, w), lambda i: (i + tile0, 0))
    gate = lambda c: BS((TM, BR), lambda i: (i + tile0, c))
    full = lambda a, b: BS((a, b), lambda i: (0, 0))
    tiles_ctx = N_CTX // TM
    ctx_row = BS((TM, BR), lambda i: (jnp.minimum(i + tile0, tiles_ctx - 1), 0))
    lat_row = BS((TM, BR), lambda i: (jnp.maximum(i + tile0 - tiles_ctx, 0), 0))
    return pl.pallas_call(
        functools.partial(_outproj_body, final=final, tile0=tile0),
        out_shape=SDS((ntiles * TM, D_MODEL), F32),
        grid=(ntiles,),
        in_specs=[row(D_MODEL),
                  BS((1, 1, 3 * D_MODEL), lambda i: (8 * l + _cond_index(i + tile0), 0, 0)),
                  ctx_row, lat_row, row(BR), row(BR), gate(1), row(BR), row(BR), gate(2), ctx_row, lat_row,
                  _layer(l, 1, BR), _layer(l, BR, BR), _layer(l, 1, BR), _layer(l, 1, BR),
                  _layer(l, D_MODEL, D_MODEL), full(1, D_MODEL)],
        out_specs=BS((TM, D_MODEL), lambda i: (i, 0)),
        compiler_params=_cparams("parallel"),
        name="outproj",
    )(x, mod, yp_ctx, yp_lat, dn_of, dn_ob, gates, s5_y, s5_u, gates, yf_ctx, yf_lat, s5_d, glu_w, glu_b, dn_g, w_out,
      final_g)


def _permute_w_in(w_in):
    cols = lambda lo, hi: w_in[..., lo:hi]
    data = [cols(0, 256), cols(512, 1280), cols(1552, 1808), cols(2064, 2320)]
    gates = [cols(256, 512), cols(1280, 1536), cols(1808, 2064), cols(2320, 2576)]
    ba = jnp.pad(cols(1536, 1552), ((0, 0), (0, 0), (0, 112)))
    return jnp.concatenate(data + gates + [ba], axis=-1).astype(BF16)


def kernel(x_prompt, x_sample, c, state_delta, state_s5, c_ctx, w_ada, b_ada, norm_g, w_in, pool_w, pool_scale,
           dn_conv, dn_a_log, dn_dt_bias, dn_norm_g, s5_a_re, s5_a_im, s5_log_dt, s5_b_re, s5_b_im, s5_c_re,
           s5_c_im, s5_d, s5_glu_w, s5_glu_b, ft_w, w_out, final_g):
    x = jnp.concatenate([x_prompt.astype(F32).reshape(N_CTX, D_MODEL),
                         x_sample.astype(F32).reshape(N_LAT, D_MODEL)], axis=0)
    cond8 = jnp.concatenate([c_ctx.astype(F32)[None], c.astype(F32),
                             jnp.zeros((8 - N_COND, D_MODEL), F32)], axis=0)
    ada = ada_all(cond8, w_ada, b_ada)
    pm_ctx, inv_ctx, pm_lat, inv_lat = _pool_constants()
    fpos, fch, g1, h2, fch2 = _ft_constants()
    s5_mt, s5_q4, s5_rt, s5_al = s5_prep(s5_a_re, s5_a_im, s5_log_dt, s5_b_re, s5_b_im, s5_c_re, s5_c_im)
    mod = ada.reshape(DEPTH * 8, 1, 3 * D_MODEL)
    norm_g3 = norm_g.reshape(DEPTH, 1, D_MODEL)
    w_in_p = _permute_w_in(w_in)
    eye_g = jnp.eye(len(POOL_WINDOWS), dtype=F32)
    w_bd = jnp.einsum('lgcd,gh->lgchd', pool_w, eye_g).reshape(DEPTH, BR, BR).astype(BF16)
    sc = pool_scale.reshape(DEPTH, 1, BR)
    ftw = ft_w.astype(BF16)
    conv_w = jnp.pad(dn_conv, ((0, 0), (0, 8 - CONV_K), (0, 0)))
    gate_p = jnp.zeros((DEPTH, 8, 128), F32)
    gate_p = gate_p.at[:, 0, 8:16].set(dn_a_log.reshape(DEPTH, 8)).at[:, 1, 8:16].set(dn_dt_bias.reshape(DEPTH, 8))
    out_params = (s5_d.reshape(DEPTH, 1, BR), s5_glu_w.astype(BF16), s5_glu_b.reshape(DEPTH, 1, BR),
                  jnp.tile(dn_norm_g, (1, DN_HEADS)).reshape(DEPTH, 1, BR), w_out.astype(BF16),
                  final_g.reshape(1, D_MODEL))
    s5_s0 = state_s5.astype(F32).transpose(1, 4, 3, 0, 2, 5).reshape(DEPTH, S5_LT, DEC_BATCH, 2 * S5_N)
    eye_h = jnp.eye(DN_HEADS, dtype=F32)
    dn_s0 = jnp.einsum('bldhkv,hg->ldbhkgv', state_delta.astype(F32), eye_h).reshape(DEPTH, 2 * DEC_BATCH, BR, BR)
    new_dn, new_s5 = [], []
    for l in range(DEPTH):
        pool_u, qkv, s5_u, ft_u, gates, ba = inproj(x, mod, norm_g3, w_in_p, l)

        yp_ctx = pool_branch(pool_u, gates, pm_ctx, inv_ctx, w_bd, sc, False, l)
        yp_lat = pool_branch(pool_u, gates, pm_lat, inv_lat, w_bd, sc, True, l)

        yf_ctx = ft_ctx(ft_u, gates, fpos, fch, ftw, l)
        yf_lat = ft_lat(ft_u, gates, g1, h2, fch2, ftw, l)

        s5_y, fin_s5 = s5_branch(s5_u, s5_mt, s5_q4, s5_rt, s5_al, s5_s0, l)
        new_s5.append(fin_s5)

        dn_of, dn_ob, fin_f, fin_b = dn_branch(qkv, ba, conv_w, gate_p, dn_s0, l)
        new_dn.append(jnp.stack([fin_f, fin_b]))

        finish = functools.partial(outproj, x, mod, yp_ctx, yp_lat, dn_of, dn_ob, gates, s5_y, s5_u, yf_ctx, yf_lat,
                                   *out_params, l)
        if l < DEPTH - 1:
            x = finish(False)
        else:
            tiles_ctx = N_CTX // TM
            y_ctx = finish(True, 0, tiles_ctx)
            y_lat = finish(True, tiles_ctx, N_LAT // TM)

    y_prompt = y_ctx.reshape(BATCH, SEQ, D_MODEL).astype(x_prompt.dtype)
    y_sample = y_lat.reshape(DEC_BATCH, DEC_SEQ, D_MODEL).astype(x_sample.dtype)
    new_state_delta = jnp.stack(new_dn).transpose(2, 0, 1, 3, 4, 5).astype(state_delta.dtype)
    fin = jnp.stack(new_s5).reshape(DEPTH, S5_G, 2, BATCH, 2, S5_N)
    new_state_s5 = fin.transpose(3, 0, 4, 2, 1, 5).astype(state_s5.dtype)
    return (y_prompt, y_sample, new_state_delta, new_state_s5)
```

```python
import functools
import math

import numpy as np
import jax
import jax.numpy as jnp
from jax import lax
from jax.experimental import pallas as pl
from jax.experimental.pallas import tpu as pltpu

F32 = jnp.float32
BF16 = jnp.bfloat16

D_MODEL = 1024
BATCH = 16
SEQ = 256
DEPTH = 4
DEC_BATCH = 2
DEC_SEQ = 4096
GRID_W = 64
GRID_H = DEC_SEQ // GRID_W
BR = 256
POOL_WINDOWS = (2, 4, 8, 16)
POOL_GD = 64
DN_HEADS = 4
DN_HD = 64
CONV_K = 5
CHUNK = 64
S5_P = 16
S5_G = 16
S5_N = 64
S5_L = 16
FT_HD = 64
EPS = 1e-6

N_CTX = BATCH * SEQ
N_LAT = DEC_BATCH * DEC_SEQ
N_TOK = N_CTX + N_LAT
N_COND = 1 + DEC_BATCH
TM = 512
W_IN_COLS = 2688
VMEM_LIMIT = 56 * 1024 * 1024

SDS = jax.ShapeDtypeStruct
BS = pl.BlockSpec


def _cparams(*sem):
    return pltpu.CompilerParams(dimension_semantics=sem, vmem_limit_bytes=VMEM_LIMIT)


def bdot(a, b):
    return jnp.dot(a.astype(BF16), b.astype(BF16), preferred_element_type=F32)


def hdot(a, b):
    return jnp.dot(a, b, preferred_element_type=F32, precision=lax.Precision.HIGHEST)


def silu(x):
    return x * jax.nn.sigmoid(x)


def _cond_index(i):
    tiles_ctx = N_CTX // TM
    tiles_seq = DEC_SEQ // TM
    return jnp.where(i < tiles_ctx, 0, 1 + (i - tiles_ctx) // tiles_seq)


def _ada_body(c_ref, w_ref, b_ref, o_ref):
    o_ref[0] = hdot(silu(c_ref[...]), w_ref[0]) + b_ref[0]


def ada_all(cond8, w_ada, b_ada):
    tn = 1536
    return pl.pallas_call(
        _ada_body,
        out_shape=SDS((DEPTH, 8, 3 * D_MODEL), F32),
        grid=(DEPTH, 3 * D_MODEL // tn),
        in_specs=[BS((8, D_MODEL), lambda l, j: (0, 0)),
                  BS((1, D_MODEL, tn), lambda l, j: (l, 0, j)),
                  BS((1, 1, tn), lambda l, j: (l, 0, j))],
        out_specs=BS((1, 8, tn), lambda l, j: (l, 0, j)),
        compiler_params=_cparams("parallel", "parallel"),
        name="ada",
    )(cond8, w_ada, b_ada.reshape(DEPTH, 1, 3 * D_MODEL))


def _inproj_body(x_ref, mod_ref, g_ref, w_ref, pool_ref, qkv_ref, s5_ref, ft_ref, z_ref, ba_ref):
    x = x_ref[...]
    m = mod_ref[0]
    shift = m[:, 0:D_MODEL]
    scale = m[:, D_MODEL:2 * D_MODEL]
    xn = x * lax.rsqrt(jnp.mean(x * x, axis=-1, keepdims=True) + EPS) * g_ref[...]
    h = (xn * (1.0 + scale) + shift).astype(BF16)

    def proj(lo, hi):
        return jnp.dot(h, w_ref[:, lo:hi], preferred_element_type=F32)

    pool_ref[...] = proj(0, 256)
    qkv_ref[...] = proj(256, 1024)
    s5_ref[...] = proj(1024, 1280)
    ft_ref[...] = proj(1280, 1536).astype(BF16)
    z_ref[...] = proj(1536, 2560).astype(BF16)
    ba_ref[...] = proj(2560, 2688)


def _layer(l, *block):
    zeros = (0,) * len(block)
    return BS((None,) + block, lambda *_: (l,) + zeros)


def inproj(x, mod, norm_g, w_in_p, l):
    widths = (256, 768, 256, 256, 1024, 128)
    dtypes = (F32, F32, F32, BF16, BF16, F32)
    return pl.pallas_call(
        _inproj_body,
        out_shape=[SDS((N_TOK, w), dt) for w, dt in zip(widths, dtypes)],
        grid=(N_TOK // TM,),
        in_specs=[BS((TM, D_MODEL), lambda i: (i, 0)),
                  BS((1, 1, 3 * D_MODEL), lambda i: (8 * l + _cond_index(i), 0, 0)),
                  _layer(l, 1, D_MODEL),
                  _layer(l, D_MODEL, W_IN_COLS)],
        out_specs=[BS((TM, w), lambda i: (i, 0)) for w in widths],
        compiler_params=_cparams("parallel"),
        name="inproj",
    )(x, mod, norm_g, w_in_p)


def _pool_body(u_ref, z_ref, pm_ref, inv_ref, w_ref, sc_ref, o_ref, *scratch, two_d):
    nblk = u_ref.shape[0] // 256
    if two_d:
        pad_ref, v_ref = scratch
        halo = 8 * GRID_W
        pad_ref[0:halo, :] = jnp.zeros((halo, BR), F32)
        pad_ref[halo + DEC_SEQ:2 * halo + DEC_SEQ, :] = jnp.zeros((halo, BR), F32)
        pad_ref[halo:halo + DEC_SEQ, :] = u_ref[...]
        lane = lax.broadcasted_iota(jnp.int32, (GRID_W, 128), 1)

        def row_body(r, c):
            base = pl.multiple_of(r * GRID_W, GRID_W)

            def slab(d, lo):
                return pad_ref[pl.ds(base + (8 + d) * GRID_W, GRID_W), lo:lo + 128]

            s2 = slab(-1, 0) + slab(0, 0)
            s4 = s2 + slab(-2, 0) + slab(1, 0)
            v_ref[pl.ds(base, GRID_W), 0:128] = jnp.where(lane < 64, s2, s4)
            s8 = slab(-4, 128)
            for d in (-3, -2, -1, 0, 1, 2, 3):
                s8 = s8 + slab(d, 128)
            s16 = s8
            for d in (-8, -7, -6, -5, 4, 5, 6, 7):
                s16 = s16 + slab(d, 128)
            v_ref[pl.ds(base, GRID_W), 128:256] = jnp.where(lane < 64, s8, s16)
            return c

        lax.fori_loop(0, GRID_H, row_body, 0)
        src = v_ref
    else:
        src = u_ref
    grp = lax.broadcasted_iota(jnp.int32, (256, BR), 1) // POOL_GD

    def blk_body(b, c):
        r0 = pl.multiple_of(b * 256, 256)
        vb = src[pl.ds(r0, 256), :]
        hi = vb.astype(BF16)
        lo = (vb - hi.astype(F32)).astype(BF16)
        res = jnp.zeros((256, BR), F32)
        for g in range(len(POOL_WINDOWS)):
            pg = (jnp.dot(pm_ref[g], hi, preferred_element_type=F32)
                  + jnp.dot(pm_ref[g], lo, preferred_element_type=F32))
            res = jnp.where(grp == g, pg, res)
        pooled = res * inv_ref[pl.ds(r0, 256), :]
        d = pooled - u_ref[pl.ds(r0, 256), :]
        y = bdot(d, w_ref[...]) * sc_ref[...]
        o_ref[pl.ds(r0, 256), :] = (y * silu(z_ref[pl.ds(r0, 256), :].astype(F32))).astype(BF16)
        return c

    lax.fori_loop(0, nblk, blk_body, 0)


def _band_matrices(seg):
    t = np.arange(256)
    out = []
    for w in POOL_WINDOWS:
        lo = t - w // 2
        hi = t - w // 2 + w
        s = t[None, :]
        m = (s >= lo[:, None]) & (s < hi[:, None]) & ((s // seg) == (t[:, None] // seg))
        out.append(m.astype(np.float32))
    return np.stack(out)


def _counts(length, w):
    pos = np.arange(length)
    return (np.clip(pos - w // 2 + w, 0, length) - np.clip(pos - w // 2, 0, length)).astype(np.float64)


def _pool_constants():
    inv_ctx = np.concatenate([np.repeat((1.0 / _counts(SEQ, w))[:, None], POOL_GD, 1) for w in POOL_WINDOWS], 1)
    inv_lat = []
    for w in POOL_WINDOWS:
        c2 = np.outer(_counts(GRID_H, w), _counts(GRID_W, w)).reshape(DEC_SEQ)
        inv_lat.append(np.repeat((1.0 / c2)[:, None], POOL_GD, 1))
    inv_lat = np.concatenate(inv_lat, 1)
    return (jnp.asarray(_band_matrices(SEQ), BF16), jnp.asarray(inv_ctx, F32),
            jnp.asarray(_band_matrices(GRID_W), BF16), jnp.asarray(inv_lat, F32))


def pool_branch(pool_u, gates, pm, inv, w_bd, scale, two_d, l):
    if two_d:
        rows, nseq, blk0 = DEC_SEQ, DEC_BATCH, N_CTX // DEC_SEQ
        scratch = [pltpu.VMEM((DEC_SEQ + 16 * GRID_W, BR), F32), pltpu.VMEM((DEC_SEQ, BR), F32)]
    else:
        rows, nseq, blk0 = SEQ, BATCH, 0
        scratch = []
    return pl.pallas_call(
        functools.partial(_pool_body, two_d=two_d),
        out_shape=SDS((nseq * rows, BR), BF16),
        grid=(nseq,),
        in_specs=[BS((rows, BR), lambda i: (blk0 + i, 0)),
                  BS((rows, BR), lambda i: (blk0 + i, 0)),
                  BS((4, 256, 256), lambda i: (0, 0, 0)),
                  BS((rows, BR), lambda i: (0, 0)),
                  _layer(l, BR, BR),
                  _layer(l, 1, BR)],
        out_specs=BS((rows, BR), lambda i: (i, 0)),
        scratch_shapes=scratch,
        compiler_params=_cparams("parallel"),
        name="pool2d" if two_d else "pool1d",
    )(pool_u, gates, pm, inv, w_bd, scale)


def _ft_ctx_body(u_ref, z_ref, fpos_ref, fch_ref, w_ref, o_ref):
    uc = bdot(u_ref[...], fch_ref[...])
    st = jnp.concatenate([uc[:, 0:BR], uc[:, BR:2 * BR]], axis=0)
    f = bdot(fpos_ref[...], st)
    o_ref[...] = (bdot(f, w_ref[...]) * silu(z_ref[...].astype(F32))).astype(BF16)


def ft_ctx(ft_u, gates, fpos, fch, ft_w, l):
    return pl.pallas_call(
        _ft_ctx_body,
        out_shape=SDS((N_CTX, BR), BF16),
        grid=(BATCH,),
        in_specs=[BS((SEQ, BR), lambda i: (i, 0)),
                  BS((SEQ, BR), lambda i: (i, 3)),
                  BS((SEQ, 2 * SEQ), lambda i: (0, 0)),
                  BS((BR, 2 * BR), lambda i: (0, 0)),
                  _layer(l, BR, BR)],
        out_specs=BS((SEQ, BR), lambda i: (i, 0)),
        compiler_params=_cparams("parallel"),
        name="ft_ctx",
    )(ft_u, gates, fpos, fch, ft_w)


def _ft_lat_body(u_ref, z_ref, g_ref, h_ref, fch_ref, w_ref, o_ref, x_ref, yr_ref, yi_ref):
    for hf in range(2):
        x_ref[hf] = u_ref[:, hf * 128:(hf + 1) * 128].astype(F32)

    def stage1(t2, c):
        xs = jnp.concatenate([x_ref[hf, pl.ds(t2, GRID_H, stride=GRID_W), :] for hf in range(2)], axis=1)
        y = jnp.dot(g_ref[t2], xs.astype(BF16), preferred_element_type=F32)
        r0 = pl.multiple_of(t2 * GRID_W, GRID_W)
        for hf in range(2):
            yr_ref[hf, pl.ds(r0, GRID_W), :] = y[0:64, hf * 128:(hf + 1) * 128]
            yi_ref[hf, pl.ds(r0, GRID_W), :] = y[64:128, hf * 128:(hf + 1) * 128]
        return c

    lax.fori_loop(0, GRID_W, stage1, 0, unroll=8)

    def stage2(kb, c):
        yr = jnp.concatenate([yr_ref[hf, pl.ds(kb, GRID_W, stride=GRID_W), :] for hf in range(2)], axis=1)
        yi = jnp.concatenate([yi_ref[hf, pl.ds(kb, GRID_W, stride=GRID_W), :] for hf in range(2)], axis=1)
        st = jnp.concatenate([yr, yi], axis=0).astype(BF16)
        a = jnp.dot(h_ref[...], st, preferred_element_type=F32)
        for hf in range(2):
            yr_ref[hf, pl.ds(kb, GRID_W, stride=GRID_W), :] = a[0:64, hf * 128:(hf + 1) * 128]
            yi_ref[hf, pl.ds(kb, GRID_W, stride=GRID_W), :] = a[64:128, hf * 128:(hf + 1) * 128]
        return c

    lax.fori_loop(0, GRID_W, stage2, 0, unroll=8)

    def stage3(b, c):
        r0 = pl.multiple_of(b * TM, TM)
        ar = jnp.concatenate([yr_ref[hf, pl.ds(r0, TM), :] for hf in range(2)], axis=1)
        ai = jnp.concatenate([yi_ref[hf, pl.ds(r0, TM), :] for hf in range(2)], axis=1)
        f = bdot(ar, fch_ref[0:BR, :]) + bdot(ai, fch_ref[BR:2 * BR, :])
        o_ref[pl.ds(r0, TM), :] = (bdot(f, w_ref[...]) * silu(z_ref[pl.ds(r0, TM), :].astype(F32))).astype(BF16)
        return c

    lax.fori_loop(0, DEC_SEQ // TM, stage3, 0)


def ft_lat(ft_u, gates, g1, h2, fch2, ft_w, l):
    blk0 = N_CTX // DEC_SEQ
    return pl.pallas_call(
        _ft_lat_body,
        out_shape=SDS((N_LAT, BR), BF16),
        grid=(DEC_BATCH,),
        in_specs=[BS((DEC_SEQ, BR), lambda i: (blk0 + i, 0)),
                  BS((DEC_SEQ, BR), lambda i: (blk0 + i, 3)),
                  BS((GRID_W, 128, GRID_H), lambda i: (0, 0, 0)),
                  BS((128, 128), lambda i: (0, 0)),
                  BS((2 * BR, BR), lambda i: (0, 0)),
                  _layer(l, BR, BR)],
        out_specs=BS((DEC_SEQ, BR), lambda i: (i, 0)),
        scratch_shapes=[pltpu.VMEM((2, DEC_SEQ, 128), F32)] * 3,
        compiler_params=_cparams("parallel"),
        name="ft_lat",
    )(ft_u, gates, g1, h2, fch2, ft_w)


def _ft_constants():
    c = np.arange(FT_HD)
    ang = 2.0 * np.pi * np.outer(c, c) / FT_HD
    eye4 = np.eye(BR // FT_HD)
    cc = np.kron(eye4, np.cos(ang)) / 8.0
    sc = np.kron(eye4, np.sin(ang)) / 8.0
    t = np.arange(SEQ)
    angt = 2.0 * np.pi * (np.outer(t, t) % SEQ) / SEQ
    fpos = np.concatenate([np.cos(angt), -np.sin(angt)], axis=1) / 16.0
    fch = np.concatenate([cc, sc], axis=1)
    kb = np.arange(GRID_W)[None, :, None]
    t1 = np.arange(GRID_H)[None, None, :]
    t2 = np.arange(GRID_W)[:, None, None]
    a1 = 2.0 * np.pi * ((kb * (GRID_W * t1 + t2)) % DEC_SEQ) / DEC_SEQ
    g1 = np.concatenate([np.cos(a1), -np.sin(a1)], axis=1) / 8.0
    a2 = 2.0 * np.pi * (np.outer(np.arange(GRID_W), np.arange(GRID_W)) % GRID_W) / GRID_W
    c2, s2 = np.cos(a2) / 8.0, np.sin(a2) / 8.0
    h2 = np.block([[c2, s2], [-s2, c2]])
    fch2 = np.concatenate([cc, sc], axis=0)
    as_bf = lambda a: jnp.asarray(a, F32).astype(BF16)
    return as_bf(fpos), as_bf(fch), as_bf(g1), as_bf(h2), as_bf(fch2)


S5_ROWS = N_TOK // S5_L
S5_TM = 2048
S5_TR = S5_TM // S5_L
S5_CTX_C = SEQ // S5_L
S5_LAT_C = DEC_SEQ // S5_L
S5_LT = 2 * S5_G


S5_PG = 4


def _s5_prep_body(*refs):
    for gi in range(S5_PG):
        _s5_prep_group(gi, *refs)


def _s5_prep_group(gi, ar_ref, ai_ref, ldt_ref, br_ref, bi_ref, btr_ref, bti_ref, cr_ref, ci_ref,
                   mt_ref, q4_ref, rt_ref, al_ref):
    L = S5_L
    m = lax.broadcasted_iota(jnp.int32, (2 * L, S5_N), 0).astype(F32)
    rts, qs, ds, als = [], [], [], []
    for d in range(2):
        a_re, a_im = ar_ref[0, d, gi], ai_ref[0, d, gi]
        dt = jnp.exp(ldt_ref[0, d, gi])
        xr, xi = a_re * dt, a_im * dt
        mag = jnp.exp(xr)
        ab_re, ab_im = mag * jnp.cos(xi), mag * jnp.sin(xi)
        den = a_re * a_re + a_im * a_im
        nr = ab_re - 1.0
        coef_re = (nr * a_re + ab_im * a_im) / den
        coef_im = (ab_im * a_re - nr * a_im) / den
        pw_re = jnp.exp(m * xr) * jnp.cos(m * xi)
        pw_im = jnp.exp(m * xr) * jnp.sin(m * xi)
        cq_re = pw_re * coef_re - pw_im * coef_im
        cq_im = pw_re * coef_im + pw_im * coef_re
        c_re, c_im = cr_ref[0, d, gi], ci_ref[0, d, gi]
        bt_re, bt_im = btr_ref[0, d, gi], bti_ref[0, d, gi]
        row = lambda x, e: x[e:e + 1, :]
        order = range(L) if d == 0 else range(L - 1, -1, -1)
        cp_re = jnp.concatenate([c_re * row(cq_re, e) - c_im * row(cq_im, e) for e in order], axis=0)
        cp_im = jnp.concatenate([c_re * row(cq_im, e) + c_im * row(cq_re, e) for e in order], axis=0)
        rts.append(hdot(cp_re, br_ref[0, d, gi]) - hdot(cp_im, bi_ref[0, d, gi]))
        inj = [L - 1 - i for i in range(L)] if d == 0 else list(range(L))
        q_re = jnp.concatenate([bt_re * row(cq_re, e) - bt_im * row(cq_im, e) for e in inj], axis=0)
        q_im = jnp.concatenate([bt_im * row(cq_re, e) + bt_re * row(cq_im, e) for e in inj], axis=0)
        qs.append((q_re, q_im))
        out = [j + 1 for j in range(L)] if d == 0 else [L - j for j in range(L)]
        d_re = jnp.concatenate([c_re * row(pw_re, e) - c_im * row(pw_im, e) for e in out], axis=0)
        d_im = jnp.concatenate([c_re * row(pw_im, e) + c_im * row(pw_re, e) for e in out], axis=0)
        ds.append((d_re, d_im))
        als.append((row(pw_re, L), row(pw_im, L)))
    pad = jnp.zeros(((L - 1) * S5_P, S5_P), F32)
    z = jnp.concatenate([pad, rts[0]], axis=0) + jnp.concatenate([rts[1], pad], axis=0)
    mt = jnp.concatenate([z[(L - 1 - i) * S5_P:(L - 1 - i) * S5_P + L * S5_P, :] for i in range(L)], axis=1)
    mt_ref[0, gi] = mt.astype(BF16)
    q4_ref[0, gi] = jnp.concatenate([qs[0][0], qs[1][0], qs[0][1], qs[1][1]], axis=1).astype(BF16)
    rt_ref[0, gi] = jnp.concatenate([ds[0][0], ds[1][0], -ds[0][1], -ds[1][1]], axis=1).astype(BF16)
    al_ref[0, gi] = jnp.concatenate([jnp.concatenate([als[0][0], als[1][0]], axis=1),
                                    jnp.concatenate([als[0][1], als[1][1]], axis=1)], axis=0)


def s5_prep(a_re, a_im, log_dt, b_re, b_im, c_re, c_im):
    vec = lambda x: x.reshape(DEPTH, 2, S5_G, 1, S5_N)
    ldt = jnp.broadcast_to(log_dt[..., None, None], (DEPTH, 2, S5_G, 1, S5_N))
    bt = lambda x: x.transpose(0, 1, 2, 4, 3)
    per = S5_G // S5_PG
    vspec = BS((1, 2, S5_PG, 1, S5_N), lambda i: (i // per, 0, i % per, 0, 0))
    bspec = BS((1, 2, S5_PG, S5_N, S5_P), lambda i: (i // per, 0, i % per, 0, 0))
    cspec = BS((1, 2, S5_PG, S5_P, S5_N), lambda i: (i // per, 0, i % per, 0, 0))
    mat = SDS((DEPTH, S5_G, 256, 256), BF16)
    mspec = BS((1, S5_PG, 256, 256), lambda i: (i // per, i % per, 0, 0))
    return pl.pallas_call(
        _s5_prep_body,
        out_shape=[mat, mat, mat, SDS((DEPTH, S5_G, 2, 128), F32)],
        grid=(DEPTH * per,),
        in_specs=[vspec, vspec, vspec, bspec, bspec, cspec, cspec, cspec, cspec],
        out_specs=[mspec, mspec, mspec, BS((1, S5_PG, 2, 128), lambda i: (i // per, i % per, 0, 0))],
        compiler_params=_cparams("parallel"),
        name="s5_prep",
    )(vec(a_re), vec(a_im), ldt, b_re, b_im, bt(b_re), bt(b_im), c_re, c_im)


def _block_transpose(arrs):
    blk = lax.broadcasted_iota(jnp.int32, arrs[0].shape, 1) // 16
    cur = list(arrs)
    for b in range(4):
        s = 16 << b
        hi = ((blk >> b) & 1) == 1
        nxt = list(cur)
        for x in range(16):
            if (x >> b) & 1:
                continue
            y = x | (1 << b)
            nxt[x] = jnp.where(hi, pltpu.roll(cur[y], s, 1), cur[x])
            nxt[y] = jnp.where(hi, cur[y], pltpu.roll(cur[x], 256 - s, 1))
        cur = nxt
    return cur


def _s5_in_body(s5_ref, mt_ref, q4_ref, y_ref, e_ref, x_ref):
    for hf in range(2):
        x_ref[hf] = s5_ref[:, hf * 128:(hf + 1) * 128]
    xs = [jnp.concatenate([x_ref[hf, pl.ds(i, S5_TR, stride=S5_L), :] for hf in range(2)], axis=1)
          for i in range(S5_L)]
    us = _block_transpose(xs)
    for g in range(S5_G):
        ub = us[g].astype(BF16)
        y_ref[:, g * 256:(g + 1) * 256] = lax.dot_general(ub, mt_ref[g], (((1,), (1,)), ((), ())),
                                                          preferred_element_type=F32)
        e = jnp.dot(ub, q4_ref[g], preferred_element_type=F32)
        e_ref[2 * g] = e[:, 0:128]
        e_ref[2 * g + 1] = e[:, 128:256]


def s5_in(s5, mt, q4, l):
    wspec = _layer(l, S5_G, 256, 256)
    return pl.pallas_call(
        _s5_in_body,
        out_shape=[SDS((S5_ROWS, S5_G * 256), F32), SDS((S5_LT, S5_ROWS, 128), F32)],
        grid=(N_TOK // S5_TM,),
        in_specs=[BS((S5_TM, BR), lambda t: (t, 0)), wspec, wspec],
        out_specs=[BS((S5_TR, S5_G * 256), lambda t: (t, 0)), BS((S5_LT, S5_TR, 128), lambda t: (0, t, 0))],
        scratch_shapes=[pltpu.VMEM((2, S5_TM, 128), F32)],
        compiler_params=_cparams("parallel"),
        name="s5_in",
    )(s5, mt, q4)


def _s5_chunk_scan_body(e_ref, al_ref, s0_ref, spf_ref, spb_ref, fin_ref):
    def update(g, s_re, s_im, e_re, e_im):
        a_re, a_im = al_ref[g, 0:1, :], al_ref[g, 1:2, :]
        return a_re * s_re - a_im * s_im + e_re, a_re * s_im + a_im * s_re + e_im

    fwd_c = lax.broadcasted_iota(jnp.int32, (BATCH, 128), 1) < S5_N

    def ctx_group(g, carry):
        s_re = jnp.zeros((BATCH, 128), F32)
        s_im = jnp.zeros((BATCH, 128), F32)
        for c in range(S5_CTX_C):
            rf = pl.ds(c, BATCH, stride=S5_CTX_C)
            rb = pl.ds(S5_CTX_C - 1 - c, BATCH, stride=S5_CTX_C)
            spf_ref[2 * g, rf, :] = s_re
            spf_ref[2 * g + 1, rf, :] = s_im
            spb_ref[2 * g, rb, :] = s_re
            spb_ref[2 * g + 1, rb, :] = s_im
            e_re = jnp.where(fwd_c, e_ref[2 * g, rf, :], e_ref[2 * g, rb, :])
            e_im = jnp.where(fwd_c, e_ref[2 * g + 1, rf, :], e_ref[2 * g + 1, rb, :])
            s_re, s_im = update(g, s_re, s_im, e_re, e_im)
        fin_ref[2 * g] = s_re
        fin_ref[2 * g + 1] = s_im
        return carry

    lax.fori_loop(0, S5_GQ, ctx_group, 0)

    row0 = BATCH * S5_CTX_C
    fwd_l = lax.broadcasted_iota(jnp.int32, (DEC_BATCH, 128), 1) < S5_N

    def lat_step(c, state):
        rf = pl.ds(row0 + c, DEC_BATCH, stride=S5_LAT_C)
        rb = pl.ds(row0 + S5_LAT_C - 1 - c, DEC_BATCH, stride=S5_LAT_C)
        new = []
        for g in range(S5_GQ):
            s_re, s_im = state[2 * g], state[2 * g + 1]
            spf_ref[2 * g, rf, :] = s_re
            spf_ref[2 * g + 1, rf, :] = s_im
            spb_ref[2 * g, rb, :] = s_re
            spb_ref[2 * g + 1, rb, :] = s_im
            e_re = jnp.where(fwd_l, e_ref[2 * g, rf, :], e_ref[2 * g, rb, :])
            e_im = jnp.where(fwd_l, e_ref[2 * g + 1, rf, :], e_ref[2 * g + 1, rb, :])
            new.extend(update(g, s_re, s_im, e_re, e_im))
        return tuple(new)

    lax.fori_loop(0, S5_LAT_C, lat_step, tuple(s0_ref[t] for t in range(2 * S5_GQ)))


S5_GQ = 4


def s5_chunk_scan(e3, al, s0_lat, l):
    sp = SDS((S5_LT, S5_ROWS, 128), F32)
    tiles = lambda rows: BS((2 * S5_GQ, rows, 128), lambda q: (q, 0, 0))
    return pl.pallas_call(
        _s5_chunk_scan_body,
        out_shape=[sp, sp, SDS((S5_LT, BATCH, 128), F32)],
        grid=(S5_G // S5_GQ,),
        in_specs=[tiles(S5_ROWS), BS((None, S5_GQ, 2, 128), lambda q: (l, q, 0, 0)),
                  BS((None, 2 * S5_GQ, DEC_BATCH, 128), lambda q: (l, q, 0, 0))],
        out_specs=[tiles(S5_ROWS), tiles(S5_ROWS), tiles(BATCH)],
        compiler_params=_cparams("parallel"),
        name="s5_chunk_scan",
    )(e3, al, s0_lat)


def _s5_fin_body(y_ref, spf_ref, spb_ref, rt_ref, o_ref, t_ref):
    fwd = lax.broadcasted_iota(jnp.int32, (S5_TR, 128), 1) < S5_N
    ys = []
    for g in range(S5_G):
        s_re = jnp.where(fwd, spf_ref[2 * g], spb_ref[2 * g])
        s_im = jnp.where(fwd, spf_ref[2 * g + 1], spb_ref[2 * g + 1])
        sp = jnp.concatenate([s_re, s_im], axis=1).astype(BF16)
        ys.append(y_ref[:, g * 256:(g + 1) * 256]
                  + lax.dot_general(sp, rt_ref[g], (((1,), (1,)), ((), ())), preferred_element_type=F32))
    xs = _block_transpose(ys)
    for j in range(S5_L):
        for hf in range(2):
            t_ref[hf, pl.ds(j, S5_TR, stride=S5_L), :] = xs[j][:, hf * 128:(hf + 1) * 128]
    o_ref[...] = jnp.concatenate([t_ref[0], t_ref[1]], axis=1)


def s5_fin(yi, spf, spb, rt, l):
    lt = BS((S5_LT, S5_TR, 128), lambda t: (0, t, 0))
    return pl.pallas_call(
        _s5_fin_body,
        out_shape=SDS((N_TOK, BR), F32),
        grid=(N_TOK // S5_TM,),
        in_specs=[BS((S5_TR, S5_G * 256), lambda t: (t, 0)), lt, lt, _layer(l, S5_G, 256, 256)],
        out_specs=BS((S5_TM, BR), lambda t: (t, 0)),
        scratch_shapes=[pltpu.VMEM((2, S5_TM, 128), F32)],
        compiler_params=_cparams("parallel"),
        name="s5_fin",
    )(yi, spf, spb, rt)


def s5_branch(s5, mt, q4, rt, al, s0, l):
    yi, e3 = s5_in(s5, mt, q4, l)
    spf, spb, fin = s5_chunk_scan(e3, al, s0, l)
    return s5_fin(yi, spf, spb, rt, l), fin


DN_TILE = 256
DN_HALO = 8


N_SEG = N_TOK // DN_TILE
assert N_CTX == DEC_SEQ


def _split3(x):
    x1 = x.astype(BF16)
    r1 = x - x1.astype(F32)
    x2 = r1.astype(BF16)
    return x1, x2, (r1 - x2.astype(F32)).astype(BF16)


def dot3(a, b, exact):
    if exact == 'b':
        return sum(jnp.dot(p, b.astype(BF16), preferred_element_type=F32) for p in _split3(a))
    return sum(jnp.dot(a.astype(BF16), p, preferred_element_type=F32) for p in _split3(b))


def _dn_conv_body(x_ref, prev_ref, next_ref, ba_ref, w_ref, gp_ref, ex_ref, o_ref, gb_ref, pad_ref):
    i = pl.program_id(0)
    tiles_ctx = N_CTX // DN_TILE
    tiles_seq = DEC_SEQ // DN_TILE
    j = (i - tiles_ctx) % tiles_seq
    first = jnp.logical_or(i < tiles_ctx, j == 0)
    last = jnp.logical_or(i < tiles_ctx, j == tiles_seq - 1)
    pad_ref[0:DN_HALO, :] = jnp.where(first, 0.0, prev_ref[...])
    pad_ref[DN_HALO:DN_HALO + DN_TILE, :] = x_ref[...]
    pad_ref[DN_HALO + DN_TILE:2 * DN_HALO + DN_TILE, :] = jnp.where(last, 0.0, next_ref[...])
    for r0 in range(0, DN_TILE, 128):
        for c0 in range(0, 3 * BR, 128):
            acc = jnp.zeros((128, 128), F32)
            for t in range(CONV_K):
                acc = acc + (pad_ref[pl.ds(r0 + DN_HALO - CONV_K // 2 + t, 128), c0:c0 + 128]
                             * w_ref[t:t + 1, c0:c0 + 128])
            o_ref[r0:r0 + 128, c0:c0 + 128] = silu(acc)
    raw = ba_ref[...]
    lane = lax.broadcasted_iota(jnp.int32, raw.shape, 1)
    xa = raw + gp_ref[1:2, :]
    sp = jnp.maximum(xa, 0.0) + jnp.log1p(jnp.exp(-jnp.abs(xa)))
    gates = jnp.where(lane < 2 * DN_HEADS, jax.nn.sigmoid(raw), -jnp.exp(gp_ref[0:1, :]) * sp)
    r = lax.broadcasted_iota(jnp.int32, (DN_TILE, DN_TILE), 0)
    c = lax.broadcasted_iota(jnp.int32, (DN_TILE, DN_TILE), 1)
    same = (r // CHUNK) == (c // CHUNK)
    ex = dot3(gates, ex_ref[...], 'b')
    gb_ref[:, 0:2 * BR] = ex[:, 0:2 * BR]
    gb_ref[:, 2 * BR:3 * BR] = dot3(jnp.logical_and(same, c <= r), ex[:, 2 * BR:3 * BR], 'a')
    gb_ref[:, 3 * BR:4 * BR] = dot3(jnp.logical_and(same, c >= r), ex[:, 3 * BR:4 * BR], 'a')


def _gate_expand():
    e = np.zeros((128, 4 * BR), np.float32)
    for blk in range(4):
        for h in range(DN_HEADS):
            e[blk * DN_HEADS + h, blk * BR + h * DN_HD:blk * BR + (h + 1) * DN_HD] = 1.0
    return jnp.asarray(e, BF16)


def _head_block_mask():
    r = lax.broadcasted_iota(jnp.int32, (BR, BR), 0) // DN_HD
    c = lax.broadcasted_iota(jnp.int32, (BR, BR), 1) // DN_HD
    return r == c


def _split2(x):
    hi = x.astype(BF16)
    return hi, (x - hi.astype(F32)).astype(BF16)


def _dn_local_body(qkv_ref, gb_ref, *out_refs):
    f_refs, b_refs = out_refs[0:6], out_refs[6:12]
    ncb = DN_TILE // CHUNK
    nb = 2 * ncb
    bmask = _head_block_mask()
    ri = lax.broadcasted_iota(jnp.int32, (nb, CHUNK, BR), 1)
    cj = lax.broadcasted_iota(jnp.int32, (nb, CHUNK, BR), 2) % DN_HD
    bwd = lax.broadcasted_iota(jnp.int32, (nb, CHUNK, BR), 0) >= ncb
    eye = cj == ri
    incl = jnp.logical_or(jnp.logical_and(bwd, cj >= ri), jnp.logical_and(jnp.logical_not(bwd), cj <= ri))
    strict = jnp.logical_and(incl, jnp.logical_not(eye))
    ones_blk = bmask.astype(BF16)

    def chunks(x):
        return x.reshape(ncb, CHUNK, BR)

    def both(x):
        return jnp.concatenate([x, x], axis=0)

    def head_sum(x):
        return dot3(x.reshape(-1, BR), ones_blk, 'b').reshape(x.shape)

    def bd(x):
        return jnp.where(bmask, jnp.concatenate([x] * DN_HEADS, axis=1), jnp.zeros((), x.dtype))

    def bmm(a, b):
        return jnp.einsum('bij,bjk->bik', a, b, preferred_element_type=F32)

    q = chunks(qkv_ref[:, 0:BR])
    k = chunks(qkv_ref[:, BR:2 * BR])
    v = both(chunks(qkv_ref[:, 2 * BR:3 * BR]))
    q = q * lax.rsqrt(head_sum(q * q) + EPS) * (DN_HD ** -0.5)
    k = k * lax.rsqrt(head_sum(k * k) + EPS)
    kq = jnp.einsum('bik,bjk->bij', jnp.concatenate([k, q], axis=1).astype(BF16), bd(k.astype(BF16)),
                    preferred_element_type=F32)
    kk, qk = both(kq[:, 0:CHUNK]), both(kq[:, CHUNK:2 * CHUNK])
    q, k = both(q), both(k)
    beta = jnp.concatenate([chunks(gb_ref[:, 0:BR]), chunks(gb_ref[:, BR:2 * BR])], axis=0)
    gc = jnp.concatenate([chunks(gb_ref[:, 2 * BR:3 * BR]), chunks(gb_ref[:, 3 * BR:4 * BR])], axis=0)
    crow = jnp.sum(jnp.where(eye, gc, 0.0), axis=1, keepdims=True)
    decay = jnp.where(incl, jnp.exp(jnp.where(incl, gc - crow, 0.0)), 0.0)
    a = jnp.where(strict, kk * decay * beta, 0.0)
    tinv = jnp.where(eye, 1.0, 0.0) - a
    pw = a
    pw_bd = bd(pw.astype(BF16))
    for _ in range(5):
        pw = bmm(pw.astype(BF16), pw_bd)
        pw_bd = bd(pw.astype(BF16))
        tinv = tinv + bmm(tinv.astype(BF16), pw_bd)
    egc = jnp.exp(gc)
    t_hi, t_lo = _split2(tinv)

    def solve(rhs):
        r_hi, r_lo = _split2(rhs)
        r_hi, r_lo = bd(r_hi), bd(r_lo)
        return bmm(t_hi, r_hi) + bmm(t_hi, r_lo) + bmm(t_lo, r_hi)

    w = bmm(t_hi, bd((k * (beta * egc)).astype(BF16)))

    bwd_row = lax.broadcasted_iota(jnp.int32, (nb, 1, BR), 0) >= ncb
    g_last = jnp.where(bwd_row, gc[:, 0:1], gc[:, CHUNK - 1:CHUNK])
    eg = jnp.exp(g_last)
    outs = (w, solve(v * beta), q * egc, k * jnp.exp(g_last - gc),
            jnp.where(incl, qk * decay, 0.0))
    for d, refs in enumerate((f_refs, b_refs)):
        for ref, x in zip(refs[0:5], outs):
            ref[...] = x[d * ncb:(d + 1) * ncb].reshape(DN_TILE, BR).astype(ref.dtype)
        refs[5][0] = jnp.concatenate([eg[d * ncb:(d + 1) * ncb, 0], jnp.zeros((8 - ncb, BR), F32)], axis=0)


_DN_LOCAL_DTYPES = (BF16, F32, BF16, BF16, BF16)


def _dn_prep_body(x_ref, prev_ref, next_ref, ba_ref, w_ref, gp_ref, ex_ref, *rest):
    out_refs, (pad_ref, qkv_ref, gb_ref) = rest[0:12], rest[12:15]
    _dn_conv_body(x_ref, prev_ref, next_ref, ba_ref, w_ref, gp_ref, ex_ref, qkv_ref, gb_ref, pad_ref)
    _dn_local_body(qkv_ref, gb_ref, *out_refs)


def dn_local(qkv, ba, conv_w, gate_p, l):
    per = DN_TILE // DN_HALO
    nhb = N_TOK // DN_HALO
    tok = BS((DN_TILE, BR), lambda i: (i, 0))
    shapes = [SDS((N_TOK, BR), dt) for dt in _DN_LOCAL_DTYPES] + [SDS((N_SEG, 8, BR), F32)]
    specs = [tok] * len(_DN_LOCAL_DTYPES) + [BS((1, 8, BR), lambda i: (i, 0, 0))]
    outs = pl.pallas_call(
        _dn_prep_body,
        out_shape=shapes * 2,
        grid=(N_SEG,),
        in_specs=[BS((DN_TILE, 3 * BR), lambda i: (i, 0)),
                  BS((DN_HALO, 3 * BR), lambda i: (jnp.maximum(i * per - 1, 0), 0)),
                  BS((DN_HALO, 3 * BR), lambda i: (jnp.minimum((i + 1) * per, nhb - 1), 0)),
                  BS((DN_TILE, 128), lambda i: (i, 0)),
                  _layer(l, 8, 3 * BR),
                  _layer(l, 8, 128),
                  BS((128, 4 * BR), lambda i: (0, 0))],
        out_specs=specs * 2,
        scratch_shapes=[pltpu.VMEM((DN_TILE + 2 * DN_HALO, 3 * BR), F32),
                        pltpu.VMEM((DN_TILE, 3 * BR), F32),
                        pltpu.VMEM((DN_TILE, 4 * BR), F32)],
        compiler_params=_cparams("parallel"),
        name="dn_local",
    )(qkv, qkv, qkv, ba, conv_w, gate_p, _gate_expand())
    return outs[0:6], outs[6:12]


def _dn_advance(chains, s, bmask):
    ncb = DN_TILE // CHUNK
    bmm = lambda a, b: jnp.einsum('bij,bjk->bik', a, b, preferred_element_type=F32)
    steps = []
    for t in range(ncb):
        cs = [ncb - 1 - t if rev else t for _, rev in chains]
        w, u, qt, kt, aqk, eg = [jnp.stack([load(k, c) for (load, _), c in zip(chains, cs)]) for k in range(6)]
        sb = s.astype(BF16)
        v_new = u - bmm(w, sb)
        vb = v_new.astype(BF16)
        v_bd = jnp.where(bmask, jnp.concatenate([vb] * DN_HEADS, axis=1), jnp.zeros((), BF16))
        o = bmm(qt, sb) + bmm(aqk, v_bd)
        upd = jnp.einsum('btk,btv->bkv', kt, vb, preferred_element_type=F32)
        s = s * eg + jnp.where(bmask, upd, 0.0)
        steps.append((cs, o))
    return s, steps


assert BATCH == DEC_SEQ // DN_TILE


def _dn_seq_body(*refs):
    f_in, b_in = refs[0:6], refs[6:12]
    s0_ref, of_ref, ob_ref, finf_ref, finb_ref, s_ref = refs[12:18]

    @pl.when(pl.program_id(0) == 0)
    def _():
        s_ref[...] = s0_ref[...]

    def loader(in_refs, q):
        def load(k, c):
            if k == 5:
                return in_refs[5][q, 0, c:c + 1, :]
            return in_refs[k][q, c * CHUNK:(c + 1) * CHUNK, :]
        return load

    groups = [1, 2, 1, 2, 0, 0]
    outs = [of_ref, of_ref, ob_ref, ob_ref, of_ref, ob_ref]
    chains = [(loader(b_in if o is ob_ref else f_in, q), o is ob_ref) for q, o in zip(groups, outs)]
    n_lat = 2 * DEC_BATCH
    s = jnp.concatenate([s_ref[...], jnp.zeros((2, BR, BR), F32)], axis=0)
    s, steps = _dn_advance(chains, s, _head_block_mask())
    s_ref[...] = s[0:n_lat]
    heads = lambda m: jnp.stack([m[h * DN_HD:(h + 1) * DN_HD, h * DN_HD:(h + 1) * DN_HD] for h in range(DN_HEADS)])
    finf_ref[0] = heads(s[n_lat])
    finb_ref[0] = heads(s[n_lat + 1])
    for cs, o in steps:
        for i, c in enumerate(cs):
            outs[i][groups[i], c * CHUNK:(c + 1) * CHUNK, :] = o[i]


def dn_seq(loc_f, loc_b, s0, l):
    nseg = DEC_SEQ // DN_TILE
    grp = lambda a: a.reshape((N_COND, nseg) + a.shape[1:])

    def specs(m):
        return ([BS((N_COND, DN_TILE, BR), lambda g: (0, m(g), 0))] * len(_DN_LOCAL_DTYPES)
                + [BS((N_COND, 1, 8, BR), lambda g: (0, m(g), 0, 0))])

    fwd = lambda g: g
    bwd = lambda g: nseg - 1 - g
    views = lambda loc: [a.reshape(N_COND, DEC_SEQ, BR) for a in loc[0:5]] + [grp(loc[5])]
    out = SDS((N_COND, DEC_SEQ, BR), F32)
    fin = SDS((BATCH, DN_HEADS, DN_HD, DN_HD), F32)
    o_f, o_b, fin_f, fin_b = pl.pallas_call(
        _dn_seq_body,
        out_shape=[out, out, fin, fin],
        grid=(nseg,),
        in_specs=specs(fwd) + specs(bwd) + [_layer(l, 2 * DEC_BATCH, BR, BR)],
        out_specs=[BS((N_COND, DN_TILE, BR), lambda g: (0, fwd(g), 0)),
                   BS((N_COND, DN_TILE, BR), lambda g: (0, bwd(g), 0)),
                   BS((1, DN_HEADS, DN_HD, DN_HD), lambda g: (fwd(g), 0, 0, 0)),
                   BS((1, DN_HEADS, DN_HD, DN_HD), lambda g: (bwd(g), 0, 0, 0))],
        scratch_shapes=[pltpu.VMEM((2 * DEC_BATCH, BR, BR), F32)],
        compiler_params=_cparams("arbitrary"),
        name="dn_seq",
    )(*views(loc_f), *views(loc_b), s0)
    return o_f.reshape(N_TOK, BR), o_b.reshape(N_TOK, BR), fin_f, fin_b


def dn_branch(qkv, ba, conv_w, gate_p, s0, l):
    loc_f, loc_b = dn_local(qkv, ba, conv_w, gate_p, l)
    return dn_seq(loc_f, loc_b, s0, l)


def _outproj_body(x_ref, mod_ref, ypc_ref, ypl_ref, dof_ref, dob_ref, dnz_ref, s5y_ref, s5u_ref, s5z_ref,
                  yfc_ref, yfl_ref, d_ref, gw_ref, gb_ref, dng_ref, w_ref, fg_ref, o_ref, *, final, tile0):
    is_ctx = pl.program_id(0) + tile0 < N_CTX // TM
    y_pool = jnp.where(is_ctx, ypc_ref[...], ypl_ref[...])
    y_ft = jnp.where(is_ctx, yfc_ref[...], yfl_ref[...])
    gate = mod_ref[0][:, 2 * D_MODEL:3 * D_MODEL]
    o = dof_ref[...] + dob_ref[...]
    head_mean = jnp.where(_head_block_mask(), 1.0 / DN_HD, 0.0)
    y_dn = o * lax.rsqrt(dot3(o * o, head_mean, 'b') + EPS) * dng_ref[...] * silu(dnz_ref[...].astype(F32))
    y = s5y_ref[...] + d_ref[...] * s5u_ref[...]
    y = jax.nn.gelu(y)
    y = y * jax.nn.sigmoid(bdot(y, gw_ref[...]) + gb_ref[...])
    y_s5 = y * silu(s5z_ref[...].astype(F32))
    acc = bdot(y_pool, w_ref[0:BR, :])
    acc = acc + bdot(y_dn, w_ref[BR:2 * BR, :])
    acc = acc + bdot(y_s5, w_ref[2 * BR:3 * BR, :])
    acc = acc + bdot(y_ft, w_ref[3 * BR:4 * BR, :])
    xn = x_ref[...] + gate * acc
    if final:
        xn = xn * lax.rsqrt(jnp.mean(xn * xn, axis=-1, keepdims=True) + EPS) * fg_ref[...]
    o_ref[...] = xn


def outproj(x, mod, yp_ctx, yp_lat, dn_of, dn_ob, gates, s5_y, s5_u, yf_ctx, yf_lat, s5_d, glu_w, glu_b, dn_g, w_out,
            final_g, l, final, tile0=0, ntiles=N_TOK // TM):
    row = lambda w: BS((TM, w), lambda i: (i + tile0, 0))
    gate = lambda c: BS((TM, BR), lambda i: (i + tile0, c))
    full = lambda a, b: BS((a, b), lambda i: (0, 0))
    tiles_ctx = N_CTX // TM
    ctx_row = BS((TM, BR), lambda i: (jnp.minimum(i + tile0, tiles_ctx - 1), 0))
    lat_row = BS((TM, BR), lambda i: (jnp.maximum(i + tile0 - tiles_ctx, 0), 0))
    return pl.pallas_call(
        functools.partial(_outproj_body, final=final, tile0=tile0),
        out_shape=SDS((ntiles * TM, D_MODEL), F32),
        grid=(ntiles,),
        in_specs=[row(D_MODEL),
                  BS((1, 1, 3 * D_MODEL), lambda i: (8 * l + _cond_index(i + tile0), 0, 0)),
                  ctx_row, lat_row, row(BR), row(BR), gate(1), row(BR), row(BR), gate(2), ctx_row, lat_row,
                  _layer(l, 1, BR), _layer(l, BR, BR), _layer(l, 1, BR), _layer(l, 1, BR),
                  _layer(l, D_MODEL, D_MODEL), full(1, D_MODEL)],
        out_specs=BS((TM, D_MODEL), lambda i: (i, 0)),
        compiler_params=_cparams("parallel"),
        name="outproj",
    )(x, mod, yp_ctx, yp_lat, dn_of, dn_ob, gates, s5_y, s5_u, gates, yf_ctx, yf_lat, s5_d, glu_w, glu_b, dn_g, w_out,
      final_g)


def _permute_w_in(w_in):
    cols = lambda lo, hi: w_in[..., lo:hi]
    data = [cols(0, 256), cols(512, 1280), cols(1552, 1808), cols(2064, 2320)]
    gates = [cols(256, 512), cols(1280, 1536), cols(1808, 2064), cols(2320, 2576)]
    ba = jnp.pad(cols(1536, 1552), ((0, 0), (0, 0), (0, 112)))
    return jnp.concatenate(data + gates + [ba], axis=-1).astype(BF16)


def kernel(x_prompt, x_sample, c, state_delta, state_s5, c_ctx, w_ada, b_ada, norm_g, w_in, pool_w, pool_scale,
           dn_conv, dn_a_log, dn_dt_bias, dn_norm_g, s5_a_re, s5_a_im, s5_log_dt, s5_b_re, s5_b_im, s5_c_re,
           s5_c_im, s5_d, s5_glu_w, s5_glu_b, ft_w, w_out, final_g):
    x = jnp.concatenate([x_prompt.astype(F32).reshape(N_CTX, D_MODEL),
                         x_sample.astype(F32).reshape(N_LAT, D_MODEL)], axis=0)
    cond8 = jnp.concatenate([c_ctx.astype(F32)[None], c.astype(F32),
                             jnp.zeros((8 - N_COND, D_MODEL), F32)], axis=0)
    ada = ada_all(cond8, w_ada, b_ada)
    pm_ctx, inv_ctx, pm_lat, inv_lat = _pool_constants()
    fpos, fch, g1, h2, fch2 = _ft_constants()
    s5_mt, s5_q4, s5_rt, s5_al = s5_prep(s5_a_re, s5_a_im, s5_log_dt, s5_b_re, s5_b_im, s5_c_re, s5_c_im)
    mod = ada.reshape(DEPTH * 8, 1, 3 * D_MODEL)
    norm_g3 = norm_g.reshape(DEPTH, 1, D_MODEL)
    w_in_p = _permute_w_in(w_in)
    eye_g = jnp.eye(len(POOL_WINDOWS), dtype=F32)
    w_bd = jnp.einsum('lgcd,gh->lgchd', pool_w, eye_g).reshape(DEPTH, BR, BR).astype(BF16)
    sc = pool_scale.reshape(DEPTH, 1, BR)
    ftw = ft_w.astype(BF16)
    conv_w = jnp.pad(dn_conv, ((0, 0), (0, 8 - CONV_K), (0, 0)))
    gate_p = jnp.zeros((DEPTH, 8, 128), F32)
    gate_p = gate_p.at[:, 0, 8:16].set(dn_a_log.reshape(DEPTH, 8)).at[:, 1, 8:16].set(dn_dt_bias.reshape(DEPTH, 8))
    out_params = (s5_d.reshape(DEPTH, 1, BR), s5_glu_w.astype(BF16), s5_glu_b.reshape(DEPTH, 1, BR),
                  jnp.tile(dn_norm_g, (1, DN_HEADS)).reshape(DEPTH, 1, BR), w_out.astype(BF16),
                  final_g.reshape(1, D_MODEL))
    s5_s0 = state_s5.astype(F32).transpose(1, 4, 3, 0, 2, 5).reshape(DEPTH, S5_LT, DEC_BATCH, 2 * S5_N)
    eye_h = jnp.eye(DN_HEADS, dtype=F32)
    dn_s0 = jnp.einsum('bldhkv,hg->ldbhkgv', state_delta.astype(F32), eye_h).reshape(DEPTH, 2 * DEC_BATCH, BR, BR)
    new_dn, new_s5 = [], []
    for l in range(DEPTH):
        pool_u, qkv, s5_u, ft_u, gates, ba = inproj(x, mod, norm_g3, w_in_p, l)

        yp_ctx = pool_branch(pool_u, gates, pm_ctx, inv_ctx, w_bd, sc, False, l)
        yp_lat = pool_branch(pool_u, gates, pm_lat, inv_lat, w_bd, sc, True, l)

        yf_ctx = ft_ctx(ft_u, gates, fpos, fch, ftw, l)
        yf_lat = ft_lat(ft_u, gates, g1, h2, fch2, ftw, l)

        s5_y, fin_s5 = s5_branch(s5_u, s5_mt, s5_q4, s5_rt, s5_al, s5_s0, l)
        new_s5.append(fin_s5)

        dn_of, dn_ob, fin_f, fin_b = dn_branch(qkv, ba, conv_w, gate_p, dn_s0, l)
        new_dn.append(jnp.stack([fin_f, fin_b]))

        finish = functools.partial(outproj, x, mod, yp_ctx, yp_lat, dn_of, dn_ob, gates, s5_y, s5_u, yf_ctx, yf_lat,
                                   *out_params, l)
        if l < DEPTH - 1:
            x = finish(False)
        else:
            tiles_ctx = N_CTX // TM
            y_ctx = finish(True, 0, tiles_ctx)
            y_lat = finish(True, tiles_ctx, N_LAT // TM)

    y_prompt = y_ctx.reshape(BATCH, SEQ, D_MODEL).astype(x_prompt.dtype)
    y_sample = y_lat.reshape(DEC_BATCH, DEC_SEQ, D_MODEL).astype(x_sample.dtype)
    new_state_delta = jnp.stack(new_dn).transpose(2, 0, 1, 3, 4, 5).astype(state_delta.dtype)
    fin = jnp.stack(new_s5).reshape(DEPTH, S5_G, 2, BATCH, 2, S5_N)
    new_state_s5 = fin.transpose(3, 0, 4, 2, 1, 5).astype(state_s5.dtype)
    return (y_prompt, y_sample, new_state_delta, new_state_s5)
```
